```python
import jax, jax.numpy as jnp
from jax import lax
import numpy as np

D_MODEL = 1024
BATCH = 4
SEQ = 4096
DEPTH = 2

HEAD_DIM = 64
FOX_HEADS = 6
FOX_WIDTH = FOX_HEADS * HEAD_DIM
POOL_GROUPS = 4
POOL_GROUP_DIM = 64
POOL_WINDOWS = (2, 4, 8, 16)
POOL_WIDTH = POOL_GROUPS * POOL_GROUP_DIM
NSA_HEADS = 6
NSA_KV_HEADS = 2
NSA_REP = NSA_HEADS // NSA_KV_HEADS
NSA_WIDTH = NSA_HEADS * HEAD_DIM
KV_WIDTH = NSA_KV_HEADS * HEAD_DIM
N_BRANCH = 3
CMP_LEN = 32
CMP_STRIDE = 16
CMP_HIDDEN = 256
SEL_BLOCK = 64
SEL_TOPN = 16
WINDOW = 512
FORCE_SCORE = 1.0e4
Q_BLOCK = 128
MIX_WIDTH = FOX_WIDTH + POOL_WIDTH + NSA_WIDTH
IN_SPLITS = (FOX_WIDTH, FOX_WIDTH, FOX_WIDTH, FOX_HEADS, POOL_WIDTH, NSA_WIDTH,
             KV_WIDTH, KV_WIDTH, KV_WIDTH, KV_WIDTH, KV_WIDTH, KV_WIDTH, NSA_HEADS * N_BRANCH)
IN_WIDTH = 3 * FOX_WIDTH + FOX_HEADS + POOL_WIDTH + NSA_WIDTH + 6 * KV_WIDTH + NSA_HEADS * N_BRANCH
MEM_LEN = 256
MEM_HEADS = 4
MEM_HEAD_DIM = D_MODEL // MEM_HEADS
N_GROUPS = 4
EXPERTS_PER_GROUP = 4
N_EXPERTS = N_GROUPS * EXPERTS_PER_GROUP
FINE_TOP_K = 2
EXPERT_FF = 512

EPS = 1e-6
NEG_INF = -1e30

kernel_name = "hybrid_fox_pool_nsa_hmoe"


def _split_points():
    pts, acc = [], 0
    for s in IN_SPLITS[:-1]:
        acc += s
        pts.append(acc)
    return pts


def rms_norm(x, g):
    xf = x.astype(jnp.float32)
    y = xf * lax.rsqrt(jnp.mean(xf * xf, axis=-1, keepdims=True) + EPS)
    return (y * g.astype(jnp.float32)).astype(x.dtype)


def masked_softmax(s, valid):
    p = jax.nn.softmax(jnp.where(valid, s, NEG_INF), axis=-1)
    return jnp.where(valid, p, 0.0)


def alibi_slopes(n_heads):
    return jnp.exp2(-8.0 * jnp.arange(1, n_heads + 1, dtype=jnp.float32) / n_heads)


def fox_attention(q, k, v, f_logit, b_f):
    bsz, T, _ = q.shape
    H, dh = FOX_HEADS, HEAD_DIM
    heads = lambda t: t.reshape(bsz, T, H, dh).transpose(0, 2, 1, 3)
    q, k, v = heads(q), heads(k), heads(v)
    log_f = jax.nn.log_sigmoid(f_logit.astype(jnp.float32) + b_f.astype(jnp.float32))
    cum = jnp.cumsum(log_f, axis=1).transpose(0, 2, 1)
    scale = dh ** -0.5
    kpos = jnp.arange(T)

    def block(i):
        q0 = i * Q_BLOCK
        qb = lax.dynamic_slice_in_dim(q, q0, Q_BLOCK, axis=2)
        cq = lax.dynamic_slice_in_dim(cum, q0, Q_BLOCK, axis=2)
        s = (jnp.einsum('bhqd,bhkd->bhqk', qb, k).astype(jnp.float32) * scale
             + cq[..., :, None] - cum[:, :, None, :])
        qpos = q0 + jnp.arange(Q_BLOCK)
        p = masked_softmax(s, kpos[None, :] <= qpos[:, None])
        return jnp.einsum('bhqk,bhkd->bhqd', p.astype(v.dtype), v)

    out = lax.map(block, jnp.arange(T // Q_BLOCK))
    return out.transpose(1, 0, 3, 2, 4).reshape(bsz, T, H * dh)


def pool_mixer(u, w_pool, scale):
    bsz, T, _ = u.shape
    uf = u.astype(jnp.float32).reshape(bsz, T, POOL_GROUPS, POOL_GROUP_DIM)
    cs = jnp.concatenate([jnp.zeros((bsz, 1, POOL_GROUPS, POOL_GROUP_DIM), jnp.float32),
                          jnp.cumsum(uf, axis=1)], axis=1)
    t = jnp.arange(T)
    outs = []
    for gi, w in enumerate(POOL_WINDOWS):
        start = jnp.maximum(t + 1 - w, 0)
        win_sum = cs[:, 1:, gi] - cs[:, start, gi]
        count = jnp.minimum(t + 1, w).astype(jnp.float32)
        outs.append(win_sum / count[None, :, None] - uf[:, :, gi])
    pooled = jnp.stack(outs, axis=2).astype(u.dtype)
    mixed = jnp.einsum('btgc,gcd->btgd', pooled, w_pool)
    return mixed.reshape(bsz, T, POOL_WIDTH) * scale


def nsa_attention(q, k_cmp, v_cmp, k_slc, v_slc, k_win, v_win, gate_logit, gate_b,
                  pe_k, pe_v, w1_k, w2_k, w1_v, w2_v):
    bsz, T, _ = q.shape
    G, R, dh = NSA_KV_HEADS, NSA_REP, HEAD_DIM
    scale = dh ** -0.5
    qh = q.reshape(bsz, T, G, R, dh).transpose(0, 2, 3, 1, 4)
    kv_heads = lambda t: t.reshape(bsz, T, G, dh).transpose(0, 2, 1, 3)
    k_cmp, v_cmp, k_slc, v_slc, k_win, v_win = (kv_heads(t) for t in (k_cmp, v_cmp, k_slc, v_slc, k_win, v_win))
    gates = jax.nn.sigmoid(gate_logit.astype(jnp.float32) + gate_b.astype(jnp.float32))
    gates = gates.reshape(bsz, T, G, R, N_BRANCH).transpose(0, 2, 3, 1, 4)

    n_chunk = T // CMP_STRIDE
    n_sub = CMP_LEN // CMP_STRIDE
    n_cmp = n_chunk - n_sub + 1

    def compress(t, pe, w1, w2):
        ch = t.reshape(bsz, G, n_chunk, CMP_STRIDE, dh)
        blocks = jnp.concatenate([ch[:, :, j:j + n_cmp] for j in range(n_sub)], axis=3)
        flat = (blocks + pe).reshape(bsz, G, n_cmp, CMP_LEN * dh)
        return jax.nn.gelu(flat @ w1) @ w2

    kc = compress(k_cmp, pe_k, w1_k, w2_k)
    vc = compress(v_cmp, pe_v, w1_v, w2_v)
    c_start = jnp.arange(n_cmp) * CMP_STRIDE
    c_end = c_start + CMP_LEN - 1

    n_sel_blk = T // SEL_BLOCK
    n_top = min(SEL_TOPN, n_sel_blk)
    s_start = jnp.arange(n_sel_blk) * SEL_BLOCK
    overlap = ((c_start[:, None] < s_start[None, :] + SEL_BLOCK)
               & (c_end[:, None] >= s_start[None, :])).astype(jnp.float32)
    ks_blk = k_slc.reshape(bsz, G, n_sel_blk, SEL_BLOCK, dh)
    vs_blk = v_slc.reshape(bsz, G, n_sel_blk, SEL_BLOCK, dh)
    kw_pad = jnp.pad(k_win, ((0, 0), (0, 0), (WINDOW, 0), (0, 0)))
    vw_pad = jnp.pad(v_win, ((0, 0), (0, 0), (WINDOW, 0), (0, 0)))
    slopes = alibi_slopes(NSA_HEADS).reshape(1, G, R, 1, 1)
    b_idx = jnp.arange(bsz)[:, None, None, None]
    g_idx = jnp.arange(G)[None, :, None, None]
    blk_ids = jnp.arange(n_sel_blk)
    in_blk = jnp.arange(SEL_BLOCK)
    win_off = jnp.arange(WINDOW + Q_BLOCK)

    def block(i):
        q0 = i * Q_BLOCK
        qb = lax.dynamic_slice_in_dim(qh, q0, Q_BLOCK, axis=3)
        gb = lax.dynamic_slice_in_dim(gates, q0, Q_BLOCK, axis=3)
        tq = q0 + jnp.arange(Q_BLOCK)

        dist_c = tq[:, None] - c_end[None, :]
        s_c = (jnp.einsum('bgrqd,bgcd->bgrqc', qb, kc).astype(jnp.float32) * scale
               - slopes * dist_c.astype(jnp.float32))
        p_c = masked_softmax(s_c, dist_c >= 0)
        o_c = jnp.einsum('bgrqc,bgcd->bgrqd', p_c.astype(vc.dtype), vc)

        imp = jnp.einsum('bgrqc,cj->bgqj', p_c, overlap)
        cur = tq // SEL_BLOCK
        forced = ((blk_ids[None, :] == 0) | (blk_ids[None, :] == cur[:, None])
                  | (blk_ids[None, :] == cur[:, None] - 1))
        causal_blk = blk_ids[None, :] <= cur[:, None]
        score = jnp.where(forced, FORCE_SCORE, jnp.where(causal_blk, imp, -1.0))
        _, sel = lax.top_k(score, n_top)

        k_sel = ks_blk[b_idx, g_idx, sel].reshape(bsz, G, Q_BLOCK, n_top * SEL_BLOCK, dh)
        v_sel = vs_blk[b_idx, g_idx, sel].reshape(bsz, G, Q_BLOCK, n_top * SEL_BLOCK, dh)
        pos = (sel[..., None] * SEL_BLOCK + in_blk).reshape(bsz, G, Q_BLOCK, n_top * SEL_BLOCK)
        dist_s = (tq[None, None, :, None] - pos)[:, :, None]
        s_s = (jnp.einsum('bgrqd,bgqld->bgrql', qb, k_sel).astype(jnp.float32) * scale
               - slopes * dist_s.astype(jnp.float32))
        p_s = masked_softmax(s_s, dist_s >= 0)
        o_s = jnp.einsum('bgrql,bgqld->bgrqd', p_s.astype(v_sel.dtype), v_sel)

        kw = lax.dynamic_slice_in_dim(kw_pad, q0, WINDOW + Q_BLOCK, axis=2)
        vw = lax.dynamic_slice_in_dim(vw_pad, q0, WINDOW + Q_BLOCK, axis=2)
        pk = q0 - WINDOW + win_off
        dist_w = tq[:, None] - pk[None, :]
        valid_w = (dist_w >= 0) & (dist_w < WINDOW) & (pk[None, :] >= 0)
        s_w = (jnp.einsum('bgrqd,bgkd->bgrqk', qb, kw).astype(jnp.float32) * scale
               - slopes * dist_w.astype(jnp.float32))
        p_w = masked_softmax(s_w, valid_w)
        o_w = jnp.einsum('bgrqk,bgkd->bgrqd', p_w.astype(vw.dtype), vw)

        o = gb[..., 0:1] * o_c + gb[..., 1:2] * o_s + gb[..., 2:3] * o_w
        return o.astype(q.dtype)

    out = lax.map(block, jnp.arange(T // Q_BLOCK))
    return out.transpose(1, 0, 4, 2, 3, 5).reshape(bsz, T, NSA_WIDTH)


def memory_cross_attention(h, mem_n, wq, wk, wv, wo):
    bsz, T, _ = h.shape
    M = mem_n.shape[1]
    q = (h @ wq).reshape(bsz, T, MEM_HEADS, MEM_HEAD_DIM)
    k = (mem_n @ wk).reshape(bsz, M, MEM_HEADS, MEM_HEAD_DIM)
    v = (mem_n @ wv).reshape(bsz, M, MEM_HEADS, MEM_HEAD_DIM)
    s = jnp.einsum('bqhd,bkhd->bhqk', q, k).astype(jnp.float32) * MEM_HEAD_DIM ** -0.5
    p = jax.nn.softmax(s, axis=-1)
    o = jnp.einsum('bhqk,bkhd->bqhd', p.astype(v.dtype), v).reshape(bsz, T, D_MODEL)
    return o @ wo


def hier_moe(h, w_coarse, b_coarse, w_fine, b_fine, w_gate, w_up, w_down):
    bsz, T, D = h.shape
    ht = h.reshape(-1, D)
    n_tok = ht.shape[0]
    p_coarse = jax.nn.softmax((ht @ w_coarse).astype(jnp.float32) + b_coarse.astype(jnp.float32), axis=-1)
    grp = jnp.argmax(p_coarse, axis=-1)
    w_grp = jnp.take_along_axis(p_coarse, grp[:, None], axis=-1)
    fine = ((ht @ w_fine).astype(jnp.float32) + b_fine.astype(jnp.float32)).reshape(n_tok, N_GROUPS, EXPERTS_PER_GROUP)
    fine = jnp.take_along_axis(fine, grp[:, None, None], axis=1)[:, 0]
    top_p, top_i = lax.top_k(jax.nn.softmax(fine, axis=-1), FINE_TOP_K)
    top_p = top_p / jnp.sum(top_p, axis=-1, keepdims=True)
    expert_id = grp[:, None] * EXPERTS_PER_GROUP + top_i
    weights = w_grp * top_p
    combine = jnp.sum(jax.nn.one_hot(expert_id, N_EXPERTS, dtype=jnp.float32) * weights[..., None], axis=1)
    combine = combine.astype(h.dtype)
    y = jnp.zeros_like(ht)
    for e in range(N_EXPERTS):
        he = jax.nn.silu(ht @ w_gate[e]) * (ht @ w_up[e])
        y = y + combine[:, e:e + 1] * (he @ w_down[e])
    return y.reshape(bsz, T, D)


def setup_inputs(seed: int = 0) -> dict:
    key = jax.random.key(seed)
    ks = list(jax.random.split(key, 32))
    def nrm(shape, scale):
        return jax.random.normal(ks.pop(), shape, jnp.float32) * scale
    def gain(shape):
        return 1.0 + nrm(shape, 0.02)
    L, D = DEPTH, D_MODEL
    x = nrm((BATCH, SEQ, D), 1.0)
    mem = nrm((BATCH, MEM_LEN, D), 1.0)
    mix_norm_g = gain((L, D))
    w_in = nrm((L, D, IN_WIDTH), D ** -0.5)
    fox_forget_b = jax.random.uniform(ks.pop(), (L, FOX_HEADS), jnp.float32, 1.0, 4.0)
    pool_w = nrm((L, POOL_GROUPS, POOL_GROUP_DIM, POOL_GROUP_DIM), POOL_GROUP_DIM ** -0.5)
    pool_scale = 1.0 + nrm((L, POOL_WIDTH), 0.1)
    cmp_pe_k = nrm((L, CMP_LEN, HEAD_DIM), 0.02)
    cmp_pe_v = nrm((L, CMP_LEN, HEAD_DIM), 0.02)
    cmp_w1_k = nrm((L, CMP_LEN * HEAD_DIM, CMP_HIDDEN), (CMP_LEN * HEAD_DIM) ** -0.5)
    cmp_w2_k = nrm((L, CMP_HIDDEN, HEAD_DIM), CMP_HIDDEN ** -0.5)
    cmp_w1_v = nrm((L, CMP_LEN * HEAD_DIM, CMP_HIDDEN), (CMP_LEN * HEAD_DIM) ** -0.5)
    cmp_w2_v = nrm((L, CMP_HIDDEN, HEAD_DIM), CMP_HIDDEN ** -0.5)
    nsa_gate_b = nrm((L, NSA_HEADS * N_BRANCH), 0.01)
    w_out = nrm((L, MIX_WIDTH, D), MIX_WIDTH ** -0.5)
    xattn_norm_g = gain((L, D))
    mem_norm_g = gain((L, D))
    xattn_wq = nrm((L, D, D), D ** -0.5)
    xattn_wk = nrm((L, D, D), D ** -0.5)
    xattn_wv = nrm((L, D, D), D ** -0.5)
    xattn_wo = nrm((L, D, D), D ** -0.5)
    ffn_norm_g = gain((L, D))
    router_coarse_w = nrm((L, D, N_GROUPS), D ** -0.5)
    router_coarse_b = nrm((L, N_GROUPS), 0.01)
    router_fine_w = nrm((L, D, N_EXPERTS), D ** -0.5)
    router_fine_b = nrm((L, N_EXPERTS), 0.01)
    exp_w_gate = nrm((L, N_EXPERTS, D, EXPERT_FF), D ** -0.5)
    exp_w_up = nrm((L, N_EXPERTS, D, EXPERT_FF), D ** -0.5)
    exp_w_down = nrm((L, N_EXPERTS, EXPERT_FF, D), EXPERT_FF ** -0.5)
    final_norm_g = gain((D,))
    return {"x": x, "mem": mem, "mix_norm_g": mix_norm_g, "w_in": w_in, "fox_forget_b": fox_forget_b,
            "pool_w": pool_w, "pool_scale": pool_scale, "cmp_pe_k": cmp_pe_k, "cmp_pe_v": cmp_pe_v,
            "cmp_w1_k": cmp_w1_k, "cmp_w2_k": cmp_w2_k, "cmp_w1_v": cmp_w1_v, "cmp_w2_v": cmp_w2_v,
            "nsa_gate_b": nsa_gate_b, "w_out": w_out, "xattn_norm_g": xattn_norm_g, "mem_norm_g": mem_norm_g,
            "xattn_wq": xattn_wq, "xattn_wk": xattn_wk, "xattn_wv": xattn_wv, "xattn_wo": xattn_wo,
            "ffn_norm_g": ffn_norm_g, "router_coarse_w": router_coarse_w, "router_coarse_b": router_coarse_b,
            "router_fine_w": router_fine_w, "router_fine_b": router_fine_b, "exp_w_gate": exp_w_gate,
            "exp_w_up": exp_w_up, "exp_w_down": exp_w_down, "final_norm_g": final_norm_g}


def reference(x, mem, mix_norm_g, w_in, fox_forget_b, pool_w, pool_scale, cmp_pe_k, cmp_pe_v,
              cmp_w1_k, cmp_w2_k, cmp_w1_v, cmp_w2_v, nsa_gate_b, w_out, xattn_norm_g, mem_norm_g,
              xattn_wq, xattn_wk, xattn_wv, xattn_wo, ffn_norm_g, router_coarse_w, router_coarse_b,
              router_fine_w, router_fine_b, exp_w_gate, exp_w_up, exp_w_down, final_norm_g):
    pts = _split_points()
    for l in range(DEPTH):
        h = rms_norm(x, mix_norm_g[l])
        (qa, ka, va, fa, ub, qc, kcm, vcm, ksl, vsl, kwn, vwn, gl) = jnp.split(h @ w_in[l], pts, axis=-1)
        o_a = fox_attention(qa, ka, va, fa, fox_forget_b[l])
        o_b = pool_mixer(ub, pool_w[l], pool_scale[l])
        o_c = nsa_attention(qc, kcm, vcm, ksl, vsl, kwn, vwn, gl, nsa_gate_b[l],
                            cmp_pe_k[l], cmp_pe_v[l], cmp_w1_k[l], cmp_w2_k[l], cmp_w1_v[l], cmp_w2_v[l])
        x = x + jnp.concatenate([o_a, o_b.astype(o_a.dtype), o_c], axis=-1) @ w_out[l]
        x = x + memory_cross_attention(rms_norm(x, xattn_norm_g[l]), rms_norm(mem, mem_norm_g[l]),
                                       xattn_wq[l], xattn_wk[l], xattn_wv[l], xattn_wo[l])
        x = x + hier_moe(rms_norm(x, ffn_norm_g[l]), router_coarse_w[l], router_coarse_b[l],
                         router_fine_w[l], router_fine_b[l], exp_w_gate[l], exp_w_up[l], exp_w_down[l])
    return rms_norm(x, final_norm_g)
```

```python
import functools

import jax
import jax.numpy as jnp
from jax import lax
from jax.experimental import pallas as pl
from jax.experimental.pallas import tpu as pltpu

F32 = jnp.float32
BF16 = jnp.bfloat16

D_MODEL = 1024
HEAD_DIM = 64
FOX_HEADS = 6
FOX_WIDTH = FOX_HEADS * HEAD_DIM
POOL_WIDTH = 256
POOL_GROUP_DIM = 64
POOL_WINDOWS = (2, 4, 8, 16)
POOL_HALO = 16
NSA_HEADS = 6
NSA_KV_HEADS = 2
NSA_REP = NSA_HEADS // NSA_KV_HEADS
NSA_WIDTH = NSA_HEADS * HEAD_DIM
KV_WIDTH = NSA_KV_HEADS * HEAD_DIM
N_BRANCH = 3
CMP_LEN = 32
CMP_STRIDE = 16
CMP_HIDDEN = 256
SEL_BLOCK = 64
SEL_TOPN = 16
WINDOW = 512
FORCE_SCORE = 1.0e4
MEM_HEADS = 4
MEM_HEAD_DIM = D_MODEL // MEM_HEADS
N_GROUPS = 4
EXPERTS_PER_GROUP = 4
N_EXPERTS = N_GROUPS * EXPERTS_PER_GROUP
EXPERT_FF = 512
EPS = 1e-6
NEG_INF = -1e30

LANES = 128
BIG_WIDTH = 3 * FOX_WIDTH + NSA_WIDTH + 6 * KV_WIDTH
COL_QC = 3 * FOX_WIDTH
COL_KV = COL_QC + NSA_WIDTH
SMALL_WIDTH = 384
COL_FA = POOL_WIDTH
COL_GL = POOL_WIDTH + FOX_HEADS
ROUTER_WIDTH = LANES

VMEM_LIMIT = 56 * 1024 * 1024

_NT = (((1,), (1,)), ((), ()))


def _params(semantics):
    return pltpu.CompilerParams(dimension_semantics=semantics, vmem_limit_bytes=VMEM_LIMIT)


def _rms(x, g):
    return x * lax.rsqrt(jnp.mean(x * x, axis=-1, keepdims=True) + EPS) * g


def _dot(a, b):
    return jnp.dot(a, b, preferred_element_type=F32)


def _dot_nt(a, b):
    return lax.dot_general(a, b, _NT, preferred_element_type=F32)


def _split3(a):
    a0 = a.astype(BF16)
    r = a - a0.astype(F32)
    a1 = r.astype(BF16)
    a2 = (r - a1.astype(F32)).astype(BF16)
    return a0, a1, a2


def _in_proj_kernel(x_ref, g_ref, wb_ref, ws_ref, big_ref, small_ref):
    yb = _rms(x_ref[...], g_ref[...]).astype(BF16)
    big_ref[...] = _dot(yb, wb_ref[...]).astype(BF16)
    small_ref[...] = _dot(yb, ws_ref[...])


def _in_proj(x, g, w_big, w_small, tm=512):
    n = x.shape[0]
    return pl.pallas_call(
        _in_proj_kernel,
        grid=(n // tm,),
        in_specs=[pl.BlockSpec((tm, D_MODEL), lambda i: (i, 0)),
                  pl.BlockSpec((1, D_MODEL), lambda i: (0, 0)),
                  pl.BlockSpec((D_MODEL, BIG_WIDTH), lambda i: (0, 0)),
                  pl.BlockSpec((D_MODEL, SMALL_WIDTH), lambda i: (0, 0))],
        out_specs=[pl.BlockSpec((tm, BIG_WIDTH), lambda i: (i, 0)),
                   pl.BlockSpec((tm, SMALL_WIDTH), lambda i: (i, 0))],
        out_shape=[jax.ShapeDtypeStruct((n, BIG_WIDTH), BF16),
                   jax.ShapeDtypeStruct((n, SMALL_WIDTH), F32)],
        compiler_params=_params(("parallel",)),
        name="in_proj",
    )(x, g, w_big, w_small)


def _forget_kernel(f_ref, b_ref, cum_ref):
    c = jax.nn.log_sigmoid(f_ref[0] + b_ref[...])
    t_len = c.shape[1]
    lane = lax.broadcasted_iota(jnp.int32, c.shape, 1)
    sh = 1
    while sh < t_len:
        c = c + jnp.where(lane >= sh, pltpu.roll(c, sh, axis=1), 0.0)
        sh *= 2
    cum_ref[0] = c


def _forget_cumsum(f_t, b_col):
    bsz, hp, t_len = f_t.shape
    return pl.pallas_call(
        _forget_kernel,
        grid=(bsz,),
        in_specs=[pl.BlockSpec((1, hp, t_len), lambda b: (b, 0, 0)),
                  pl.BlockSpec((hp, 1), lambda b: (0, 0))],
        out_specs=pl.BlockSpec((1, hp, t_len), lambda b: (b, 0, 0)),
        out_shape=jax.ShapeDtypeStruct((bsz, hp, t_len), F32),
        compiler_params=_params(("parallel",)),
        name="forget",
    )(f_t, b_col)


def _fox_kernel(q_ref, k_ref, v_ref, cq_ref, ck_ref, o_ref, *, tq):
    i = pl.program_id(1)
    row = lax.broadcasted_iota(jnp.int32, (tq, tq), 0)
    col = lax.broadcasted_iota(jnp.int32, (tq, tq), 1)
    causal = col <= row
    for h in range(FOX_HEADS):
        sl = slice(h * HEAD_DIM, (h + 1) * HEAD_DIM)
        qh = q_ref[:, sl] * jnp.asarray(HEAD_DIM ** -0.5, BF16)
        cq = cq_ref[0, :, h:h + 1]

        def tile(j, carry, diagonal):
            m, l, acc = carry
            ks = pl.multiple_of(j * tq, tq)
            kh = k_ref[pl.ds(ks, tq), sl]
            vh = v_ref[pl.ds(ks, tq), sl]
            ck = ck_ref[0, j, h:h + 1, :]
            s = _dot_nt(qh, kh) + (cq - ck)
            if diagonal:
                s = jnp.where(causal, s, NEG_INF)
            m_new = jnp.maximum(m, jnp.max(s, axis=1, keepdims=True))
            alpha = jnp.exp(m - m_new)
            p = jnp.exp(s - m_new)
            l = alpha * l + jnp.sum(p, axis=1, keepdims=True)
            acc = alpha * acc + _dot(p.astype(BF16), vh)
            return m_new, l, acc

        init = (jnp.full((tq, 1), NEG_INF, F32), jnp.zeros((tq, 1), F32), jnp.zeros((tq, HEAD_DIM), F32))
        carry = lax.fori_loop(0, i, functools.partial(tile, diagonal=False), init)
        _, l, acc = tile(i, carry, True)
        o_ref[:, sl] = (acc / l).astype(BF16)


def _fox(big, cum_t, cum_k, bsz, t_len, tq=256):
    nq = t_len // tq
    return pl.pallas_call(
        functools.partial(_fox_kernel, tq=tq),
        grid=(bsz, nq),
        in_specs=[pl.BlockSpec((tq, FOX_WIDTH), lambda b, i: (b * nq + i, 0)),
                  pl.BlockSpec((t_len, FOX_WIDTH), lambda b, i: (b, 1)),
                  pl.BlockSpec((t_len, FOX_WIDTH), lambda b, i: (b, 2)),
                  pl.BlockSpec((1, tq, 8), lambda b, i: (b, i, 0)),
                  pl.BlockSpec((1, nq, 8, tq), lambda b, i: (b, 0, 0, 0))],
        out_specs=pl.BlockSpec((tq, FOX_WIDTH), lambda b, i: (b * nq + i, 0)),
        out_shape=jax.ShapeDtypeStruct((bsz * t_len, FOX_WIDTH), BF16),
        compiler_params=_params(("parallel", "arbitrary")),
        name="fox",
    )(big, big, big, cum_t, cum_k)


def _pool_kernel(u_ref, halo_ref, w_ref, scale_ref, o_ref, x_scr, *, tt):
    i = pl.program_id(1)
    u = u_ref[...]
    x_scr[POOL_HALO:, :] = u
    x_scr[:POOL_HALO, :] = jnp.where(i > 0, halo_ref[...], 0.0)
    t_pos = i * tt + lax.broadcasted_iota(jnp.int32, (tt, 1), 0)
    lane = lax.broadcasted_iota(jnp.int32, (tt, POOL_WIDTH), 1)
    run = u
    pooled = jnp.zeros_like(u)
    gi = 0
    for back in range(1, max(POOL_WINDOWS)):
        run = run + x_scr[POOL_HALO - back:POOL_HALO - back + tt, :]
        w = back + 1
        if w in POOL_WINDOWS:
            gi = POOL_WINDOWS.index(w)
            count = jnp.minimum(t_pos + 1, w).astype(F32)
            in_group = (lane >= gi * POOL_GROUP_DIM) & (lane < (gi + 1) * POOL_GROUP_DIM)
            pooled = jnp.where(in_group, run / count - u, pooled)
    mixed = _dot(pooled.astype(BF16), w_ref[...])
    o_ref[...] = (mixed * scale_ref[...]).astype(BF16)


def _pool(small, w_bd, scale, bsz, t_len, tt=512):
    nt = t_len // tt
    per = tt // POOL_HALO
    return pl.pallas_call(
        functools.partial(_pool_kernel, tt=tt),
        grid=(bsz, nt),
        in_specs=[pl.BlockSpec((tt, POOL_WIDTH), lambda b, i: (b * nt + i, 0)),
                  pl.BlockSpec((POOL_HALO, POOL_WIDTH), lambda b, i: (jnp.maximum((b * nt + i) * per - 1, 0), 0)),
                  pl.BlockSpec((POOL_WIDTH, POOL_WIDTH), lambda b, i: (0, 0)),
                  pl.BlockSpec((1, POOL_WIDTH), lambda b, i: (0, 0))],
        out_specs=pl.BlockSpec((tt, POOL_WIDTH), lambda b, i: (b * nt + i, 0)),
        out_shape=jax.ShapeDtypeStruct((bsz * t_len, POOL_WIDTH), BF16),
        scratch_shapes=[pltpu.VMEM((tt + POOL_HALO, POOL_WIDTH), F32)],
        compiler_params=_params(("parallel", "arbitrary")),
        name="pool",
    )(small, small, w_bd, scale)


def _compress_kernel(xk_ref, xv_ref, pek_ref, pev_ref, w1k_ref, w2k_ref, w1v_ref, w2v_ref, kc_ref, vc_ref):
    half = CMP_STRIDE * HEAD_DIM

    def one(x_ref, pe_ref, w1_ref, w2_ref, out_ref):
        x = x_ref[0]
        n_chunk = x.shape[0]
        first = _dot(x, w1_ref[:half, :])
        second = _dot(x, w1_ref[half:, :])
        pe_term = _dot(pe_ref[...], w1_ref[...])[0:1, :]
        hidden = first + pltpu.roll(second, n_chunk - 1, axis=0) + pe_term
        out_ref[0] = _dot(jax.nn.gelu(hidden).astype(BF16), w2_ref[...])

    one(xk_ref, pek_ref, w1k_ref, w2k_ref, kc_ref)
    one(xv_ref, pev_ref, w1v_ref, w2v_ref, vc_ref)


def _compress(xk, xv, pe_k, pe_v, w1_k, w2_k, w1_v, w2_v):
    nbg, n_chunk, width = xk.shape
    full = lambda shape: pl.BlockSpec(shape, lambda b: tuple(0 for _ in shape))
    x_spec = pl.BlockSpec((1, n_chunk, width), lambda b: (b, 0, 0))
    o_spec = pl.BlockSpec((1, n_chunk, HEAD_DIM), lambda b: (b, 0, 0))
    return pl.pallas_call(
        _compress_kernel,
        grid=(nbg,),
        in_specs=[x_spec, x_spec, full(pe_k.shape), full(pe_v.shape),
                  full(w1_k.shape), full(w2_k.shape), full(w1_v.shape), full(w2_v.shape)],
        out_specs=[o_spec, o_spec],
        out_shape=[jax.ShapeDtypeStruct((nbg, n_chunk, HEAD_DIM), F32)] * 2,
        compiler_params=_params(("parallel",)),
        name="compress",
    )(xk, xv, pe_k, pe_v, w1_k, w2_k, w1_v, w2_v)


def _alibi_slope(h):
    return float(2.0 ** (-8.0 * (h + 1) / NSA_HEADS))


def _online_step(carry, s, valid, v):
    m, l, acc = carry
    s = jnp.where(valid, s, NEG_INF)
    m_new = jnp.maximum(m, jnp.max(s, axis=1, keepdims=True))
    alpha = jnp.exp(m - m_new)
    p = jnp.where(valid, jnp.exp(s - m_new), 0.0)
    l = alpha * l + jnp.sum(p, axis=1, keepdims=True)
    acc = alpha * acc + _dot(p.astype(BF16), v)
    return m_new, l, acc


def _finish(carry):
    _, l, acc = carry
    return acc * jnp.where(l > 0.0, 1.0 / l, 0.0)


def _nsa_kernel(q_ref, ksl_ref, vsl_ref, kwn_ref, vwn_ref, kc_ref, vc_ref, sm_ref, gb_ref, ex_ref, o_ref,
                *, tq, tks, tkw):
    i = pl.program_id(1)
    t0 = i * tq
    t_row = t0 + lax.broadcasted_iota(jnp.int32, (tq, 1), 0)
    gates = jax.nn.sigmoid(sm_ref[:, COL_GL:COL_GL + NSA_HEADS * N_BRANCH] + gb_ref[...])
    scale = jnp.asarray(HEAD_DIM ** -0.5, BF16)
    n_cmp_pad = kc_ref.shape[2]
    n_blk = ex_ref.shape[1]

    def fresh():
        return (jnp.full((tq, 1), NEG_INF, F32), jnp.zeros((tq, 1), F32), jnp.zeros((tq, HEAD_DIM), F32))

    for g in range(NSA_KV_HEADS):
        gsl = slice(g * HEAD_DIM, (g + 1) * HEAD_DIM)
        qs = [q_ref[:, (g * NSA_REP + r) * HEAD_DIM:(g * NSA_REP + r + 1) * HEAD_DIM] * scale
              for r in range(NSA_REP)]
        slopes = [_alibi_slope(g * NSA_REP + r) for r in range(NSA_REP)]

        kc0, kc1, _ = _split3(kc_ref[0, g])
        vc = vc_ref[0, g].astype(BF16)
        c_end = lax.broadcasted_iota(jnp.int32, (1, n_cmp_pad), 1) * CMP_STRIDE + (CMP_LEN - 1)
        dist_c = t_row - c_end
        valid_c = dist_c >= 0
        dist_cf = dist_c.astype(F32)
        o_cmp = []
        p_sum = jnp.zeros((tq, n_cmp_pad), F32)
        for r in range(NSA_REP):
            s = _dot_nt(qs[r], kc0) + _dot_nt(qs[r], kc1) - slopes[r] * dist_cf
            s = jnp.where(valid_c, s, NEG_INF)
            m = jnp.max(s, axis=1, keepdims=True)
            p = jnp.where(valid_c, jnp.exp(s - m), 0.0)
            l = jnp.sum(p, axis=1, keepdims=True)
            p = p * jnp.where(l > 0.0, 1.0 / l, 0.0)
            o_cmp.append(_dot(p.astype(BF16), vc))
            p_sum = p_sum + p

        c_start = lax.broadcasted_iota(jnp.int32, (n_cmp_pad, n_blk), 0) * CMP_STRIDE
        s_start = lax.broadcasted_iota(jnp.int32, (n_cmp_pad, n_blk), 1) * SEL_BLOCK
        overlap = jnp.where((c_start < s_start + SEL_BLOCK) & (c_start + (CMP_LEN - 1) >= s_start), 1.0, 0.0).astype(BF16)
        p0, p1, p2 = _split3(p_sum)
        imp = _dot(p0, overlap) + _dot(p1, overlap) + _dot(p2, overlap)
        blk = lax.broadcasted_iota(jnp.int32, (tq, n_blk), 1)
        cur = t_row // SEL_BLOCK
        forced = (blk == 0) | (blk == cur) | (blk == cur - 1)
        score = jnp.where(forced, FORCE_SCORE, jnp.where(blk <= cur, imp, -1.0))
        rank = jnp.zeros((tq, n_blk), F32)
        for jp in range(n_blk):
            other = score[:, jp:jp + 1]
            rank = rank + jnp.where(blk > jp, jnp.where(other >= score, 1.0, 0.0), jnp.where(other > score, 1.0, 0.0))
        sel = jnp.where(rank < float(min(SEL_TOPN, n_blk)), 1.0, 0.0).astype(BF16)

        def slc_tile(j, carries, diagonal):
            ks = pl.multiple_of(j * tks, tks)
            k = ksl_ref[pl.ds(ks, tks), gsl]
            v = vsl_ref[pl.ds(ks, tks), gsl]
            pos = ks + lax.broadcasted_iota(jnp.int32, (1, tks), 1)
            valid = _dot(sel, ex_ref[j]) > 0.5
            if diagonal:
                valid = valid & (pos <= t_row)
            pos_f = pos.astype(F32)
            return tuple(_online_step(carries[r], _dot_nt(qs[r], k) + slopes[r] * pos_f, valid, v)
                         for r in range(NSA_REP))

        n_full = (t0 + tq - 1) // tks
        carries = lax.fori_loop(0, n_full, functools.partial(slc_tile, diagonal=False),
                                tuple(fresh() for _ in range(NSA_REP)))
        carries = slc_tile(n_full, carries, True)
        o_slc = [_finish(c) for c in carries]

        def win_tile(j, carries):
            ks = pl.multiple_of(j * tkw, tkw)
            k = kwn_ref[pl.ds(ks, tkw), gsl]
            v = vwn_ref[pl.ds(ks, tkw), gsl]
            pos = ks + lax.broadcasted_iota(jnp.int32, (1, tkw), 1)
            dist = t_row - pos
            valid = (dist >= 0) & (dist < WINDOW)
            pos_f = pos.astype(F32)
            return tuple(_online_step(carries[r], _dot_nt(qs[r], k) + slopes[r] * pos_f, valid, v)
                         for r in range(NSA_REP))

        j_lo = jnp.maximum((t0 - WINDOW + 1) // tkw, 0)
        j_hi = (t0 + tq - 1) // tkw + 1
        carries = lax.fori_loop(j_lo, j_hi, win_tile, tuple(fresh() for _ in range(NSA_REP)))
        o_win = [_finish(c) for c in carries]

        for r in range(NSA_REP):
            h = g * NSA_REP + r
            gate = lambda br: gates[:, h * N_BRANCH + br:h * N_BRANCH + br + 1]
            o = gate(0) * o_cmp[r] + gate(1) * o_slc[r] + gate(2) * o_win[r]
            o_ref[:, h * HEAD_DIM:(h + 1) * HEAD_DIM] = o.astype(BF16)


def _nsa(big, kc, vc, small, gate_b, expand, bsz, t_len, tq=128, tks=256, tkw=128):
    nq = t_len // tq
    kv_col = COL_KV // KV_WIDTH
    kv_spec = lambda c: pl.BlockSpec((t_len, KV_WIDTH), lambda b, i: (b, kv_col + c))
    n_cmp_pad = kc.shape[2]
    return pl.pallas_call(
        functools.partial(_nsa_kernel, tq=tq, tks=tks, tkw=tkw),
        grid=(bsz, nq),
        in_specs=[pl.BlockSpec((tq, NSA_WIDTH), lambda b, i: (b * nq + i, COL_QC // NSA_WIDTH)),
                  kv_spec(2), kv_spec(3), kv_spec(4), kv_spec(5),
                  pl.BlockSpec((1, NSA_KV_HEADS, n_cmp_pad, HEAD_DIM), lambda b, i: (b, 0, 0, 0)),
                  pl.BlockSpec((1, NSA_KV_HEADS, n_cmp_pad, HEAD_DIM), lambda b, i: (b, 0, 0, 0)),
                  pl.BlockSpec((tq, SMALL_WIDTH), lambda b, i: (b * nq + i, 0)),
                  pl.BlockSpec((1, NSA_HEADS * N_BRANCH), lambda b, i: (0, 0)),
                  pl.BlockSpec(expand.shape, lambda b, i: (0, 0, 0))],
        out_specs=pl.BlockSpec((tq, NSA_WIDTH), lambda b, i: (b * nq + i, 0)),
        out_shape=jax.ShapeDtypeStruct((bsz * t_len, NSA_WIDTH), BF16),
        compiler_params=_params(("parallel", "arbitrary")),
        name="nsa",
    )(big, big, big, big, big, kc, vc, small, gate_b, expand)


def _mem_kv_kernel(mem_ref, g_ref, wk_ref, wv_ref, k_ref, v_ref):
    mb = _rms(mem_ref[0], g_ref[...]).astype(BF16)
    k_ref[0] = _dot(mb, wk_ref[...]).astype(BF16)
    v_ref[0] = _dot(mb, wv_ref[...]).astype(BF16)


def _mem_kv(mem, g, wk, wv):
    bsz, m_len, _ = mem.shape
    w_spec = pl.BlockSpec((D_MODEL, D_MODEL), lambda b: (0, 0))
    m_spec = pl.BlockSpec((1, m_len, D_MODEL), lambda b: (b, 0, 0))
    return pl.pallas_call(
        _mem_kv_kernel,
        grid=(bsz,),
        in_specs=[m_spec, pl.BlockSpec((1, D_MODEL), lambda b: (0, 0)), w_spec, w_spec],
        out_specs=[m_spec, m_spec],
        out_shape=[jax.ShapeDtypeStruct((bsz, m_len, D_MODEL), BF16)] * 2,
        compiler_params=_params(("parallel",)),
        name="mem_kv",
    )(mem, g, wk, wv)


def _route(logits):
    lane = lax.broadcasted_iota(jnp.int32, logits.shape, 1)
    big_i = jnp.int32(1 << 20)
    is_coarse = lane < N_GROUPS
    lc = jnp.where(is_coarse, logits, NEG_INF)
    mc = jnp.max(lc, axis=1, keepdims=True)
    grp = jnp.min(jnp.where(is_coarse & (lc == mc), lane, big_i), axis=1, keepdims=True)
    w_grp = 1.0 / jnp.sum(jnp.where(is_coarse, jnp.exp(lc - mc), 0.0), axis=1, keepdims=True)
    e_id = lane - N_GROUPS
    in_grp = (e_id >= grp * EXPERTS_PER_GROUP) & (e_id < (grp + 1) * EXPERTS_PER_GROUP)
    lf = jnp.where(in_grp, logits, NEG_INF)
    m1 = jnp.max(lf, axis=1, keepdims=True)
    i1 = jnp.min(jnp.where(in_grp & (lf == m1), lane, big_i), axis=1, keepdims=True)
    lf2 = jnp.where(lane == i1, NEG_INF, lf)
    m2 = jnp.max(lf2, axis=1, keepdims=True)
    i2 = jnp.min(jnp.where(in_grp & (lane != i1) & (lf2 == m2), lane, big_i), axis=1, keepdims=True)
    denom = jnp.sum(jnp.where(in_grp, jnp.exp(lf - m1), 0.0), axis=1, keepdims=True)
    p1 = 1.0 / denom
    p2 = jnp.exp(m2 - m1) / denom
    tot = p1 + p2
    w1 = w_grp * (p1 / tot)
    w2 = w_grp * (p2 / tot)
    return jnp.where(lane + N_GROUPS == i1, w1, jnp.where(lane + N_GROUPS == i2, w2, 0.0))


def _mix_out_kernel(x_ref, oa_ref, ob_ref, oc_ref, woa_ref, wob_ref, woc_ref, gx_ref, wq_ref, mk_ref, mv_ref,
                    wo_ref, gf_ref, wr_ref, br_ref, x2_ref, hb_ref, comb_ref):
    x1 = (x_ref[...] + _dot(oa_ref[...], woa_ref[...]) + _dot(ob_ref[...], wob_ref[...])
          + _dot(oc_ref[...], woc_ref[...]))
    hq = _rms(x1, gx_ref[...]).astype(BF16)
    q = (_dot(hq, wq_ref[...]) * (MEM_HEAD_DIM ** -0.5)).astype(BF16)
    heads = []
    for h in range(MEM_HEADS):
        sl = slice(h * MEM_HEAD_DIM, (h + 1) * MEM_HEAD_DIM)
        s = _dot_nt(q[:, sl], mk_ref[0, :, sl])
        p = jnp.exp(s - jnp.max(s, axis=1, keepdims=True))
        p = p / jnp.sum(p, axis=1, keepdims=True)
        heads.append(_dot(p.astype(BF16), mv_ref[0, :, sl]).astype(BF16))
    x2 = x1 + _dot(jnp.concatenate(heads, axis=1), wo_ref[...])
    x2_ref[...] = x2
    hn = _rms(x2, gf_ref[...])
    hb_ref[...] = hn.astype(BF16)
    h0, h1, h2 = _split3(hn)
    w0, w1, w2 = wr_ref[0], wr_ref[1], wr_ref[2]
    logits = (_dot(h0, w0) + (_dot(h0, w1) + _dot(h1, w0)) + (_dot(h0, w2) + _dot(h1, w1) + _dot(h2, w0))
              + br_ref[...])
    comb_ref[...] = _route(logits)


def _mix_out(x, o_a, o_b, o_c, wo_a, wo_b, wo_c, g_x, wq, mem_k, mem_v, wo, g_f, w_r, b_r, t_len, tm=512):
    n = x.shape[0]
    per_batch = t_len // tm
    m_len = mem_k.shape[1]
    tile = lambda width: pl.BlockSpec((tm, width), lambda i: (i, 0))
    full = lambda shape: pl.BlockSpec(shape, lambda i: tuple(0 for _ in shape))
    mem_spec = pl.BlockSpec((1, m_len, D_MODEL), lambda i: (i // per_batch, 0, 0))
    return pl.pallas_call(
        _mix_out_kernel,
        grid=(n // tm,),
        in_specs=[tile(D_MODEL), tile(FOX_WIDTH), tile(POOL_WIDTH), tile(NSA_WIDTH),
                  full(wo_a.shape), full(wo_b.shape), full(wo_c.shape), full(g_x.shape), full(wq.shape),
                  mem_spec, mem_spec, full(wo.shape), full(g_f.shape), full(w_r.shape), full(b_r.shape)],
        out_specs=[tile(D_MODEL), tile(D_MODEL), tile(ROUTER_WIDTH)],
        out_shape=[jax.ShapeDtypeStruct((n, D_MODEL), F32),
                   jax.ShapeDtypeStruct((n, D_MODEL), BF16),
                   jax.ShapeDtypeStruct((n, ROUTER_WIDTH), F32)],
        compiler_params=_params(("parallel",)),
        name="mix_out",
    )(x, o_a, o_b, o_c, wo_a, wo_b, wo_c, g_x, wq, mem_k, mem_v, wo, g_f, w_r, b_r)


def _moe_kernel(x_ref, hb_ref, comb_ref, wg_ref, wu_ref, wd_ref, gfin_ref, o_ref, acc_ref, *, final_norm):
    e = pl.program_id(1)

    @pl.when(e == 0)
    def _():
        acc_ref[...] = x_ref[...]

    hb = hb_ref[...]
    he = (jax.nn.silu(_dot(hb, wg_ref[0])) * _dot(hb, wu_ref[0])).astype(BF16)
    lane = lax.broadcasted_iota(jnp.int32, comb_ref.shape, 1)
    c_e = jnp.sum(jnp.where(lane == e, comb_ref[...], 0.0), axis=1, keepdims=True)
    acc_ref[...] += c_e * _dot(he, wd_ref[0])

    @pl.when(e == pl.num_programs(1) - 1)
    def _():
        if final_norm:
            o_ref[...] = _rms(acc_ref[...], gfin_ref[...])
        else:
            o_ref[...] = acc_ref[...]


def _moe(x2, hb, comb, wg, wu, wd, g_fin, final_norm, tm=512):
    n = x2.shape[0]
    n_exp = wg.shape[0]
    tile = lambda width: pl.BlockSpec((tm, width), lambda i, e: (i, 0))
    return pl.pallas_call(
        functools.partial(_moe_kernel, final_norm=final_norm),
        grid=(n // tm, n_exp),
        in_specs=[tile(D_MODEL), tile(D_MODEL), tile(ROUTER_WIDTH),
                  pl.BlockSpec((1, D_MODEL, EXPERT_FF), lambda i, e: (e, 0, 0)),
                  pl.BlockSpec((1, D_MODEL, EXPERT_FF), lambda i, e: (e, 0, 0)),
                  pl.BlockSpec((1, EXPERT_FF, D_MODEL), lambda i, e: (e, 0, 0)),
                  pl.BlockSpec((1, D_MODEL), lambda i, e: (0, 0))],
        out_specs=tile(D_MODEL),
        out_shape=jax.ShapeDtypeStruct((n, D_MODEL), F32),
        scratch_shapes=[pltpu.VMEM((tm, D_MODEL), F32)],
        compiler_params=_params(("parallel", "arbitrary")),
        name="moe",
    )(x2, hb, comb, wg, wu, wd, g_fin)


def _split_in_proj_weight(w):
    edges = [0]
    for width in (FOX_WIDTH, FOX_WIDTH, FOX_WIDTH, FOX_HEADS, POOL_WIDTH, NSA_WIDTH) + (KV_WIDTH,) * 6 + (NSA_HEADS * N_BRANCH,):
        edges.append(edges[-1] + width)
    part = lambda k: w[:, edges[k]:edges[k + 1]]
    qa, ka, va, fa, ub, qc = (part(k) for k in range(6))
    kvs = [part(k) for k in range(6, 12)]
    gl = part(12)
    w_big = jnp.concatenate([qa, ka, va, qc] + kvs, axis=1).astype(BF16)
    pad = jnp.zeros((w.shape[0], SMALL_WIDTH - POOL_WIDTH - FOX_HEADS - NSA_HEADS * N_BRANCH), w.dtype)
    w_small = jnp.concatenate([ub, fa, gl, pad], axis=1).astype(BF16)
    return w_big, w_small


def _block_diag(w):
    groups, cin, cout = w.shape
    out = jnp.zeros((groups * cin, groups * cout), w.dtype)
    for gi in range(groups):
        out = out.at[gi * cin:(gi + 1) * cin, gi * cout:(gi + 1) * cout].set(w[gi])
    return out


def _expand_matrix(t_len, tks):
    n_blk = t_len // SEL_BLOCK
    tile = jnp.arange(t_len // tks)[:, None, None]
    blk = jnp.arange(n_blk)[None, :, None]
    pos = jnp.arange(tks)[None, None, :]
    return ((tile * tks + pos) // SEL_BLOCK == blk).astype(BF16)


def _layer(x, mem, p, bsz, t_len, final_g, final_norm, fox_tq=256, nsa_tks=256):
    n = bsz * t_len
    row = lambda v: v.reshape(1, -1)
    w_big, w_small = _split_in_proj_weight(p["w_in"])
    big, small = _in_proj(x, row(p["mix_norm_g"]), w_big, w_small)

    f_t = small[:, COL_FA:COL_FA + FOX_HEADS].reshape(bsz, t_len, FOX_HEADS).transpose(0, 2, 1)
    f_t = jnp.pad(f_t, ((0, 0), (0, 8 - FOX_HEADS), (0, 0)))
    b_col = jnp.pad(p["fox_forget_b"], (0, 8 - FOX_HEADS)).reshape(8, 1)
    cum = _forget_cumsum(f_t, b_col)
    cum_t = cum.transpose(0, 2, 1)
    cum_k = cum.reshape(bsz, 8, t_len // fox_tq, fox_tq).transpose(0, 2, 1, 3)
    o_a = _fox(big, cum_t, cum_k, bsz, t_len, tq=fox_tq)

    o_b = _pool(small, _block_diag(p["pool_w"]).astype(BF16), row(p["pool_scale"]), bsz, t_len)

    def chunks(col):
        a = big[:, col:col + KV_WIDTH].reshape(bsz, t_len // CMP_STRIDE, CMP_STRIDE, NSA_KV_HEADS, HEAD_DIM)
        return a.transpose(0, 3, 1, 2, 4).reshape(bsz * NSA_KV_HEADS, t_len // CMP_STRIDE, CMP_STRIDE * HEAD_DIM)

    pe_rows = lambda pe: jnp.broadcast_to(pe.reshape(1, -1), (8, CMP_LEN * HEAD_DIM)).astype(BF16)
    kc, vc = _compress(chunks(COL_KV), chunks(COL_KV + KV_WIDTH), pe_rows(p["cmp_pe_k"]), pe_rows(p["cmp_pe_v"]),
                       p["cmp_w1_k"].astype(BF16), p["cmp_w2_k"].astype(BF16),
                       p["cmp_w1_v"].astype(BF16), p["cmp_w2_v"].astype(BF16))
    kc = kc.reshape(bsz, NSA_KV_HEADS, t_len // CMP_STRIDE, HEAD_DIM)
    vc = vc.reshape(bsz, NSA_KV_HEADS, t_len // CMP_STRIDE, HEAD_DIM)
    o_c = _nsa(big, kc, vc, small, row(p["nsa_gate_b"]), _expand_matrix(t_len, nsa_tks), bsz, t_len, tks=nsa_tks)

    mem_k, mem_v = _mem_kv(mem, row(p["mem_norm_g"]), p["xattn_wk"].astype(BF16), p["xattn_wv"].astype(BF16))
    w_out = p["w_out"].astype(BF16)
    w_r = jnp.concatenate([p["router_coarse_w"], p["router_fine_w"],
                           jnp.zeros((D_MODEL, ROUTER_WIDTH - N_GROUPS - N_EXPERTS), F32)], axis=1)
    b_r = jnp.concatenate([p["router_coarse_b"], p["router_fine_b"],
                           jnp.zeros((ROUTER_WIDTH - N_GROUPS - N_EXPERTS,), F32)]).reshape(1, -1)
    x2, hb, comb = _mix_out(x, o_a, o_b, o_c, w_out[:FOX_WIDTH], w_out[FOX_WIDTH:FOX_WIDTH + POOL_WIDTH],
                            w_out[FOX_WIDTH + POOL_WIDTH:], row(p["xattn_norm_g"]), p["xattn_wq"].astype(BF16),
                            mem_k, mem_v, p["xattn_wo"].astype(BF16), row(p["ffn_norm_g"]),
                            jnp.stack(_split3(w_r)), b_r, t_len)

    return _moe(x2, hb, comb, p["exp_w_gate"].astype(BF16), p["exp_w_up"].astype(BF16),
                p["exp_w_down"].astype(BF16), row(final_g), final_norm)


def kernel(x, mem, mix_norm_g, w_in, fox_forget_b, pool_w, pool_scale, cmp_pe_k, cmp_pe_v, cmp_w1_k, cmp_w2_k, cmp_w1_v, cmp_w2_v, nsa_gate_b, w_out, xattn_norm_g, mem_norm_g, xattn_wq, xattn_wk, xattn_wv, xattn_wo, ffn_norm_g, router_coarse_w, router_coarse_b, router_fine_w, router_fine_b, exp_w_gate, exp_w_up, exp_w_down, final_norm_g):
    stacked = dict(mix_norm_g=mix_norm_g, w_in=w_in, fox_forget_b=fox_forget_b, pool_w=pool_w, pool_scale=pool_scale,
                   cmp_pe_k=cmp_pe_k, cmp_pe_v=cmp_pe_v, cmp_w1_k=cmp_w1_k, cmp_w2_k=cmp_w2_k, cmp_w1_v=cmp_w1_v,
                   cmp_w2_v=cmp_w2_v, nsa_gate_b=nsa_gate_b, w_out=w_out, xattn_norm_g=xattn_norm_g,
                   mem_norm_g=mem_norm_g, xattn_wq=xattn_wq, xattn_wk=xattn_wk, xattn_wv=xattn_wv, xattn_wo=xattn_wo,
                   ffn_norm_g=ffn_norm_g, router_coarse_w=router_coarse_w, router_coarse_b=router_coarse_b,
                   router_fine_w=router_fine_w, router_fine_b=router_fine_b, exp_w_gate=exp_w_gate,
                   exp_w_up=exp_w_up, exp_w_down=exp_w_down)
    bsz, t_len, d = x.shape
    depth = w_in.shape[0]
    h = x.reshape(bsz * t_len, d)
    for layer in range(depth):
        p = {name: value[layer] for name, value in stacked.items()}
        h = _layer(h, mem, p, bsz, t_len, final_norm_g, final_norm=(layer == depth - 1))
    return h.reshape(bsz, t_len, d)
```

```python
import functools

import jax
import jax.numpy as jnp
from jax import lax
from jax.experimental import pallas as pl
from jax.experimental.pallas import tpu as pltpu

F32 = jnp.float32
BF16 = jnp.bfloat16

D_MODEL = 1024
HEAD_DIM = 64
FOX_HEADS = 6
FOX_WIDTH = FOX_HEADS * HEAD_DIM
POOL_WIDTH = 256
POOL_GROUP_DIM = 64
POOL_WINDOWS = (2, 4, 8, 16)
POOL_HALO = 16
NSA_HEADS = 6
NSA_KV_HEADS = 2
NSA_REP = NSA_HEADS // NSA_KV_HEADS
NSA_WIDTH = NSA_HEADS * HEAD_DIM
KV_WIDTH = NSA_KV_HEADS * HEAD_DIM
N_BRANCH = 3
CMP_LEN = 32
CMP_STRIDE = 16
CMP_HIDDEN = 256
SEL_BLOCK = 64
SEL_TOPN = 16
WINDOW = 512
FORCE_SCORE = 1.0e4
MEM_HEADS = 4
MEM_HEAD_DIM = D_MODEL // MEM_HEADS
N_GROUPS = 4
EXPERTS_PER_GROUP = 4
N_EXPERTS = N_GROUPS * EXPERTS_PER_GROUP
EXPERT_FF = 512
EPS = 1e-6
NEG_INF = -1e30

LANES = 128
SUBLANES = 8
BIG_WIDTH = 3 * FOX_WIDTH + NSA_WIDTH + 6 * KV_WIDTH
COL_QC = 3 * FOX_WIDTH
COL_KV = COL_QC + NSA_WIDTH
SMALL_WIDTH = 384
COL_FA = POOL_WIDTH
COL_GL = POOL_WIDTH + FOX_HEADS
ROUTER_WIDTH = LANES
NSA_PAIR_ORDER = tuple(g * NSA_REP + r for r in range(NSA_REP) for g in range(NSA_KV_HEADS))
COL_POS = HEAD_DIM

VMEM_LIMIT = 56 * 1024 * 1024

_NT = (((1,), (1,)), ((), ()))


def _params(semantics):
    return pltpu.CompilerParams(dimension_semantics=semantics, vmem_limit_bytes=VMEM_LIMIT)


def _rms(x, g):
    return x * lax.rsqrt(jnp.mean(x * x, axis=-1, keepdims=True) + EPS) * g


def _dot(a, b):
    return jnp.dot(a, b, preferred_element_type=F32)


def _dot_nt(a, b):
    return lax.dot_general(a, b, _NT, preferred_element_type=F32)


def _split3(a):
    a0 = a.astype(BF16)
    r = a - a0.astype(F32)
    a1 = r.astype(BF16)
    a2 = (r - a1.astype(F32)).astype(BF16)
    return a0, a1, a2


def _in_proj_kernel(x_ref, g_ref, wb_ref, ws_ref, big_ref, small_ref):
    yb = _rms(x_ref[...], g_ref[...]).astype(BF16)
    big_ref[...] = _dot(yb, wb_ref[...]).astype(BF16)
    small_ref[...] = _dot(yb, ws_ref[...])


def _in_proj(x, g, w_big, w_small, tm=512):
    n = x.shape[0]
    return pl.pallas_call(
        _in_proj_kernel,
        grid=(n // tm,),
        in_specs=[pl.BlockSpec((tm, D_MODEL), lambda i: (i, 0)),
                  pl.BlockSpec((1, D_MODEL), lambda i: (0, 0)),
                  pl.BlockSpec((D_MODEL, BIG_WIDTH), lambda i: (0, 0)),
                  pl.BlockSpec((D_MODEL, SMALL_WIDTH), lambda i: (0, 0))],
        out_specs=[pl.BlockSpec((tm, BIG_WIDTH), lambda i: (i, 0)),
                   pl.BlockSpec((tm, SMALL_WIDTH), lambda i: (i, 0))],
        out_shape=[jax.ShapeDtypeStruct((n, BIG_WIDTH), BF16),
                   jax.ShapeDtypeStruct((n, SMALL_WIDTH), F32)],
        compiler_params=_params(("parallel",)),
        name="in_proj",
    )(x, g, w_big, w_small)


def _forget_kernel(f_ref, b_ref, place_ref, ck_ref):
    c = jax.nn.log_sigmoid(f_ref[...] + b_ref[...])
    t_len = c.shape[0]
    row = lax.broadcasted_iota(jnp.int32, c.shape, 0)
    sh = 1
    while sh < t_len:
        c = c + jnp.where(row >= sh, pltpu.roll(c, sh, axis=0), 0.0)
        sh *= 2
    n0, n1, n2 = _split3(-c)
    ck_ref[...] = (_dot(n0, place_ref[0]) + _dot(n1, place_ref[1]) + _dot(n2, place_ref[2])).astype(BF16)


def _forget_cumsum(small, b_row, place, bsz, t_len):
    return pl.pallas_call(
        _forget_kernel,
        grid=(bsz,),
        in_specs=[pl.BlockSpec((t_len, LANES), lambda b: (b, COL_FA // LANES)),
                  pl.BlockSpec((1, LANES), lambda b: (0, 0)),
                  pl.BlockSpec(place.shape, lambda b: (0, 0, 0))],
        out_specs=pl.BlockSpec((t_len, FOX_WIDTH), lambda b: (b, 0)),
        out_shape=jax.ShapeDtypeStruct((bsz * t_len, FOX_WIDTH), BF16),
        compiler_params=_params(("parallel",)),
        name="forget",
    )(small, b_row, place)


def _forget_placement():
    src = jnp.arange(LANES)[None, :, None]
    dst = jnp.arange(FOX_WIDTH)[None, None, :]
    piece = jnp.arange(3)[:, None, None]
    target = LANES * (src // 2) + 3 * (src % 2) + piece
    return ((src < FOX_HEADS) & (dst == target)).astype(BF16)


def _fox_kernel(q_ref, k_ref, v_ref, ck_ref, o_ref, q_scr, m_scr, acc_scr, *, tq, tk):
    i = pl.program_id(1)
    t0 = i * tq
    n_pair = FOX_HEADS // 2
    lane = lax.broadcasted_iota(jnp.int32, (tq, LANES), 1)
    first = lane < HEAD_DIM
    scale = jnp.asarray(HEAD_DIM ** -0.5, BF16)
    pick_a = jnp.where(lane < 3, 1.0, 0.0).astype(BF16)
    pick_b = jnp.where((lane >= 3) & (lane < 6), 1.0, 0.0).astype(BF16)
    for p in range(n_pair):
        qp = q_ref[:, p * LANES:(p + 1) * LANES] * scale
        zero = jnp.zeros_like(qp)
        qa = jnp.concatenate([jnp.where(first, qp, zero), pick_a], axis=1)
        qb = jnp.concatenate([jnp.where(first, zero, qp), pick_b], axis=1)
        q_scr[p] = jnp.concatenate([qa, qb], axis=0)
    m_scr[...] = jnp.full(m_scr.shape, NEG_INF, F32)
    acc_scr[...] = jnp.zeros(acc_scr.shape, F32)
    ones_v = jnp.ones((tk, LANES), BF16)

    def tile(j, diagonal):
        ks = pl.multiple_of(j * tk, tk)
        if diagonal:
            t_row = t0 + lax.broadcasted_iota(jnp.int32, (tq, tk), 0)
            pos = ks + lax.broadcasted_iota(jnp.int32, (tq, tk), 1)
            causal = (pos <= t_row)[None]
        for p in range(n_pair):
            psl = slice(p * LANES, (p + 1) * LANES)
            kx = jnp.concatenate([k_ref[pl.ds(ks, tk), psl], ck_ref[pl.ds(ks, tk), psl]], axis=1)
            vx = jnp.concatenate([v_ref[pl.ds(ks, tk), psl], ones_v], axis=1)
            s = _dot_nt(q_scr[p], kx)
            if diagonal:
                s = jnp.where(causal, s.reshape(2, tq, tk), NEG_INF).reshape(2 * tq, tk)
            m_old = m_scr[p]
            m_new = jnp.maximum(m_old, jnp.max(s, axis=1, keepdims=True))
            prob = jnp.exp(s - m_new)
            acc_scr[p] = jnp.exp(m_old - m_new) * acc_scr[p] + _dot(prob.astype(BF16), vx)
            m_scr[p] = m_new

    def full_tile(j, carry):
        tile(j, False)
        return carry

    n_full = t0 // tk
    lax.fori_loop(0, n_full, full_tile, 0)
    tile(n_full, True)
    for p in range(n_pair):
        acc = acc_scr[p]
        oa = acc[:tq, :LANES] / acc[:tq, LANES:]
        ob = acc[tq:, :LANES] / acc[tq:, LANES:]
        o_ref[:, p * LANES:(p + 1) * LANES] = jnp.where(first, oa, ob).astype(BF16)


def _fox(big, ck, bsz, t_len, tq=256, tk=512):
    nq = t_len // tq
    n_pair = FOX_HEADS // 2
    seq = lambda c: pl.BlockSpec((t_len, FOX_WIDTH), lambda b, i: (b, c))
    return pl.pallas_call(
        functools.partial(_fox_kernel, tq=tq, tk=tk),
        grid=(bsz, nq),
        in_specs=[pl.BlockSpec((tq, FOX_WIDTH), lambda b, i: (b * nq + i, 0)), seq(1), seq(2), seq(0)],
        out_specs=pl.BlockSpec((tq, FOX_WIDTH), lambda b, i: (b * nq + i, 0)),
        out_shape=jax.ShapeDtypeStruct((bsz * t_len, FOX_WIDTH), BF16),
        scratch_shapes=[pltpu.VMEM((n_pair, 2 * tq, 2 * LANES), BF16),
                        pltpu.VMEM((n_pair, 2 * tq, 1), F32),
                        pltpu.VMEM((n_pair, 2 * tq, 2 * LANES), F32)],
        compiler_params=_params(("parallel", "arbitrary")),
        name="fox",
    )(big, big, big, ck)


def _pool_kernel(u_ref, halo_ref, w_ref, scale_ref, o_ref, x_scr, *, tt):
    i = pl.program_id(1)
    u = u_ref[...]
    x_scr[POOL_HALO:, :] = u
    x_scr[:POOL_HALO, :] = jnp.where(i > 0, halo_ref[...], 0.0)
    t_pos = i * tt + lax.broadcasted_iota(jnp.int32, (tt, 1), 0)
    lane = lax.broadcasted_iota(jnp.int32, (tt, POOL_WIDTH), 1)
    run = u
    pooled = jnp.zeros_like(u)
    for back in range(1, max(POOL_WINDOWS)):
        run = run + x_scr[POOL_HALO - back:POOL_HALO - back + tt, :]
        w = back + 1
        if w in POOL_WINDOWS:
            gi = POOL_WINDOWS.index(w)
            count = jnp.minimum(t_pos + 1, w).astype(F32)
            in_group = (lane >= gi * POOL_GROUP_DIM) & (lane < (gi + 1) * POOL_GROUP_DIM)
            pooled = jnp.where(in_group, run / count - u, pooled)
    mixed = _dot(pooled.astype(BF16), w_ref[...])
    o_ref[...] = (mixed * scale_ref[...]).astype(BF16)


def _pool(small, w_bd, scale, bsz, t_len, tt=512):
    nt = t_len // tt
    per = tt // POOL_HALO
    return pl.pallas_call(
        functools.partial(_pool_kernel, tt=tt),
        grid=(bsz, nt),
        in_specs=[pl.BlockSpec((tt, POOL_WIDTH), lambda b, i: (b * nt + i, 0)),
                  pl.BlockSpec((POOL_HALO, POOL_WIDTH), lambda b, i: (jnp.maximum((b * nt + i) * per - 1, 0), 0)),
                  pl.BlockSpec((POOL_WIDTH, POOL_WIDTH), lambda b, i: (0, 0)),
                  pl.BlockSpec((1, POOL_WIDTH), lambda b, i: (0, 0))],
        out_specs=pl.BlockSpec((tt, POOL_WIDTH), lambda b, i: (b * nt + i, 0)),
        out_shape=jax.ShapeDtypeStruct((bsz * t_len, POOL_WIDTH), BF16),
        scratch_shapes=[pltpu.VMEM((tt + POOL_HALO, POOL_WIDTH), F32)],
        compiler_params=_params(("parallel", "arbitrary")),
        name="pool",
    )(small, small, w_bd, scale)


def _compress_kernel(xk_ref, xv_ref, pek_ref, pev_ref, w1k_ref, w2k_ref, w1v_ref, w2v_ref, kc_ref, vc_ref):
    half = CMP_STRIDE * HEAD_DIM

    def one(x_ref, pe_ref, w1_ref, w2_ref, out_ref):
        x = x_ref[0]
        n_chunk = x.shape[0]
        first = _dot(x, w1_ref[:half, :])
        second = _dot(x, w1_ref[half:, :])
        pe_term = _dot(pe_ref[...], w1_ref[...])[0:1, :]
        hidden = first + pltpu.roll(second, n_chunk - 1, axis=0) + pe_term
        out_ref[0] = _dot(jax.nn.gelu(hidden).astype(BF16), w2_ref[...])

    one(xk_ref, pek_ref, w1k_ref, w2k_ref, kc_ref)
    one(xv_ref, pev_ref, w1v_ref, w2v_ref, vc_ref)


def _compress(xk, xv, pe_k, pe_v, w1_k, w2_k, w1_v, w2_v):
    nbg, n_chunk, width = xk.shape
    full = lambda shape: pl.BlockSpec(shape, lambda b: tuple(0 for _ in shape))
    x_spec = pl.BlockSpec((1, n_chunk, width), lambda b: (b, 0, 0))
    o_spec = pl.BlockSpec((1, n_chunk, HEAD_DIM), lambda b: (b, 0, 0))
    return pl.pallas_call(
        _compress_kernel,
        grid=(nbg,),
        in_specs=[x_spec, x_spec, full(pe_k.shape), full(pe_v.shape),
                  full(w1_k.shape), full(w2_k.shape), full(w1_v.shape), full(w2_v.shape)],
        out_specs=[o_spec, o_spec],
        out_shape=[jax.ShapeDtypeStruct((nbg, n_chunk, HEAD_DIM), F32)] * 2,
        compiler_params=_params(("parallel",)),
        name="compress",
    )(xk, xv, pe_k, pe_v, w1_k, w2_k, w1_v, w2_v)


def _nsa_kernel(q_ref, ksl_ref, vsl_ref, kwn_ref, vwn_ref, kpos_ref, kc_ref, vc_ref, cpos_ref, ovt_ref, slope_ref,
                sm_ref, gb_ref, o_ref, qs_scr, m_scr, acc_scr, *, tq, tks):
    i = pl.program_id(1)
    t0 = i * tq
    n_head = NSA_HEADS
    rows = n_head * tq
    n_cmp = kc_ref.shape[1]
    n_blk = ovt_ref.shape[0]
    win_keys = WINDOW + tq
    lane = lax.broadcasted_iota(jnp.int32, (tq, LANES), 1)
    first = lane < HEAD_DIM
    scale = jnp.asarray(HEAD_DIM ** -0.5, BF16)

    pairs = [q_ref[:, r * LANES:(r + 1) * LANES] * scale for r in range(NSA_REP)]
    zero = jnp.zeros_like(pairs[0])
    q_heads, q_slopes = [], []
    for g in range(NSA_KV_HEADS):
        for r in range(NSA_REP):
            q_heads.append(jnp.where(first, pairs[r], zero) if g == 0 else jnp.where(first, zero, pairs[r]))
            q_slopes.append(jnp.broadcast_to(slope_ref[g * NSA_REP + r:g * NSA_REP + r + 1, :], (tq, LANES)))
    q_main = jnp.concatenate(q_heads, axis=0)
    q_plain = jnp.concatenate([q_main, jnp.concatenate(q_slopes, axis=0).astype(BF16)], axis=1)

    def per_head(mask, s, fill):
        return jnp.where(mask[None], s.reshape(n_head, tq, s.shape[-1]), fill).reshape(s.shape)

    t_col = t0 + lax.broadcasted_iota(jnp.int32, (tq, n_cmp), 0)
    c_end = lax.broadcasted_iota(jnp.int32, (tq, n_cmp), 1) * CMP_STRIDE + (CMP_LEN - 1)
    valid_c = c_end <= t_col
    kc0, kc1, _ = _split3(kc_ref[0])
    s = _dot_nt(q_plain, jnp.concatenate([kc0, cpos_ref[...]], axis=1)) + _dot_nt(q_main, kc1)
    s = per_head(valid_c, s, NEG_INF)
    p_c = per_head(valid_c, jnp.exp(s - jnp.max(s, axis=1, keepdims=True)), 0.0)
    l_c = jnp.sum(p_c, axis=1, keepdims=True)
    p_c = p_c * jnp.where(l_c > 0.0, 1.0 / l_c, 0.0)
    o_cmp = _dot(p_c.astype(BF16), vc_ref[0].astype(BF16))

    blk_t = lax.broadcasted_iota(jnp.int32, (n_blk, tq), 0)
    cur_t = (t0 + lax.broadcasted_iota(jnp.int32, (n_blk, tq), 1)) // SEL_BLOCK
    forced_t = (blk_t == 0) | (blk_t == cur_t) | (blk_t == cur_t - 1)
    sub = lax.broadcasted_iota(jnp.int32, (SUBLANES, tq), 0)
    n_top = float(min(SEL_TOPN, n_blk))
    for g in range(NSA_KV_HEADS):
        p_sum = p_c[g * NSA_REP * tq:(g * NSA_REP + 1) * tq]
        for r in range(1, NSA_REP):
            p_sum = p_sum + p_c[(g * NSA_REP + r) * tq:(g * NSA_REP + r + 1) * tq]
        p0, p1, p2 = _split3(p_sum)
        ovt = ovt_ref[...]
        imp_t = _dot_nt(ovt, p0) + _dot_nt(ovt, p1) + _dot_nt(ovt, p2)
        score = jnp.where(forced_t, FORCE_SCORE, jnp.where(blk_t <= cur_t, imp_t, -1.0))
        chunks = [score[c * SUBLANES:(c + 1) * SUBLANES] for c in range(n_blk // SUBLANES)]
        ranks = [jnp.zeros((SUBLANES, tq), F32) for _ in chunks]
        for jp in range(n_blk):
            other = jnp.broadcast_to(score[jp:jp + 1], (SUBLANES, tq))
            for c, chunk in enumerate(chunks):
                ge = jnp.where(other >= chunk, 1.0, 0.0)
                gt = jnp.where(other > chunk, 1.0, 0.0)
                if jp < c * SUBLANES:
                    ranks[c] = ranks[c] + ge
                elif jp >= (c + 1) * SUBLANES:
                    ranks[c] = ranks[c] + gt
                else:
                    ranks[c] = ranks[c] + jnp.where(sub > jp - c * SUBLANES, ge, gt)
        rank = jnp.concatenate(ranks, axis=0)
        neg_t = jnp.where(rank < n_top, 0.0, NEG_INF)
        neg = jnp.transpose(jnp.concatenate([neg_t, jnp.zeros((LANES - n_blk, tq), F32)], axis=0))
        for r in range(NSA_REP):
            k = g * NSA_REP + r
            q_const = (neg + q_slopes[k]).astype(BF16)
            qs_scr[k * tq:(k + 1) * tq, :] = jnp.concatenate([q_heads[k], q_const], axis=1)

    m_scr[...] = jnp.full(m_scr.shape, NEG_INF, F32)
    acc_scr[...] = jnp.zeros(acc_scr.shape, F32)
    ones_s = jnp.ones((tks, LANES), BF16)
    n_part = NSA_REP
    part_rows = rows // n_part

    def slc_tile(j, diagonal):
        ks = pl.multiple_of(j * tks, tks)
        kx = jnp.concatenate([ksl_ref[pl.ds(ks, tks), :], kpos_ref[pl.ds(ks, tks), :]], axis=1)
        vx = jnp.concatenate([vsl_ref[pl.ds(ks, tks), :], ones_s], axis=1)
        if diagonal:
            t_row = t0 + lax.broadcasted_iota(jnp.int32, (tq, tks), 0)
            pos = ks + lax.broadcasted_iota(jnp.int32, (tq, tks), 1)
            causal = (pos <= t_row)[None]
        for part in range(n_part):
            rsl = slice(part * part_rows, (part + 1) * part_rows)
            s = _dot_nt(qs_scr[rsl, :], kx)
            if diagonal:
                s = jnp.where(causal, s.reshape(part_rows // tq, tq, tks), NEG_INF).reshape(part_rows, tks)
            m_old = m_scr[rsl, :]
            m_new = jnp.maximum(m_old, jnp.max(s, axis=1, keepdims=True))
            prob = jnp.exp(s - m_new)
            acc_scr[rsl, :] = jnp.exp(m_old - m_new) * acc_scr[rsl, :] + _dot(prob.astype(BF16), vx)
            m_scr[rsl, :] = m_new

    def slc_full(j, carry):
        slc_tile(j, False)
        return carry

    n_full = t0 // tks
    lax.fori_loop(0, n_full, slc_full, 0)
    slc_tile(n_full, True)
    acc = acc_scr[...]
    o_slc = acc[:, :LANES] / acc[:, LANES:]

    start = pl.multiple_of(jnp.maximum(t0 - WINDOW, 0), tq)
    kx = jnp.concatenate([kwn_ref[pl.ds(start, win_keys), :], kpos_ref[pl.ds(start, win_keys), :]], axis=1)
    vx = jnp.concatenate([vwn_ref[pl.ds(start, win_keys), :], jnp.ones((win_keys, LANES), BF16)], axis=1)
    dist = (t0 + lax.broadcasted_iota(jnp.int32, (tq, win_keys), 0)
            - (start + lax.broadcasted_iota(jnp.int32, (tq, win_keys), 1)))
    s = per_head((dist >= 0) & (dist < WINDOW), _dot_nt(q_plain, kx), NEG_INF)
    prob = jnp.exp(s - jnp.max(s, axis=1, keepdims=True))
    acc = _dot(prob.astype(BF16), vx)
    o_win = acc[:, :LANES] / acc[:, LANES:]

    gates = jax.nn.sigmoid(sm_ref[:, COL_GL:COL_GL + NSA_HEADS * N_BRANCH] + gb_ref[...])
    for r in range(NSA_REP):
        halves = []
        for g in range(NSA_KV_HEADS):
            k = g * NSA_REP + r
            rsl = slice(k * tq, (k + 1) * tq)
            gate = lambda br: gates[:, k * N_BRANCH + br:k * N_BRANCH + br + 1]
            halves.append(gate(0) * o_cmp[rsl] + gate(1) * o_slc[rsl] + gate(2) * o_win[rsl])
        o_ref[:, r * LANES:(r + 1) * LANES] = jnp.where(first, halves[0], halves[1]).astype(BF16)


def _nsa(big, kpos, kc, vc, cpos, ovt, slope_tab, small, gate_b, bsz, t_len, tq=128, tks=512):
    nq = t_len // tq
    kv_col = COL_KV // KV_WIDTH
    kv_spec = lambda c: pl.BlockSpec((t_len, KV_WIDTH), lambda b, i: (b, kv_col + c))
    full = lambda a: pl.BlockSpec(a.shape, lambda b, i: tuple(0 for _ in a.shape))
    n_cmp = kc.shape[1]
    rows = NSA_HEADS * tq
    return pl.pallas_call(
        functools.partial(_nsa_kernel, tq=tq, tks=tks),
        grid=(bsz, nq),
        in_specs=[pl.BlockSpec((tq, NSA_WIDTH), lambda b, i: (b * nq + i, COL_QC // NSA_WIDTH)),
                  kv_spec(2), kv_spec(3), kv_spec(4), kv_spec(5), full(kpos),
                  pl.BlockSpec((1, n_cmp, KV_WIDTH), lambda b, i: (b, 0, 0)),
                  pl.BlockSpec((1, n_cmp, KV_WIDTH), lambda b, i: (b, 0, 0)),
                  full(cpos), full(ovt), full(slope_tab),
                  pl.BlockSpec((tq, SMALL_WIDTH), lambda b, i: (b * nq + i, 0)),
                  full(gate_b)],
        out_specs=pl.BlockSpec((tq, NSA_WIDTH), lambda b, i: (b * nq + i, 0)),
        out_shape=jax.ShapeDtypeStruct((bsz * t_len, NSA_WIDTH), BF16),
        scratch_shapes=[pltpu.VMEM((rows, 2 * LANES), BF16),
                        pltpu.VMEM((rows, 1), F32),
                        pltpu.VMEM((rows, 2 * LANES), F32)],
        compiler_params=_params(("parallel", "arbitrary")),
        name="nsa",
    )(big, big, big, big, big, kpos, kc, vc, cpos, ovt, slope_tab, small, gate_b)


def _position_columns(pos):
    lane = jnp.arange(LANES)[None, :]
    hi = (SEL_BLOCK * (pos // SEL_BLOCK))[:, None]
    lo = (pos % SEL_BLOCK)[:, None]
    cols = jnp.where((lane >= COL_POS) & (lane < COL_POS + 3), hi, 0) + jnp.where((lane >= COL_POS + 3) & (lane < COL_POS + 6), lo, 0)
    return cols.astype(BF16)


def _nsa_tables(t_len):
    pos = jnp.arange(t_len)
    lane = jnp.arange(LANES)[None, :]
    block_one_hot = ((lane < t_len // SEL_BLOCK) & (lane == (pos // SEL_BLOCK)[:, None])).astype(BF16)
    kpos = _position_columns(pos) + block_one_hot
    n_cmp = t_len // CMP_STRIDE
    c_start = jnp.arange(n_cmp) * CMP_STRIDE
    cpos = _position_columns(c_start + CMP_LEN - 1)
    s_start = (jnp.arange(t_len // SEL_BLOCK) * SEL_BLOCK)[:, None]
    ovt = ((c_start[None, :] < s_start + SEL_BLOCK) & (c_start[None, :] + CMP_LEN - 1 >= s_start)).astype(BF16)
    slopes = jnp.exp2(-8.0 * jnp.arange(1, NSA_HEADS + 1, dtype=F32) / NSA_HEADS)
    pieces = jnp.stack([piece.astype(F32) for piece in _split3(slopes)], axis=1)
    slope_tab = jnp.zeros((NSA_HEADS, LANES), F32)
    slope_tab = slope_tab.at[:, COL_POS:COL_POS + 3].set(pieces).at[:, COL_POS + 3:COL_POS + 6].set(pieces)
    return kpos, cpos, ovt, slope_tab


def _mem_kv_kernel(mem_ref, g_ref, wk_ref, wv_ref, k_ref, v_ref):
    mb = _rms(mem_ref[0], g_ref[...]).astype(BF16)
    k_ref[0] = _dot(mb, wk_ref[...]).astype(BF16)
    v_ref[0] = _dot(mb, wv_ref[...]).astype(BF16)


def _mem_kv(mem, g, wk, wv):
    bsz, m_len, _ = mem.shape
    w_spec = pl.BlockSpec((D_MODEL, D_MODEL), lambda b: (0, 0))
    m_spec = pl.BlockSpec((1, m_len, D_MODEL), lambda b: (b, 0, 0))
    return pl.pallas_call(
        _mem_kv_kernel,
        grid=(bsz,),
        in_specs=[m_spec, pl.BlockSpec((1, D_MODEL), lambda b: (0, 0)), w_spec, w_spec],
        out_specs=[m_spec, m_spec],
        out_shape=[jax.ShapeDtypeStruct((bsz, m_len, D_MODEL), BF16)] * 2,
        compiler_params=_params(("parallel",)),
        name="mem_kv",
    )(mem, g, wk, wv)


def _route(logits):
    lane = lax.broadcasted_iota(jnp.int32, logits.shape, 1)
    big_i = jnp.int32(1 << 20)
    is_coarse = lane < N_GROUPS
    lc = jnp.where(is_coarse, logits, NEG_INF)
    mc = jnp.max(lc, axis=1, keepdims=True)
    grp = jnp.min(jnp.where(is_coarse & (lc == mc), lane, big_i), axis=1, keepdims=True)
    w_grp = 1.0 / jnp.sum(jnp.where(is_coarse, jnp.exp(lc - mc), 0.0), axis=1, keepdims=True)
    e_id = lane - N_GROUPS
    in_grp = (e_id >= grp * EXPERTS_PER_GROUP) & (e_id < (grp + 1) * EXPERTS_PER_GROUP)
    lf = jnp.where(in_grp, logits, NEG_INF)
    m1 = jnp.max(lf, axis=1, keepdims=True)
    i1 = jnp.min(jnp.where(in_grp & (lf == m1), lane, big_i), axis=1, keepdims=True)
    lf2 = jnp.where(lane == i1, NEG_INF, lf)
    m2 = jnp.max(lf2, axis=1, keepdims=True)
    i2 = jnp.min(jnp.where(in_grp & (lane != i1) & (lf2 == m2), lane, big_i), axis=1, keepdims=True)
    denom = jnp.sum(jnp.where(in_grp, jnp.exp(lf - m1), 0.0), axis=1, keepdims=True)
    p1 = 1.0 / denom
    p2 = jnp.exp(m2 - m1) / denom
    tot = p1 + p2
    w1 = w_grp * (p1 / tot)
    w2 = w_grp * (p2 / tot)
    return jnp.where(lane + N_GROUPS == i1, w1, jnp.where(lane + N_GROUPS == i2, w2, 0.0))


def _mix_out_kernel(x_ref, oa_ref, ob_ref, oc_ref, woa_ref, wob_ref, woc_ref, gx_ref, wq_ref, mk_ref, mv_ref,
                    wo_ref, gf_ref, wr_ref, br_ref, x2_ref, hb_ref, comb_ref):
    x1 = (x_ref[...] + _dot(oa_ref[...], woa_ref[...]) + _dot(ob_ref[...], wob_ref[...])
          + _dot(oc_ref[...], woc_ref[...]))
    hq = _rms(x1, gx_ref[...]).astype(BF16)
    q = (_dot(hq, wq_ref[...]) * (MEM_HEAD_DIM ** -0.5)).astype(BF16)
    heads = []
    for h in range(MEM_HEADS):
        sl = slice(h * MEM_HEAD_DIM, (h + 1) * MEM_HEAD_DIM)
        s = _dot_nt(q[:, sl], mk_ref[0, :, sl])
        p = jnp.exp(s - jnp.max(s, axis=1, keepdims=True))
        p = p / jnp.sum(p, axis=1, keepdims=True)
        heads.append(_dot(p.astype(BF16), mv_ref[0, :, sl]).astype(BF16))
    x2 = x1 + _dot(jnp.concatenate(heads, axis=1), wo_ref[...])
    x2_ref[...] = x2
    hn = _rms(x2, gf_ref[...])
    hb_ref[...] = hn.astype(BF16)
    h0, h1, h2 = _split3(hn)
    w0, w1, w2 = wr_ref[0], wr_ref[1], wr_ref[2]
    logits = (_dot(h0, w0) + (_dot(h0, w1) + _dot(h1, w0)) + (_dot(h0, w2) + _dot(h1, w1) + _dot(h2, w0))
              + br_ref[...])
    comb_ref[...] = _route(logits)


def _mix_out(x, o_a, o_b, o_c, wo_a, wo_b, wo_c, g_x, wq, mem_k, mem_v, wo, g_f, w_r, b_r, t_len, tm=512):
    n = x.shape[0]
    per_batch = t_len // tm
    m_len = mem_k.shape[1]
    tile = lambda width: pl.BlockSpec((tm, width), lambda i: (i, 0))
    full = lambda shape: pl.BlockSpec(shape, lambda i: tuple(0 for _ in shape))
    mem_spec = pl.BlockSpec((1, m_len, D_MODEL), lambda i: (i // per_batch, 0, 0))
    return pl.pallas_call(
        _mix_out_kernel,
        grid=(n // tm,),
        in_specs=[tile(D_MODEL), tile(FOX_WIDTH), tile(POOL_WIDTH), tile(NSA_WIDTH),
                  full(wo_a.shape), full(wo_b.shape), full(wo_c.shape), full(g_x.shape), full(wq.shape),
                  mem_spec, mem_spec, full(wo.shape), full(g_f.shape), full(w_r.shape), full(b_r.shape)],
        out_specs=[tile(D_MODEL), tile(D_MODEL), tile(ROUTER_WIDTH)],
        out_shape=[jax.ShapeDtypeStruct((n, D_MODEL), F32),
                   jax.ShapeDtypeStruct((n, D_MODEL), BF16),
                   jax.ShapeDtypeStruct((n, ROUTER_WIDTH), F32)],
        compiler_params=_params(("parallel",)),
        name="mix_out",
    )(x, o_a, o_b, o_c, wo_a, wo_b, wo_c, g_x, wq, mem_k, mem_v, wo, g_f, w_r, b_r)


def _moe_kernel(x_ref, hb_ref, comb_ref, wg_ref, wu_ref, wd_ref, gfin_ref, o_ref, acc_ref, *, final_norm):
    e = pl.program_id(1)

    @pl.when(e == 0)
    def _():
        acc_ref[...] = x_ref[...]

    hb = hb_ref[...]
    he = (jax.nn.silu(_dot(hb, wg_ref[0])) * _dot(hb, wu_ref[0])).astype(BF16)
    lane = lax.broadcasted_iota(jnp.int32, comb_ref.shape, 1)
    c_e = jnp.sum(jnp.where(lane == e, comb_ref[...], 0.0), axis=1, keepdims=True)
    acc_ref[...] += c_e * _dot(he, wd_ref[0])

    @pl.when(e == pl.num_programs(1) - 1)
    def _():
        if final_norm:
            o_ref[...] = _rms(acc_ref[...], gfin_ref[...])
        else:
            o_ref[...] = acc_ref[...]


def _moe(x2, hb, comb, wg, wu, wd, g_fin, final_norm, tm=512):
    n = x2.shape[0]
    n_exp = wg.shape[0]
    tile = lambda width: pl.BlockSpec((tm, width), lambda i, e: (i, 0))
    return pl.pallas_call(
        functools.partial(_moe_kernel, final_norm=final_norm),
        grid=(n // tm, n_exp),
        in_specs=[tile(D_MODEL), tile(D_MODEL), tile(ROUTER_WIDTH),
                  pl.BlockSpec((1, D_MODEL, EXPERT_FF), lambda i, e: (e, 0, 0)),
                  pl.BlockSpec((1, D_MODEL, EXPERT_FF), lambda i, e: (e, 0, 0)),
                  pl.BlockSpec((1, EXPERT_FF, D_MODEL), lambda i, e: (e, 0, 0)),
                  pl.BlockSpec((1, D_MODEL), lambda i, e: (0, 0))],
        out_specs=tile(D_MODEL),
        out_shape=jax.ShapeDtypeStruct((n, D_MODEL), F32),
        scratch_shapes=[pltpu.VMEM((tm, D_MODEL), F32)],
        compiler_params=_params(("parallel", "arbitrary")),
        name="moe",
    )(x2, hb, comb, wg, wu, wd, g_fin)


def _pair_order(a, axis):
    shape = a.shape
    a = a.reshape(shape[:axis] + (NSA_HEADS, HEAD_DIM) + shape[axis + 1:])
    a = jnp.take(a, jnp.array(NSA_PAIR_ORDER), axis=axis)
    return a.reshape(shape)


def _split_in_proj_weight(w):
    edges = [0]
    for width in (FOX_WIDTH, FOX_WIDTH, FOX_WIDTH, FOX_HEADS, POOL_WIDTH, NSA_WIDTH) + (KV_WIDTH,) * 6 + (NSA_HEADS * N_BRANCH,):
        edges.append(edges[-1] + width)
    part = lambda k: w[:, edges[k]:edges[k + 1]]
    qa, ka, va, fa, ub, qc = (part(k) for k in range(6))
    kvs = [part(k) for k in range(6, 12)]
    gl = part(12)
    w_big = jnp.concatenate([qa, ka, va, _pair_order(qc, 1)] + kvs, axis=1).astype(BF16)
    pad = jnp.zeros((w.shape[0], SMALL_WIDTH - POOL_WIDTH - FOX_HEADS - NSA_HEADS * N_BRANCH), w.dtype)
    w_small = jnp.concatenate([ub, fa, gl, pad], axis=1).astype(BF16)
    return w_big, w_small


def _block_diag(w):
    groups, cin, cout = w.shape
    out = jnp.zeros((groups * cin, groups * cout), w.dtype)
    for gi in range(groups):
        out = out.at[gi * cin:(gi + 1) * cin, gi * cout:(gi + 1) * cout].set(w[gi])
    return out


def _layer(x, mem, p, bsz, t_len, final_g, final_norm):
    row = lambda v: v.reshape(1, -1)
    w_big, w_small = _split_in_proj_weight(p["w_in"])
    big, small = _in_proj(x, row(p["mix_norm_g"]), w_big, w_small)

    b_row = jnp.pad(p["fox_forget_b"], (0, LANES - FOX_HEADS)).reshape(1, LANES)
    ck = _forget_cumsum(small, b_row, _forget_placement(), bsz, t_len)
    o_a = _fox(big, ck, bsz, t_len)

    o_b = _pool(small, _block_diag(p["pool_w"]).astype(BF16), row(p["pool_scale"]), bsz, t_len)

    n_chunk = t_len // CMP_STRIDE

    def chunks(col):
        a = big[:, col:col + KV_WIDTH].reshape(bsz, n_chunk, CMP_STRIDE, NSA_KV_HEADS, HEAD_DIM)
        return a.transpose(0, 3, 1, 2, 4).reshape(bsz * NSA_KV_HEADS, n_chunk, CMP_STRIDE * HEAD_DIM)

    pe_rows = lambda pe: jnp.broadcast_to(pe.reshape(1, -1), (SUBLANES, CMP_LEN * HEAD_DIM)).astype(BF16)
    kc, vc = _compress(chunks(COL_KV), chunks(COL_KV + KV_WIDTH), pe_rows(p["cmp_pe_k"]), pe_rows(p["cmp_pe_v"]),
                       p["cmp_w1_k"].astype(BF16), p["cmp_w2_k"].astype(BF16),
                       p["cmp_w1_v"].astype(BF16), p["cmp_w2_v"].astype(BF16))
    group_lanes = lambda a: a.reshape(bsz, NSA_KV_HEADS, n_chunk, HEAD_DIM).transpose(0, 2, 1, 3).reshape(bsz, n_chunk, KV_WIDTH)
    kpos, cpos, ovt, slope_tab = _nsa_tables(t_len)
    o_c = _nsa(big, kpos, group_lanes(kc), group_lanes(vc), cpos, ovt, slope_tab, small, row(p["nsa_gate_b"]),
               bsz, t_len)

    mem_k, mem_v = _mem_kv(mem, row(p["mem_norm_g"]), p["xattn_wk"].astype(BF16), p["xattn_wv"].astype(BF16))
    w_out = p["w_out"]
    wo_a = w_out[:FOX_WIDTH].astype(BF16)
    wo_b = w_out[FOX_WIDTH:FOX_WIDTH + POOL_WIDTH].astype(BF16)
    wo_c = _pair_order(w_out[FOX_WIDTH + POOL_WIDTH:], 0).astype(BF16)
    w_r = jnp.concatenate([p["router_coarse_w"], p["router_fine_w"],
                           jnp.zeros((D_MODEL, ROUTER_WIDTH - N_GROUPS - N_EXPERTS), F32)], axis=1)
    b_r = jnp.concatenate([p["router_coarse_b"], p["router_fine_b"],
                           jnp.zeros((ROUTER_WIDTH - N_GROUPS - N_EXPERTS,), F32)]).reshape(1, -1)
    x2, hb, comb = _mix_out(x, o_a, o_b, o_c, wo_a, wo_b, wo_c, row(p["xattn_norm_g"]), p["xattn_wq"].astype(BF16),
                            mem_k, mem_v, p["xattn_wo"].astype(BF16), row(p["ffn_norm_g"]),
                            jnp.stack(_split3(w_r)), b_r, t_len)

    return _moe(x2, hb, comb, p["exp_w_gate"].astype(BF16), p["exp_w_up"].astype(BF16),
                p["exp_w_down"].astype(BF16), row(final_g), final_norm)


def kernel(x, mem, mix_norm_g, w_in, fox_forget_b, pool_w, pool_scale, cmp_pe_k, cmp_pe_v, cmp_w1_k, cmp_w2_k, cmp_w1_v, cmp_w2_v, nsa_gate_b, w_out, xattn_norm_g, mem_norm_g, xattn_wq, xattn_wk, xattn_wv, xattn_wo, ffn_norm_g, router_coarse_w, router_coarse_b, router_fine_w, router_fine_b, exp_w_gate, exp_w_up, exp_w_down, final_norm_g):
    stacked = dict(mix_norm_g=mix_norm_g, w_in=w_in, fox_forget_b=fox_forget_b, pool_w=pool_w, pool_scale=pool_scale,
                   cmp_pe_k=cmp_pe_k, cmp_pe_v=cmp_pe_v, cmp_w1_k=cmp_w1_k, cmp_w2_k=cmp_w2_k, cmp_w1_v=cmp_w1_v,
                   cmp_w2_v=cmp_w2_v, nsa_gate_b=nsa_gate_b, w_out=w_out, xattn_norm_g=xattn_norm_g,
                   mem_norm_g=mem_norm_g, xattn_wq=xattn_wq, xattn_wk=xattn_wk, xattn_wv=xattn_wv, xattn_wo=xattn_wo,
                   ffn_norm_g=ffn_norm_g, router_coarse_w=router_coarse_w, router_coarse_b=router_coarse_b,
                   router_fine_w=router_fine_w, router_fine_b=router_fine_b, exp_w_gate=exp_w_gate,
                   exp_w_up=exp_w_up, exp_w_down=exp_w_down)
    bsz, t_len, d = x.shape
    depth = w_in.shape[0]
    h = x.reshape(bsz * t_len, d)
    for layer in range(depth):
        p = {name: value[layer] for name, value in stacked.items()}
        h = _layer(h, mem, p, bsz, t_len, final_norm_g, final_norm=(layer == depth - 1))
    return h.reshape(bsz, t_len, d)
```

```python
import functools

import jax
import jax.numpy as jnp
from jax import lax
from jax.experimental import pallas as pl
from jax.experimental.pallas import tpu as pltpu

F32 = jnp.float32
BF16 = jnp.bfloat16

D_MODEL = 1024
HEAD_DIM = 64
FOX_HEADS = 6
FOX_WIDTH = FOX_HEADS * HEAD_DIM
POOL_WIDTH = 256
POOL_GROUP_DIM = 64
POOL_WINDOWS = (2, 4, 8, 16)
POOL_HALO = 16
NSA_HEADS = 6
NSA_KV_HEADS = 2
NSA_REP = NSA_HEADS // NSA_KV_HEADS
NSA_WIDTH = NSA_HEADS * HEAD_DIM
KV_WIDTH = NSA_KV_HEADS * HEAD_DIM
N_BRANCH = 3
CMP_LEN = 32
CMP_STRIDE = 16
CMP_HIDDEN = 256
SEL_BLOCK = 64
SEL_TOPN = 16
WINDOW = 512
FORCE_SCORE = 1.0e4
MEM_HEADS = 4
MEM_HEAD_DIM = D_MODEL // MEM_HEADS
N_GROUPS = 4
EXPERTS_PER_GROUP = 4
N_EXPERTS = N_GROUPS * EXPERTS_PER_GROUP
EXPERT_FF = 512
EPS = 1e-6
NEG_INF = -1e30

LANES = 128
SUBLANES = 8
BIG_WIDTH = 3 * FOX_WIDTH + NSA_WIDTH + 4 * KV_WIDTH
COL_QC = 3 * FOX_WIDTH
COL_KV = COL_QC + NSA_WIDTH
SMALL_WIDTH = POOL_WIDTH + LANES + 2 * KV_WIDTH
COL_FA = POOL_WIDTH
COL_GL = POOL_WIDTH + FOX_HEADS
COL_CMP = POOL_WIDTH + LANES
ROUTER_WIDTH = LANES
NSA_PAIR_ORDER = tuple(g * NSA_REP + r for r in range(NSA_REP) for g in range(NSA_KV_HEADS))
COL_POS = HEAD_DIM

VMEM_LIMIT = 56 * 1024 * 1024

_NT = (((1,), (1,)), ((), ()))


def _params(semantics):
    return pltpu.CompilerParams(dimension_semantics=semantics, vmem_limit_bytes=VMEM_LIMIT)


def _rms(x, g):
    return x * lax.rsqrt(jnp.mean(x * x, axis=-1, keepdims=True) + EPS) * g


def _dot(a, b):
    return jnp.dot(a, b, preferred_element_type=F32)


def _dot_nt(a, b):
    return lax.dot_general(a, b, _NT, preferred_element_type=F32)


def _split3(a):
    a0 = a.astype(BF16)
    r = a - a0.astype(F32)
    a1 = r.astype(BF16)
    a2 = (r - a1.astype(F32)).astype(BF16)
    return a0, a1, a2


def _in_proj_kernel(x_ref, g_ref, wb_ref, ws_ref, big_ref, small_ref):
    yb = _rms(x_ref[...], g_ref[...]).astype(BF16)
    big_ref[...] = _dot(yb, wb_ref[...]).astype(BF16)
    small_ref[...] = _dot(yb, ws_ref[...])


def _in_proj(x, g, w_big, w_small, tm=512):
    n = x.shape[0]
    return pl.pallas_call(
        _in_proj_kernel,
        grid=(n // tm,),
        in_specs=[pl.BlockSpec((tm, D_MODEL), lambda i: (i, 0)),
                  pl.BlockSpec((1, D_MODEL), lambda i: (0, 0)),
                  pl.BlockSpec((D_MODEL, BIG_WIDTH), lambda i: (0, 0)),
                  pl.BlockSpec((D_MODEL, SMALL_WIDTH), lambda i: (0, 0))],
        out_specs=[pl.BlockSpec((tm, BIG_WIDTH), lambda i: (i, 0)),
                   pl.BlockSpec((tm, SMALL_WIDTH), lambda i: (i, 0))],
        out_shape=[jax.ShapeDtypeStruct((n, BIG_WIDTH), BF16),
                   jax.ShapeDtypeStruct((n, SMALL_WIDTH), F32)],
        compiler_params=_params(("parallel",)),
        name="in_proj",
    )(x, g, w_big, w_small)


def _forget_kernel(f_ref, b_ref, place_ref, ck_ref):
    c = jax.nn.log_sigmoid(f_ref[...] + b_ref[...])
    t_len = c.shape[0]
    row = lax.broadcasted_iota(jnp.int32, c.shape, 0)
    sh = 1
    while sh < t_len:
        c = c + jnp.where(row >= sh, pltpu.roll(c, sh, axis=0), 0.0)
        sh *= 2
    n0, n1, n2 = _split3(-c)
    ck_ref[...] = (_dot(n0, place_ref[0]) + _dot(n1, place_ref[1]) + _dot(n2, place_ref[2])).astype(BF16)


def _forget_cumsum(small, b_row, place, bsz, t_len):
    return pl.pallas_call(
        _forget_kernel,
        grid=(bsz,),
        in_specs=[pl.BlockSpec((t_len, LANES), lambda b: (b, COL_FA // LANES)),
                  pl.BlockSpec((1, LANES), lambda b: (0, 0)),
                  pl.BlockSpec(place.shape, lambda b: (0, 0, 0))],
        out_specs=pl.BlockSpec((t_len, FOX_WIDTH), lambda b: (b, 0)),
        out_shape=jax.ShapeDtypeStruct((bsz * t_len, FOX_WIDTH), BF16),
        compiler_params=_params(("parallel",)),
        name="forget",
    )(small, b_row, place)


def _forget_placement():
    src = jnp.arange(LANES)[None, :, None]
    dst = jnp.arange(FOX_WIDTH)[None, None, :]
    piece = jnp.arange(3)[:, None, None]
    target = LANES * (src // 2) + 3 * (src % 2) + piece
    return ((src < FOX_HEADS) & (dst == target)).astype(BF16)


def _flash_sweep(n_tiles, n_stream, score, values, last_mask, s_scr, p_scr, m_scr, acc_scr):
    m_scr[...] = jnp.full(m_scr.shape, NEG_INF, F32)
    acc_scr[...] = jnp.zeros(acc_scr.shape, F32)
    p_scr[1] = jnp.zeros(p_scr.shape[1:], BF16)
    for st in range(n_stream):
        s_scr[0, st] = score(0, st)

    def stage(j, a, last):
        b = 1 - a
        for st in range(n_stream):
            s = s_scr[a, st]
            if last:
                s = last_mask(s, j)
            m_old = m_scr[st]
            m_new = jnp.maximum(m_old, jnp.max(s, axis=1, keepdims=True))
            p_scr[a, st] = jnp.exp(s - m_new).astype(BF16)
            if not last:
                s_scr[b, st] = score(j + 1, st)
            prev = _dot(p_scr[b, st], values(jnp.maximum(j - 1, 0), st))
            acc_scr[st] = (acc_scr[st] + prev) * jnp.exp(m_old - m_new)
            m_scr[st] = m_new

    n_main = n_tiles - 1

    def two_stages(jj, carry):
        stage(2 * jj, 0, False)
        stage(2 * jj + 1, 1, False)
        return carry

    lax.fori_loop(0, n_main // 2, two_stages, 0)

    @pl.when(n_main % 2 == 1)
    def _():
        stage(n_main - 1, 0, False)

    for a in range(2):
        @pl.when(n_main % 2 == a)
        def _():
            stage(n_main, a, True)
            for st in range(n_stream):
                acc_scr[st] = acc_scr[st] + _dot(p_scr[a, st], values(n_main, st))


def _fox_kernel(q_ref, k_ref, v_ref, ck_ref, o_ref, q_scr, s_scr, p_scr, m_scr, acc_scr, *, tq, tk):
    i = pl.program_id(1)
    t0 = i * tq
    n_pair = FOX_HEADS // 2
    lane = lax.broadcasted_iota(jnp.int32, (tq, LANES), 1)
    first = lane < HEAD_DIM
    scale = jnp.asarray(HEAD_DIM ** -0.5, BF16)
    pick_a = jnp.where(lane < 3, 1.0, 0.0).astype(BF16)
    pick_b = jnp.where((lane >= 3) & (lane < 6), 1.0, 0.0).astype(BF16)
    for p in range(n_pair):
        qp = q_ref[:, p * LANES:(p + 1) * LANES] * scale
        zero = jnp.zeros_like(qp)
        qa = jnp.concatenate([jnp.where(first, qp, zero), pick_a], axis=1)
        qb = jnp.concatenate([jnp.where(first, zero, qp), pick_b], axis=1)
        q_scr[p] = jnp.concatenate([qa, qb], axis=0)
    ones_v = jnp.ones((tk, LANES), BF16)

    def score(j, p):
        ks = pl.multiple_of(j * tk, tk)
        psl = slice(p * LANES, (p + 1) * LANES)
        kx = jnp.concatenate([k_ref[pl.ds(ks, tk), psl], ck_ref[pl.ds(ks, tk), psl]], axis=1)
        return _dot_nt(q_scr[p], kx)

    def values(j, p):
        ks = pl.multiple_of(j * tk, tk)
        return jnp.concatenate([v_ref[pl.ds(ks, tk), p * LANES:(p + 1) * LANES], ones_v], axis=1)

    def causal_mask(s, j):
        t_row = t0 + lax.broadcasted_iota(jnp.int32, (tq, tk), 0)
        pos = j * tk + lax.broadcasted_iota(jnp.int32, (tq, tk), 1)
        return jnp.where((pos <= t_row)[None], s.reshape(2, tq, tk), NEG_INF).reshape(2 * tq, tk)

    _flash_sweep(t0 // tk + 1, n_pair, score, values, causal_mask, s_scr, p_scr, m_scr, acc_scr)
    for p in range(n_pair):
        acc = acc_scr[p]
        oa = acc[:tq, :LANES] / acc[:tq, LANES:]
        ob = acc[tq:, :LANES] / acc[tq:, LANES:]
        o_ref[:, p * LANES:(p + 1) * LANES] = jnp.where(first, oa, ob).astype(BF16)


def _fox(big, ck, bsz, t_len, tq=256, tk=512):
    nq = t_len // tq
    n_pair = FOX_HEADS // 2
    seq = lambda c: pl.BlockSpec((t_len, FOX_WIDTH), lambda b, i: (b, c))
    return pl.pallas_call(
        functools.partial(_fox_kernel, tq=tq, tk=tk),
        grid=(bsz, nq),
        in_specs=[pl.BlockSpec((tq, FOX_WIDTH), lambda b, i: (b * nq + i, 0)), seq(1), seq(2), seq(0)],
        out_specs=pl.BlockSpec((tq, FOX_WIDTH), lambda b, i: (b * nq + i, 0)),
        out_shape=jax.ShapeDtypeStruct((bsz * t_len, FOX_WIDTH), BF16),
        scratch_shapes=[pltpu.VMEM((n_pair, 2 * tq, 2 * LANES), BF16),
                        pltpu.VMEM((2, n_pair, 2 * tq, tk), F32),
                        pltpu.VMEM((2, n_pair, 2 * tq, tk), BF16),
                        pltpu.VMEM((n_pair, 2 * tq, 1), F32),
                        pltpu.VMEM((n_pair, 2 * tq, 2 * LANES), F32)],
        compiler_params=_params(("parallel", "arbitrary")),
        name="fox",
    )(big, big, big, ck)


def _pool_kernel(u_ref, halo_ref, w_ref, scale_ref, o_ref, x_scr, *, tt):
    i = pl.program_id(1)
    u = u_ref[...]
    x_scr[POOL_HALO:, :] = u
    x_scr[:POOL_HALO, :] = jnp.where(i > 0, halo_ref[...], 0.0)
    t_pos = i * tt + lax.broadcasted_iota(jnp.int32, (tt, 1), 0)
    lane = lax.broadcasted_iota(jnp.int32, (tt, POOL_WIDTH), 1)
    run = u
    pooled = jnp.zeros_like(u)
    for back in range(1, max(POOL_WINDOWS)):
        run = run + x_scr[POOL_HALO - back:POOL_HALO - back + tt, :]
        w = back + 1
        if w in POOL_WINDOWS:
            gi = POOL_WINDOWS.index(w)
            count = jnp.minimum(t_pos + 1, w).astype(F32)
            in_group = (lane >= gi * POOL_GROUP_DIM) & (lane < (gi + 1) * POOL_GROUP_DIM)
            pooled = jnp.where(in_group, run / count - u, pooled)
    mixed = _dot(pooled.astype(BF16), w_ref[...])
    o_ref[...] = (mixed * scale_ref[...]).astype(BF16)


def _pool(small, w_bd, scale, bsz, t_len, tt=512):
    nt = t_len // tt
    per = tt // POOL_HALO
    return pl.pallas_call(
        functools.partial(_pool_kernel, tt=tt),
        grid=(bsz, nt),
        in_specs=[pl.BlockSpec((tt, POOL_WIDTH), lambda b, i: (b * nt + i, 0)),
                  pl.BlockSpec((POOL_HALO, POOL_WIDTH), lambda b, i: (jnp.maximum((b * nt + i) * per - 1, 0), 0)),
                  pl.BlockSpec((POOL_WIDTH, POOL_WIDTH), lambda b, i: (0, 0)),
                  pl.BlockSpec((1, POOL_WIDTH), lambda b, i: (0, 0))],
        out_specs=pl.BlockSpec((tt, POOL_WIDTH), lambda b, i: (b * nt + i, 0)),
        out_shape=jax.ShapeDtypeStruct((bsz * t_len, POOL_WIDTH), BF16),
        scratch_shapes=[pltpu.VMEM((tt + POOL_HALO, POOL_WIDTH), F32)],
        compiler_params=_params(("parallel", "arbitrary")),
        name="pool",
    )(small, small, w_bd, scale)


def _compress_kernel(xk_ref, xv_ref, pek_ref, pev_ref, w1k_ref, w2k_ref, w1v_ref, w2v_ref, kc_ref, vc_ref):
    def one(x_ref, pe_ref, w1_ref, w2_ref, out_ref):
        n_chunk = x_ref.shape[0] // CMP_STRIDE
        first = jnp.zeros((n_chunk, w1_ref.shape[2]), F32)
        second = jnp.zeros((n_chunk, w1_ref.shape[2]), F32)
        for r in range(CMP_STRIDE):
            x_r = x_ref[pl.ds(r, n_chunk, stride=CMP_STRIDE), :]
            first = first + _dot((x_r + pe_ref[r:r + 1, :]).astype(BF16), w1_ref[r])
            second = second + _dot((x_r + pe_ref[CMP_STRIDE + r:CMP_STRIDE + r + 1, :]).astype(BF16),
                                   w1_ref[CMP_STRIDE + r])
        hidden = first + pltpu.roll(second, n_chunk - 1, axis=0)
        out_ref[0] = _dot(jax.nn.gelu(hidden).astype(BF16), w2_ref[...])

    one(xk_ref, pek_ref, w1k_ref, w2k_ref, kc_ref)
    one(xv_ref, pev_ref, w1v_ref, w2v_ref, vc_ref)


def _compress(small, pe_k, pe_v, w1_k, w2_k, w1_v, w2_v, bsz, t_len):
    n_chunk = t_len // CMP_STRIDE
    full = lambda a: pl.BlockSpec(a.shape, lambda b: tuple(0 for _ in a.shape))
    x_spec = lambda col: pl.BlockSpec((t_len, KV_WIDTH), lambda b: (b, col // KV_WIDTH))
    o_spec = pl.BlockSpec((1, n_chunk, KV_WIDTH), lambda b: (b, 0, 0))
    return pl.pallas_call(
        _compress_kernel,
        grid=(bsz,),
        in_specs=[x_spec(COL_CMP), x_spec(COL_CMP + KV_WIDTH), full(pe_k), full(pe_v),
                  full(w1_k), full(w2_k), full(w1_v), full(w2_v)],
        out_specs=[o_spec, o_spec],
        out_shape=[jax.ShapeDtypeStruct((bsz, n_chunk, KV_WIDTH), F32)] * 2,
        compiler_params=_params(("parallel",)),
        name="compress",
    )(small, small, pe_k, pe_v, w1_k, w2_k, w1_v, w2_v)


def _compress_weights(pe, w1, w2):
    eye = jnp.eye(NSA_KV_HEADS, dtype=w1.dtype)
    w1_r = w1.reshape(CMP_LEN, HEAD_DIM, CMP_HIDDEN)
    w1_bd = jnp.einsum("gh,rdc->rgdhc", eye, w1_r).reshape(CMP_LEN, KV_WIDTH, NSA_KV_HEADS * CMP_HIDDEN)
    w2_bd = jnp.einsum("gh,cd->gchd", eye, w2).reshape(NSA_KV_HEADS * CMP_HIDDEN, KV_WIDTH)
    return jnp.tile(pe, (1, NSA_KV_HEADS)), w1_bd.astype(BF16), w2_bd.astype(BF16)


def _nsa_kernel(q_ref, ksl_ref, vsl_ref, kwn_ref, vwn_ref, kpos_ref, kc_ref, vc_ref, cpos_ref, ovt_ref, slope_ref,
                sm_ref, gb_ref, o_ref, qs_scr, s_scr, p_scr, m_scr, acc_scr, *, tq, tks):
    i = pl.program_id(1)
    t0 = i * tq
    n_head = NSA_HEADS
    rows = n_head * tq
    n_cmp = kc_ref.shape[1]
    n_blk = ovt_ref.shape[0]
    win_keys = WINDOW + tq
    lane = lax.broadcasted_iota(jnp.int32, (tq, LANES), 1)
    first = lane < HEAD_DIM
    scale = jnp.asarray(HEAD_DIM ** -0.5, BF16)

    pairs = [q_ref[:, r * LANES:(r + 1) * LANES] * scale for r in range(NSA_REP)]
    zero = jnp.zeros_like(pairs[0])
    q_heads, q_slopes = [], []
    for g in range(NSA_KV_HEADS):
        for r in range(NSA_REP):
            q_heads.append(jnp.where(first, pairs[r], zero) if g == 0 else jnp.where(first, zero, pairs[r]))
            q_slopes.append(jnp.broadcast_to(slope_ref[g * NSA_REP + r:g * NSA_REP + r + 1, :], (tq, LANES)))
    q_main = jnp.concatenate(q_heads, axis=0)
    q_plain = jnp.concatenate([q_main, jnp.concatenate(q_slopes, axis=0).astype(BF16)], axis=1)

    n_part = s_scr.shape[1]
    part_rows = rows // n_part

    def per_head(mask, s, fill):
        return jnp.where(mask[None], s.reshape(s.shape[0] // tq, tq, s.shape[-1]), fill).reshape(s.shape)

    t_col = t0 + lax.broadcasted_iota(jnp.int32, (tq, n_cmp), 0)
    c_end = lax.broadcasted_iota(jnp.int32, (tq, n_cmp), 1) * CMP_STRIDE + (CMP_LEN - 1)
    valid_c = c_end <= t_col
    kc0, kc1, _ = _split3(kc_ref[0])
    kc_x = jnp.concatenate([kc0, cpos_ref[...]], axis=1)
    vc_b = vc_ref[0].astype(BF16)
    p_parts, o_parts = [], []
    for part in range(n_part):
        rsl = slice(part * part_rows, (part + 1) * part_rows)
        s = per_head(valid_c, _dot_nt(q_plain[rsl], kc_x) + _dot_nt(q_main[rsl], kc1), NEG_INF)
        p = per_head(valid_c, jnp.exp(s - jnp.max(s, axis=1, keepdims=True)), 0.0)
        l_c = jnp.sum(p, axis=1, keepdims=True)
        p = p * jnp.where(l_c > 0.0, 1.0 / l_c, 0.0)
        p_parts.append(p)
        o_parts.append(_dot(p.astype(BF16), vc_b))
    p_c = jnp.concatenate(p_parts, axis=0)
    o_cmp = jnp.concatenate(o_parts, axis=0)

    blk_t = lax.broadcasted_iota(jnp.int32, (n_blk, tq), 0)
    cur_t = (t0 + lax.broadcasted_iota(jnp.int32, (n_blk, tq), 1)) // SEL_BLOCK
    forced_t = (blk_t == 0) | (blk_t == cur_t) | (blk_t == cur_t - 1)
    sub = lax.broadcasted_iota(jnp.int32, (SUBLANES, tq), 0)
    n_top = float(min(SEL_TOPN, n_blk))
    for g in range(NSA_KV_HEADS):
        p_sum = p_c[g * NSA_REP * tq:(g * NSA_REP + 1) * tq]
        for r in range(1, NSA_REP):
            p_sum = p_sum + p_c[(g * NSA_REP + r) * tq:(g * NSA_REP + r + 1) * tq]
        p0, p1, p2 = _split3(p_sum)
        ovt = ovt_ref[...]
        imp_t = _dot_nt(ovt, p0) + _dot_nt(ovt, p1) + _dot_nt(ovt, p2)
        score = jnp.where(forced_t, FORCE_SCORE, jnp.where(blk_t <= cur_t, imp_t, -1.0))
        chunks = [score[c * SUBLANES:(c + 1) * SUBLANES] for c in range(n_blk // SUBLANES)]
        ranks = [jnp.zeros((SUBLANES, tq), F32) for _ in chunks]
        for jp in range(n_blk):
            other = jnp.broadcast_to(score[jp:jp + 1], (SUBLANES, tq))
            for c, chunk in enumerate(chunks):
                ge = jnp.where(other >= chunk, 1.0, 0.0)
                gt = jnp.where(other > chunk, 1.0, 0.0)
                if jp < c * SUBLANES:
                    ranks[c] = ranks[c] + ge
                elif jp >= (c + 1) * SUBLANES:
                    ranks[c] = ranks[c] + gt
                else:
                    ranks[c] = ranks[c] + jnp.where(sub > jp - c * SUBLANES, ge, gt)
        rank = jnp.concatenate(ranks, axis=0)
        neg_t = jnp.where(rank < n_top, 0.0, NEG_INF)
        neg = jnp.transpose(jnp.concatenate([neg_t, jnp.zeros((LANES - n_blk, tq), F32)], axis=0))
        for r in range(NSA_REP):
            k = g * NSA_REP + r
            q_const = (neg + q_slopes[k]).astype(BF16)
            qs_scr[k * tq:(k + 1) * tq, :] = jnp.concatenate([q_heads[k], q_const], axis=1)

    ones_s = jnp.ones((tks, LANES), BF16)

    def slc_score(j, part):
        ks = pl.multiple_of(j * tks, tks)
        kx = jnp.concatenate([ksl_ref[pl.ds(ks, tks), :], kpos_ref[pl.ds(ks, tks), :]], axis=1)
        return _dot_nt(qs_scr[part * part_rows:(part + 1) * part_rows, :], kx)

    def slc_values(j, part):
        ks = pl.multiple_of(j * tks, tks)
        return jnp.concatenate([vsl_ref[pl.ds(ks, tks), :], ones_s], axis=1)

    def slc_mask(s, j):
        t_row = t0 + lax.broadcasted_iota(jnp.int32, (tq, tks), 0)
        pos = j * tks + lax.broadcasted_iota(jnp.int32, (tq, tks), 1)
        return jnp.where((pos <= t_row)[None], s.reshape(part_rows // tq, tq, tks), NEG_INF).reshape(part_rows, tks)

    _flash_sweep(t0 // tks + 1, n_part, slc_score, slc_values, slc_mask, s_scr, p_scr, m_scr, acc_scr)
    acc = acc_scr[...].reshape(rows, 2 * LANES)
    o_slc = acc[:, :LANES] / acc[:, LANES:]

    start = pl.multiple_of(jnp.maximum(t0 - WINDOW, 0), tq)
    kx = jnp.concatenate([kwn_ref[pl.ds(start, win_keys), :], kpos_ref[pl.ds(start, win_keys), :]], axis=1)
    vx = jnp.concatenate([vwn_ref[pl.ds(start, win_keys), :], jnp.ones((win_keys, LANES), BF16)], axis=1)
    dist = (t0 + lax.broadcasted_iota(jnp.int32, (tq, win_keys), 0)
            - (start + lax.broadcasted_iota(jnp.int32, (tq, win_keys), 1)))
    in_window = (dist >= 0) & (dist < WINDOW)
    o_parts = []
    win_score = lambda part: _dot_nt(q_plain[part * part_rows:(part + 1) * part_rows], kx)
    s_next = win_score(0)
    for part in range(n_part):
        s = per_head(in_window, s_next, NEG_INF)
        if part + 1 < n_part:
            s_next = win_score(part + 1)
        prob = jnp.exp(s - jnp.max(s, axis=1, keepdims=True))
        acc = _dot(prob.astype(BF16), vx)
        o_parts.append(acc[:, :LANES] / acc[:, LANES:])
    o_win = jnp.concatenate(o_parts, axis=0)

    gates = jax.nn.sigmoid(sm_ref[:, COL_GL - COL_FA:COL_GL - COL_FA + NSA_HEADS * N_BRANCH] + gb_ref[...])
    for r in range(NSA_REP):
        halves = []
        for g in range(NSA_KV_HEADS):
            k = g * NSA_REP + r
            rsl = slice(k * tq, (k + 1) * tq)
            gate = lambda br: gates[:, k * N_BRANCH + br:k * N_BRANCH + br + 1]
            halves.append(gate(0) * o_cmp[rsl] + gate(1) * o_slc[rsl] + gate(2) * o_win[rsl])
        o_ref[:, r * LANES:(r + 1) * LANES] = jnp.where(first, halves[0], halves[1]).astype(BF16)


def _nsa(big, kpos, kc, vc, cpos, ovt, slope_tab, small, gate_b, bsz, t_len, tq=128, tks=512, n_part=3):
    nq = t_len // tq
    kv_col = COL_KV // KV_WIDTH
    kv_spec = lambda c: pl.BlockSpec((t_len, KV_WIDTH), lambda b, i: (b, kv_col + c))
    full = lambda a: pl.BlockSpec(a.shape, lambda b, i: tuple(0 for _ in a.shape))
    n_cmp = kc.shape[1]
    rows = NSA_HEADS * tq
    return pl.pallas_call(
        functools.partial(_nsa_kernel, tq=tq, tks=tks),
        grid=(bsz, nq),
        in_specs=[pl.BlockSpec((tq, NSA_WIDTH), lambda b, i: (b * nq + i, COL_QC // NSA_WIDTH)),
                  kv_spec(0), kv_spec(1), kv_spec(2), kv_spec(3), full(kpos),
                  pl.BlockSpec((1, n_cmp, KV_WIDTH), lambda b, i: (b, 0, 0)),
                  pl.BlockSpec((1, n_cmp, KV_WIDTH), lambda b, i: (b, 0, 0)),
                  full(cpos), full(ovt), full(slope_tab),
                  pl.BlockSpec((tq, LANES), lambda b, i: (b * nq + i, COL_FA // LANES)),
                  full(gate_b)],
        out_specs=pl.BlockSpec((tq, NSA_WIDTH), lambda b, i: (b * nq + i, 0)),
        out_shape=jax.ShapeDtypeStruct((bsz * t_len, NSA_WIDTH), BF16),
        scratch_shapes=[pltpu.VMEM((rows, 2 * LANES), BF16),
                        pltpu.VMEM((2, n_part, rows // n_part, tks), F32),
                        pltpu.VMEM((2, n_part, rows // n_part, tks), BF16),
                        pltpu.VMEM((n_part, rows // n_part, 1), F32),
                        pltpu.VMEM((n_part, rows // n_part, 2 * LANES), F32)],
        compiler_params=_params(("parallel", "arbitrary")),
        name="nsa",
    )(big, big, big, big, big, kpos, kc, vc, cpos, ovt, slope_tab, small, gate_b)


def _position_columns(pos):
    lane = jnp.arange(LANES)[None, :]
    hi = (SEL_BLOCK * (pos // SEL_BLOCK))[:, None]
    lo = (pos % SEL_BLOCK)[:, None]
    cols = jnp.where((lane >= COL_POS) & (lane < COL_POS + 3), hi, 0) + jnp.where((lane >= COL_POS + 3) & (lane < COL_POS + 6), lo, 0)
    return cols.astype(BF16)


def _nsa_tables(t_len):
    pos = jnp.arange(t_len)
    lane = jnp.arange(LANES)[None, :]
    block_one_hot = ((lane < t_len // SEL_BLOCK) & (lane == (pos // SEL_BLOCK)[:, None])).astype(BF16)
    kpos = _position_columns(pos) + block_one_hot
    n_cmp = t_len // CMP_STRIDE
    c_start = jnp.arange(n_cmp) * CMP_STRIDE
    cpos = _position_columns(c_start + CMP_LEN - 1)
    s_start = (jnp.arange(t_len // SEL_BLOCK) * SEL_BLOCK)[:, None]
    ovt = ((c_start[None, :] < s_start + SEL_BLOCK) & (c_start[None, :] + CMP_LEN - 1 >= s_start)).astype(BF16)
    slopes = jnp.exp2(-8.0 * jnp.arange(1, NSA_HEADS + 1, dtype=F32) / NSA_HEADS)
    pieces = jnp.stack([piece.astype(F32) for piece in _split3(slopes)], axis=1)
    slope_tab = jnp.zeros((NSA_HEADS, LANES), F32)
    slope_tab = slope_tab.at[:, COL_POS:COL_POS + 3].set(pieces).at[:, COL_POS + 3:COL_POS + 6].set(pieces)
    return kpos, cpos, ovt, slope_tab


def _mem_kv_kernel(mem_ref, g_ref, wk_ref, wv_ref, k_ref, v_ref):
    mb = _rms(mem_ref[0], g_ref[...]).astype(BF16)
    k_ref[0] = _dot(mb, wk_ref[...]).astype(BF16)
    v_ref[0] = _dot(mb, wv_ref[...]).astype(BF16)


def _mem_kv(mem, g, wk, wv):
    bsz, m_len, _ = mem.shape
    w_spec = pl.BlockSpec((D_MODEL, D_MODEL), lambda b: (0, 0))
    m_spec = pl.BlockSpec((1, m_len, D_MODEL), lambda b: (b, 0, 0))
    return pl.pallas_call(
        _mem_kv_kernel,
        grid=(bsz,),
        in_specs=[m_spec, pl.BlockSpec((1, D_MODEL), lambda b: (0, 0)), w_spec, w_spec],
        out_specs=[m_spec, m_spec],
        out_shape=[jax.ShapeDtypeStruct((bsz, m_len, D_MODEL), BF16)] * 2,
        compiler_params=_params(("parallel",)),
        name="mem_kv",
    )(mem, g, wk, wv)


def _route(logits):
    lane = lax.broadcasted_iota(jnp.int32, logits.shape, 1)
    big_i = jnp.int32(1 << 20)
    is_coarse = lane < N_GROUPS
    lc = jnp.where(is_coarse, logits, NEG_INF)
    mc = jnp.max(lc, axis=1, keepdims=True)
    grp = jnp.min(jnp.where(is_coarse & (lc == mc), lane, big_i), axis=1, keepdims=True)
    w_grp = 1.0 / jnp.sum(jnp.where(is_coarse, jnp.exp(lc - mc), 0.0), axis=1, keepdims=True)
    e_id = lane - N_GROUPS
    in_grp = (e_id >= grp * EXPERTS_PER_GROUP) & (e_id < (grp + 1) * EXPERTS_PER_GROUP)
    lf = jnp.where(in_grp, logits, NEG_INF)
    m1 = jnp.max(lf, axis=1, keepdims=True)
    i1 = jnp.min(jnp.where(in_grp & (lf == m1), lane, big_i), axis=1, keepdims=True)
    lf2 = jnp.where(lane == i1, NEG_INF, lf)
    m2 = jnp.max(lf2, axis=1, keepdims=True)
    i2 = jnp.min(jnp.where(in_grp & (lane != i1) & (lf2 == m2), lane, big_i), axis=1, keepdims=True)
    denom = jnp.sum(jnp.where(in_grp, jnp.exp(lf - m1), 0.0), axis=1, keepdims=True)
    p1 = 1.0 / denom
    p2 = jnp.exp(m2 - m1) / denom
    tot = p1 + p2
    w1 = w_grp * (p1 / tot)
    w2 = w_grp * (p2 / tot)
    return jnp.where(lane + N_GROUPS == i1, w1, jnp.where(lane + N_GROUPS == i2, w2, 0.0))


def _mix_out_kernel(x_ref, oa_ref, ob_ref, oc_ref, woa_ref, wob_ref, woc_ref, gx_ref, wq_ref, mk_ref, mv_ref,
                    wo_ref, gf_ref, wr_ref, br_ref, x2_ref, hb_ref, comb_ref):
    x1 = (x_ref[...] + _dot(oa_ref[...], woa_ref[...]) + _dot(ob_ref[...], wob_ref[...])
          + _dot(oc_ref[...], woc_ref[...]))
    hq = _rms(x1, gx_ref[...]).astype(BF16)
    q = (_dot(hq, wq_ref[...]) * (MEM_HEAD_DIM ** -0.5)).astype(BF16)
    heads = []
    for h in range(MEM_HEADS):
        sl = slice(h * MEM_HEAD_DIM, (h + 1) * MEM_HEAD_DIM)
        s = _dot_nt(q[:, sl], mk_ref[0, :, sl])
        p = jnp.exp(s - jnp.max(s, axis=1, keepdims=True))
        p = p / jnp.sum(p, axis=1, keepdims=True)
        heads.append(_dot(p.astype(BF16), mv_ref[0, :, sl]).astype(BF16))
    x2 = x1 + _dot(jnp.concatenate(heads, axis=1), wo_ref[...])
    x2_ref[...] = x2
    hn = _rms(x2, gf_ref[...])
    hb_ref[...] = hn.astype(BF16)
    h0, h1, h2 = _split3(hn)
    w0, w1, w2 = wr_ref[0], wr_ref[1], wr_ref[2]
    logits = (_dot(h0, w0) + (_dot(h0, w1) + _dot(h1, w0)) + (_dot(h0, w2) + _dot(h1, w1) + _dot(h2, w0))
              + br_ref[...])
    comb_ref[...] = _route(logits)


def _mix_out(x, o_a, o_b, o_c, wo_a, wo_b, wo_c, g_x, wq, mem_k, mem_v, wo, g_f, w_r, b_r, t_len, tm=512):
    n = x.shape[0]
    per_batch = t_len // tm
    m_len = mem_k.shape[1]
    tile = lambda width: pl.BlockSpec((tm, width), lambda i: (i, 0))
    full = lambda shape: pl.BlockSpec(shape, lambda i: tuple(0 for _ in shape))
    mem_spec = pl.BlockSpec((1, m_len, D_MODEL), lambda i: (i // per_batch, 0, 0))
    return pl.pallas_call(
        _mix_out_kernel,
        grid=(n // tm,),
        in_specs=[tile(D_MODEL), tile(FOX_WIDTH), tile(POOL_WIDTH), tile(NSA_WIDTH),
                  full(wo_a.shape), full(wo_b.shape), full(wo_c.shape), full(g_x.shape), full(wq.shape),
                  mem_spec, mem_spec, full(wo.shape), full(g_f.shape), full(w_r.shape), full(b_r.shape)],
        out_specs=[tile(D_MODEL), tile(D_MODEL), tile(ROUTER_WIDTH)],
        out_shape=[jax.ShapeDtypeStruct((n, D_MODEL), F32),
                   jax.ShapeDtypeStruct((n, D_MODEL), BF16),
                   jax.ShapeDtypeStruct((n, ROUTER_WIDTH), F32)],
        compiler_params=_params(("parallel",)),
        name="mix_out",
    )(x, o_a, o_b, o_c, wo_a, wo_b, wo_c, g_x, wq, mem_k, mem_v, wo, g_f, w_r, b_r)


def _moe_kernel(x_ref, hb_ref, comb_ref, wg_ref, wu_ref, wd_ref, gfin_ref, o_ref, *, final_norm):
    e = pl.program_id(1)

    @pl.when(e == 0)
    def _():
        o_ref[...] = x_ref[...]

    hb = hb_ref[...]
    gate = _dot(hb, wg_ref[0, 0].astype(BF16))
    up = _dot(hb, wu_ref[0, 0].astype(BF16))
    he = (jax.nn.silu(gate) * up).astype(BF16)
    lane = lax.broadcasted_iota(jnp.int32, comb_ref.shape, 1)
    c_e = jnp.sum(jnp.where(lane == e, comb_ref[...], 0.0), axis=1, keepdims=True)
    o_ref[...] += c_e * _dot(he, wd_ref[0, 0].astype(BF16))

    if final_norm:
        @pl.when(e == pl.num_programs(1) - 1)
        def _():
            o_ref[...] = _rms(o_ref[...], gfin_ref[...])


def _moe(x2, hb, comb, wg, wu, wd, layer, g_fin, final_norm, tm=1024):
    n = x2.shape[0]
    n_exp = wg.shape[1]
    tile = lambda width: pl.BlockSpec((tm, width), lambda i, e: (i, 0))
    expert = lambda shape: pl.BlockSpec((1, 1) + shape, lambda i, e: (layer, e, 0, 0))
    return pl.pallas_call(
        functools.partial(_moe_kernel, final_norm=final_norm),
        grid=(n // tm, n_exp),
        in_specs=[tile(D_MODEL), tile(D_MODEL), tile(ROUTER_WIDTH),
                  expert((D_MODEL, EXPERT_FF)), expert((D_MODEL, EXPERT_FF)), expert((EXPERT_FF, D_MODEL)),
                  pl.BlockSpec((1, D_MODEL), lambda i, e: (0, 0))],
        out_specs=tile(D_MODEL),
        out_shape=jax.ShapeDtypeStruct((n, D_MODEL), F32),
        compiler_params=_params(("parallel", "arbitrary")),
        name="moe",
    )(x2, hb, comb, wg, wu, wd, g_fin)


def _pair_order(a, axis):
    shape = a.shape
    a = a.reshape(shape[:axis] + (NSA_HEADS, HEAD_DIM) + shape[axis + 1:])
    a = jnp.take(a, jnp.array(NSA_PAIR_ORDER), axis=axis)
    return a.reshape(shape)


def _split_in_proj_weight(w):
    edges = [0]
    for width in (FOX_WIDTH, FOX_WIDTH, FOX_WIDTH, FOX_HEADS, POOL_WIDTH, NSA_WIDTH) + (KV_WIDTH,) * 6 + (NSA_HEADS * N_BRANCH,):
        edges.append(edges[-1] + width)
    part = lambda k: w[:, edges[k]:edges[k + 1]]
    qa, ka, va, fa, ub, qc = (part(k) for k in range(6))
    kvs = [part(k) for k in range(6, 12)]
    gl = part(12)
    w_big = jnp.concatenate([qa, ka, va, _pair_order(qc, 1)] + kvs[2:], axis=1).astype(BF16)
    pad = jnp.zeros((w.shape[0], LANES - FOX_HEADS - NSA_HEADS * N_BRANCH), w.dtype)
    w_small = jnp.concatenate([ub, fa, gl, pad] + kvs[:2], axis=1).astype(BF16)
    return w_big, w_small


def _block_diag(w):
    groups, cin, cout = w.shape
    out = jnp.zeros((groups * cin, groups * cout), w.dtype)
    for gi in range(groups):
        out = out.at[gi * cin:(gi + 1) * cin, gi * cout:(gi + 1) * cout].set(w[gi])
    return out


def _layer(x, mem, p, experts, layer, bsz, t_len, final_g, final_norm):
    row = lambda v: v.reshape(1, -1)
    w_big, w_small = _split_in_proj_weight(p["w_in"])
    big, small = _in_proj(x, row(p["mix_norm_g"]), w_big, w_small)

    b_row = jnp.pad(p["fox_forget_b"], (0, LANES - FOX_HEADS)).reshape(1, LANES)
    ck = _forget_cumsum(small, b_row, _forget_placement(), bsz, t_len)
    o_a = _fox(big, ck, bsz, t_len)

    o_b = _pool(small, _block_diag(p["pool_w"]).astype(BF16), row(p["pool_scale"]), bsz, t_len)

    pe_k, w1_k, w2_k = _compress_weights(p["cmp_pe_k"], p["cmp_w1_k"], p["cmp_w2_k"])
    pe_v, w1_v, w2_v = _compress_weights(p["cmp_pe_v"], p["cmp_w1_v"], p["cmp_w2_v"])
    kc, vc = _compress(small, pe_k, pe_v, w1_k, w2_k, w1_v, w2_v, bsz, t_len)
    kpos, cpos, ovt, slope_tab = _nsa_tables(t_len)
    o_c = _nsa(big, kpos, kc, vc, cpos, ovt, slope_tab, small, row(p["nsa_gate_b"]), bsz, t_len)

    mem_k, mem_v = _mem_kv(mem, row(p["mem_norm_g"]), p["xattn_wk"].astype(BF16), p["xattn_wv"].astype(BF16))
    w_out = p["w_out"]
    wo_a = w_out[:FOX_WIDTH].astype(BF16)
    wo_b = w_out[FOX_WIDTH:FOX_WIDTH + POOL_WIDTH].astype(BF16)
    wo_c = _pair_order(w_out[FOX_WIDTH + POOL_WIDTH:], 0).astype(BF16)
    w_r = jnp.concatenate([p["router_coarse_w"], p["router_fine_w"],
                           jnp.zeros((D_MODEL, ROUTER_WIDTH - N_GROUPS - N_EXPERTS), F32)], axis=1)
    b_r = jnp.concatenate([p["router_coarse_b"], p["router_fine_b"],
                           jnp.zeros((ROUTER_WIDTH - N_GROUPS - N_EXPERTS,), F32)]).reshape(1, -1)
    x2, hb, comb = _mix_out(x, o_a, o_b, o_c, wo_a, wo_b, wo_c, row(p["xattn_norm_g"]), p["xattn_wq"].astype(BF16),
                            mem_k, mem_v, p["xattn_wo"].astype(BF16), row(p["ffn_norm_g"]),
                            jnp.stack(_split3(w_r)), b_r, t_len)

    return _moe(x2, hb, comb, *experts, layer, row(final_g), final_norm)


def kernel(x, mem, mix_norm_g, w_in, fox_forget_b, pool_w, pool_scale, cmp_pe_k, cmp_pe_v, cmp_w1_k, cmp_w2_k, cmp_w1_v, cmp_w2_v, nsa_gate_b, w_out, xattn_norm_g, mem_norm_g, xattn_wq, xattn_wk, xattn_wv, xattn_wo, ffn_norm_g, router_coarse_w, router_coarse_b, router_fine_w, router_fine_b, exp_w_gate, exp_w_up, exp_w_down, final_norm_g):
    stacked = dict(mix_norm_g=mix_norm_g, w_in=w_in, fox_forget_b=fox_forget_b, pool_w=pool_w, pool_scale=pool_scale,
                   cmp_pe_k=cmp_pe_k, cmp_pe_v=cmp_pe_v, cmp_w1_k=cmp_w1_k, cmp_w2_k=cmp_w2_k, cmp_w1_v=cmp_w1_v,
                   cmp_w2_v=cmp_w2_v, nsa_gate_b=nsa_gate_b, w_out=w_out, xattn_norm_g=xattn_norm_g,
                   mem_norm_g=mem_norm_g, xattn_wq=xattn_wq, xattn_wk=xattn_wk, xattn_wv=xattn_wv, xattn_wo=xattn_wo,
                   ffn_norm_g=ffn_norm_g, router_coarse_w=router_coarse_w, router_coarse_b=router_coarse_b,
                   router_fine_w=router_fine_w, router_fine_b=router_fine_b)
    experts = (exp_w_gate, exp_w_up, exp_w_down)
    bsz, t_len, d = x.shape
    depth = w_in.shape[0]
    h = x.reshape(bsz * t_len, d)
    for layer in range(depth):
        p = {name: value[layer] for name, value in stacked.items()}
        h = _layer(h, mem, p, experts, layer, bsz, t_len, final_norm_g, final_norm=(layer == depth - 1))
    return h.reshape(bsz, t_len, d)
```

```python
import functools

import jax
import jax.numpy as jnp
from jax import lax
from jax.experimental import pallas as pl
from jax.experimental.pallas import tpu as pltpu

F32 = jnp.float32
BF16 = jnp.bfloat16

D_MODEL = 1024
HEAD_DIM = 64
FOX_HEADS = 6
FOX_WIDTH = FOX_HEADS * HEAD_DIM
POOL_WIDTH = 256
POOL_GROUP_DIM = 64
POOL_WINDOWS = (2, 4, 8, 16)
POOL_HALO = 16
NSA_HEADS = 6
NSA_KV_HEADS = 2
NSA_REP = NSA_HEADS // NSA_KV_HEADS
NSA_WIDTH = NSA_HEADS * HEAD_DIM
KV_WIDTH = NSA_KV_HEADS * HEAD_DIM
N_BRANCH = 3
CMP_LEN = 32
CMP_STRIDE = 16
CMP_HIDDEN = 256
SEL_BLOCK = 64
SEL_TOPN = 16
WINDOW = 512
FORCE_SCORE = 1.0e4
MEM_HEADS = 4
MEM_HEAD_DIM = D_MODEL // MEM_HEADS
N_GROUPS = 4
EXPERTS_PER_GROUP = 4
N_EXPERTS = N_GROUPS * EXPERTS_PER_GROUP
EXPERT_FF = 512
EPS = 1e-6
NEG_INF = -1e30

LANES = 128
SUBLANES = 8
BIG_WIDTH = 3 * FOX_WIDTH + NSA_WIDTH + 4 * KV_WIDTH
COL_QC = 3 * FOX_WIDTH
COL_KV = COL_QC + NSA_WIDTH
SMALL_WIDTH = POOL_WIDTH + LANES + 2 * KV_WIDTH
COL_FA = POOL_WIDTH
COL_GL = POOL_WIDTH + FOX_HEADS
COL_CMP = POOL_WIDTH + LANES
ROUTER_WIDTH = LANES
ROUTER_PIECE = 32
NSA_PAIR_ORDER = tuple(g * NSA_REP + r for r in range(NSA_REP) for g in range(NSA_KV_HEADS))
COL_POS = HEAD_DIM

VMEM_LIMIT = 56 * 1024 * 1024

_NT = (((1,), (1,)), ((), ()))


def _params(semantics):
    return pltpu.CompilerParams(dimension_semantics=semantics, vmem_limit_bytes=VMEM_LIMIT)


def _rms(x, g):
    return x * lax.rsqrt(jnp.mean(x * x, axis=-1, keepdims=True) + EPS) * g


def _dot(a, b):
    return jnp.dot(a, b, preferred_element_type=F32)


def _dot_nt(a, b):
    return lax.dot_general(a, b, _NT, preferred_element_type=F32)


def _split3(a):
    a0 = a.astype(BF16)
    r = a - a0.astype(F32)
    a1 = r.astype(BF16)
    a2 = (r - a1.astype(F32)).astype(BF16)
    return a0, a1, a2


def _in_proj_kernel(x_ref, g_ref, wb_ref, ws_ref, big_ref, small_ref):
    yb = _rms(x_ref[...], g_ref[...]).astype(BF16)
    big_ref[...] = _dot(yb, wb_ref[...]).astype(BF16)
    small_ref[...] = _dot(yb, ws_ref[...])


def _in_proj(x, g, w_big, w_small, tm=512):
    n = x.shape[0]
    return pl.pallas_call(
        _in_proj_kernel,
        grid=(n // tm,),
        in_specs=[pl.BlockSpec((tm, D_MODEL), lambda i: (i, 0)),
                  pl.BlockSpec((1, D_MODEL), lambda i: (0, 0)),
                  pl.BlockSpec((D_MODEL, BIG_WIDTH), lambda i: (0, 0)),
                  pl.BlockSpec((D_MODEL, SMALL_WIDTH), lambda i: (0, 0))],
        out_specs=[pl.BlockSpec((tm, BIG_WIDTH), lambda i: (i, 0)),
                   pl.BlockSpec((tm, SMALL_WIDTH), lambda i: (i, 0))],
        out_shape=[jax.ShapeDtypeStruct((n, BIG_WIDTH), BF16),
                   jax.ShapeDtypeStruct((n, SMALL_WIDTH), F32)],
        compiler_params=_params(("parallel",)),
        name="in_proj",
    )(x, g, w_big, w_small)


def _forget_kernel(f_ref, b_ref, place_ref, ck_ref):
    c = jax.nn.log_sigmoid(f_ref[...] + b_ref[...])
    t_len = c.shape[0]
    row = lax.broadcasted_iota(jnp.int32, c.shape, 0)
    sh = 1
    while sh < t_len:
        c = c + jnp.where(row >= sh, pltpu.roll(c, sh, axis=0), 0.0)
        sh *= 2
    n0, n1, n2 = _split3(-c)
    ck_ref[...] = (_dot(n0, place_ref[0]) + _dot(n1, place_ref[1]) + _dot(n2, place_ref[2])).astype(BF16)


def _forget_cumsum(small, b_row, place, bsz, t_len):
    return pl.pallas_call(
        _forget_kernel,
        grid=(bsz,),
        in_specs=[pl.BlockSpec((t_len, LANES), lambda b: (b, COL_FA // LANES)),
                  pl.BlockSpec((1, LANES), lambda b: (0, 0)),
                  pl.BlockSpec(place.shape, lambda b: (0, 0, 0))],
        out_specs=pl.BlockSpec((t_len, FOX_WIDTH), lambda b: (b, 0)),
        out_shape=jax.ShapeDtypeStruct((bsz * t_len, FOX_WIDTH), BF16),
        compiler_params=_params(("parallel",)),
        name="forget",
    )(small, b_row, place)


def _forget_placement():
    src = jnp.arange(LANES)[None, :, None]
    dst = jnp.arange(FOX_WIDTH)[None, None, :]
    piece = jnp.arange(3)[:, None, None]
    target = LANES * (src // 2) + 3 * (src % 2) + piece
    return ((src < FOX_HEADS) & (dst == target)).astype(BF16)


def _flash_sweep(n_tiles, n_stream, score, values, last_mask, s_scr, p_scr, m_scr, acc_scr):
    m_scr[...] = jnp.full(m_scr.shape, NEG_INF, F32)
    acc_scr[...] = jnp.zeros(acc_scr.shape, F32)
    for st in range(n_stream):
        s_scr[0, st] = score(0, st)

    def run_stage(j, a, first=False, last=False):
        b = 1 - a
        for st in range(n_stream):
            s = s_scr[a, st]
            if last:
                s = last_mask(s, j)
            m_old = m_scr[st]
            m_new = jnp.maximum(m_old, jnp.max(s, axis=1, keepdims=True))
            p_scr[a, st] = jnp.exp(s - m_new).astype(BF16)
            if not last:
                s_scr[b, st] = score(j + 1, st)
            if not first:
                prev = _dot(p_scr[b, st], values(j - 1, st))
                acc_scr[st] = (acc_scr[st] + prev) * jnp.exp(m_old - m_new)
            m_scr[st] = m_new

    def finish(j, a):
        for st in range(n_stream):
            acc_scr[st] = acc_scr[st] + _dot(p_scr[a, st], values(j, st))

    n_main = n_tiles - 1

    @pl.when(n_main == 0)
    def _():
        run_stage(0, 0, first=True, last=True)
        finish(0, 0)

    @pl.when(n_main > 0)
    def _():
        run_stage(0, 0, first=True)

    def two_stages(jj, carry):
        run_stage(2 * jj + 1, 1)
        run_stage(2 * jj + 2, 0)
        return carry

    lax.fori_loop(0, jnp.maximum(n_main - 1, 0) // 2, two_stages, 0)

    @pl.when(jnp.logical_and(n_main >= 2, n_main % 2 == 0))
    def _():
        run_stage(n_main - 1, 1)

    for a in range(2):
        @pl.when(jnp.logical_and(n_main >= 1, n_main % 2 == a))
        def _():
            run_stage(n_main, a, last=True)
            finish(n_main, a)


def _fox_kernel(q_ref, k_ref, v_ref, ck_ref, o_ref, q_scr, s_scr, p_scr, m_scr, acc_scr, *, tq, tk):
    i = pl.program_id(1)
    t0 = i * tq
    n_pair = FOX_HEADS // 2
    lane = lax.broadcasted_iota(jnp.int32, (tq, LANES), 1)
    first = lane < HEAD_DIM
    scale = jnp.asarray(HEAD_DIM ** -0.5, BF16)
    pick_a = jnp.where(lane < 3, 1.0, 0.0).astype(BF16)
    pick_b = jnp.where((lane >= 3) & (lane < 6), 1.0, 0.0).astype(BF16)
    for p in range(n_pair):
        qp = q_ref[:, p * LANES:(p + 1) * LANES] * scale
        zero = jnp.zeros_like(qp)
        qa = jnp.concatenate([jnp.where(first, qp, zero), pick_a], axis=1)
        qb = jnp.concatenate([jnp.where(first, zero, qp), pick_b], axis=1)
        q_scr[p] = jnp.concatenate([qa, qb], axis=0)
    ones_v = jnp.ones((tk, LANES), BF16)

    def score(j, p):
        ks = pl.multiple_of(j * tk, tk)
        psl = slice(p * LANES, (p + 1) * LANES)
        kx = jnp.concatenate([k_ref[pl.ds(ks, tk), psl], ck_ref[pl.ds(ks, tk), psl]], axis=1)
        return _dot_nt(q_scr[p], kx)

    def values(j, p):
        ks = pl.multiple_of(j * tk, tk)
        return jnp.concatenate([v_ref[pl.ds(ks, tk), p * LANES:(p + 1) * LANES], ones_v], axis=1)

    def causal_mask(s, j):
        t_row = t0 + lax.broadcasted_iota(jnp.int32, (tq, tk), 0)
        pos = j * tk + lax.broadcasted_iota(jnp.int32, (tq, tk), 1)
        return jnp.where((pos <= t_row)[None], s.reshape(2, tq, tk), NEG_INF).reshape(2 * tq, tk)

    _flash_sweep(t0 // tk + 1, n_pair, score, values, causal_mask, s_scr, p_scr, m_scr, acc_scr)
    for p in range(n_pair):
        acc = acc_scr[p]
        oa = acc[:tq, :LANES] / acc[:tq, LANES:]
        ob = acc[tq:, :LANES] / acc[tq:, LANES:]
        o_ref[:, p * LANES:(p + 1) * LANES] = jnp.where(first, oa, ob).astype(BF16)


def _fox(big, ck, bsz, t_len, tq=256, tk=512):
    nq = t_len // tq
    n_pair = FOX_HEADS // 2
    seq = lambda c: pl.BlockSpec((t_len, FOX_WIDTH), lambda b, i: (b, c))
    return pl.pallas_call(
        functools.partial(_fox_kernel, tq=tq, tk=tk),
        grid=(bsz, nq),
        in_specs=[pl.BlockSpec((tq, FOX_WIDTH), lambda b, i: (b * nq + i, 0)), seq(1), seq(2), seq(0)],
        out_specs=pl.BlockSpec((tq, FOX_WIDTH), lambda b, i: (b * nq + i, 0)),
        out_shape=jax.ShapeDtypeStruct((bsz * t_len, FOX_WIDTH), BF16),
        scratch_shapes=[pltpu.VMEM((n_pair, 2 * tq, 2 * LANES), BF16),
                        pltpu.VMEM((2, n_pair, 2 * tq, tk), F32),
                        pltpu.VMEM((2, n_pair, 2 * tq, tk), BF16),
                        pltpu.VMEM((n_pair, 2 * tq, 1), F32),
                        pltpu.VMEM((n_pair, 2 * tq, 2 * LANES), F32)],
        compiler_params=_params(("parallel", "arbitrary")),
        name="fox",
    )(big, big, big, ck)


def _pool_kernel(u_ref, halo_ref, w_ref, scale_ref, o_ref, x_scr, *, tt):
    i = pl.program_id(1)
    u = u_ref[...]
    x_scr[POOL_HALO:, :] = u
    x_scr[:POOL_HALO, :] = jnp.where(i > 0, halo_ref[...], 0.0)
    t_pos = i * tt + lax.broadcasted_iota(jnp.int32, (tt, 1), 0)
    lane = lax.broadcasted_iota(jnp.int32, (tt, POOL_WIDTH), 1)
    run = u
    pooled = jnp.zeros_like(u)
    for back in range(1, max(POOL_WINDOWS)):
        run = run + x_scr[POOL_HALO - back:POOL_HALO - back + tt, :]
        w = back + 1
        if w in POOL_WINDOWS:
            gi = POOL_WINDOWS.index(w)
            count = jnp.minimum(t_pos + 1, w).astype(F32)
            in_group = (lane >= gi * POOL_GROUP_DIM) & (lane < (gi + 1) * POOL_GROUP_DIM)
            pooled = jnp.where(in_group, run / count - u, pooled)
    mixed = _dot(pooled.astype(BF16), w_ref[...])
    o_ref[...] = (mixed * scale_ref[...]).astype(BF16)


def _pool(small, w_bd, scale, bsz, t_len, tt=512):
    nt = t_len // tt
    per = tt // POOL_HALO
    return pl.pallas_call(
        functools.partial(_pool_kernel, tt=tt),
        grid=(bsz, nt),
        in_specs=[pl.BlockSpec((tt, POOL_WIDTH), lambda b, i: (b * nt + i, 0)),
                  pl.BlockSpec((POOL_HALO, POOL_WIDTH), lambda b, i: (jnp.maximum((b * nt + i) * per - 1, 0), 0)),
                  pl.BlockSpec((POOL_WIDTH, POOL_WIDTH), lambda b, i: (0, 0)),
                  pl.BlockSpec((1, POOL_WIDTH), lambda b, i: (0, 0))],
        out_specs=pl.BlockSpec((tt, POOL_WIDTH), lambda b, i: (b * nt + i, 0)),
        out_shape=jax.ShapeDtypeStruct((bsz * t_len, POOL_WIDTH), BF16),
        scratch_shapes=[pltpu.VMEM((tt + POOL_HALO, POOL_WIDTH), F32)],
        compiler_params=_params(("parallel", "arbitrary")),
        name="pool",
    )(small, small, w_bd, scale)


def _compress_kernel(xk_ref, xv_ref, pek_ref, pev_ref, w1k_ref, w2k_ref, w1v_ref, w2v_ref, kc_ref, vc_ref):
    def one(x_ref, pe_ref, w1_ref, w2_ref, out_ref):
        n_chunk = x_ref.shape[0] // CMP_STRIDE
        first = jnp.zeros((n_chunk, w1_ref.shape[2]), F32)
        second = jnp.zeros((n_chunk, w1_ref.shape[2]), F32)
        for r in range(CMP_STRIDE):
            x_r = x_ref[pl.ds(r, n_chunk, stride=CMP_STRIDE), :]
            first = first + _dot((x_r + pe_ref[r:r + 1, :]).astype(BF16), w1_ref[r])
            second = second + _dot((x_r + pe_ref[CMP_STRIDE + r:CMP_STRIDE + r + 1, :]).astype(BF16),
                                   w1_ref[CMP_STRIDE + r])
        hidden = first + pltpu.roll(second, n_chunk - 1, axis=0)
        out_ref[0] = _dot(jax.nn.gelu(hidden).astype(BF16), w2_ref[...])

    one(xk_ref, pek_ref, w1k_ref, w2k_ref, kc_ref)
    one(xv_ref, pev_ref, w1v_ref, w2v_ref, vc_ref)


def _compress(small, pe_k, pe_v, w1_k, w2_k, w1_v, w2_v, bsz, t_len):
    n_chunk = t_len // CMP_STRIDE
    full = lambda a: pl.BlockSpec(a.shape, lambda b: tuple(0 for _ in a.shape))
    x_spec = lambda col: pl.BlockSpec((t_len, KV_WIDTH), lambda b: (b, col // KV_WIDTH))
    o_spec = pl.BlockSpec((1, n_chunk, KV_WIDTH), lambda b: (b, 0, 0))
    return pl.pallas_call(
        _compress_kernel,
        grid=(bsz,),
        in_specs=[x_spec(COL_CMP), x_spec(COL_CMP + KV_WIDTH), full(pe_k), full(pe_v),
                  full(w1_k), full(w2_k), full(w1_v), full(w2_v)],
        out_specs=[o_spec, o_spec],
        out_shape=[jax.ShapeDtypeStruct((bsz, n_chunk, KV_WIDTH), F32)] * 2,
        compiler_params=_params(("parallel",)),
        name="compress",
    )(small, small, pe_k, pe_v, w1_k, w2_k, w1_v, w2_v)


def _compress_weights(pe, w1, w2):
    eye = jnp.eye(NSA_KV_HEADS, dtype=w1.dtype)
    w1_r = w1.reshape(CMP_LEN, HEAD_DIM, CMP_HIDDEN)
    w1_bd = jnp.einsum("gh,rdc->rgdhc", eye, w1_r).reshape(CMP_LEN, KV_WIDTH, NSA_KV_HEADS * CMP_HIDDEN)
    w2_bd = jnp.einsum("gh,cd->gchd", eye, w2).reshape(NSA_KV_HEADS * CMP_HIDDEN, KV_WIDTH)
    return jnp.tile(pe, (1, NSA_KV_HEADS)), w1_bd.astype(BF16), w2_bd.astype(BF16)


def _nsa_kernel(q_ref, ksl_ref, vsl_ref, kwn_ref, vwn_ref, kpos_ref, kc_ref, vc_ref, cpos_ref, ovt_ref, slope_ref,
                sm_ref, gb_ref, o_ref, qs_scr, s_scr, p_scr, m_scr, acc_scr, *, tq, tks):
    i = pl.program_id(1)
    t0 = i * tq
    n_head = NSA_HEADS
    rows = n_head * tq
    n_cmp = kc_ref.shape[1]
    n_blk = ovt_ref.shape[0]
    win_keys = WINDOW + tq
    lane = lax.broadcasted_iota(jnp.int32, (tq, LANES), 1)
    first = lane < HEAD_DIM
    scale = jnp.asarray(HEAD_DIM ** -0.5, BF16)

    pairs = [q_ref[:, r * LANES:(r + 1) * LANES] * scale for r in range(NSA_REP)]
    zero = jnp.zeros_like(pairs[0])
    q_heads, q_slopes = [], []
    for g in range(NSA_KV_HEADS):
        for r in range(NSA_REP):
            q_heads.append(jnp.where(first, pairs[r], zero) if g == 0 else jnp.where(first, zero, pairs[r]))
            q_slopes.append(jnp.broadcast_to(slope_ref[g * NSA_REP + r:g * NSA_REP + r + 1, :], (tq, LANES)))
    q_main = jnp.concatenate(q_heads, axis=0)
    q_plain = jnp.concatenate([q_main, jnp.concatenate(q_slopes, axis=0).astype(BF16)], axis=1)

    n_part = s_scr.shape[1]
    part_rows = rows // n_part

    def per_head(mask, s, fill):
        return jnp.where(mask[None], s.reshape(s.shape[0] // tq, tq, s.shape[-1]), fill).reshape(s.shape)

    t_col = t0 + lax.broadcasted_iota(jnp.int32, (tq, n_cmp), 0)
    c_end = lax.broadcasted_iota(jnp.int32, (tq, n_cmp), 1) * CMP_STRIDE + (CMP_LEN - 1)
    valid_c = c_end <= t_col
    kc_x = jnp.concatenate([kc_ref[0].astype(BF16), cpos_ref[...]], axis=1)
    vc_b = vc_ref[0].astype(BF16)
    p_parts, o_parts = [], []
    for part in range(n_part):
        rsl = slice(part * part_rows, (part + 1) * part_rows)
        s = per_head(valid_c, _dot_nt(q_plain[rsl], kc_x), NEG_INF)
        p = per_head(valid_c, jnp.exp(s - jnp.max(s, axis=1, keepdims=True)), 0.0)
        l_c = jnp.sum(p, axis=1, keepdims=True)
        p = p * jnp.where(l_c > 0.0, 1.0 / l_c, 0.0)
        p_parts.append(p)
        o_parts.append(_dot(p.astype(BF16), vc_b))
    p_c = jnp.concatenate(p_parts, axis=0)
    o_cmp = jnp.concatenate(o_parts, axis=0)

    blk_t = lax.broadcasted_iota(jnp.int32, (n_blk, tq), 0)
    cur_t = (t0 + lax.broadcasted_iota(jnp.int32, (n_blk, tq), 1)) // SEL_BLOCK
    forced_t = (blk_t == 0) | (blk_t == cur_t) | (blk_t == cur_t - 1)
    sub = lax.broadcasted_iota(jnp.int32, (SUBLANES, tq), 0)
    n_top = float(min(SEL_TOPN, n_blk))
    for g in range(NSA_KV_HEADS):
        p_sum = p_c[g * NSA_REP * tq:(g * NSA_REP + 1) * tq]
        for r in range(1, NSA_REP):
            p_sum = p_sum + p_c[(g * NSA_REP + r) * tq:(g * NSA_REP + r + 1) * tq]
        p0, p1, p2 = _split3(p_sum)
        ovt = ovt_ref[...]
        imp_t = _dot_nt(ovt, p0) + _dot_nt(ovt, p1) + _dot_nt(ovt, p2)
        score = jnp.where(forced_t, FORCE_SCORE, jnp.where(blk_t <= cur_t, imp_t, -1.0))
        chunks = [score[c * SUBLANES:(c + 1) * SUBLANES] for c in range(n_blk // SUBLANES)]
        ranks = [jnp.zeros((SUBLANES, tq), F32) for _ in chunks]
        for jp in range(n_blk):
            other = jnp.broadcast_to(score[jp:jp + 1], (SUBLANES, tq))
            for c, chunk in enumerate(chunks):
                ge = jnp.where(other >= chunk, 1.0, 0.0)
                gt = jnp.where(other > chunk, 1.0, 0.0)
                if jp < c * SUBLANES:
                    ranks[c] = ranks[c] + ge
                elif jp >= (c + 1) * SUBLANES:
                    ranks[c] = ranks[c] + gt
                else:
                    ranks[c] = ranks[c] + jnp.where(sub > jp - c * SUBLANES, ge, gt)
        rank = jnp.concatenate(ranks, axis=0)
        neg_t = jnp.where(rank < n_top, 0.0, NEG_INF)
        neg = jnp.transpose(jnp.concatenate([neg_t, jnp.zeros((LANES - n_blk, tq), F32)], axis=0))
        for r in range(NSA_REP):
            k = g * NSA_REP + r
            q_const = (neg + q_slopes[k]).astype(BF16)
            qs_scr[k * tq:(k + 1) * tq, :] = jnp.concatenate([q_heads[k], q_const], axis=1)

    ones_s = jnp.ones((tks, LANES), BF16)

    def slc_score(j, part):
        ks = pl.multiple_of(j * tks, tks)
        kx = jnp.concatenate([ksl_ref[pl.ds(ks, tks), :], kpos_ref[pl.ds(ks, tks), :]], axis=1)
        return _dot_nt(qs_scr[part * part_rows:(part + 1) * part_rows, :], kx)

    def slc_values(j, part):
        ks = pl.multiple_of(j * tks, tks)
        return jnp.concatenate([vsl_ref[pl.ds(ks, tks), :], ones_s], axis=1)

    def slc_mask(s, j):
        t_row = t0 + lax.broadcasted_iota(jnp.int32, (tq, tks), 0)
        pos = j * tks + lax.broadcasted_iota(jnp.int32, (tq, tks), 1)
        return jnp.where((pos <= t_row)[None], s.reshape(part_rows // tq, tq, tks), NEG_INF).reshape(part_rows, tks)

    _flash_sweep(t0 // tks + 1, n_part, slc_score, slc_values, slc_mask, s_scr, p_scr, m_scr, acc_scr)
    acc = acc_scr[...].reshape(rows, 2 * LANES)
    o_slc = acc[:, :LANES] / acc[:, LANES:]

    start = pl.multiple_of(jnp.maximum(t0 - WINDOW, 0), tq)
    kx = jnp.concatenate([kwn_ref[pl.ds(start, win_keys), :], kpos_ref[pl.ds(start, win_keys), :]], axis=1)
    vx = jnp.concatenate([vwn_ref[pl.ds(start, win_keys), :], jnp.ones((win_keys, LANES), BF16)], axis=1)
    dist = (t0 + lax.broadcasted_iota(jnp.int32, (tq, win_keys), 0)
            - (start + lax.broadcasted_iota(jnp.int32, (tq, win_keys), 1)))
    in_window = (dist >= 0) & (dist < WINDOW)
    o_parts = []
    win_score = lambda part: _dot_nt(q_plain[part * part_rows:(part + 1) * part_rows], kx)
    s_next = win_score(0)
    for part in range(n_part):
        s = per_head(in_window, s_next, NEG_INF)
        if part + 1 < n_part:
            s_next = win_score(part + 1)
        prob = jnp.exp(s - jnp.max(s, axis=1, keepdims=True))
        acc = _dot(prob.astype(BF16), vx)
        o_parts.append(acc[:, :LANES] / acc[:, LANES:])
    o_win = jnp.concatenate(o_parts, axis=0)

    gates = jax.nn.sigmoid(sm_ref[:, COL_GL - COL_FA:COL_GL - COL_FA + NSA_HEADS * N_BRANCH] + gb_ref[...])
    for r in range(NSA_REP):
        halves = []
        for g in range(NSA_KV_HEADS):
            k = g * NSA_REP + r
            rsl = slice(k * tq, (k + 1) * tq)
            gate = lambda br: gates[:, k * N_BRANCH + br:k * N_BRANCH + br + 1]
            halves.append(gate(0) * o_cmp[rsl] + gate(1) * o_slc[rsl] + gate(2) * o_win[rsl])
        o_ref[:, r * LANES:(r + 1) * LANES] = jnp.where(first, halves[0], halves[1]).astype(BF16)


def _nsa(big, kpos, kc, vc, cpos, ovt, slope_tab, small, gate_b, bsz, t_len, tq=256, tks=512, n_part=3):
    nq = t_len // tq
    kv_col = COL_KV // KV_WIDTH
    kv_spec = lambda c: pl.BlockSpec((t_len, KV_WIDTH), lambda b, i: (b, kv_col + c))
    full = lambda a: pl.BlockSpec(a.shape, lambda b, i: tuple(0 for _ in a.shape))
    n_cmp = kc.shape[1]
    rows = NSA_HEADS * tq
    return pl.pallas_call(
        functools.partial(_nsa_kernel, tq=tq, tks=tks),
        grid=(bsz, nq),
        in_specs=[pl.BlockSpec((tq, NSA_WIDTH), lambda b, i: (b * nq + i, COL_QC // NSA_WIDTH)),
                  kv_spec(0), kv_spec(1), kv_spec(2), kv_spec(3), full(kpos),
                  pl.BlockSpec((1, n_cmp, KV_WIDTH), lambda b, i: (b, 0, 0)),
                  pl.BlockSpec((1, n_cmp, KV_WIDTH), lambda b, i: (b, 0, 0)),
                  full(cpos), full(ovt), full(slope_tab),
                  pl.BlockSpec((tq, LANES), lambda b, i: (b * nq + i, COL_FA // LANES)),
                  full(gate_b)],
        out_specs=pl.BlockSpec((tq, NSA_WIDTH), lambda b, i: (b * nq + i, 0)),
        out_shape=jax.ShapeDtypeStruct((bsz * t_len, NSA_WIDTH), BF16),
        scratch_shapes=[pltpu.VMEM((rows, 2 * LANES), BF16),
                        pltpu.VMEM((2, n_part, rows // n_part, tks), F32),
                        pltpu.VMEM((2, n_part, rows // n_part, tks), BF16),
                        pltpu.VMEM((n_part, rows // n_part, 1), F32),
                        pltpu.VMEM((n_part, rows // n_part, 2 * LANES), F32)],
        compiler_params=_params(("parallel", "arbitrary")),
        name="nsa",
    )(big, big, big, big, big, kpos, kc, vc, cpos, ovt, slope_tab, small, gate_b)


def _position_columns(pos):
    lane = jnp.arange(LANES)[None, :]
    hi = (SEL_BLOCK * (pos // SEL_BLOCK))[:, None]
    lo = (pos % SEL_BLOCK)[:, None]
    cols = jnp.where((lane >= COL_POS) & (lane < COL_POS + 3), hi, 0) + jnp.where((lane >= COL_POS + 3) & (lane < COL_POS + 6), lo, 0)
    return cols.astype(BF16)


def _nsa_tables(t_len):
    pos = jnp.arange(t_len)
    lane = jnp.arange(LANES)[None, :]
    block_one_hot = ((lane < t_len // SEL_BLOCK) & (lane == (pos // SEL_BLOCK)[:, None])).astype(BF16)
    kpos = _position_columns(pos) + block_one_hot
    n_cmp = t_len // CMP_STRIDE
    c_start = jnp.arange(n_cmp) * CMP_STRIDE
    cpos = _position_columns(c_start + CMP_LEN - 1)
    s_start = (jnp.arange(t_len // SEL_BLOCK) * SEL_BLOCK)[:, None]
    ovt = ((c_start[None, :] < s_start + SEL_BLOCK) & (c_start[None, :] + CMP_LEN - 1 >= s_start)).astype(BF16)
    slopes = jnp.exp2(-8.0 * jnp.arange(1, NSA_HEADS + 1, dtype=F32) / NSA_HEADS)
    pieces = jnp.stack([piece.astype(F32) for piece in _split3(slopes)], axis=1)
    slope_tab = jnp.zeros((NSA_HEADS, LANES), F32)
    slope_tab = slope_tab.at[:, COL_POS:COL_POS + 3].set(pieces).at[:, COL_POS + 3:COL_POS + 6].set(pieces)
    return kpos, cpos, ovt, slope_tab


def _mem_kv_kernel(mem_ref, g_ref, wk_ref, wv_ref, k_ref, v_ref):
    mb = _rms(mem_ref[0], g_ref[...]).astype(BF16)
    k_ref[0] = _dot(mb, wk_ref[...]).astype(BF16)
    v_ref[0] = _dot(mb, wv_ref[...]).astype(BF16)


def _mem_kv(mem, g, wk, wv):
    bsz, m_len, _ = mem.shape
    w_spec = pl.BlockSpec((D_MODEL, D_MODEL), lambda b: (0, 0))
    m_spec = pl.BlockSpec((1, m_len, D_MODEL), lambda b: (b, 0, 0))
    return pl.pallas_call(
        _mem_kv_kernel,
        grid=(bsz,),
        in_specs=[m_spec, pl.BlockSpec((1, D_MODEL), lambda b: (0, 0)), w_spec, w_spec],
        out_specs=[m_spec, m_spec],
        out_shape=[jax.ShapeDtypeStruct((bsz, m_len, D_MODEL), BF16)] * 2,
        compiler_params=_params(("parallel",)),
        name="mem_kv",
    )(mem, g, wk, wv)


def _route(logits):
    lane = lax.broadcasted_iota(jnp.int32, logits.shape, 1)
    big_i = jnp.int32(1 << 20)
    is_coarse = lane < N_GROUPS
    lc = jnp.where(is_coarse, logits, NEG_INF)
    mc = jnp.max(lc, axis=1, keepdims=True)
    grp = jnp.min(jnp.where(is_coarse & (lc == mc), lane, big_i), axis=1, keepdims=True)
    w_grp = 1.0 / jnp.sum(jnp.where(is_coarse, jnp.exp(lc - mc), 0.0), axis=1, keepdims=True)
    e_id = lane - N_GROUPS
    in_grp = (e_id >= grp * EXPERTS_PER_GROUP) & (e_id < (grp + 1) * EXPERTS_PER_GROUP)
    lf = jnp.where(in_grp, logits, NEG_INF)
    m1 = jnp.max(lf, axis=1, keepdims=True)
    i1 = jnp.min(jnp.where(in_grp & (lf == m1), lane, big_i), axis=1, keepdims=True)
    lf2 = jnp.where(lane == i1, NEG_INF, lf)
    m2 = jnp.max(lf2, axis=1, keepdims=True)
    i2 = jnp.min(jnp.where(in_grp & (lane != i1) & (lf2 == m2), lane, big_i), axis=1, keepdims=True)
    denom = jnp.sum(jnp.where(in_grp, jnp.exp(lf - m1), 0.0), axis=1, keepdims=True)
    p1 = 1.0 / denom
    p2 = jnp.exp(m2 - m1) / denom
    tot = p1 + p2
    w1 = w_grp * (p1 / tot)
    w2 = w_grp * (p2 / tot)
    return jnp.where(lane + N_GROUPS == i1, w1, jnp.where(lane + N_GROUPS == i2, w2, 0.0))


def _mix_out_kernel(x_ref, oa_ref, ob_ref, oc_ref, woa_ref, wob_ref, woc_ref, gx_ref, wq_ref, mk_ref, mv_ref,
                    wo_ref, gf_ref, wr_ref, br_ref, x2_ref, hb_ref, comb_ref):
    n_part = 2
    rows = x_ref.shape[0] // n_part
    parts = [slice(k * rows, (k + 1) * rows) for k in range(n_part)]
    x1 = [x_ref[r, :] + _dot(oa_ref[r, :], woa_ref[...]) + _dot(ob_ref[r, :], wob_ref[...])
          + _dot(oc_ref[r, :], woc_ref[...]) for r in parts]
    hq = [_rms(v, gx_ref[...]).astype(BF16) for v in x1]
    q = [(_dot(v, wq_ref[...]) * (MEM_HEAD_DIM ** -0.5)).astype(BF16) for v in hq]
    heads = [[] for _ in parts]
    for h in range(MEM_HEADS):
        sl = slice(h * MEM_HEAD_DIM, (h + 1) * MEM_HEAD_DIM)
        for k in range(n_part):
            s = _dot_nt(q[k][:, sl], mk_ref[0, :, sl])
            p = jnp.exp(s - jnp.max(s, axis=1, keepdims=True))
            p = p / jnp.sum(p, axis=1, keepdims=True)
            heads[k].append(_dot(p.astype(BF16), mv_ref[0, :, sl]).astype(BF16))
    x2 = [x1[k] + _dot(jnp.concatenate(heads[k], axis=1), wo_ref[...]) for k in range(n_part)]
    hn = [_rms(v, gf_ref[...]) for v in x2]
    for k, r in enumerate(parts):
        x2_ref[r, :] = x2[k]
        hb_ref[r, :] = hn[k].astype(BF16)
        h0, h1, _ = _split3(hn[k])
        partial = _dot(h0, wr_ref[...]) + _dot(h1, wr_ref[...])
        logits = partial + pltpu.roll(partial, ROUTER_WIDTH - ROUTER_PIECE, axis=1) + br_ref[...]
        comb_ref[r, :] = _route(logits)


def _mix_out(x, o_a, o_b, o_c, wo_a, wo_b, wo_c, g_x, wq, mem_k, mem_v, wo, g_f, w_r, b_r, t_len, tm=1024):
    n = x.shape[0]
    per_batch = t_len // tm
    m_len = mem_k.shape[1]
    tile = lambda width: pl.BlockSpec((tm, width), lambda i: (i, 0))
    full = lambda shape: pl.BlockSpec(shape, lambda i: tuple(0 for _ in shape))
    mem_spec = pl.BlockSpec((1, m_len, D_MODEL), lambda i: (i // per_batch, 0, 0))
    return pl.pallas_call(
        _mix_out_kernel,
        grid=(n // tm,),
        in_specs=[tile(D_MODEL), tile(FOX_WIDTH), tile(POOL_WIDTH), tile(NSA_WIDTH),
                  full(wo_a.shape), full(wo_b.shape), full(wo_c.shape), full(g_x.shape), full(wq.shape),
                  mem_spec, mem_spec, full(wo.shape), full(g_f.shape), full(w_r.shape), full(b_r.shape)],
        out_specs=[tile(D_MODEL), tile(D_MODEL), tile(ROUTER_WIDTH)],
        out_shape=[jax.ShapeDtypeStruct((n, D_MODEL), F32),
                   jax.ShapeDtypeStruct((n, D_MODEL), BF16),
                   jax.ShapeDtypeStruct((n, ROUTER_WIDTH), F32)],
        compiler_params=_params(("parallel",)),
        name="mix_out",
    )(x, o_a, o_b, o_c, wo_a, wo_b, wo_c, g_x, wq, mem_k, mem_v, wo, g_f, w_r, b_r)


def _moe_kernel(x_ref, hb_ref, comb_ref, wg_ref, wu_ref, wd_ref, gfin_ref, o_ref, *, final_norm):
    e = pl.program_id(1)

    @pl.when(e == 0)
    def _():
        o_ref[...] = x_ref[...]

    hb = hb_ref[...]
    gate = _dot(hb, wg_ref[0, 0].astype(BF16))
    up = _dot(hb, wu_ref[0, 0].astype(BF16))
    he = (jax.nn.silu(gate) * up).astype(BF16)
    lane = lax.broadcasted_iota(jnp.int32, comb_ref.shape, 1)
    c_e = jnp.sum(jnp.where(lane == e, comb_ref[...], 0.0), axis=1, keepdims=True)
    o_ref[...] += c_e * _dot(he, wd_ref[0, 0].astype(BF16))

    if final_norm:
        @pl.when(e == pl.num_programs(1) - 1)
        def _():
            o_ref[...] = _rms(o_ref[...], gfin_ref[...])


def _moe(x2, hb, comb, wg, wu, wd, layer, g_fin, final_norm, tm=1024):
    n = x2.shape[0]
    n_exp = wg.shape[1]
    tile = lambda width: pl.BlockSpec((tm, width), lambda i, e: (i, 0))
    expert = lambda shape: pl.BlockSpec((1, 1) + shape, lambda i, e: (layer, e, 0, 0))
    return pl.pallas_call(
        functools.partial(_moe_kernel, final_norm=final_norm),
        grid=(n // tm, n_exp),
        in_specs=[tile(D_MODEL), tile(D_MODEL), tile(ROUTER_WIDTH),
                  expert((D_MODEL, EXPERT_FF)), expert((D_MODEL, EXPERT_FF)), expert((EXPERT_FF, D_MODEL)),
                  pl.BlockSpec((1, D_MODEL), lambda i, e: (0, 0))],
        out_specs=tile(D_MODEL),
        out_shape=jax.ShapeDtypeStruct((n, D_MODEL), F32),
        compiler_params=_params(("parallel", "arbitrary")),
        name="moe",
    )(x2, hb, comb, wg, wu, wd, g_fin)


def _pair_order(a, axis):
    shape = a.shape
    a = a.reshape(shape[:axis] + (NSA_HEADS, HEAD_DIM) + shape[axis + 1:])
    a = jnp.take(a, jnp.array(NSA_PAIR_ORDER), axis=axis)
    return a.reshape(shape)


def _split_in_proj_weight(w):
    edges = [0]
    for width in (FOX_WIDTH, FOX_WIDTH, FOX_WIDTH, FOX_HEADS, POOL_WIDTH, NSA_WIDTH) + (KV_WIDTH,) * 6 + (NSA_HEADS * N_BRANCH,):
        edges.append(edges[-1] + width)
    part = lambda k: w[:, edges[k]:edges[k + 1]]
    qa, ka, va, fa, ub, qc = (part(k) for k in range(6))
    kvs = [part(k) for k in range(6, 12)]
    gl = part(12)
    w_big = jnp.concatenate([qa, ka, va, _pair_order(qc, 1)] + kvs[2:], axis=1).astype(BF16)
    pad = jnp.zeros((w.shape[0], LANES - FOX_HEADS - NSA_HEADS * N_BRANCH), w.dtype)
    w_small = jnp.concatenate([ub, fa, gl, pad] + kvs[:2], axis=1).astype(BF16)
    return w_big, w_small


def _block_diag(w):
    groups, cin, cout = w.shape
    out = jnp.zeros((groups * cin, groups * cout), w.dtype)
    for gi in range(groups):
        out = out.at[gi * cin:(gi + 1) * cin, gi * cout:(gi + 1) * cout].set(w[gi])
    return out


def _layer(x, mem, p, experts, layer, bsz, t_len, final_g, final_norm):
    row = lambda v: v.reshape(1, -1)
    w_big, w_small = _split_in_proj_weight(p["w_in"])
    big, small = _in_proj(x, row(p["mix_norm_g"]), w_big, w_small)

    b_row = jnp.pad(p["fox_forget_b"], (0, LANES - FOX_HEADS)).reshape(1, LANES)
    ck = _forget_cumsum(small, b_row, _forget_placement(), bsz, t_len)
    o_a = _fox(big, ck, bsz, t_len)

    o_b = _pool(small, _block_diag(p["pool_w"]).astype(BF16), row(p["pool_scale"]), bsz, t_len)

    pe_k, w1_k, w2_k = _compress_weights(p["cmp_pe_k"], p["cmp_w1_k"], p["cmp_w2_k"])
    pe_v, w1_v, w2_v = _compress_weights(p["cmp_pe_v"], p["cmp_w1_v"], p["cmp_w2_v"])
    kc, vc = _compress(small, pe_k, pe_v, w1_k, w2_k, w1_v, w2_v, bsz, t_len)
    kpos, cpos, ovt, slope_tab = _nsa_tables(t_len)
    o_c = _nsa(big, kpos, kc, vc, cpos, ovt, slope_tab, small, row(p["nsa_gate_b"]), bsz, t_len)

    mem_k, mem_v = _mem_kv(mem, row(p["mem_norm_g"]), p["xattn_wk"].astype(BF16), p["xattn_wv"].astype(BF16))
    w_out = p["w_out"]
    wo_a = w_out[:FOX_WIDTH].astype(BF16)
    wo_b = w_out[FOX_WIDTH:FOX_WIDTH + POOL_WIDTH].astype(BF16)
    wo_c = _pair_order(w_out[FOX_WIDTH + POOL_WIDTH:], 0).astype(BF16)
    w_r = jnp.concatenate([p["router_coarse_w"], p["router_fine_w"],
                           jnp.zeros((D_MODEL, ROUTER_PIECE - N_GROUPS - N_EXPERTS), F32)], axis=1)
    w_r0, w_r1, _ = _split3(w_r)
    w_r = jnp.concatenate([w_r0, w_r1, jnp.zeros((D_MODEL, ROUTER_WIDTH - 2 * ROUTER_PIECE), BF16)], axis=1)
    b_r = jnp.concatenate([p["router_coarse_b"], p["router_fine_b"],
                           jnp.zeros((ROUTER_WIDTH - N_GROUPS - N_EXPERTS,), F32)]).reshape(1, -1)
    x2, hb, comb = _mix_out(x, o_a, o_b, o_c, wo_a, wo_b, wo_c, row(p["xattn_norm_g"]), p["xattn_wq"].astype(BF16),
                            mem_k, mem_v, p["xattn_wo"].astype(BF16), row(p["ffn_norm_g"]),
                            w_r, b_r, t_len)

    return _moe(x2, hb, comb, *experts, layer, row(final_g), final_norm)


def kernel(x, mem, mix_norm_g, w_in, fox_forget_b, pool_w, pool_scale, cmp_pe_k, cmp_pe_v, cmp_w1_k, cmp_w2_k, cmp_w1_v, cmp_w2_v, nsa_gate_b, w_out, xattn_norm_g, mem_norm_g, xattn_wq, xattn_wk, xattn_wv, xattn_wo, ffn_norm_g, router_coarse_w, router_coarse_b, router_fine_w, router_fine_b, exp_w_gate, exp_w_up, exp_w_down, final_norm_g):
    stacked = dict(mix_norm_g=mix_norm_g, w_in=w_in, fox_forget_b=fox_forget_b, pool_w=pool_w, pool_scale=pool_scale,
                   cmp_pe_k=cmp_pe_k, cmp_pe_v=cmp_pe_v, cmp_w1_k=cmp_w1_k, cmp_w2_k=cmp_w2_k, cmp_w1_v=cmp_w1_v,
                   cmp_w2_v=cmp_w2_v, nsa_gate_b=nsa_gate_b, w_out=w_out, xattn_norm_g=xattn_norm_g,
                   mem_norm_g=mem_norm_g, xattn_wq=xattn_wq, xattn_wk=xattn_wk, xattn_wv=xattn_wv, xattn_wo=xattn_wo,
                   ffn_norm_g=ffn_norm_g, router_coarse_w=router_coarse_w, router_coarse_b=router_coarse_b,
                   router_fine_w=router_fine_w, router_fine_b=router_fine_b)
    experts = (exp_w_gate, exp_w_up, exp_w_down)
    bsz, t_len, d = x.shape
    depth = w_in.shape[0]
    h = x.reshape(bsz * t_len, d)
    for layer in range(depth):
        p = {name: value[layer] for name, value in stacked.items()}
        h = _layer(h, mem, p, experts, layer, bsz, t_len, final_norm_g, final_norm=(layer == depth - 1))
    return h.reshape(bsz, t_len, d)
```

```python
import functools

import jax
import jax.numpy as jnp
from jax import lax
from jax.experimental import pallas as pl
from jax.experimental.pallas import tpu as pltpu

F32 = jnp.float32
BF16 = jnp.bfloat16

D_MODEL = 1024
HEAD_DIM = 64
FOX_HEADS = 6
FOX_WIDTH = FOX_HEADS * HEAD_DIM
POOL_WIDTH = 256
POOL_GROUP_DIM = 64
POOL_WINDOWS = (2, 4, 8, 16)
POOL_HALO = 16
NSA_HEADS = 6
NSA_KV_HEADS = 2
NSA_REP = NSA_HEADS // NSA_KV_HEADS
NSA_WIDTH = NSA_HEADS * HEAD_DIM
KV_WIDTH = NSA_KV_HEADS * HEAD_DIM
N_BRANCH = 3
CMP_LEN = 32
CMP_STRIDE = 16
CMP_HIDDEN = 256
SEL_BLOCK = 64
SEL_TOPN = 16
WINDOW = 512
FORCE_SCORE = 1.0e4
MEM_HEADS = 4
MEM_HEAD_DIM = D_MODEL // MEM_HEADS
N_GROUPS = 4
EXPERTS_PER_GROUP = 4
N_EXPERTS = N_GROUPS * EXPERTS_PER_GROUP
EXPERT_FF = 512
EPS = 1e-6
NEG_INF = -1e30

LANES = 128
SUBLANES = 8
BIG_WIDTH = 3 * FOX_WIDTH + NSA_WIDTH + 4 * KV_WIDTH
COL_QC = 3 * FOX_WIDTH
COL_KV = COL_QC + NSA_WIDTH
SMALL_WIDTH = POOL_WIDTH + LANES + 2 * KV_WIDTH
COL_FA = POOL_WIDTH
COL_GL = POOL_WIDTH + FOX_HEADS
COL_CMP = POOL_WIDTH + LANES
ROUTER_WIDTH = LANES
ROUTER_PIECE = 32
MOE_TILE = 256
MOE_SEG_ALIGN = 16
MOE_FFN_TILE = 512
MOE_GAP = 512
NSA_PAIR_ORDER = tuple(g * NSA_REP + r for r in range(NSA_REP) for g in range(NSA_KV_HEADS))
COL_POS = HEAD_DIM

VMEM_LIMIT = 56 * 1024 * 1024

_NT = (((1,), (1,)), ((), ()))


def _params(semantics):
    return pltpu.CompilerParams(dimension_semantics=semantics, vmem_limit_bytes=VMEM_LIMIT)


def _rms(x, g):
    return x * lax.rsqrt(jnp.mean(x * x, axis=-1, keepdims=True) + EPS) * g


def _dot(a, b):
    return jnp.dot(a, b, preferred_element_type=F32)


def _dot_nt(a, b):
    return lax.dot_general(a, b, _NT, preferred_element_type=F32)


def _split3(a):
    a0 = a.astype(BF16)
    r = a - a0.astype(F32)
    a1 = r.astype(BF16)
    a2 = (r - a1.astype(F32)).astype(BF16)
    return a0, a1, a2


def _in_proj_kernel(x_ref, g_ref, wb_ref, ws_ref, big_ref, small_ref):
    yb = _rms(x_ref[...], g_ref[...]).astype(BF16)
    big_ref[...] = _dot(yb, wb_ref[...]).astype(BF16)
    small_ref[...] = _dot(yb, ws_ref[...])


def _in_proj(x, g, w_big, w_small, tm=512):
    n = x.shape[0]
    return pl.pallas_call(
        _in_proj_kernel,
        grid=(n // tm,),
        in_specs=[pl.BlockSpec((tm, D_MODEL), lambda i: (i, 0)),
                  pl.BlockSpec((1, D_MODEL), lambda i: (0, 0)),
                  pl.BlockSpec((D_MODEL, BIG_WIDTH), lambda i: (0, 0)),
                  pl.BlockSpec((D_MODEL, SMALL_WIDTH), lambda i: (0, 0))],
        out_specs=[pl.BlockSpec((tm, BIG_WIDTH), lambda i: (i, 0)),
                   pl.BlockSpec((tm, SMALL_WIDTH), lambda i: (i, 0))],
        out_shape=[jax.ShapeDtypeStruct((n, BIG_WIDTH), BF16),
                   jax.ShapeDtypeStruct((n, SMALL_WIDTH), F32)],
        compiler_params=_params(("parallel",)),
        name="in_proj",
    )(x, g, w_big, w_small)


def _forget_kernel(f_ref, b_ref, place_ref, ck_ref):
    c = jax.nn.log_sigmoid(f_ref[...] + b_ref[...])
    t_len = c.shape[0]
    row = lax.broadcasted_iota(jnp.int32, c.shape, 0)
    sh = 1
    while sh < t_len:
        c = c + jnp.where(row >= sh, pltpu.roll(c, sh, axis=0), 0.0)
        sh *= 2
    n0, n1, n2 = _split3(-c)
    ck_ref[...] = (_dot(n0, place_ref[0]) + _dot(n1, place_ref[1]) + _dot(n2, place_ref[2])).astype(BF16)


def _forget_cumsum(small, b_row, place, bsz, t_len):
    return pl.pallas_call(
        _forget_kernel,
        grid=(bsz,),
        in_specs=[pl.BlockSpec((t_len, LANES), lambda b: (b, COL_FA // LANES)),
                  pl.BlockSpec((1, LANES), lambda b: (0, 0)),
                  pl.BlockSpec(place.shape, lambda b: (0, 0, 0))],
        out_specs=pl.BlockSpec((t_len, FOX_WIDTH), lambda b: (b, 0)),
        out_shape=jax.ShapeDtypeStruct((bsz * t_len, FOX_WIDTH), BF16),
        compiler_params=_params(("parallel",)),
        name="forget",
    )(small, b_row, place)


def _forget_placement():
    src = jnp.arange(LANES)[None, :, None]
    dst = jnp.arange(FOX_WIDTH)[None, None, :]
    piece = jnp.arange(3)[:, None, None]
    target = LANES * (src // 2) + 3 * (src % 2) + piece
    return ((src < FOX_HEADS) & (dst == target)).astype(BF16)


def _flash_sweep(n_tiles, n_stream, score, values, last_mask, s_scr, p_scr, m_scr, acc_scr):
    m_scr[...] = jnp.full(m_scr.shape, NEG_INF, F32)
    acc_scr[...] = jnp.zeros(acc_scr.shape, F32)
    for st in range(n_stream):
        s_scr[0, st] = score(0, st)

    def run_stage(j, a, first=False, last=False):
        b = 1 - a
        for st in range(n_stream):
            s = s_scr[a, st]
            if last:
                s = last_mask(s, j)
            m_old = m_scr[st]
            m_new = jnp.maximum(m_old, jnp.max(s, axis=1, keepdims=True))
            p_scr[a, st] = jnp.exp(s - m_new).astype(BF16)
            if not last:
                s_scr[b, st] = score(j + 1, st)
            if not first:
                prev = _dot(p_scr[b, st], values(j - 1, st))
                acc_scr[st] = (acc_scr[st] + prev) * jnp.exp(m_old - m_new)
            m_scr[st] = m_new

    def finish(j, a):
        for st in range(n_stream):
            acc_scr[st] = acc_scr[st] + _dot(p_scr[a, st], values(j, st))

    n_main = n_tiles - 1

    @pl.when(n_main == 0)
    def _():
        run_stage(0, 0, first=True, last=True)
        finish(0, 0)

    @pl.when(n_main > 0)
    def _():
        run_stage(0, 0, first=True)

    def two_stages(jj, carry):
        run_stage(2 * jj + 1, 1)
        run_stage(2 * jj + 2, 0)
        return carry

    lax.fori_loop(0, jnp.maximum(n_main - 1, 0) // 2, two_stages, 0)

    @pl.when(jnp.logical_and(n_main >= 2, n_main % 2 == 0))
    def _():
        run_stage(n_main - 1, 1)

    for a in range(2):
        @pl.when(jnp.logical_and(n_main >= 1, n_main % 2 == a))
        def _():
            run_stage(n_main, a, last=True)
            finish(n_main, a)


def _fox_kernel(q_ref, k_ref, v_ref, ck_ref, o_ref, q_scr, s_scr, p_scr, m_scr, acc_scr, *, tq, tk):
    i = pl.program_id(1)
    t0 = i * tq
    n_pair = FOX_HEADS // 2
    lane = lax.broadcasted_iota(jnp.int32, (tq, LANES), 1)
    first = lane < HEAD_DIM
    scale = jnp.asarray(HEAD_DIM ** -0.5, BF16)
    pick_a = jnp.where(lane < 3, 1.0, 0.0).astype(BF16)
    pick_b = jnp.where((lane >= 3) & (lane < 6), 1.0, 0.0).astype(BF16)
    for p in range(n_pair):
        qp = q_ref[:, p * LANES:(p + 1) * LANES] * scale
        zero = jnp.zeros_like(qp)
        qa = jnp.concatenate([jnp.where(first, qp, zero), pick_a], axis=1)
        qb = jnp.concatenate([jnp.where(first, zero, qp), pick_b], axis=1)
        q_scr[p] = jnp.concatenate([qa, qb], axis=0)
    ones_v = jnp.ones((tk, LANES), BF16)

    def score(j, p):
        ks = pl.multiple_of(j * tk, tk)
        psl = slice(p * LANES, (p + 1) * LANES)
        kx = jnp.concatenate([k_ref[pl.ds(ks, tk), psl], ck_ref[pl.ds(ks, tk), psl]], axis=1)
        return _dot_nt(q_scr[p], kx)

    def values(j, p):
        ks = pl.multiple_of(j * tk, tk)
        return jnp.concatenate([v_ref[pl.ds(ks, tk), p * LANES:(p + 1) * LANES], ones_v], axis=1)

    def causal_mask(s, j):
        t_row = t0 + lax.broadcasted_iota(jnp.int32, (tq, tk), 0)
        pos = j * tk + lax.broadcasted_iota(jnp.int32, (tq, tk), 1)
        return jnp.where((pos <= t_row)[None], s.reshape(2, tq, tk), NEG_INF).reshape(2 * tq, tk)

    _flash_sweep(t0 // tk + 1, n_pair, score, values, causal_mask, s_scr, p_scr, m_scr, acc_scr)
    for p in range(n_pair):
        acc = acc_scr[p]
        oa = acc[:tq, :LANES] / acc[:tq, LANES:]
        ob = acc[tq:, :LANES] / acc[tq:, LANES:]
        o_ref[:, p * LANES:(p + 1) * LANES] = jnp.where(first, oa, ob).astype(BF16)


def _fox(big, ck, bsz, t_len, tq=256, tk=512):
    nq = t_len // tq
    n_pair = FOX_HEADS // 2
    seq = lambda c: pl.BlockSpec((t_len, FOX_WIDTH), lambda b, i: (b, c))
    return pl.pallas_call(
        functools.partial(_fox_kernel, tq=tq, tk=tk),
        grid=(bsz, nq),
        in_specs=[pl.BlockSpec((tq, FOX_WIDTH), lambda b, i: (b * nq + i, 0)), seq(1), seq(2), seq(0)],
        out_specs=pl.BlockSpec((tq, FOX_WIDTH), lambda b, i: (b * nq + i, 0)),
        out_shape=jax.ShapeDtypeStruct((bsz * t_len, FOX_WIDTH), BF16),
        scratch_shapes=[pltpu.VMEM((n_pair, 2 * tq, 2 * LANES), BF16),
                        pltpu.VMEM((2, n_pair, 2 * tq, tk), F32),
                        pltpu.VMEM((2, n_pair, 2 * tq, tk), BF16),
                        pltpu.VMEM((n_pair, 2 * tq, 1), F32),
                        pltpu.VMEM((n_pair, 2 * tq, 2 * LANES), F32)],
        compiler_params=_params(("parallel", "arbitrary")),
        name="fox",
    )(big, big, big, ck)


def _pool_kernel(u_ref, halo_ref, w_ref, scale_ref, o_ref, x_scr, *, tt):
    i = pl.program_id(1)
    u = u_ref[...]
    x_scr[POOL_HALO:, :] = u
    x_scr[:POOL_HALO, :] = jnp.where(i > 0, halo_ref[...], 0.0)
    t_pos = i * tt + lax.broadcasted_iota(jnp.int32, (tt, 1), 0)
    lane = lax.broadcasted_iota(jnp.int32, (tt, POOL_WIDTH), 1)
    run = u
    pooled = jnp.zeros_like(u)
    for back in range(1, max(POOL_WINDOWS)):
        run = run + x_scr[POOL_HALO - back:POOL_HALO - back + tt, :]
        w = back + 1
        if w in POOL_WINDOWS:
            gi = POOL_WINDOWS.index(w)
            count = jnp.minimum(t_pos + 1, w).astype(F32)
            in_group = (lane >= gi * POOL_GROUP_DIM) & (lane < (gi + 1) * POOL_GROUP_DIM)
            pooled = jnp.where(in_group, run / count - u, pooled)
    mixed = _dot(pooled.astype(BF16), w_ref[...])
    o_ref[...] = (mixed * scale_ref[...]).astype(BF16)


def _pool(small, w_bd, scale, bsz, t_len, tt=512):
    nt = t_len // tt
    per = tt // POOL_HALO
    return pl.pallas_call(
        functools.partial(_pool_kernel, tt=tt),
        grid=(bsz, nt),
        in_specs=[pl.BlockSpec((tt, POOL_WIDTH), lambda b, i: (b * nt + i, 0)),
                  pl.BlockSpec((POOL_HALO, POOL_WIDTH), lambda b, i: (jnp.maximum((b * nt + i) * per - 1, 0), 0)),
                  pl.BlockSpec((POOL_WIDTH, POOL_WIDTH), lambda b, i: (0, 0)),
                  pl.BlockSpec((1, POOL_WIDTH), lambda b, i: (0, 0))],
        out_specs=pl.BlockSpec((tt, POOL_WIDTH), lambda b, i: (b * nt + i, 0)),
        out_shape=jax.ShapeDtypeStruct((bsz * t_len, POOL_WIDTH), BF16),
        scratch_shapes=[pltpu.VMEM((tt + POOL_HALO, POOL_WIDTH), F32)],
        compiler_params=_params(("parallel", "arbitrary")),
        name="pool",
    )(small, small, w_bd, scale)


def _compress_kernel(xk_ref, xv_ref, pek_ref, pev_ref, w1k_ref, w2k_ref, w1v_ref, w2v_ref, kc_ref, vc_ref):
    def one(x_ref, pe_ref, w1_ref, w2_ref, out_ref):
        n_chunk = x_ref.shape[0] // CMP_STRIDE
        first = jnp.zeros((n_chunk, w1_ref.shape[2]), F32)
        second = jnp.zeros((n_chunk, w1_ref.shape[2]), F32)
        for r in range(CMP_STRIDE):
            x_r = x_ref[pl.ds(r, n_chunk, stride=CMP_STRIDE), :]
            first = first + _dot((x_r + pe_ref[r:r + 1, :]).astype(BF16), w1_ref[r])
            second = second + _dot((x_r + pe_ref[CMP_STRIDE + r:CMP_STRIDE + r + 1, :]).astype(BF16),
                                   w1_ref[CMP_STRIDE + r])
        hidden = first + pltpu.roll(second, n_chunk - 1, axis=0)
        out_ref[0] = _dot(jax.nn.gelu(hidden).astype(BF16), w2_ref[...])

    one(xk_ref, pek_ref, w1k_ref, w2k_ref, kc_ref)
    one(xv_ref, pev_ref, w1v_ref, w2v_ref, vc_ref)


def _compress(small, pe_k, pe_v, w1_k, w2_k, w1_v, w2_v, bsz, t_len):
    n_chunk = t_len // CMP_STRIDE
    full = lambda a: pl.BlockSpec(a.shape, lambda b: tuple(0 for _ in a.shape))
    x_spec = lambda col: pl.BlockSpec((t_len, KV_WIDTH), lambda b: (b, col // KV_WIDTH))
    o_spec = pl.BlockSpec((1, n_chunk, KV_WIDTH), lambda b: (b, 0, 0))
    return pl.pallas_call(
        _compress_kernel,
        grid=(bsz,),
        in_specs=[x_spec(COL_CMP), x_spec(COL_CMP + KV_WIDTH), full(pe_k), full(pe_v),
                  full(w1_k), full(w2_k), full(w1_v), full(w2_v)],
        out_specs=[o_spec, o_spec],
        out_shape=[jax.ShapeDtypeStruct((bsz, n_chunk, KV_WIDTH), F32)] * 2,
        compiler_params=_params(("parallel",)),
        name="compress",
    )(small, small, pe_k, pe_v, w1_k, w2_k, w1_v, w2_v)


def _compress_weights(pe, w1, w2):
    eye = jnp.eye(NSA_KV_HEADS, dtype=w1.dtype)
    w1_r = w1.reshape(CMP_LEN, HEAD_DIM, CMP_HIDDEN)
    w1_bd = jnp.einsum("gh,rdc->rgdhc", eye, w1_r).reshape(CMP_LEN, KV_WIDTH, NSA_KV_HEADS * CMP_HIDDEN)
    w2_bd = jnp.einsum("gh,cd->gchd", eye, w2).reshape(NSA_KV_HEADS * CMP_HIDDEN, KV_WIDTH)
    return jnp.tile(pe, (1, NSA_KV_HEADS)), w1_bd.astype(BF16), w2_bd.astype(BF16)


def _nsa_kernel(q_ref, ksl_ref, vsl_ref, kwn_ref, vwn_ref, kpos_ref, kc_ref, vc_ref, cpos_ref, ovt_ref, slope_ref,
                sm_ref, gb_ref, o_ref, qs_scr, s_scr, p_scr, m_scr, acc_scr, *, tq, tks):
    i = pl.program_id(1)
    t0 = i * tq
    n_head = NSA_HEADS
    rows = n_head * tq
    n_cmp = kc_ref.shape[1]
    n_blk = ovt_ref.shape[0]
    win_keys = WINDOW + tq
    lane = lax.broadcasted_iota(jnp.int32, (tq, LANES), 1)
    first = lane < HEAD_DIM
    scale = jnp.asarray(HEAD_DIM ** -0.5, BF16)

    pairs = [q_ref[:, r * LANES:(r + 1) * LANES] * scale for r in range(NSA_REP)]
    zero = jnp.zeros_like(pairs[0])
    q_heads, q_slopes = [], []
    for g in range(NSA_KV_HEADS):
        for r in range(NSA_REP):
            q_heads.append(jnp.where(first, pairs[r], zero) if g == 0 else jnp.where(first, zero, pairs[r]))
            q_slopes.append(jnp.broadcast_to(slope_ref[g * NSA_REP + r:g * NSA_REP + r + 1, :], (tq, LANES)))
    q_main = jnp.concatenate(q_heads, axis=0)
    q_plain = jnp.concatenate([q_main, jnp.concatenate(q_slopes, axis=0).astype(BF16)], axis=1)

    n_part = s_scr.shape[1]
    part_rows = rows // n_part

    def per_head(mask, s, fill):
        return jnp.where(mask[None], s.reshape(s.shape[0] // tq, tq, s.shape[-1]), fill).reshape(s.shape)

    t_col = t0 + lax.broadcasted_iota(jnp.int32, (tq, n_cmp), 0)
    c_end = lax.broadcasted_iota(jnp.int32, (tq, n_cmp), 1) * CMP_STRIDE + (CMP_LEN - 1)
    valid_c = c_end <= t_col
    kc_x = jnp.concatenate([kc_ref[0].astype(BF16), cpos_ref[...]], axis=1)
    vc_b = vc_ref[0].astype(BF16)
    p_parts, o_parts = [], []
    for part in range(n_part):
        rsl = slice(part * part_rows, (part + 1) * part_rows)
        s = per_head(valid_c, _dot_nt(q_plain[rsl], kc_x), NEG_INF)
        p = per_head(valid_c, jnp.exp(s - jnp.max(s, axis=1, keepdims=True)), 0.0)
        l_c = jnp.sum(p, axis=1, keepdims=True)
        p = p * jnp.where(l_c > 0.0, 1.0 / l_c, 0.0)
        p_parts.append(p)
        o_parts.append(_dot(p.astype(BF16), vc_b))
    p_c = jnp.concatenate(p_parts, axis=0)
    o_cmp = jnp.concatenate(o_parts, axis=0)

    blk_t = lax.broadcasted_iota(jnp.int32, (n_blk, tq), 0)
    cur_t = (t0 + lax.broadcasted_iota(jnp.int32, (n_blk, tq), 1)) // SEL_BLOCK
    forced_t = (blk_t == 0) | (blk_t == cur_t) | (blk_t == cur_t - 1)
    sub = lax.broadcasted_iota(jnp.int32, (SUBLANES, tq), 0)
    n_top = float(min(SEL_TOPN, n_blk))
    for g in range(NSA_KV_HEADS):
        p_sum = p_c[g * NSA_REP * tq:(g * NSA_REP + 1) * tq]
        for r in range(1, NSA_REP):
            p_sum = p_sum + p_c[(g * NSA_REP + r) * tq:(g * NSA_REP + r + 1) * tq]
        p0, p1, p2 = _split3(p_sum)
        ovt = ovt_ref[...]
        imp_t = _dot_nt(ovt, p0) + _dot_nt(ovt, p1) + _dot_nt(ovt, p2)
        score = jnp.where(forced_t, FORCE_SCORE, jnp.where(blk_t <= cur_t, imp_t, -1.0))
        chunks = [score[c * SUBLANES:(c + 1) * SUBLANES] for c in range(n_blk // SUBLANES)]
        ranks = [jnp.zeros((SUBLANES, tq), F32) for _ in chunks]
        for jp in range(n_blk):
            other = jnp.broadcast_to(score[jp:jp + 1], (SUBLANES, tq))
            for c, chunk in enumerate(chunks):
                ge = jnp.where(other >= chunk, 1.0, 0.0)
                gt = jnp.where(other > chunk, 1.0, 0.0)
                if jp < c * SUBLANES:
                    ranks[c] = ranks[c] + ge
                elif jp >= (c + 1) * SUBLANES:
                    ranks[c] = ranks[c] + gt
                else:
                    ranks[c] = ranks[c] + jnp.where(sub > jp - c * SUBLANES, ge, gt)
        rank = jnp.concatenate(ranks, axis=0)
        neg_t = jnp.where(rank < n_top, 0.0, NEG_INF)
        neg = jnp.transpose(jnp.concatenate([neg_t, jnp.zeros((LANES - n_blk, tq), F32)], axis=0))
        for r in range(NSA_REP):
            k = g * NSA_REP + r
            q_const = (neg + q_slopes[k]).astype(BF16)
            qs_scr[k * tq:(k + 1) * tq, :] = jnp.concatenate([q_heads[k], q_const], axis=1)

    ones_s = jnp.ones((tks, LANES), BF16)

    def slc_score(j, part):
        ks = pl.multiple_of(j * tks, tks)
        kx = jnp.concatenate([ksl_ref[pl.ds(ks, tks), :], kpos_ref[pl.ds(ks, tks), :]], axis=1)
        return _dot_nt(qs_scr[part * part_rows:(part + 1) * part_rows, :], kx)

    def slc_values(j, part):
        ks = pl.multiple_of(j * tks, tks)
        return jnp.concatenate([vsl_ref[pl.ds(ks, tks), :], ones_s], axis=1)

    def slc_mask(s, j):
        t_row = t0 + lax.broadcasted_iota(jnp.int32, (tq, tks), 0)
        pos = j * tks + lax.broadcasted_iota(jnp.int32, (tq, tks), 1)
        return jnp.where((pos <= t_row)[None], s.reshape(part_rows // tq, tq, tks), NEG_INF).reshape(part_rows, tks)

    _flash_sweep(t0 // tks + 1, n_part, slc_score, slc_values, slc_mask, s_scr, p_scr, m_scr, acc_scr)
    acc = acc_scr[...].reshape(rows, 2 * LANES)
    o_slc = acc[:, :LANES] / acc[:, LANES:]

    start = pl.multiple_of(jnp.maximum(t0 - WINDOW, 0), tq)
    kx = jnp.concatenate([kwn_ref[pl.ds(start, win_keys), :], kpos_ref[pl.ds(start, win_keys), :]], axis=1)
    vx = jnp.concatenate([vwn_ref[pl.ds(start, win_keys), :], jnp.ones((win_keys, LANES), BF16)], axis=1)
    dist = (t0 + lax.broadcasted_iota(jnp.int32, (tq, win_keys), 0)
            - (start + lax.broadcasted_iota(jnp.int32, (tq, win_keys), 1)))
    in_window = (dist >= 0) & (dist < WINDOW)
    o_parts = []
    win_score = lambda part: _dot_nt(q_plain[part * part_rows:(part + 1) * part_rows], kx)
    s_next = win_score(0)
    for part in range(n_part):
        s = per_head(in_window, s_next, NEG_INF)
        if part + 1 < n_part:
            s_next = win_score(part + 1)
        prob = jnp.exp(s - jnp.max(s, axis=1, keepdims=True))
        acc = _dot(prob.astype(BF16), vx)
        o_parts.append(acc[:, :LANES] / acc[:, LANES:])
    o_win = jnp.concatenate(o_parts, axis=0)

    gates = jax.nn.sigmoid(sm_ref[:, COL_GL - COL_FA:COL_GL - COL_FA + NSA_HEADS * N_BRANCH] + gb_ref[...])
    for r in range(NSA_REP):
        halves = []
        for g in range(NSA_KV_HEADS):
            k = g * NSA_REP + r
            rsl = slice(k * tq, (k + 1) * tq)
            gate = lambda br: gates[:, k * N_BRANCH + br:k * N_BRANCH + br + 1]
            halves.append(gate(0) * o_cmp[rsl] + gate(1) * o_slc[rsl] + gate(2) * o_win[rsl])
        o_ref[:, r * LANES:(r + 1) * LANES] = jnp.where(first, halves[0], halves[1]).astype(BF16)


def _nsa(big, kpos, kc, vc, cpos, ovt, slope_tab, small, gate_b, bsz, t_len, tq=256, tks=512, n_part=3):
    nq = t_len // tq
    kv_col = COL_KV // KV_WIDTH
    kv_spec = lambda c: pl.BlockSpec((t_len, KV_WIDTH), lambda b, i: (b, kv_col + c))
    full = lambda a: pl.BlockSpec(a.shape, lambda b, i: tuple(0 for _ in a.shape))
    n_cmp = kc.shape[1]
    rows = NSA_HEADS * tq
    return pl.pallas_call(
        functools.partial(_nsa_kernel, tq=tq, tks=tks),
        grid=(bsz, nq),
        in_specs=[pl.BlockSpec((tq, NSA_WIDTH), lambda b, i: (b * nq + i, COL_QC // NSA_WIDTH)),
                  kv_spec(0), kv_spec(1), kv_spec(2), kv_spec(3), full(kpos),
                  pl.BlockSpec((1, n_cmp, KV_WIDTH), lambda b, i: (b, 0, 0)),
                  pl.BlockSpec((1, n_cmp, KV_WIDTH), lambda b, i: (b, 0, 0)),
                  full(cpos), full(ovt), full(slope_tab),
                  pl.BlockSpec((tq, LANES), lambda b, i: (b * nq + i, COL_FA // LANES)),
                  full(gate_b)],
        out_specs=pl.BlockSpec((tq, NSA_WIDTH), lambda b, i: (b * nq + i, 0)),
        out_shape=jax.ShapeDtypeStruct((bsz * t_len, NSA_WIDTH), BF16),
        scratch_shapes=[pltpu.VMEM((rows, 2 * LANES), BF16),
                        pltpu.VMEM((2, n_part, rows // n_part, tks), F32),
                        pltpu.VMEM((2, n_part, rows // n_part, tks), BF16),
                        pltpu.VMEM((n_part, rows // n_part, 1), F32),
                        pltpu.VMEM((n_part, rows // n_part, 2 * LANES), F32)],
        compiler_params=_params(("parallel", "arbitrary")),
        name="nsa",
    )(big, big, big, big, big, kpos, kc, vc, cpos, ovt, slope_tab, small, gate_b)


def _position_columns(pos):
    lane = jnp.arange(LANES)[None, :]
    hi = (SEL_BLOCK * (pos // SEL_BLOCK))[:, None]
    lo = (pos % SEL_BLOCK)[:, None]
    cols = jnp.where((lane >= COL_POS) & (lane < COL_POS + 3), hi, 0) + jnp.where((lane >= COL_POS + 3) & (lane < COL_POS + 6), lo, 0)
    return cols.astype(BF16)


def _nsa_tables(t_len):
    pos = jnp.arange(t_len)
    lane = jnp.arange(LANES)[None, :]
    block_one_hot = ((lane < t_len // SEL_BLOCK) & (lane == (pos // SEL_BLOCK)[:, None])).astype(BF16)
    kpos = _position_columns(pos) + block_one_hot
    n_cmp = t_len // CMP_STRIDE
    c_start = jnp.arange(n_cmp) * CMP_STRIDE
    cpos = _position_columns(c_start + CMP_LEN - 1)
    s_start = (jnp.arange(t_len // SEL_BLOCK) * SEL_BLOCK)[:, None]
    ovt = ((c_start[None, :] < s_start + SEL_BLOCK) & (c_start[None, :] + CMP_LEN - 1 >= s_start)).astype(BF16)
    slopes = jnp.exp2(-8.0 * jnp.arange(1, NSA_HEADS + 1, dtype=F32) / NSA_HEADS)
    pieces = jnp.stack([piece.astype(F32) for piece in _split3(slopes)], axis=1)
    slope_tab = jnp.zeros((NSA_HEADS, LANES), F32)
    slope_tab = slope_tab.at[:, COL_POS:COL_POS + 3].set(pieces).at[:, COL_POS + 3:COL_POS + 6].set(pieces)
    return kpos, cpos, ovt, slope_tab


def _mem_kv_kernel(mem_ref, g_ref, wk_ref, wv_ref, k_ref, v_ref):
    mb = _rms(mem_ref[0], g_ref[...]).astype(BF16)
    k_ref[0] = _dot(mb, wk_ref[...]).astype(BF16)
    v_ref[0] = _dot(mb, wv_ref[...]).astype(BF16)


def _mem_kv(mem, g, wk, wv):
    bsz, m_len, _ = mem.shape
    w_spec = pl.BlockSpec((D_MODEL, D_MODEL), lambda b: (0, 0))
    m_spec = pl.BlockSpec((1, m_len, D_MODEL), lambda b: (b, 0, 0))
    return pl.pallas_call(
        _mem_kv_kernel,
        grid=(bsz,),
        in_specs=[m_spec, pl.BlockSpec((1, D_MODEL), lambda b: (0, 0)), w_spec, w_spec],
        out_specs=[m_spec, m_spec],
        out_shape=[jax.ShapeDtypeStruct((bsz, m_len, D_MODEL), BF16)] * 2,
        compiler_params=_params(("parallel",)),
        name="mem_kv",
    )(mem, g, wk, wv)


def _route(logits):
    lane = lax.broadcasted_iota(jnp.int32, logits.shape, 1)
    big_i = jnp.int32(1 << 20)
    is_coarse = lane < N_GROUPS
    lc = jnp.where(is_coarse, logits, NEG_INF)
    mc = jnp.max(lc, axis=1, keepdims=True)
    grp = jnp.min(jnp.where(is_coarse & (lc == mc), lane, big_i), axis=1, keepdims=True)
    w_grp = 1.0 / jnp.sum(jnp.where(is_coarse, jnp.exp(lc - mc), 0.0), axis=1, keepdims=True)
    e_id = lane - N_GROUPS
    in_grp = (e_id >= grp * EXPERTS_PER_GROUP) & (e_id < (grp + 1) * EXPERTS_PER_GROUP)
    lf = jnp.where(in_grp, logits, NEG_INF)
    m1 = jnp.max(lf, axis=1, keepdims=True)
    i1 = jnp.min(jnp.where(in_grp & (lf == m1), lane, big_i), axis=1, keepdims=True)
    lf2 = jnp.where(lane == i1, NEG_INF, lf)
    m2 = jnp.max(lf2, axis=1, keepdims=True)
    i2 = jnp.min(jnp.where(in_grp & (lane != i1) & (lf2 == m2), lane, big_i), axis=1, keepdims=True)
    denom = jnp.sum(jnp.where(in_grp, jnp.exp(lf - m1), 0.0), axis=1, keepdims=True)
    p1 = 1.0 / denom
    p2 = jnp.exp(m2 - m1) / denom
    tot = p1 + p2
    w1 = w_grp * (p1 / tot)
    w2 = w_grp * (p2 / tot)
    weights = jnp.where(lane + N_GROUPS == i1, w1, jnp.where(lane + N_GROUPS == i2, w2, 0.0))
    return jnp.where(lane == grp + N_EXPERTS, 1.0, weights)


def _mix_out_kernel(x_ref, oa_ref, ob_ref, oc_ref, woa_ref, wob_ref, woc_ref, gx_ref, wq_ref, mk_ref, mv_ref,
                    wo_ref, gf_ref, wr_ref, br_ref, x2_ref, hb_ref, comb_ref, cnt_ref):
    n_part = 2
    rows = x_ref.shape[0] // n_part
    parts = [slice(k * rows, (k + 1) * rows) for k in range(n_part)]
    x1 = [x_ref[r, :] + _dot(oa_ref[r, :], woa_ref[...]) + _dot(ob_ref[r, :], wob_ref[...])
          + _dot(oc_ref[r, :], woc_ref[...]) for r in parts]
    hq = [_rms(v, gx_ref[...]).astype(BF16) for v in x1]
    q = [(_dot(v, wq_ref[...]) * (MEM_HEAD_DIM ** -0.5)).astype(BF16) for v in hq]
    heads = [[] for _ in parts]
    for h in range(MEM_HEADS):
        sl = slice(h * MEM_HEAD_DIM, (h + 1) * MEM_HEAD_DIM)
        for k in range(n_part):
            s = _dot_nt(q[k][:, sl], mk_ref[0, :, sl])
            p = jnp.exp(s - jnp.max(s, axis=1, keepdims=True))
            p = p / jnp.sum(p, axis=1, keepdims=True)
            heads[k].append(_dot(p.astype(BF16), mv_ref[0, :, sl]).astype(BF16))
    x2 = [x1[k] + _dot(jnp.concatenate(heads[k], axis=1), wo_ref[...]) for k in range(n_part)]
    hn = [_rms(v, gf_ref[...]) for v in x2]
    for k, r in enumerate(parts):
        x2_ref[r, :] = x2[k]
        hb_ref[r, :] = hn[k].astype(BF16)
        h0, h1, _ = _split3(hn[k])
        partial = _dot(h0, wr_ref[...]) + _dot(h1, wr_ref[...])
        logits = partial + pltpu.roll(partial, ROUTER_WIDTH - ROUTER_PIECE, axis=1) + br_ref[...]
        routed = _route(logits)
        comb_ref[r, :] = routed
        per_part = rows // MOE_TILE
        for sub in range(per_part):
            cnt_ref[k * per_part + sub] = jnp.sum(routed[sub * MOE_TILE:(sub + 1) * MOE_TILE], axis=0, keepdims=True)


def _mix_out(x, o_a, o_b, o_c, wo_a, wo_b, wo_c, g_x, wq, mem_k, mem_v, wo, g_f, w_r, b_r, t_len, tm=1024):
    n = x.shape[0]
    per_batch = t_len // tm
    m_len = mem_k.shape[1]
    tile = lambda width: pl.BlockSpec((tm, width), lambda i: (i, 0))
    full = lambda shape: pl.BlockSpec(shape, lambda i: tuple(0 for _ in shape))
    mem_spec = pl.BlockSpec((1, m_len, D_MODEL), lambda i: (i // per_batch, 0, 0))
    return pl.pallas_call(
        _mix_out_kernel,
        grid=(n // tm,),
        in_specs=[tile(D_MODEL), tile(FOX_WIDTH), tile(POOL_WIDTH), tile(NSA_WIDTH),
                  full(wo_a.shape), full(wo_b.shape), full(wo_c.shape), full(g_x.shape), full(wq.shape),
                  mem_spec, mem_spec, full(wo.shape), full(g_f.shape), full(w_r.shape), full(b_r.shape)],
        out_specs=[tile(D_MODEL), tile(D_MODEL), tile(ROUTER_WIDTH),
                   pl.BlockSpec((tm // MOE_TILE, 1, ROUTER_WIDTH), lambda i: (i, 0, 0))],
        out_shape=[jax.ShapeDtypeStruct((n, D_MODEL), F32),
                   jax.ShapeDtypeStruct((n, D_MODEL), BF16),
                   jax.ShapeDtypeStruct((n, ROUTER_WIDTH), F32),
                   jax.ShapeDtypeStruct((n // MOE_TILE, 1, ROUTER_WIDTH), F32)],
        compiler_params=_params(("parallel",)),
        name="mix_out",
    )(x, o_a, o_b, o_c, wo_a, wo_b, wo_c, g_x, wq, mem_k, mem_v, wo, g_f, w_r, b_r)


def _group_ranks(comb, tri):
    lane = lax.broadcasted_iota(jnp.int32, comb.shape, 1)
    member = jnp.where((lane >= N_EXPERTS) & (lane < N_EXPERTS + N_GROUPS), comb, 0.0)
    return member, _dot(tri, member.astype(BF16))


def _moe_sort_kernel(dst_ref, hb_ref, comb_ref, tri_ref, xs_in, cs_in, xs_ref, cs_ref, stage_x, stage_c, sem):
    del xs_in, cs_in
    t = pl.program_id(0)
    comb = comb_ref[...]
    member, rank = _group_ranks(comb, tri_ref[...])
    rank_t = jnp.transpose(rank)
    member_t = jnp.transpose(member)
    slot = lax.broadcasted_iota(jnp.int32, (MOE_TILE, MOE_TILE), 0).astype(F32)
    pieces = _split3(comb)

    def copies(g, dst):
        return (pltpu.make_async_copy(stage_x.at[g], xs_ref.at[pl.ds(dst, MOE_TILE)], sem.at[0, g]),
                pltpu.make_async_copy(stage_c.at[g], cs_ref.at[pl.ds(dst, MOE_TILE)], sem.at[1, g]))

    for g in range(N_GROUPS):
        row = N_EXPERTS + g
        take = jnp.where((rank_t[row:row + 1, :] == slot) & (member_t[row:row + 1, :] > 0.5), 1.0, 0.0).astype(BF16)
        xg = _dot(take, hb_ref[...]).astype(BF16)
        cg = _dot(take, pieces[0]) + _dot(take, pieces[1]) + _dot(take, pieces[2])
        dst = pl.multiple_of(dst_ref[t * N_GROUPS + g], MOE_SEG_ALIGN)

        @pl.when(t > 0)
        def _():
            for copy in copies(g, dst):
                copy.wait()

        stage_x[g] = xg
        stage_c[g] = cg
        for copy in copies(g, dst):
            copy.start()

    @pl.when(t == pl.num_programs(0) - 1)
    def _():
        for g in range(N_GROUPS):
            for copy in copies(g, 0):
                copy.wait()


def _moe_sort(hb, comb, tri, seg_dst, rows):
    n = hb.shape[0]
    any_spec = pl.BlockSpec(memory_space=pl.ANY)
    xs0 = jnp.zeros((rows, D_MODEL), BF16)
    cs0 = jnp.zeros((rows, ROUTER_WIDTH), F32)
    return pl.pallas_call(
        _moe_sort_kernel,
        grid_spec=pltpu.PrefetchScalarGridSpec(
            num_scalar_prefetch=1,
            grid=(n // MOE_TILE,),
            in_specs=[pl.BlockSpec((MOE_TILE, D_MODEL), lambda t, dst: (t, 0)),
                      pl.BlockSpec((MOE_TILE, ROUTER_WIDTH), lambda t, dst: (t, 0)),
                      pl.BlockSpec((MOE_TILE, MOE_TILE), lambda t, dst: (0, 0)),
                      any_spec, any_spec],
            out_specs=[any_spec, any_spec],
            scratch_shapes=[pltpu.VMEM((N_GROUPS, MOE_TILE, D_MODEL), BF16),
                            pltpu.VMEM((N_GROUPS, MOE_TILE, ROUTER_WIDTH), F32),
                            pltpu.SemaphoreType.DMA((2, N_GROUPS))]),
        out_shape=[jax.ShapeDtypeStruct((rows, D_MODEL), BF16), jax.ShapeDtypeStruct((rows, ROUTER_WIDTH), F32)],
        input_output_aliases={4: 0, 5: 1},
        compiler_params=_params(("arbitrary",)),
        name="moe_sort",
    )(seg_dst, hb, comb, tri, xs0, cs0)


def _moe_ffn_kernel(group_ref, valid_ref, xs_ref, cs_ref, wg_ref, wu_ref, wd_ref, ys_ref):
    i = pl.program_id(0)

    @pl.when(valid_ref[i] == 0)
    def _():
        ys_ref[...] = jnp.zeros(ys_ref.shape, F32)

    @pl.when(valid_ref[i] != 0)
    def _():
        x = xs_ref[...]
        cs = cs_ref[...]
        lane = lax.broadcasted_iota(jnp.int32, cs.shape, 1)
        first_expert = group_ref[i] * EXPERTS_PER_GROUP
        total = None
        for e in range(EXPERTS_PER_GROUP):
            he = (jax.nn.silu(_dot(x, wg_ref[0, e])) * _dot(x, wu_ref[0, e])).astype(BF16)
            c_e = jnp.sum(jnp.where(lane == first_expert + e, cs, 0.0), axis=1, keepdims=True)
            y = c_e * _dot(he, wd_ref[0, e])
            total = y if total is None else total + y
        ys_ref[...] = total


def _moe_ffn(xs, cs, wg, wu, wd, layer, tile_group, tile_valid):
    rows = xs.shape[0]
    tile = lambda width: pl.BlockSpec((MOE_FFN_TILE, width), lambda i, grp, ok: (i, 0))
    experts = lambda shape: pl.BlockSpec((1, EXPERTS_PER_GROUP) + shape, lambda i, grp, ok: (layer, grp[i], 0, 0))
    return pl.pallas_call(
        _moe_ffn_kernel,
        grid_spec=pltpu.PrefetchScalarGridSpec(
            num_scalar_prefetch=2,
            grid=(rows // MOE_FFN_TILE,),
            in_specs=[tile(D_MODEL), tile(ROUTER_WIDTH),
                      experts((D_MODEL, EXPERT_FF)), experts((D_MODEL, EXPERT_FF)), experts((EXPERT_FF, D_MODEL))],
            out_specs=tile(D_MODEL)),
        out_shape=jax.ShapeDtypeStruct((rows, D_MODEL), F32),
        compiler_params=_params(("arbitrary",)),
        name="moe_ffn",
    )(tile_group, tile_valid, xs, cs, wg, wu, wd)


def _moe_unsort_kernel(dst_ref, ys_ref, comb_ref, tri_ref, x_ref, gfin_ref, o_ref, buf, sem, *, final_norm):
    t = pl.program_id(0)
    slot = t % 2

    def copy(tile, g, into):
        dst = pl.multiple_of(dst_ref[tile * N_GROUPS + g], MOE_SEG_ALIGN)
        return pltpu.make_async_copy(ys_ref.at[pl.ds(dst, MOE_TILE)], buf.at[into, g], sem.at[into, g])

    @pl.when(t == 0)
    def _():
        for g in range(N_GROUPS):
            copy(0, g, 0).start()

    @pl.when(t + 1 < pl.num_programs(0))
    def _():
        for g in range(N_GROUPS):
            copy(t + 1, g, 1 - slot).start()

    member, rank = _group_ranks(comb_ref[...], tri_ref[...])
    lane_slot = lax.broadcasted_iota(jnp.int32, (MOE_TILE, MOE_TILE), 1).astype(F32)
    y = x_ref[...]
    for g in range(N_GROUPS):
        copy(t, g, slot).wait()
        col = N_EXPERTS + g
        put = jnp.where((rank[:, col:col + 1] == lane_slot) & (member[:, col:col + 1] > 0.5), 1.0, 0.0).astype(BF16)
        seg = buf[slot, g]
        hi = seg.astype(BF16)
        lo = (seg - hi.astype(F32)).astype(BF16)
        y = y + (_dot(put, hi) + _dot(put, lo))
    o_ref[...] = _rms(y, gfin_ref[...]) if final_norm else y


def _moe_unsort(ys, comb, tri, x2, g_fin, seg_dst, final_norm):
    n = x2.shape[0]
    return pl.pallas_call(
        functools.partial(_moe_unsort_kernel, final_norm=final_norm),
        grid_spec=pltpu.PrefetchScalarGridSpec(
            num_scalar_prefetch=1,
            grid=(n // MOE_TILE,),
            in_specs=[pl.BlockSpec(memory_space=pl.ANY),
                      pl.BlockSpec((MOE_TILE, ROUTER_WIDTH), lambda t, dst: (t, 0)),
                      pl.BlockSpec((MOE_TILE, MOE_TILE), lambda t, dst: (0, 0)),
                      pl.BlockSpec((MOE_TILE, D_MODEL), lambda t, dst: (t, 0)),
                      pl.BlockSpec((1, D_MODEL), lambda t, dst: (0, 0))],
            out_specs=pl.BlockSpec((MOE_TILE, D_MODEL), lambda t, dst: (t, 0)),
            scratch_shapes=[pltpu.VMEM((2, N_GROUPS, MOE_TILE, D_MODEL), F32),
                            pltpu.SemaphoreType.DMA((2, N_GROUPS))]),
        out_shape=jax.ShapeDtypeStruct((n, D_MODEL), F32),
        compiler_params=_params(("arbitrary",)),
        name="moe_unsort",
    )(seg_dst, ys, comb, tri, x2, g_fin)


def _moe_plan(cnt, n_tok):
    n_tiles = cnt.shape[0]
    rows = n_tok + n_tiles * N_GROUPS * MOE_SEG_ALIGN + N_GROUPS * (MOE_FFN_TILE + MOE_GAP)
    rows = -(-rows // MOE_FFN_TILE) * MOE_FFN_TILE
    padded = (cnt + MOE_SEG_ALIGN - 1) // MOE_SEG_ALIGN * MOE_SEG_ALIGN
    within = jnp.cumsum(padded, axis=0) - padded
    region = (jnp.sum(padded, axis=0) + MOE_FFN_TILE - 1) // MOE_FFN_TILE * MOE_FFN_TILE
    start = jnp.cumsum(region + MOE_GAP) - (region + MOE_GAP)
    seg_dst = (start[None, :] + within).reshape(-1).astype(jnp.int32)
    tile_row = jnp.arange(rows // MOE_FFN_TILE) * MOE_FFN_TILE
    begun = tile_row[:, None] >= start[None, :]
    tile_valid = jnp.any(begun & (tile_row[:, None] < (start + region)[None, :]), axis=1).astype(jnp.int32)
    tile_group = jnp.clip(jnp.sum(begun, axis=1) - 1, 0, N_GROUPS - 1).astype(jnp.int32)
    return seg_dst, tile_group, tile_valid, rows


def _moe(x2, hb, comb, cnt, wg, wu, wd, layer, g_fin, final_norm):
    n = x2.shape[0]
    counts = jnp.round(cnt[:, 0, N_EXPERTS:N_EXPERTS + N_GROUPS]).astype(jnp.int32)
    seg_dst, tile_group, tile_valid, rows = _moe_plan(counts, n)
    tri = (jnp.arange(MOE_TILE)[:, None] > jnp.arange(MOE_TILE)[None, :]).astype(BF16)
    xs, cs = _moe_sort(hb, comb, tri, seg_dst, rows)
    ys = _moe_ffn(xs, cs, wg, wu, wd, layer, tile_group, tile_valid)
    return _moe_unsort(ys, comb, tri, x2, g_fin, seg_dst, final_norm)


def _pair_order(a, axis):
    shape = a.shape
    a = a.reshape(shape[:axis] + (NSA_HEADS, HEAD_DIM) + shape[axis + 1:])
    a = jnp.take(a, jnp.array(NSA_PAIR_ORDER), axis=axis)
    return a.reshape(shape)


def _split_in_proj_weight(w):
    edges = [0]
    for width in (FOX_WIDTH, FOX_WIDTH, FOX_WIDTH, FOX_HEADS, POOL_WIDTH, NSA_WIDTH) + (KV_WIDTH,) * 6 + (NSA_HEADS * N_BRANCH,):
        edges.append(edges[-1] + width)
    part = lambda k: w[:, edges[k]:edges[k + 1]]
    qa, ka, va, fa, ub, qc = (part(k) for k in range(6))
    kvs = [part(k) for k in range(6, 12)]
    gl = part(12)
    w_big = jnp.concatenate([qa, ka, va, _pair_order(qc, 1)] + kvs[2:], axis=1).astype(BF16)
    pad = jnp.zeros((w.shape[0], LANES - FOX_HEADS - NSA_HEADS * N_BRANCH), w.dtype)
    w_small = jnp.concatenate([ub, fa, gl, pad] + kvs[:2], axis=1).astype(BF16)
    return w_big, w_small


def _block_diag(w):
    groups, cin, cout = w.shape
    out = jnp.zeros((groups * cin, groups * cout), w.dtype)
    for gi in range(groups):
        out = out.at[gi * cin:(gi + 1) * cin, gi * cout:(gi + 1) * cout].set(w[gi])
    return out


def _layer(x, mem, p, experts, layer, bsz, t_len, final_g, final_norm):
    row = lambda v: v.reshape(1, -1)
    w_big, w_small = _split_in_proj_weight(p["w_in"])
    big, small = _in_proj(x, row(p["mix_norm_g"]), w_big, w_small)

    b_row = jnp.pad(p["fox_forget_b"], (0, LANES - FOX_HEADS)).reshape(1, LANES)
    ck = _forget_cumsum(small, b_row, _forget_placement(), bsz, t_len)
    o_a = _fox(big, ck, bsz, t_len)

    o_b = _pool(small, _block_diag(p["pool_w"]).astype(BF16), row(p["pool_scale"]), bsz, t_len)

    pe_k, w1_k, w2_k = _compress_weights(p["cmp_pe_k"], p["cmp_w1_k"], p["cmp_w2_k"])
    pe_v, w1_v, w2_v = _compress_weights(p["cmp_pe_v"], p["cmp_w1_v"], p["cmp_w2_v"])
    kc, vc = _compress(small, pe_k, pe_v, w1_k, w2_k, w1_v, w2_v, bsz, t_len)
    kpos, cpos, ovt, slope_tab = _nsa_tables(t_len)
    o_c = _nsa(big, kpos, kc, vc, cpos, ovt, slope_tab, small, row(p["nsa_gate_b"]), bsz, t_len)

    mem_k, mem_v = _mem_kv(mem, row(p["mem_norm_g"]), p["xattn_wk"].astype(BF16), p["xattn_wv"].astype(BF16))
    w_out = p["w_out"]
    wo_a = w_out[:FOX_WIDTH].astype(BF16)
    wo_b = w_out[FOX_WIDTH:FOX_WIDTH + POOL_WIDTH].astype(BF16)
    wo_c = _pair_order(w_out[FOX_WIDTH + POOL_WIDTH:], 0).astype(BF16)
    w_r = jnp.concatenate([p["router_coarse_w"], p["router_fine_w"],
                           jnp.zeros((D_MODEL, ROUTER_PIECE - N_GROUPS - N_EXPERTS), F32)], axis=1)
    w_r0, w_r1, _ = _split3(w_r)
    w_r = jnp.concatenate([w_r0, w_r1, jnp.zeros((D_MODEL, ROUTER_WIDTH - 2 * ROUTER_PIECE), BF16)], axis=1)
    b_r = jnp.concatenate([p["router_coarse_b"], p["router_fine_b"],
                           jnp.zeros((ROUTER_WIDTH - N_GROUPS - N_EXPERTS,), F32)]).reshape(1, -1)
    x2, hb, comb, cnt = _mix_out(x, o_a, o_b, o_c, wo_a, wo_b, wo_c, row(p["xattn_norm_g"]),
                                 p["xattn_wq"].astype(BF16), mem_k, mem_v, p["xattn_wo"].astype(BF16),
                                 row(p["ffn_norm_g"]), w_r, b_r, t_len)

    return _moe(x2, hb, comb, cnt, *experts, layer, row(final_g), final_norm)


def kernel(x, mem, mix_norm_g, w_in, fox_forget_b, pool_w, pool_scale, cmp_pe_k, cmp_pe_v, cmp_w1_k, cmp_w2_k, cmp_w1_v, cmp_w2_v, nsa_gate_b, w_out, xattn_norm_g, mem_norm_g, xattn_wq, xattn_wk, xattn_wv, xattn_wo, ffn_norm_g, router_coarse_w, router_coarse_b, router_fine_w, router_fine_b, exp_w_gate, exp_w_up, exp_w_down, final_norm_g):
    stacked = dict(mix_norm_g=mix_norm_g, w_in=w_in, fox_forget_b=fox_forget_b, pool_w=pool_w, pool_scale=pool_scale,
                   cmp_pe_k=cmp_pe_k, cmp_pe_v=cmp_pe_v, cmp_w1_k=cmp_w1_k, cmp_w2_k=cmp_w2_k, cmp_w1_v=cmp_w1_v,
                   cmp_w2_v=cmp_w2_v, nsa_gate_b=nsa_gate_b, w_out=w_out, xattn_norm_g=xattn_norm_g,
                   mem_norm_g=mem_norm_g, xattn_wq=xattn_wq, xattn_wk=xattn_wk, xattn_wv=xattn_wv, xattn_wo=xattn_wo,
                   ffn_norm_g=ffn_norm_g, router_coarse_w=router_coarse_w, router_coarse_b=router_coarse_b,
                   router_fine_w=router_fine_w, router_fine_b=router_fine_b)
    experts = (exp_w_gate.astype(BF16), exp_w_up.astype(BF16), exp_w_down.astype(BF16))
    bsz, t_len, d = x.shape
    depth = w_in.shape[0]
    h = x.reshape(bsz * t_len, d)
    for layer in range(depth):
        p = {name: value[layer] for name, value in stacked.items()}
        h = _layer(h, mem, p, experts, layer, bsz, t_len, final_norm_g, final_norm=(layer == depth - 1))
    return h.reshape(bsz, t_len, d)
```

```python
import functools

import jax
import jax.numpy as jnp
from jax import lax
from jax.experimental import pallas as pl
from jax.experimental.pallas import tpu as pltpu

F32 = jnp.float32
BF16 = jnp.bfloat16

D_MODEL = 1024
HEAD_DIM = 64
FOX_HEADS = 6
FOX_WIDTH = FOX_HEADS * HEAD_DIM
POOL_WIDTH = 256
POOL_GROUP_DIM = 64
POOL_WINDOWS = (2, 4, 8, 16)
POOL_HALO = 16
NSA_HEADS = 6
NSA_KV_HEADS = 2
NSA_REP = NSA_HEADS // NSA_KV_HEADS
NSA_WIDTH = NSA_HEADS * HEAD_DIM
KV_WIDTH = NSA_KV_HEADS * HEAD_DIM
N_BRANCH = 3
CMP_LEN = 32
CMP_STRIDE = 16
CMP_HIDDEN = 256
SEL_BLOCK = 64
SEL_TOPN = 16
WINDOW = 512
FORCE_SCORE = 1.0e4
MEM_HEADS = 4
MEM_HEAD_DIM = D_MODEL // MEM_HEADS
N_GROUPS = 4
EXPERTS_PER_GROUP = 4
N_EXPERTS = N_GROUPS * EXPERTS_PER_GROUP
EXPERT_FF = 512
EPS = 1e-6
NEG_INF = -1e30

LANES = 128
SUBLANES = 8
BIG_WIDTH = 3 * FOX_WIDTH + NSA_WIDTH + 4 * KV_WIDTH
COL_QC = 3 * FOX_WIDTH
COL_KV = COL_QC + NSA_WIDTH
SMALL_WIDTH = POOL_WIDTH + LANES + 2 * KV_WIDTH
COL_FA = POOL_WIDTH
COL_GL = POOL_WIDTH + FOX_HEADS
COL_CMP = POOL_WIDTH + LANES
ROUTER_WIDTH = LANES
ROUTER_PIECE = 32
MOE_TILE = 256
MOE_SEG_ALIGN = 16
MOE_FFN_TILE = 512
MOE_GAP = 512
NSA_PAIR_ORDER = tuple(g * NSA_REP + r for r in range(NSA_REP) for g in range(NSA_KV_HEADS))
COL_POS = HEAD_DIM

VMEM_LIMIT = 56 * 1024 * 1024

_NT = (((1,), (1,)), ((), ()))


def _params(semantics):
    return pltpu.CompilerParams(dimension_semantics=semantics, vmem_limit_bytes=VMEM_LIMIT)


def _rms(x, g):
    return x * lax.rsqrt(jnp.mean(x * x, axis=-1, keepdims=True) + EPS) * g


def _dot(a, b):
    return jnp.dot(a, b, preferred_element_type=F32)


def _dot_nt(a, b):
    return lax.dot_general(a, b, _NT, preferred_element_type=F32)


def _split3(a):
    a0 = a.astype(BF16)
    r = a - a0.astype(F32)
    a1 = r.astype(BF16)
    a2 = (r - a1.astype(F32)).astype(BF16)
    return a0, a1, a2


def _in_proj_kernel(x_ref, g_ref, wb_ref, ws_ref, big_ref, small_ref):
    yb = _rms(x_ref[...], g_ref[...]).astype(BF16)
    big_ref[...] = _dot(yb, wb_ref[...]).astype(BF16)
    small_ref[...] = _dot(yb, ws_ref[...])


def _in_proj(x, g, w_big, w_small, tm=512):
    n = x.shape[0]
    return pl.pallas_call(
        _in_proj_kernel,
        grid=(n // tm,),
        in_specs=[pl.BlockSpec((tm, D_MODEL), lambda i: (i, 0)),
                  pl.BlockSpec((1, D_MODEL), lambda i: (0, 0)),
                  pl.BlockSpec((D_MODEL, BIG_WIDTH), lambda i: (0, 0)),
                  pl.BlockSpec((D_MODEL, SMALL_WIDTH), lambda i: (0, 0))],
        out_specs=[pl.BlockSpec((tm, BIG_WIDTH), lambda i: (i, 0)),
                   pl.BlockSpec((tm, SMALL_WIDTH), lambda i: (i, 0))],
        out_shape=[jax.ShapeDtypeStruct((n, BIG_WIDTH), BF16),
                   jax.ShapeDtypeStruct((n, SMALL_WIDTH), F32)],
        compiler_params=_params(("parallel",)),
        name="in_proj",
    )(x, g, w_big, w_small)


def _forget_kernel(f_ref, b_ref, place_ref, ck_ref):
    c = jax.nn.log_sigmoid(f_ref[...] + b_ref[...])
    t_len = c.shape[0]
    row = lax.broadcasted_iota(jnp.int32, c.shape, 0)
    sh = 1
    while sh < t_len:
        c = c + jnp.where(row >= sh, pltpu.roll(c, sh, axis=0), 0.0)
        sh *= 2
    n0, n1, n2 = _split3(-c)
    ck_ref[...] = (_dot(n0, place_ref[0]) + _dot(n1, place_ref[1]) + _dot(n2, place_ref[2])).astype(BF16)


def _forget_cumsum(small, b_row, place, bsz, t_len):
    return pl.pallas_call(
        _forget_kernel,
        grid=(bsz,),
        in_specs=[pl.BlockSpec((t_len, LANES), lambda b: (b, COL_FA // LANES)),
                  pl.BlockSpec((1, LANES), lambda b: (0, 0)),
                  pl.BlockSpec(place.shape, lambda b: (0, 0, 0))],
        out_specs=pl.BlockSpec((t_len, FOX_WIDTH), lambda b: (b, 0)),
        out_shape=jax.ShapeDtypeStruct((bsz * t_len, FOX_WIDTH), BF16),
        compiler_params=_params(("parallel",)),
        name="forget",
    )(small, b_row, place)


def _forget_placement():
    src = jnp.arange(LANES)[None, :, None]
    dst = jnp.arange(FOX_WIDTH)[None, None, :]
    piece = jnp.arange(3)[:, None, None]
    target = LANES * (src // 2) + 3 * (src % 2) + piece
    return ((src < FOX_HEADS) & (dst == target)).astype(BF16)


def _flash_sweep(n_tiles, n_stream, score, values, last_mask, s_scr, p_scr, m_scr, acc_scr):
    def run_stage(j, a, first=False, last=False):
        b = 1 - a
        for st in range(n_stream):
            s = score(0, st) if first else s_scr[a, st]
            if last:
                s = last_mask(s, j)
            m_old = jnp.full(m_scr.shape[1:], NEG_INF, F32) if first else m_scr[st]
            m_new = jnp.maximum(m_old, jnp.max(s, axis=1, keepdims=True))
            prob = jnp.exp(s - m_new).astype(BF16)
            if not last:
                p_scr[a, st] = prob
                s_scr[b, st] = score(j + 1, st)
            acc = jnp.zeros(acc_scr.shape[1:], F32) if first else acc_scr[st]
            if not first:
                acc = (acc + _dot(p_scr[b, st], values(j - 1, st))) * jnp.exp(m_old - m_new)
            if last:
                acc = acc + _dot(prob, values(j, st))
            acc_scr[st] = acc
            m_scr[st] = m_new

    n_main = n_tiles - 1

    @pl.when(n_main == 0)
    def _():
        run_stage(0, 0, first=True, last=True)

    @pl.when(n_main > 0)
    def _():
        run_stage(0, 0, first=True)

    def two_stages(jj, carry):
        run_stage(2 * jj + 1, 1)
        run_stage(2 * jj + 2, 0)
        return carry

    lax.fori_loop(0, jnp.maximum(n_main - 1, 0) // 2, two_stages, 0)

    @pl.when(jnp.logical_and(n_main >= 2, n_main % 2 == 0))
    def _():
        run_stage(n_main - 1, 1)

    for a in range(2):
        @pl.when(jnp.logical_and(n_main >= 1, n_main % 2 == a))
        def _():
            run_stage(n_main, a, last=True)


def _fox_kernel(q_ref, k_ref, v_ref, ck_ref, o_ref, q_scr, s_scr, p_scr, m_scr, acc_scr, *, tq, tk):
    i = pl.program_id(1)
    t0 = i * tq
    n_pair = FOX_HEADS // 2
    lane = lax.broadcasted_iota(jnp.int32, (tq, LANES), 1)
    first = lane < HEAD_DIM
    scale = jnp.asarray(HEAD_DIM ** -0.5, BF16)
    pick_a = jnp.where(lane < 3, 1.0, 0.0).astype(BF16)
    pick_b = jnp.where((lane >= 3) & (lane < 6), 1.0, 0.0).astype(BF16)
    for p in range(n_pair):
        qp = q_ref[:, p * LANES:(p + 1) * LANES] * scale
        zero = jnp.zeros_like(qp)
        qa = jnp.concatenate([jnp.where(first, qp, zero), pick_a], axis=1)
        qb = jnp.concatenate([jnp.where(first, zero, qp), pick_b], axis=1)
        q_scr[p] = jnp.concatenate([qa, qb], axis=0)
    ones_v = jnp.ones((tk, LANES), BF16)

    def score(j, p):
        ks = pl.multiple_of(j * tk, tk)
        psl = slice(p * LANES, (p + 1) * LANES)
        kx = jnp.concatenate([k_ref[pl.ds(ks, tk), psl], ck_ref[pl.ds(ks, tk), psl]], axis=1)
        return _dot_nt(q_scr[p], kx)

    def values(j, p):
        ks = pl.multiple_of(j * tk, tk)
        return jnp.concatenate([v_ref[pl.ds(ks, tk), p * LANES:(p + 1) * LANES], ones_v], axis=1)

    def causal_mask(s, j):
        t_row = t0 + lax.broadcasted_iota(jnp.int32, (tq, tk), 0)
        pos = j * tk + lax.broadcasted_iota(jnp.int32, (tq, tk), 1)
        return jnp.where((pos <= t_row)[None], s.reshape(2, tq, tk), NEG_INF).reshape(2 * tq, tk)

    _flash_sweep(t0 // tk + 1, n_pair, score, values, causal_mask, s_scr, p_scr, m_scr, acc_scr)
    for p in range(n_pair):
        acc = acc_scr[p]
        oa = acc[:tq, :LANES] / acc[:tq, LANES:]
        ob = acc[tq:, :LANES] / acc[tq:, LANES:]
        o_ref[:, p * LANES:(p + 1) * LANES] = jnp.where(first, oa, ob).astype(BF16)


def _fox(big, ck, bsz, t_len, tq=256, tk=512):
    nq = t_len // tq
    n_pair = FOX_HEADS // 2
    seq = lambda c: pl.BlockSpec((t_len, FOX_WIDTH), lambda b, i: (b, c))
    return pl.pallas_call(
        functools.partial(_fox_kernel, tq=tq, tk=tk),
        grid=(bsz, nq),
        in_specs=[pl.BlockSpec((tq, FOX_WIDTH), lambda b, i: (b * nq + i, 0)), seq(1), seq(2), seq(0)],
        out_specs=pl.BlockSpec((tq, FOX_WIDTH), lambda b, i: (b * nq + i, 0)),
        out_shape=jax.ShapeDtypeStruct((bsz * t_len, FOX_WIDTH), BF16),
        scratch_shapes=[pltpu.VMEM((n_pair, 2 * tq, 2 * LANES), BF16),
                        pltpu.VMEM((2, n_pair, 2 * tq, tk), F32),
                        pltpu.VMEM((2, n_pair, 2 * tq, tk), BF16),
                        pltpu.VMEM((n_pair, 2 * tq, 1), F32),
                        pltpu.VMEM((n_pair, 2 * tq, 2 * LANES), F32)],
        compiler_params=_params(("parallel", "arbitrary")),
        name="fox",
    )(big, big, big, ck)


def _pool_kernel(u_ref, halo_ref, w_ref, scale_ref, o_ref, x_scr, *, tt):
    i = pl.program_id(1)
    u = u_ref[...]
    x_scr[POOL_HALO:, :] = u
    x_scr[:POOL_HALO, :] = jnp.where(i > 0, halo_ref[...], 0.0)
    t_pos = i * tt + lax.broadcasted_iota(jnp.int32, (tt, 1), 0)
    lane = lax.broadcasted_iota(jnp.int32, (tt, POOL_WIDTH), 1)
    run = u
    pooled = jnp.zeros_like(u)
    for back in range(1, max(POOL_WINDOWS)):
        run = run + x_scr[POOL_HALO - back:POOL_HALO - back + tt, :]
        w = back + 1
        if w in POOL_WINDOWS:
            gi = POOL_WINDOWS.index(w)
            count = jnp.minimum(t_pos + 1, w).astype(F32)
            in_group = (lane >= gi * POOL_GROUP_DIM) & (lane < (gi + 1) * POOL_GROUP_DIM)
            pooled = jnp.where(in_group, run / count - u, pooled)
    mixed = _dot(pooled.astype(BF16), w_ref[...])
    o_ref[...] = (mixed * scale_ref[...]).astype(BF16)


def _pool(small, w_bd, scale, bsz, t_len, tt=512):
    nt = t_len // tt
    per = tt // POOL_HALO
    return pl.pallas_call(
        functools.partial(_pool_kernel, tt=tt),
        grid=(bsz, nt),
        in_specs=[pl.BlockSpec((tt, POOL_WIDTH), lambda b, i: (b * nt + i, 0)),
                  pl.BlockSpec((POOL_HALO, POOL_WIDTH), lambda b, i: (jnp.maximum((b * nt + i) * per - 1, 0), 0)),
                  pl.BlockSpec((POOL_WIDTH, POOL_WIDTH), lambda b, i: (0, 0)),
                  pl.BlockSpec((1, POOL_WIDTH), lambda b, i: (0, 0))],
        out_specs=pl.BlockSpec((tt, POOL_WIDTH), lambda b, i: (b * nt + i, 0)),
        out_shape=jax.ShapeDtypeStruct((bsz * t_len, POOL_WIDTH), BF16),
        scratch_shapes=[pltpu.VMEM((tt + POOL_HALO, POOL_WIDTH), F32)],
        compiler_params=_params(("parallel", "arbitrary")),
        name="pool",
    )(small, small, w_bd, scale)


def _compress_kernel(xk_ref, xv_ref, pek_ref, pev_ref, w1k_ref, w2k_ref, w1v_ref, w2v_ref, kc_ref, vc_ref):
    def one(x_ref, pe_ref, w1_ref, w2_ref, out_ref):
        n_chunk = x_ref.shape[0] // CMP_STRIDE
        first = jnp.zeros((n_chunk, w1_ref.shape[2]), F32)
        second = jnp.zeros((n_chunk, w1_ref.shape[2]), F32)
        for r in range(CMP_STRIDE):
            x_r = x_ref[pl.ds(r, n_chunk, stride=CMP_STRIDE), :]
            first = first + _dot((x_r + pe_ref[r:r + 1, :]).astype(BF16), w1_ref[r])
            second = second + _dot((x_r + pe_ref[CMP_STRIDE + r:CMP_STRIDE + r + 1, :]).astype(BF16),
                                   w1_ref[CMP_STRIDE + r])
        hidden = first + pltpu.roll(second, n_chunk - 1, axis=0)
        out_ref[0] = _dot(jax.nn.gelu(hidden).astype(BF16), w2_ref[...])

    one(xk_ref, pek_ref, w1k_ref, w2k_ref, kc_ref)
    one(xv_ref, pev_ref, w1v_ref, w2v_ref, vc_ref)


def _compress(small, pe_k, pe_v, w1_k, w2_k, w1_v, w2_v, bsz, t_len):
    n_chunk = t_len // CMP_STRIDE
    full = lambda a: pl.BlockSpec(a.shape, lambda b: tuple(0 for _ in a.shape))
    x_spec = lambda col: pl.BlockSpec((t_len, KV_WIDTH), lambda b: (b, col // KV_WIDTH))
    o_spec = pl.BlockSpec((1, n_chunk, KV_WIDTH), lambda b: (b, 0, 0))
    return pl.pallas_call(
        _compress_kernel,
        grid=(bsz,),
        in_specs=[x_spec(COL_CMP), x_spec(COL_CMP + KV_WIDTH), full(pe_k), full(pe_v),
                  full(w1_k), full(w2_k), full(w1_v), full(w2_v)],
        out_specs=[o_spec, o_spec],
        out_shape=[jax.ShapeDtypeStruct((bsz, n_chunk, KV_WIDTH), F32)] * 2,
        compiler_params=_params(("parallel",)),
        name="compress",
    )(small, small, pe_k, pe_v, w1_k, w2_k, w1_v, w2_v)


def _compress_weights(pe, w1, w2):
    eye = jnp.eye(NSA_KV_HEADS, dtype=w1.dtype)
    w1_r = w1.reshape(CMP_LEN, HEAD_DIM, CMP_HIDDEN)
    w1_bd = jnp.einsum("gh,rdc->rgdhc", eye, w1_r).reshape(CMP_LEN, KV_WIDTH, NSA_KV_HEADS * CMP_HIDDEN)
    w2_bd = jnp.einsum("gh,cd->gchd", eye, w2).reshape(NSA_KV_HEADS * CMP_HIDDEN, KV_WIDTH)
    return jnp.tile(pe, (1, NSA_KV_HEADS)), w1_bd.astype(BF16), w2_bd.astype(BF16)


def _nsa_kernel(q_ref, ksl_ref, vsl_ref, kwn_ref, vwn_ref, kpos_ref, kc_ref, vc_ref, cpos_ref, ovt_ref, slope_ref,
                sm_ref, gb_ref, o_ref, qs_scr, s_scr, p_scr, m_scr, acc_scr, *, tq, tks):
    i = pl.program_id(1)
    t0 = i * tq
    n_head = NSA_HEADS
    rows = n_head * tq
    n_cmp = kc_ref.shape[1]
    n_blk = ovt_ref.shape[0]
    win_keys = WINDOW + tq
    lane = lax.broadcasted_iota(jnp.int32, (tq, LANES), 1)
    first = lane < HEAD_DIM
    scale = jnp.asarray(HEAD_DIM ** -0.5, BF16)

    pairs = [q_ref[:, r * LANES:(r + 1) * LANES] * scale for r in range(NSA_REP)]
    zero = jnp.zeros_like(pairs[0])
    q_heads, q_slopes = [], []
    for g in range(NSA_KV_HEADS):
        for r in range(NSA_REP):
            q_heads.append(jnp.where(first, pairs[r], zero) if g == 0 else jnp.where(first, zero, pairs[r]))
            q_slopes.append(jnp.broadcast_to(slope_ref[g * NSA_REP + r:g * NSA_REP + r + 1, :], (tq, LANES)))
    q_main = jnp.concatenate(q_heads, axis=0)
    q_plain = jnp.concatenate([q_main, jnp.concatenate(q_slopes, axis=0).astype(BF16)], axis=1)

    n_part = s_scr.shape[1]
    part_rows = rows // n_part

    def per_head(mask, s, fill):
        return jnp.where(mask[None], s.reshape(s.shape[0] // tq, tq, s.shape[-1]), fill).reshape(s.shape)

    t_col = t0 + lax.broadcasted_iota(jnp.int32, (tq, n_cmp), 0)
    c_end = lax.broadcasted_iota(jnp.int32, (tq, n_cmp), 1) * CMP_STRIDE + (CMP_LEN - 1)
    valid_c = c_end <= t_col
    kc_x = jnp.concatenate([kc_ref[0].astype(BF16), cpos_ref[...]], axis=1)
    vc_b = vc_ref[0].astype(BF16)
    p_parts, o_parts = [], []
    for part in range(n_part):
        rsl = slice(part * part_rows, (part + 1) * part_rows)
        s = per_head(valid_c, _dot_nt(q_plain[rsl], kc_x), NEG_INF)
        p = per_head(valid_c, jnp.exp(s - jnp.max(s, axis=1, keepdims=True)), 0.0)
        l_c = jnp.sum(p, axis=1, keepdims=True)
        p = p * jnp.where(l_c > 0.0, 1.0 / l_c, 0.0)
        p_parts.append(p)
        o_parts.append(_dot(p.astype(BF16), vc_b))
    p_c = jnp.concatenate(p_parts, axis=0)
    o_cmp = jnp.concatenate(o_parts, axis=0)

    blk_t = lax.broadcasted_iota(jnp.int32, (n_blk, tq), 0)
    cur_t = (t0 + lax.broadcasted_iota(jnp.int32, (n_blk, tq), 1)) // SEL_BLOCK
    forced_t = (blk_t == 0) | (blk_t == cur_t) | (blk_t == cur_t - 1)
    sub = lax.broadcasted_iota(jnp.int32, (SUBLANES, tq), 0)
    n_top = float(min(SEL_TOPN, n_blk))
    for g in range(NSA_KV_HEADS):
        p_sum = p_c[g * NSA_REP * tq:(g * NSA_REP + 1) * tq]
        for r in range(1, NSA_REP):
            p_sum = p_sum + p_c[(g * NSA_REP + r) * tq:(g * NSA_REP + r + 1) * tq]
        p0, p1, p2 = _split3(p_sum)
        ovt = ovt_ref[...]
        imp_t = _dot_nt(ovt, p0) + _dot_nt(ovt, p1) + _dot_nt(ovt, p2)
        score = jnp.where(forced_t, FORCE_SCORE, jnp.where(blk_t <= cur_t, imp_t, -1.0))
        chunks = [score[c * SUBLANES:(c + 1) * SUBLANES] for c in range(n_blk // SUBLANES)]
        ranks = [jnp.zeros((SUBLANES, tq), F32) for _ in chunks]
        for jp in range(n_blk):
            other = jnp.broadcast_to(score[jp:jp + 1], (SUBLANES, tq))
            for c, chunk in enumerate(chunks):
                ge = jnp.where(other >= chunk, 1.0, 0.0)
                gt = jnp.where(other > chunk, 1.0, 0.0)
                if jp < c * SUBLANES:
                    ranks[c] = ranks[c] + ge
                elif jp >= (c + 1) * SUBLANES:
                    ranks[c] = ranks[c] + gt
                else:
                    ranks[c] = ranks[c] + jnp.where(sub > jp - c * SUBLANES, ge, gt)
        rank = jnp.concatenate(ranks, axis=0)
        neg_t = jnp.where(rank < n_top, 0.0, NEG_INF)
        neg = jnp.transpose(jnp.concatenate([neg_t, jnp.zeros((LANES - n_blk, tq), F32)], axis=0))
        for r in range(NSA_REP):
            k = g * NSA_REP + r
            q_const = (neg + q_slopes[k]).astype(BF16)
            qs_scr[k * tq:(k + 1) * tq, :] = jnp.concatenate([q_heads[k], q_const], axis=1)

    ones_s = jnp.ones((tks, LANES), BF16)

    def slc_score(j, part):
        ks = pl.multiple_of(j * tks, tks)
        kx = jnp.concatenate([ksl_ref[pl.ds(ks, tks), :], kpos_ref[pl.ds(ks, tks), :]], axis=1)
        return _dot_nt(qs_scr[part * part_rows:(part + 1) * part_rows, :], kx)

    def slc_values(j, part):
        ks = pl.multiple_of(j * tks, tks)
        return jnp.concatenate([vsl_ref[pl.ds(ks, tks), :], ones_s], axis=1)

    def slc_mask(s, j):
        t_row = t0 + lax.broadcasted_iota(jnp.int32, (tq, tks), 0)
        pos = j * tks + lax.broadcasted_iota(jnp.int32, (tq, tks), 1)
        return jnp.where((pos <= t_row)[None], s.reshape(part_rows // tq, tq, tks), NEG_INF).reshape(part_rows, tks)

    _flash_sweep(t0 // tks + 1, n_part, slc_score, slc_values, slc_mask, s_scr, p_scr, m_scr, acc_scr)
    acc = acc_scr[...].reshape(rows, 2 * LANES)
    o_slc = acc[:, :LANES] / acc[:, LANES:]

    start = pl.multiple_of(jnp.maximum(t0 - WINDOW, 0), tq)
    kx = jnp.concatenate([kwn_ref[pl.ds(start, win_keys), :], kpos_ref[pl.ds(start, win_keys), :]], axis=1)
    vx = jnp.concatenate([vwn_ref[pl.ds(start, win_keys), :], jnp.ones((win_keys, LANES), BF16)], axis=1)
    dist = (t0 + lax.broadcasted_iota(jnp.int32, (tq, win_keys), 0)
            - (start + lax.broadcasted_iota(jnp.int32, (tq, win_keys), 1)))
    in_window = (dist >= 0) & (dist < WINDOW)
    o_parts = []
    win_score = lambda part: _dot_nt(q_plain[part * part_rows:(part + 1) * part_rows], kx)
    s_next = win_score(0)
    for part in range(n_part):
        s = per_head(in_window, s_next, NEG_INF)
        if part + 1 < n_part:
            s_next = win_score(part + 1)
        prob = jnp.exp(s - jnp.max(s, axis=1, keepdims=True))
        acc = _dot(prob.astype(BF16), vx)
        o_parts.append(acc[:, :LANES] / acc[:, LANES:])
    o_win = jnp.concatenate(o_parts, axis=0)

    gates = jax.nn.sigmoid(sm_ref[:, COL_GL - COL_FA:COL_GL - COL_FA + NSA_HEADS * N_BRANCH] + gb_ref[...])
    for r in range(NSA_REP):
        halves = []
        for g in range(NSA_KV_HEADS):
            k = g * NSA_REP + r
            rsl = slice(k * tq, (k + 1) * tq)
            gate = lambda br: gates[:, k * N_BRANCH + br:k * N_BRANCH + br + 1]
            halves.append(gate(0) * o_cmp[rsl] + gate(1) * o_slc[rsl] + gate(2) * o_win[rsl])
        o_ref[:, r * LANES:(r + 1) * LANES] = jnp.where(first, halves[0], halves[1]).astype(BF16)


def _nsa(big, kpos, kc, vc, cpos, ovt, slope_tab, small, gate_b, bsz, t_len, tq=256, tks=512, n_part=3):
    nq = t_len // tq
    kv_col = COL_KV // KV_WIDTH
    kv_spec = lambda c: pl.BlockSpec((t_len, KV_WIDTH), lambda b, i: (b, kv_col + c))
    full = lambda a: pl.BlockSpec(a.shape, lambda b, i: tuple(0 for _ in a.shape))
    n_cmp = kc.shape[1]
    rows = NSA_HEADS * tq
    return pl.pallas_call(
        functools.partial(_nsa_kernel, tq=tq, tks=tks),
        grid=(bsz, nq),
        in_specs=[pl.BlockSpec((tq, NSA_WIDTH), lambda b, i: (b * nq + i, COL_QC // NSA_WIDTH)),
                  kv_spec(0), kv_spec(1), kv_spec(2), kv_spec(3), full(kpos),
                  pl.BlockSpec((1, n_cmp, KV_WIDTH), lambda b, i: (b, 0, 0)),
                  pl.BlockSpec((1, n_cmp, KV_WIDTH), lambda b, i: (b, 0, 0)),
                  full(cpos), full(ovt), full(slope_tab),
                  pl.BlockSpec((tq, LANES), lambda b, i: (b * nq + i, COL_FA // LANES)),
                  full(gate_b)],
        out_specs=pl.BlockSpec((tq, NSA_WIDTH), lambda b, i: (b * nq + i, 0)),
        out_shape=jax.ShapeDtypeStruct((bsz * t_len, NSA_WIDTH), BF16),
        scratch_shapes=[pltpu.VMEM((rows, 2 * LANES), BF16),
                        pltpu.VMEM((2, n_part, rows // n_part, tks), F32),
                        pltpu.VMEM((2, n_part, rows // n_part, tks), BF16),
                        pltpu.VMEM((n_part, rows // n_part, 1), F32),
                        pltpu.VMEM((n_part, rows // n_part, 2 * LANES), F32)],
        compiler_params=_params(("parallel", "arbitrary")),
        name="nsa",
    )(big, big, big, big, big, kpos, kc, vc, cpos, ovt, slope_tab, small, gate_b)


def _position_columns(pos):
    lane = jnp.arange(LANES)[None, :]
    hi = (SEL_BLOCK * (pos // SEL_BLOCK))[:, None]
    lo = (pos % SEL_BLOCK)[:, None]
    cols = jnp.where((lane >= COL_POS) & (lane < COL_POS + 3), hi, 0) + jnp.where((lane >= COL_POS + 3) & (lane < COL_POS + 6), lo, 0)
    return cols.astype(BF16)


def _nsa_tables(t_len):
    pos = jnp.arange(t_len)
    lane = jnp.arange(LANES)[None, :]
    block_one_hot = ((lane < t_len // SEL_BLOCK) & (lane == (pos // SEL_BLOCK)[:, None])).astype(BF16)
    kpos = _position_columns(pos) + block_one_hot
    n_cmp = t_len // CMP_STRIDE
    c_start = jnp.arange(n_cmp) * CMP_STRIDE
    cpos = _position_columns(c_start + CMP_LEN - 1)
    s_start = (jnp.arange(t_len // SEL_BLOCK) * SEL_BLOCK)[:, None]
    ovt = ((c_start[None, :] < s_start + SEL_BLOCK) & (c_start[None, :] + CMP_LEN - 1 >= s_start)).astype(BF16)
    slopes = jnp.exp2(-8.0 * jnp.arange(1, NSA_HEADS + 1, dtype=F32) / NSA_HEADS)
    pieces = jnp.stack([piece.astype(F32) for piece in _split3(slopes)], axis=1)
    slope_tab = jnp.zeros((NSA_HEADS, LANES), F32)
    slope_tab = slope_tab.at[:, COL_POS:COL_POS + 3].set(pieces).at[:, COL_POS + 3:COL_POS + 6].set(pieces)
    return kpos, cpos, ovt, slope_tab


def _mem_kv_kernel(mem_ref, g_ref, wk_ref, wv_ref, k_ref, v_ref):
    mb = _rms(mem_ref[0], g_ref[...]).astype(BF16)
    k_ref[0] = _dot(mb, wk_ref[...]).astype(BF16)
    v_ref[0] = _dot(mb, wv_ref[...]).astype(BF16)


def _mem_kv(mem, g, wk, wv):
    bsz, m_len, _ = mem.shape
    w_spec = pl.BlockSpec((D_MODEL, D_MODEL), lambda b: (0, 0))
    m_spec = pl.BlockSpec((1, m_len, D_MODEL), lambda b: (b, 0, 0))
    return pl.pallas_call(
        _mem_kv_kernel,
        grid=(bsz,),
        in_specs=[m_spec, pl.BlockSpec((1, D_MODEL), lambda b: (0, 0)), w_spec, w_spec],
        out_specs=[m_spec, m_spec],
        out_shape=[jax.ShapeDtypeStruct((bsz, m_len, D_MODEL), BF16)] * 2,
        compiler_params=_params(("parallel",)),
        name="mem_kv",
    )(mem, g, wk, wv)


def _route(logits):
    lane = lax.broadcasted_iota(jnp.int32, logits.shape, 1)
    big_i = jnp.int32(1 << 20)
    is_coarse = lane < N_GROUPS
    lc = jnp.where(is_coarse, logits, NEG_INF)
    mc = jnp.max(lc, axis=1, keepdims=True)
    grp = jnp.min(jnp.where(is_coarse & (lc == mc), lane, big_i), axis=1, keepdims=True)
    w_grp = 1.0 / jnp.sum(jnp.where(is_coarse, jnp.exp(lc - mc), 0.0), axis=1, keepdims=True)
    e_id = lane - N_GROUPS
    in_grp = (e_id >= grp * EXPERTS_PER_GROUP) & (e_id < (grp + 1) * EXPERTS_PER_GROUP)
    lf = jnp.where(in_grp, logits, NEG_INF)
    m1 = jnp.max(lf, axis=1, keepdims=True)
    i1 = jnp.min(jnp.where(in_grp & (lf == m1), lane, big_i), axis=1, keepdims=True)
    lf2 = jnp.where(lane == i1, NEG_INF, lf)
    m2 = jnp.max(lf2, axis=1, keepdims=True)
    i2 = jnp.min(jnp.where(in_grp & (lane != i1) & (lf2 == m2), lane, big_i), axis=1, keepdims=True)
    denom = jnp.sum(jnp.where(in_grp, jnp.exp(lf - m1), 0.0), axis=1, keepdims=True)
    p1 = 1.0 / denom
    p2 = jnp.exp(m2 - m1) / denom
    tot = p1 + p2
    w1 = w_grp * (p1 / tot)
    w2 = w_grp * (p2 / tot)
    weights = jnp.where(lane + N_GROUPS == i1, w1, jnp.where(lane + N_GROUPS == i2, w2, 0.0))
    return jnp.where(lane == grp + N_EXPERTS, 1.0, weights)


def _mix_out_kernel(x_ref, oa_ref, ob_ref, oc_ref, woa_ref, wob_ref, woc_ref, gx_ref, wq_ref, mk_ref, mv_ref,
                    wo_ref, gf_ref, wr_ref, br_ref, x2_ref, hb_ref, comb_ref, cnt_ref):
    n_part = 2
    rows = x_ref.shape[0] // n_part
    parts = [slice(k * rows, (k + 1) * rows) for k in range(n_part)]
    x1 = [x_ref[r, :] + _dot(oa_ref[r, :], woa_ref[...]) + _dot(ob_ref[r, :], wob_ref[...])
          + _dot(oc_ref[r, :], woc_ref[...]) for r in parts]
    hq = [_rms(v, gx_ref[...]).astype(BF16) for v in x1]
    q = [(_dot(v, wq_ref[...]) * (MEM_HEAD_DIM ** -0.5)).astype(BF16) for v in hq]
    heads = [[] for _ in parts]
    for h in range(MEM_HEADS):
        sl = slice(h * MEM_HEAD_DIM, (h + 1) * MEM_HEAD_DIM)
        for k in range(n_part):
            s = _dot_nt(q[k][:, sl], mk_ref[0, :, sl])
            p = jnp.exp(s - jnp.max(s, axis=1, keepdims=True))
            p = p / jnp.sum(p, axis=1, keepdims=True)
            heads[k].append(_dot(p.astype(BF16), mv_ref[0, :, sl]).astype(BF16))
    x2 = [x1[k] + _dot(jnp.concatenate(heads[k], axis=1), wo_ref[...]) for k in range(n_part)]
    hn = [_rms(v, gf_ref[...]) for v in x2]
    for k, r in enumerate(parts):
        x2_ref[r, :] = x2[k]
        hb_ref[r, :] = hn[k].astype(BF16)
        h0, h1, _ = _split3(hn[k])
        partial = _dot(h0, wr_ref[...]) + _dot(h1, wr_ref[...])
        logits = partial + pltpu.roll(partial, ROUTER_WIDTH - ROUTER_PIECE, axis=1) + br_ref[...]
        routed = _route(logits)
        comb_ref[r, :] = routed
        per_part = rows // MOE_TILE
        for sub in range(per_part):
            cnt_ref[k * per_part + sub] = jnp.sum(routed[sub * MOE_TILE:(sub + 1) * MOE_TILE], axis=0, keepdims=True)


def _mix_out(x, o_a, o_b, o_c, wo_a, wo_b, wo_c, g_x, wq, mem_k, mem_v, wo, g_f, w_r, b_r, t_len, tm=1024):
    n = x.shape[0]
    per_batch = t_len // tm
    m_len = mem_k.shape[1]
    tile = lambda width: pl.BlockSpec((tm, width), lambda i: (i, 0))
    full = lambda shape: pl.BlockSpec(shape, lambda i: tuple(0 for _ in shape))
    mem_spec = pl.BlockSpec((1, m_len, D_MODEL), lambda i: (i // per_batch, 0, 0))
    return pl.pallas_call(
        _mix_out_kernel,
        grid=(n // tm,),
        in_specs=[tile(D_MODEL), tile(FOX_WIDTH), tile(POOL_WIDTH), tile(NSA_WIDTH),
                  full(wo_a.shape), full(wo_b.shape), full(wo_c.shape), full(g_x.shape), full(wq.shape),
                  mem_spec, mem_spec, full(wo.shape), full(g_f.shape), full(w_r.shape), full(b_r.shape)],
        out_specs=[tile(D_MODEL), tile(D_MODEL), tile(ROUTER_WIDTH),
                   pl.BlockSpec((tm // MOE_TILE, 1, ROUTER_WIDTH), lambda i: (i, 0, 0))],
        out_shape=[jax.ShapeDtypeStruct((n, D_MODEL), F32),
                   jax.ShapeDtypeStruct((n, D_MODEL), BF16),
                   jax.ShapeDtypeStruct((n, ROUTER_WIDTH), F32),
                   jax.ShapeDtypeStruct((n // MOE_TILE, 1, ROUTER_WIDTH), F32)],
        compiler_params=_params(("parallel",)),
        name="mix_out",
    )(x, o_a, o_b, o_c, wo_a, wo_b, wo_c, g_x, wq, mem_k, mem_v, wo, g_f, w_r, b_r)


def _group_ranks(comb, tri):
    lane = lax.broadcasted_iota(jnp.int32, comb.shape, 1)
    member = jnp.where((lane >= N_EXPERTS) & (lane < N_EXPERTS + N_GROUPS), comb, 0.0)
    return member, _dot(tri, member.astype(BF16))


def _moe_sort_kernel(dst_ref, hb_ref, comb_ref, tri_ref, xs_in, cs_in, xs_ref, cs_ref, stage_x, stage_c, sem):
    del xs_in, cs_in
    t = pl.program_id(0)
    comb = comb_ref[...]
    member, rank = _group_ranks(comb, tri_ref[...])
    rank_t = jnp.transpose(rank)
    member_t = jnp.transpose(member)
    slot = lax.broadcasted_iota(jnp.int32, (MOE_TILE, MOE_TILE), 0).astype(F32)
    pieces = _split3(comb)
    buf = t % 2

    def copies(into, g, dst):
        return (pltpu.make_async_copy(stage_x.at[into, g], xs_ref.at[pl.ds(dst, MOE_TILE)], sem.at[0, into, g]),
                pltpu.make_async_copy(stage_c.at[into, g], cs_ref.at[pl.ds(dst, MOE_TILE)], sem.at[1, into, g]))

    for g in range(N_GROUPS):
        row = N_EXPERTS + g
        take = jnp.where((rank_t[row:row + 1, :] == slot) & (member_t[row:row + 1, :] > 0.5), 1.0, 0.0).astype(BF16)
        stage_x[buf, g] = _dot(take, hb_ref[...]).astype(BF16)
        stage_c[buf, g] = _dot(take, pieces[0]) + _dot(take, pieces[1]) + _dot(take, pieces[2])

    @pl.when(t > 0)
    def _():
        for g in range(N_GROUPS):
            for copy in copies(1 - buf, g, 0):
                copy.wait()

    for g in range(N_GROUPS):
        dst = pl.multiple_of(dst_ref[t * N_GROUPS + g], MOE_SEG_ALIGN)
        for copy in copies(buf, g, dst):
            copy.start()

    @pl.when(t == pl.num_programs(0) - 1)
    def _():
        for g in range(N_GROUPS):
            for copy in copies(buf, g, 0):
                copy.wait()


def _moe_sort(hb, comb, tri, seg_dst, rows):
    n = hb.shape[0]
    any_spec = pl.BlockSpec(memory_space=pl.ANY)
    xs0 = jnp.zeros((rows, D_MODEL), BF16)
    cs0 = jnp.zeros((rows, ROUTER_WIDTH), F32)
    return pl.pallas_call(
        _moe_sort_kernel,
        grid_spec=pltpu.PrefetchScalarGridSpec(
            num_scalar_prefetch=1,
            grid=(n // MOE_TILE,),
            in_specs=[pl.BlockSpec((MOE_TILE, D_MODEL), lambda t, dst: (t, 0)),
                      pl.BlockSpec((MOE_TILE, ROUTER_WIDTH), lambda t, dst: (t, 0)),
                      pl.BlockSpec((MOE_TILE, MOE_TILE), lambda t, dst: (0, 0)),
                      any_spec, any_spec],
            out_specs=[any_spec, any_spec],
            scratch_shapes=[pltpu.VMEM((2, N_GROUPS, MOE_TILE, D_MODEL), BF16),
                            pltpu.VMEM((2, N_GROUPS, MOE_TILE, ROUTER_WIDTH), F32),
                            pltpu.SemaphoreType.DMA((2, 2, N_GROUPS))]),
        out_shape=[jax.ShapeDtypeStruct((rows, D_MODEL), BF16), jax.ShapeDtypeStruct((rows, ROUTER_WIDTH), F32)],
        input_output_aliases={4: 0, 5: 1},
        compiler_params=_params(("arbitrary",)),
        name="moe_sort",
    )(seg_dst, hb, comb, tri, xs0, cs0)


def _moe_ffn_kernel(group_ref, valid_ref, xs_ref, cs_ref, wg_ref, wu_ref, wd_ref, ys_ref):
    i = pl.program_id(0)

    @pl.when(valid_ref[i] == 0)
    def _():
        ys_ref[...] = jnp.zeros(ys_ref.shape, F32)

    @pl.when(valid_ref[i] != 0)
    def _():
        x = xs_ref[...]
        cs = cs_ref[...]
        lane = lax.broadcasted_iota(jnp.int32, cs.shape, 1)
        first_expert = group_ref[i] * EXPERTS_PER_GROUP
        total = None
        for e in range(EXPERTS_PER_GROUP):
            he = (jax.nn.silu(_dot(x, wg_ref[0, e])) * _dot(x, wu_ref[0, e])).astype(BF16)
            c_e = jnp.sum(jnp.where(lane == first_expert + e, cs, 0.0), axis=1, keepdims=True)
            y = c_e * _dot(he, wd_ref[0, e])
            total = y if total is None else total + y
        ys_ref[...] = total


def _moe_ffn(xs, cs, wg, wu, wd, layer, tile_group, tile_valid):
    rows = xs.shape[0]
    tile = lambda width: pl.BlockSpec((MOE_FFN_TILE, width), lambda i, grp, ok: (i, 0))
    experts = lambda shape: pl.BlockSpec((1, EXPERTS_PER_GROUP) + shape, lambda i, grp, ok: (layer, grp[i], 0, 0))
    return pl.pallas_call(
        _moe_ffn_kernel,
        grid_spec=pltpu.PrefetchScalarGridSpec(
            num_scalar_prefetch=2,
            grid=(rows // MOE_FFN_TILE,),
            in_specs=[tile(D_MODEL), tile(ROUTER_WIDTH),
                      experts((D_MODEL, EXPERT_FF)), experts((D_MODEL, EXPERT_FF)), experts((EXPERT_FF, D_MODEL))],
            out_specs=tile(D_MODEL)),
        out_shape=jax.ShapeDtypeStruct((rows, D_MODEL), F32),
        compiler_params=_params(("arbitrary",)),
        name="moe_ffn",
    )(tile_group, tile_valid, xs, cs, wg, wu, wd)


def _moe_unsort_kernel(dst_ref, ys_ref, comb_ref, tri_ref, x_ref, gfin_ref, o_ref, buf, sem, *, final_norm):
    t = pl.program_id(0)
    slot = t % 2

    def copy(tile, g, into):
        dst = pl.multiple_of(dst_ref[tile * N_GROUPS + g], MOE_SEG_ALIGN)
        return pltpu.make_async_copy(ys_ref.at[pl.ds(dst, MOE_TILE)], buf.at[into, g], sem.at[into, g])

    @pl.when(t == 0)
    def _():
        for g in range(N_GROUPS):
            copy(0, g, 0).start()

    @pl.when(t + 1 < pl.num_programs(0))
    def _():
        for g in range(N_GROUPS):
            copy(t + 1, g, 1 - slot).start()

    member, rank = _group_ranks(comb_ref[...], tri_ref[...])
    lane_slot = lax.broadcasted_iota(jnp.int32, (MOE_TILE, MOE_TILE), 1).astype(F32)
    y = x_ref[...]
    for g in range(N_GROUPS):
        copy(t, g, slot).wait()
        col = N_EXPERTS + g
        put = jnp.where((rank[:, col:col + 1] == lane_slot) & (member[:, col:col + 1] > 0.5), 1.0, 0.0).astype(BF16)
        seg = buf[slot, g]
        hi = seg.astype(BF16)
        lo = (seg - hi.astype(F32)).astype(BF16)
        y = y + (_dot(put, hi) + _dot(put, lo))
    o_ref[...] = _rms(y, gfin_ref[...]) if final_norm else y


def _moe_unsort(ys, comb, tri, x2, g_fin, seg_dst, final_norm):
    n = x2.shape[0]
    return pl.pallas_call(
        functools.partial(_moe_unsort_kernel, final_norm=final_norm),
        grid_spec=pltpu.PrefetchScalarGridSpec(
            num_scalar_prefetch=1,
            grid=(n // MOE_TILE,),
            in_specs=[pl.BlockSpec(memory_space=pl.ANY),
                      pl.BlockSpec((MOE_TILE, ROUTER_WIDTH), lambda t, dst: (t, 0)),
                      pl.BlockSpec((MOE_TILE, MOE_TILE), lambda t, dst: (0, 0)),
                      pl.BlockSpec((MOE_TILE, D_MODEL), lambda t, dst: (t, 0)),
                      pl.BlockSpec((1, D_MODEL), lambda t, dst: (0, 0))],
            out_specs=pl.BlockSpec((MOE_TILE, D_MODEL), lambda t, dst: (t, 0)),
            scratch_shapes=[pltpu.VMEM((2, N_GROUPS, MOE_TILE, D_MODEL), F32),
                            pltpu.SemaphoreType.DMA((2, N_GROUPS))]),
        out_shape=jax.ShapeDtypeStruct((n, D_MODEL), F32),
        compiler_params=_params(("arbitrary",)),
        name="moe_unsort",
    )(seg_dst, ys, comb, tri, x2, g_fin)


def _moe_plan(cnt, n_tok):
    n_tiles = cnt.shape[0]
    rows = n_tok + n_tiles * N_GROUPS * MOE_SEG_ALIGN + N_GROUPS * (MOE_FFN_TILE + MOE_GAP)
    rows = -(-rows // MOE_FFN_TILE) * MOE_FFN_TILE
    padded = (cnt + MOE_SEG_ALIGN - 1) // MOE_SEG_ALIGN * MOE_SEG_ALIGN
    within = jnp.cumsum(padded, axis=0) - padded
    region = (jnp.sum(padded, axis=0) + MOE_FFN_TILE - 1) // MOE_FFN_TILE * MOE_FFN_TILE
    start = jnp.cumsum(region + MOE_GAP) - (region + MOE_GAP)
    seg_dst = (start[None, :] + within).reshape(-1).astype(jnp.int32)
    tile_row = jnp.arange(rows // MOE_FFN_TILE) * MOE_FFN_TILE
    begun = tile_row[:, None] >= start[None, :]
    tile_valid = jnp.any(begun & (tile_row[:, None] < (start + region)[None, :]), axis=1).astype(jnp.int32)
    tile_group = jnp.clip(jnp.sum(begun, axis=1) - 1, 0, N_GROUPS - 1).astype(jnp.int32)
    return seg_dst, tile_group, tile_valid, rows


def _moe(x2, hb, comb, cnt, wg, wu, wd, layer, g_fin, final_norm):
    n = x2.shape[0]
    counts = jnp.round(cnt[:, 0, N_EXPERTS:N_EXPERTS + N_GROUPS]).astype(jnp.int32)
    seg_dst, tile_group, tile_valid, rows = _moe_plan(counts, n)
    tri = (jnp.arange(MOE_TILE)[:, None] > jnp.arange(MOE_TILE)[None, :]).astype(BF16)
    xs, cs = _moe_sort(hb, comb, tri, seg_dst, rows)
    ys = _moe_ffn(xs, cs, wg, wu, wd, layer, tile_group, tile_valid)
    return _moe_unsort(ys, comb, tri, x2, g_fin, seg_dst, final_norm)


def _pair_order(a, axis):
    shape = a.shape
    a = a.reshape(shape[:axis] + (NSA_HEADS, HEAD_DIM) + shape[axis + 1:])
    a = jnp.take(a, jnp.array(NSA_PAIR_ORDER), axis=axis)
    return a.reshape(shape)


def _split_in_proj_weight(w):
    edges = [0]
    for width in (FOX_WIDTH, FOX_WIDTH, FOX_WIDTH, FOX_HEADS, POOL_WIDTH, NSA_WIDTH) + (KV_WIDTH,) * 6 + (NSA_HEADS * N_BRANCH,):
        edges.append(edges[-1] + width)
    part = lambda k: w[:, edges[k]:edges[k + 1]]
    qa, ka, va, fa, ub, qc = (part(k) for k in range(6))
    kvs = [part(k) for k in range(6, 12)]
    gl = part(12)
    w_big = jnp.concatenate([qa, ka, va, _pair_order(qc, 1)] + kvs[2:], axis=1).astype(BF16)
    pad = jnp.zeros((w.shape[0], LANES - FOX_HEADS - NSA_HEADS * N_BRANCH), w.dtype)
    w_small = jnp.concatenate([ub, fa, gl, pad] + kvs[:2], axis=1).astype(BF16)
    return w_big, w_small


def _block_diag(w):
    groups, cin, cout = w.shape
    out = jnp.zeros((groups * cin, groups * cout), w.dtype)
    for gi in range(groups):
        out = out.at[gi * cin:(gi + 1) * cin, gi * cout:(gi + 1) * cout].set(w[gi])
    return out


def _layer(x, mem, p, experts, layer, bsz, t_len, final_g, final_norm):
    row = lambda v: v.reshape(1, -1)
    w_big, w_small = _split_in_proj_weight(p["w_in"])
    big, small = _in_proj(x, row(p["mix_norm_g"]), w_big, w_small)

    b_row = jnp.pad(p["fox_forget_b"], (0, LANES - FOX_HEADS)).reshape(1, LANES)
    ck = _forget_cumsum(small, b_row, _forget_placement(), bsz, t_len)
    o_a = _fox(big, ck, bsz, t_len)

    o_b = _pool(small, _block_diag(p["pool_w"]).astype(BF16), row(p["pool_scale"]), bsz, t_len)

    pe_k, w1_k, w2_k = _compress_weights(p["cmp_pe_k"], p["cmp_w1_k"], p["cmp_w2_k"])
    pe_v, w1_v, w2_v = _compress_weights(p["cmp_pe_v"], p["cmp_w1_v"], p["cmp_w2_v"])
    kc, vc = _compress(small, pe_k, pe_v, w1_k, w2_k, w1_v, w2_v, bsz, t_len)
    kpos, cpos, ovt, slope_tab = _nsa_tables(t_len)
    o_c = _nsa(big, kpos, kc, vc, cpos, ovt, slope_tab, small, row(p["nsa_gate_b"]), bsz, t_len)

    mem_k, mem_v = _mem_kv(mem, row(p["mem_norm_g"]), p["xattn_wk"].astype(BF16), p["xattn_wv"].astype(BF16))
    w_out = p["w_out"]
    wo_a = w_out[:FOX_WIDTH].astype(BF16)
    wo_b = w_out[FOX_WIDTH:FOX_WIDTH + POOL_WIDTH].astype(BF16)
    wo_c = _pair_order(w_out[FOX_WIDTH + POOL_WIDTH:], 0).astype(BF16)
    w_r = jnp.concatenate([p["router_coarse_w"], p["router_fine_w"],
                           jnp.zeros((D_MODEL, ROUTER_PIECE - N_GROUPS - N_EXPERTS), F32)], axis=1)
    w_r0, w_r1, _ = _split3(w_r)
    w_r = jnp.concatenate([w_r0, w_r1, jnp.zeros((D_MODEL, ROUTER_WIDTH - 2 * ROUTER_PIECE), BF16)], axis=1)
    b_r = jnp.concatenate([p["router_coarse_b"], p["router_fine_b"],
                           jnp.zeros((ROUTER_WIDTH - N_GROUPS - N_EXPERTS,), F32)]).reshape(1, -1)
    x2, hb, comb, cnt = _mix_out(x, o_a, o_b, o_c, wo_a, wo_b, wo_c, row(p["xattn_norm_g"]),
                                 p["xattn_wq"].astype(BF16), mem_k, mem_v, p["xattn_wo"].astype(BF16),
                                 row(p["ffn_norm_g"]), w_r, b_r, t_len)

    return _moe(x2, hb, comb, cnt, *experts, layer, row(final_g), final_norm)


def kernel(x, mem, mix_norm_g, w_in, fox_forget_b, pool_w, pool_scale, cmp_pe_k, cmp_pe_v, cmp_w1_k, cmp_w2_k, cmp_w1_v, cmp_w2_v, nsa_gate_b, w_out, xattn_norm_g, mem_norm_g, xattn_wq, xattn_wk, xattn_wv, xattn_wo, ffn_norm_g, router_coarse_w, router_coarse_b, router_fine_w, router_fine_b, exp_w_gate, exp_w_up, exp_w_down, final_norm_g):
    stacked = dict(mix_norm_g=mix_norm_g, w_in=w_in, fox_forget_b=fox_forget_b, pool_w=pool_w, pool_scale=pool_scale,
                   cmp_pe_k=cmp_pe_k, cmp_pe_v=cmp_pe_v, cmp_w1_k=cmp_w1_k, cmp_w2_k=cmp_w2_k, cmp_w1_v=cmp_w1_v,
                   cmp_w2_v=cmp_w2_v, nsa_gate_b=nsa_gate_b, w_out=w_out, xattn_norm_g=xattn_norm_g,
                   mem_norm_g=mem_norm_g, xattn_wq=xattn_wq, xattn_wk=xattn_wk, xattn_wv=xattn_wv, xattn_wo=xattn_wo,
                   ffn_norm_g=ffn_norm_g, router_coarse_w=router_coarse_w, router_coarse_b=router_coarse_b,
                   router_fine_w=router_fine_w, router_fine_b=router_fine_b)
    experts = (exp_w_gate.astype(BF16), exp_w_up.astype(BF16), exp_w_down.astype(BF16))
    bsz, t_len, d = x.shape
    depth = w_in.shape[0]
    h = x.reshape(bsz * t_len, d)
    for layer in range(depth):
        p = {name: value[layer] for name, value in stacked.items()}
        h = _layer(h, mem, p, experts, layer, bsz, t_len, final_norm_g, final_norm=(layer == depth - 1))
    return h.reshape(bsz, t_len, d)
```

```python
import functools

import jax
import jax.numpy as jnp
from jax import lax
from jax.experimental import pallas as pl
from jax.experimental.pallas import tpu as pltpu

F32 = jnp.float32
BF16 = jnp.bfloat16

D_MODEL = 1024
HEAD_DIM = 64
FOX_HEADS = 6
FOX_WIDTH = FOX_HEADS * HEAD_DIM
POOL_WIDTH = 256
POOL_GROUP_DIM = 64
POOL_WINDOWS = (2, 4, 8, 16)
POOL_HALO = 16
NSA_HEADS = 6
NSA_KV_HEADS = 2
NSA_REP = NSA_HEADS // NSA_KV_HEADS
NSA_WIDTH = NSA_HEADS * HEAD_DIM
KV_WIDTH = NSA_KV_HEADS * HEAD_DIM
N_BRANCH = 3
CMP_LEN = 32
CMP_STRIDE = 16
CMP_HIDDEN = 256
SEL_BLOCK = 64
SEL_TOPN = 16
WINDOW = 512
FORCE_SCORE = 1.0e4
MEM_HEADS = 4
MEM_HEAD_DIM = D_MODEL // MEM_HEADS
N_GROUPS = 4
EXPERTS_PER_GROUP = 4
N_EXPERTS = N_GROUPS * EXPERTS_PER_GROUP
EXPERT_FF = 512
EPS = 1e-6
NEG_INF = -1e30

LANES = 128
SUBLANES = 8
BIG_WIDTH = 3 * FOX_WIDTH + NSA_WIDTH + 4 * KV_WIDTH
COL_QC = 3 * FOX_WIDTH
COL_KV = COL_QC + NSA_WIDTH
SMALL_WIDTH = POOL_WIDTH + LANES + 2 * KV_WIDTH
COL_FA = POOL_WIDTH
COL_GL = POOL_WIDTH + FOX_HEADS
COL_CMP = POOL_WIDTH + LANES
ROUTER_WIDTH = LANES
ROUTER_PIECE = 32
MOE_TILE = 256
MOE_SEG_ALIGN = 16
MOE_FFN_TILE = 512
MOE_GAP = 512
NSA_PAIR_ORDER = tuple(g * NSA_REP + r for r in range(NSA_REP) for g in range(NSA_KV_HEADS))
COL_POS = HEAD_DIM

VMEM_LIMIT = 56 * 1024 * 1024

_NT = (((1,), (1,)), ((), ()))


def _params(semantics):
    return pltpu.CompilerParams(dimension_semantics=semantics, vmem_limit_bytes=VMEM_LIMIT)


def _rms(x, g):
    return x * lax.rsqrt(jnp.mean(x * x, axis=-1, keepdims=True) + EPS) * g


def _dot(a, b):
    return jnp.dot(a, b, preferred_element_type=F32)


def _dot_nt(a, b):
    return lax.dot_general(a, b, _NT, preferred_element_type=F32)


def _split3(a):
    a0 = a.astype(BF16)
    r = a - a0.astype(F32)
    a1 = r.astype(BF16)
    a2 = (r - a1.astype(F32)).astype(BF16)
    return a0, a1, a2


def _in_proj_kernel(x_ref, g_ref, wb_ref, ws_ref, big_ref, small_ref):
    yb = _rms(x_ref[...], g_ref[...]).astype(BF16)
    big_ref[...] = _dot(yb, wb_ref[...]).astype(BF16)
    small_ref[...] = _dot(yb, ws_ref[...])


def _in_proj(x, g, w_big, w_small, tm=512):
    n = x.shape[0]
    return pl.pallas_call(
        _in_proj_kernel,
        grid=(n // tm,),
        in_specs=[pl.BlockSpec((tm, D_MODEL), lambda i: (i, 0)),
                  pl.BlockSpec((1, D_MODEL), lambda i: (0, 0)),
                  pl.BlockSpec((D_MODEL, BIG_WIDTH), lambda i: (0, 0)),
                  pl.BlockSpec((D_MODEL, SMALL_WIDTH), lambda i: (0, 0))],
        out_specs=[pl.BlockSpec((tm, BIG_WIDTH), lambda i: (i, 0)),
                   pl.BlockSpec((tm, SMALL_WIDTH), lambda i: (i, 0))],
        out_shape=[jax.ShapeDtypeStruct((n, BIG_WIDTH), BF16),
                   jax.ShapeDtypeStruct((n, SMALL_WIDTH), F32)],
        compiler_params=_params(("parallel",)),
        name="in_proj",
    )(x, g, w_big, w_small)


def _forget_kernel(f_ref, b_ref, place_ref, ck_ref):
    c = jax.nn.log_sigmoid(f_ref[...] + b_ref[...])
    t_len = c.shape[0]
    row = lax.broadcasted_iota(jnp.int32, c.shape, 0)
    sh = 1
    while sh < t_len:
        c = c + jnp.where(row >= sh, pltpu.roll(c, sh, axis=0), 0.0)
        sh *= 2
    n0, n1, n2 = _split3(-c)
    ck_ref[...] = (_dot(n0, place_ref[0]) + _dot(n1, place_ref[1]) + _dot(n2, place_ref[2])).astype(BF16)


def _forget_cumsum(small, b_row, place, bsz, t_len):
    return pl.pallas_call(
        _forget_kernel,
        grid=(bsz,),
        in_specs=[pl.BlockSpec((t_len, LANES), lambda b: (b, COL_FA // LANES)),
                  pl.BlockSpec((1, LANES), lambda b: (0, 0)),
                  pl.BlockSpec(place.shape, lambda b: (0, 0, 0))],
        out_specs=pl.BlockSpec((t_len, FOX_WIDTH), lambda b: (b, 0)),
        out_shape=jax.ShapeDtypeStruct((bsz * t_len, FOX_WIDTH), BF16),
        compiler_params=_params(("parallel",)),
        name="forget",
    )(small, b_row, place)


def _forget_placement():
    src = jnp.arange(LANES)[None, :, None]
    dst = jnp.arange(FOX_WIDTH)[None, None, :]
    piece = jnp.arange(3)[:, None, None]
    target = LANES * (src // 2) + 3 * (src % 2) + piece
    return ((src < FOX_HEADS) & (dst == target)).astype(BF16)


def _flash_sweep(n_tiles, n_stream, score, values, last_mask, s_scr, p_scr, m_scr, acc_scr):
    def run_stage(j, a, first=False, last=False):
        b = 1 - a
        for st in range(n_stream):
            s = score(0, st) if first else s_scr[a, st]
            if last:
                s = last_mask(s, j)
            m_old = jnp.full(m_scr.shape[1:], NEG_INF, F32) if first else m_scr[st]
            m_new = jnp.maximum(m_old, jnp.max(s, axis=1, keepdims=True))
            prob = jnp.exp(s - m_new).astype(BF16)
            if not last:
                p_scr[a, st] = prob
                s_scr[b, st] = score(j + 1, st)
            acc = jnp.zeros(acc_scr.shape[1:], F32) if first else acc_scr[st]
            if not first:
                acc = (acc + _dot(p_scr[b, st], values(j - 1, st))) * jnp.exp(m_old - m_new)
            if last:
                acc = acc + _dot(prob, values(j, st))
            acc_scr[st] = acc
            m_scr[st] = m_new

    n_main = n_tiles - 1

    @pl.when(n_main == 0)
    def _():
        run_stage(0, 0, first=True, last=True)

    @pl.when(n_main > 0)
    def _():
        run_stage(0, 0, first=True)

    def two_stages(jj, carry):
        run_stage(2 * jj + 1, 1)
        run_stage(2 * jj + 2, 0)
        return carry

    lax.fori_loop(0, jnp.maximum(n_main - 1, 0) // 2, two_stages, 0)

    @pl.when(jnp.logical_and(n_main >= 2, n_main % 2 == 0))
    def _():
        run_stage(n_main - 1, 1)

    for a in range(2):
        @pl.when(jnp.logical_and(n_main >= 1, n_main % 2 == a))
        def _():
            run_stage(n_main, a, last=True)


def _fox_kernel(q_ref, k_ref, v_ref, ck_ref, o_ref, q_scr, s_scr, p_scr, m_scr, acc_scr, *, tq, tk):
    i = pl.program_id(1)
    t0 = i * tq
    n_pair = FOX_HEADS // 2
    lane = lax.broadcasted_iota(jnp.int32, (tq, LANES), 1)
    first = lane < HEAD_DIM
    scale = jnp.asarray(HEAD_DIM ** -0.5, BF16)
    pick_a = jnp.where(lane < 3, 1.0, 0.0).astype(BF16)
    pick_b = jnp.where((lane >= 3) & (lane < 6), 1.0, 0.0).astype(BF16)
    for p in range(n_pair):
        qp = q_ref[:, p * LANES:(p + 1) * LANES] * scale
        zero = jnp.zeros_like(qp)
        qa = jnp.concatenate([jnp.where(first, qp, zero), pick_a], axis=1)
        qb = jnp.concatenate([jnp.where(first, zero, qp), pick_b], axis=1)
        q_scr[p] = jnp.concatenate([qa, qb], axis=0)
    ones_v = jnp.ones((tk, LANES), BF16)

    def score(j, p):
        ks = pl.multiple_of(j * tk, tk)
        psl = slice(p * LANES, (p + 1) * LANES)
        kx = jnp.concatenate([k_ref[pl.ds(ks, tk), psl], ck_ref[pl.ds(ks, tk), psl]], axis=1)
        return _dot_nt(q_scr[p], kx)

    def values(j, p):
        ks = pl.multiple_of(j * tk, tk)
        return jnp.concatenate([v_ref[pl.ds(ks, tk), p * LANES:(p + 1) * LANES], ones_v], axis=1)

    def causal_mask(s, j):
        t_row = t0 + lax.broadcasted_iota(jnp.int32, (tq, tk), 0)
        pos = j * tk + lax.broadcasted_iota(jnp.int32, (tq, tk), 1)
        return jnp.where((pos <= t_row)[None], s.reshape(2, tq, tk), NEG_INF).reshape(2 * tq, tk)

    _flash_sweep(t0 // tk + 1, n_pair, score, values, causal_mask, s_scr, p_scr, m_scr, acc_scr)
    for p in range(n_pair):
        acc = acc_scr[p]
        oa = acc[:tq, :LANES] / acc[:tq, LANES:]
        ob = acc[tq:, :LANES] / acc[tq:, LANES:]
        o_ref[:, p * LANES:(p + 1) * LANES] = jnp.where(first, oa, ob).astype(BF16)


def _fox(big, ck, bsz, t_len, tq=256, tk=512):
    nq = t_len // tq
    n_pair = FOX_HEADS // 2
    seq = lambda c: pl.BlockSpec((t_len, FOX_WIDTH), lambda b, i: (b, c))
    return pl.pallas_call(
        functools.partial(_fox_kernel, tq=tq, tk=tk),
        grid=(bsz, nq),
        in_specs=[pl.BlockSpec((tq, FOX_WIDTH), lambda b, i: (b * nq + i, 0)), seq(1), seq(2), seq(0)],
        out_specs=pl.BlockSpec((tq, FOX_WIDTH), lambda b, i: (b * nq + i, 0)),
        out_shape=jax.ShapeDtypeStruct((bsz * t_len, FOX_WIDTH), BF16),
        scratch_shapes=[pltpu.VMEM((n_pair, 2 * tq, 2 * LANES), BF16),
                        pltpu.VMEM((2, n_pair, 2 * tq, tk), F32),
                        pltpu.VMEM((2, n_pair, 2 * tq, tk), BF16),
                        pltpu.VMEM((n_pair, 2 * tq, 1), F32),
                        pltpu.VMEM((n_pair, 2 * tq, 2 * LANES), F32)],
        compiler_params=_params(("parallel", "arbitrary")),
        name="fox",
    )(big, big, big, ck)


def _pool_kernel(u_ref, halo_ref, w_ref, scale_ref, o_ref, x_scr, *, tt):
    i = pl.program_id(1)
    u = u_ref[...]
    x_scr[POOL_HALO:, :] = u
    x_scr[:POOL_HALO, :] = jnp.where(i > 0, halo_ref[...], 0.0)
    t_pos = i * tt + lax.broadcasted_iota(jnp.int32, (tt, 1), 0)
    lane = lax.broadcasted_iota(jnp.int32, (tt, POOL_WIDTH), 1)
    run = u
    pooled = jnp.zeros_like(u)
    for back in range(1, max(POOL_WINDOWS)):
        run = run + x_scr[POOL_HALO - back:POOL_HALO - back + tt, :]
        w = back + 1
        if w in POOL_WINDOWS:
            gi = POOL_WINDOWS.index(w)
            count = jnp.minimum(t_pos + 1, w).astype(F32)
            in_group = (lane >= gi * POOL_GROUP_DIM) & (lane < (gi + 1) * POOL_GROUP_DIM)
            pooled = jnp.where(in_group, run / count - u, pooled)
    mixed = _dot(pooled.astype(BF16), w_ref[...])
    o_ref[...] = (mixed * scale_ref[...]).astype(BF16)


def _pool(small, w_bd, scale, bsz, t_len, tt=512):
    nt = t_len // tt
    per = tt // POOL_HALO
    return pl.pallas_call(
        functools.partial(_pool_kernel, tt=tt),
        grid=(bsz, nt),
        in_specs=[pl.BlockSpec((tt, POOL_WIDTH), lambda b, i: (b * nt + i, 0)),
                  pl.BlockSpec((POOL_HALO, POOL_WIDTH), lambda b, i: (jnp.maximum((b * nt + i) * per - 1, 0), 0)),
                  pl.BlockSpec((POOL_WIDTH, POOL_WIDTH), lambda b, i: (0, 0)),
                  pl.BlockSpec((1, POOL_WIDTH), lambda b, i: (0, 0))],
        out_specs=pl.BlockSpec((tt, POOL_WIDTH), lambda b, i: (b * nt + i, 0)),
        out_shape=jax.ShapeDtypeStruct((bsz * t_len, POOL_WIDTH), BF16),
        scratch_shapes=[pltpu.VMEM((tt + POOL_HALO, POOL_WIDTH), F32)],
        compiler_params=_params(("parallel", "arbitrary")),
        name="pool",
    )(small, small, w_bd, scale)


def _compress_kernel(xk_ref, xv_ref, pek_ref, pev_ref, w1k_ref, w2k_ref, w1v_ref, w2v_ref, kc_ref, vc_ref):
    def one(x_ref, pe_ref, w1_ref, w2_ref, out_ref):
        n_chunk = x_ref.shape[0] // CMP_STRIDE
        first = jnp.zeros((n_chunk, w1_ref.shape[2]), F32)
        second = jnp.zeros((n_chunk, w1_ref.shape[2]), F32)
        for r in range(CMP_STRIDE):
            x_r = x_ref[pl.ds(r, n_chunk, stride=CMP_STRIDE), :]
            first = first + _dot((x_r + pe_ref[r:r + 1, :]).astype(BF16), w1_ref[r])
            second = second + _dot((x_r + pe_ref[CMP_STRIDE + r:CMP_STRIDE + r + 1, :]).astype(BF16),
                                   w1_ref[CMP_STRIDE + r])
        hidden = first + pltpu.roll(second, n_chunk - 1, axis=0)
        out_ref[0] = _dot(jax.nn.gelu(hidden).astype(BF16), w2_ref[...])

    one(xk_ref, pek_ref, w1k_ref, w2k_ref, kc_ref)
    one(xv_ref, pev_ref, w1v_ref, w2v_ref, vc_ref)


def _compress(small, pe_k, pe_v, w1_k, w2_k, w1_v, w2_v, bsz, t_len):
    n_chunk = t_len // CMP_STRIDE
    full = lambda a: pl.BlockSpec(a.shape, lambda b: tuple(0 for _ in a.shape))
    x_spec = lambda col: pl.BlockSpec((t_len, KV_WIDTH), lambda b: (b, col // KV_WIDTH))
    o_spec = pl.BlockSpec((1, n_chunk, KV_WIDTH), lambda b: (b, 0, 0))
    return pl.pallas_call(
        _compress_kernel,
        grid=(bsz,),
        in_specs=[x_spec(COL_CMP), x_spec(COL_CMP + KV_WIDTH), full(pe_k), full(pe_v),
                  full(w1_k), full(w2_k), full(w1_v), full(w2_v)],
        out_specs=[o_spec, o_spec],
        out_shape=[jax.ShapeDtypeStruct((bsz, n_chunk, KV_WIDTH), F32)] * 2,
        compiler_params=_params(("parallel",)),
        name="compress",
    )(small, small, pe_k, pe_v, w1_k, w2_k, w1_v, w2_v)


def _compress_weights(pe, w1, w2):
    eye = jnp.eye(NSA_KV_HEADS, dtype=w1.dtype)
    w1_r = w1.reshape(CMP_LEN, HEAD_DIM, CMP_HIDDEN)
    w1_bd = jnp.einsum("gh,rdc->rgdhc", eye, w1_r).reshape(CMP_LEN, KV_WIDTH, NSA_KV_HEADS * CMP_HIDDEN)
    w2_bd = jnp.einsum("gh,cd->gchd", eye, w2).reshape(NSA_KV_HEADS * CMP_HIDDEN, KV_WIDTH)
    return jnp.tile(pe, (1, NSA_KV_HEADS)), w1_bd.astype(BF16), w2_bd.astype(BF16)


def _nsa_kernel(q_ref, ksl_ref, vsl_ref, kwn_ref, vwn_ref, kpos_ref, kc_ref, vc_ref, cpos_ref, ovt_ref, slope_ref,
                sm_ref, gb_ref, o_ref, qs_scr, rank_scr, s_scr, p_scr, m_scr, acc_scr, *, tq, tks):
    i = pl.program_id(1)
    t0 = i * tq
    n_head = NSA_HEADS
    rows = n_head * tq
    n_cmp = kc_ref.shape[1]
    n_blk = ovt_ref.shape[0]
    win_keys = WINDOW + tq
    lane = lax.broadcasted_iota(jnp.int32, (tq, LANES), 1)
    first = lane < HEAD_DIM
    scale = jnp.asarray(HEAD_DIM ** -0.5, BF16)

    pairs = [q_ref[:, r * LANES:(r + 1) * LANES] * scale for r in range(NSA_REP)]
    zero = jnp.zeros_like(pairs[0])
    q_heads, q_slopes = [], []
    for g in range(NSA_KV_HEADS):
        for r in range(NSA_REP):
            q_heads.append(jnp.where(first, pairs[r], zero) if g == 0 else jnp.where(first, zero, pairs[r]))
            q_slopes.append(jnp.broadcast_to(slope_ref[g * NSA_REP + r:g * NSA_REP + r + 1, :], (tq, LANES)))
    q_main = jnp.concatenate(q_heads, axis=0)
    q_plain = jnp.concatenate([q_main, jnp.concatenate(q_slopes, axis=0).astype(BF16)], axis=1)

    n_part = s_scr.shape[1]
    part_rows = rows // n_part

    def per_head(mask, s, fill):
        return jnp.where(mask[None], s.reshape(s.shape[0] // tq, tq, s.shape[-1]), fill).reshape(s.shape)

    t_col = t0 + lax.broadcasted_iota(jnp.int32, (tq, n_cmp), 0)
    c_end = lax.broadcasted_iota(jnp.int32, (tq, n_cmp), 1) * CMP_STRIDE + (CMP_LEN - 1)
    valid_c = c_end <= t_col
    kc_x = jnp.concatenate([kc_ref[0].astype(BF16), cpos_ref[...]], axis=1)
    vc_b = vc_ref[0].astype(BF16)
    p_parts, o_parts = [], []
    for part in range(n_part):
        rsl = slice(part * part_rows, (part + 1) * part_rows)
        s = per_head(valid_c, _dot_nt(q_plain[rsl], kc_x), NEG_INF)
        p = per_head(valid_c, jnp.exp(s - jnp.max(s, axis=1, keepdims=True)), 0.0)
        l_c = jnp.sum(p, axis=1, keepdims=True)
        p = p * jnp.where(l_c > 0.0, 1.0 / l_c, 0.0)
        p_parts.append(p)
        o_parts.append(_dot(p.astype(BF16), vc_b))
    p_c = jnp.concatenate(p_parts, axis=0)
    o_cmp = jnp.concatenate(o_parts, axis=0)

    blk_t = lax.broadcasted_iota(jnp.int32, (n_blk, tq), 0)
    cur_t = (t0 + lax.broadcasted_iota(jnp.int32, (n_blk, tq), 1)) // SEL_BLOCK
    forced_t = (blk_t == 0) | (blk_t == cur_t) | (blk_t == cur_t - 1)
    sub = lax.broadcasted_iota(jnp.int32, (SUBLANES, tq), 0)
    n_top = float(min(SEL_TOPN, n_blk))
    for g in range(NSA_KV_HEADS):
        p_sum = p_c[g * NSA_REP * tq:(g * NSA_REP + 1) * tq]
        for r in range(1, NSA_REP):
            p_sum = p_sum + p_c[(g * NSA_REP + r) * tq:(g * NSA_REP + r + 1) * tq]
        p0, p1, p2 = _split3(p_sum)
        ovt = ovt_ref[...]
        imp_t = _dot_nt(ovt, p0) + _dot_nt(ovt, p1) + _dot_nt(ovt, p2)
        score = jnp.where(forced_t, FORCE_SCORE, jnp.where(blk_t <= cur_t, imp_t, -1.0))
        rank_scr[...] = jnp.full(rank_scr.shape, n_top, F32)

        def rank_chunk(c):
            chunk = score[c * SUBLANES:(c + 1) * SUBLANES]
            beaten = jnp.zeros((SUBLANES, tq), F32)
            for jp in range(n_blk):
                other = jnp.broadcast_to(score[jp:jp + 1], (SUBLANES, tq))
                ge = jnp.where(other >= chunk, 1.0, 0.0)
                gt = jnp.where(other > chunk, 1.0, 0.0)
                if jp < c * SUBLANES:
                    beaten = beaten + ge
                elif jp >= (c + 1) * SUBLANES:
                    beaten = beaten + gt
                else:
                    beaten = beaten + jnp.where(sub > jp - c * SUBLANES, ge, gt)
            rank_scr[c * SUBLANES:(c + 1) * SUBLANES, :] = beaten

        rank_chunk(0)
        for c in range(1, n_blk // SUBLANES):
            pl.when(c * SUBLANES <= (t0 + tq - 1) // SEL_BLOCK)(functools.partial(rank_chunk, c))
        neg_t = jnp.where(rank_scr[...] < n_top, 0.0, NEG_INF)
        neg = jnp.transpose(jnp.concatenate([neg_t, jnp.zeros((LANES - n_blk, tq), F32)], axis=0))
        for r in range(NSA_REP):
            k = g * NSA_REP + r
            q_const = (neg + q_slopes[k]).astype(BF16)
            qs_scr[k * tq:(k + 1) * tq, :] = jnp.concatenate([q_heads[k], q_const], axis=1)

    ones_s = jnp.ones((tks, LANES), BF16)

    def slc_score(j, part):
        ks = pl.multiple_of(j * tks, tks)
        kx = jnp.concatenate([ksl_ref[pl.ds(ks, tks), :], kpos_ref[pl.ds(ks, tks), :]], axis=1)
        return _dot_nt(qs_scr[part * part_rows:(part + 1) * part_rows, :], kx)

    def slc_values(j, part):
        ks = pl.multiple_of(j * tks, tks)
        return jnp.concatenate([vsl_ref[pl.ds(ks, tks), :], ones_s], axis=1)

    def slc_mask(s, j):
        t_row = t0 + lax.broadcasted_iota(jnp.int32, (tq, tks), 0)
        pos = j * tks + lax.broadcasted_iota(jnp.int32, (tq, tks), 1)
        return jnp.where((pos <= t_row)[None], s.reshape(part_rows // tq, tq, tks), NEG_INF).reshape(part_rows, tks)

    _flash_sweep(t0 // tks + 1, n_part, slc_score, slc_values, slc_mask, s_scr, p_scr, m_scr, acc_scr)
    acc = acc_scr[...].reshape(rows, 2 * LANES)
    o_slc = acc[:, :LANES] / acc[:, LANES:]

    start = pl.multiple_of(jnp.maximum(t0 - WINDOW, 0), tq)
    kx = jnp.concatenate([kwn_ref[pl.ds(start, win_keys), :], kpos_ref[pl.ds(start, win_keys), :]], axis=1)
    vx = jnp.concatenate([vwn_ref[pl.ds(start, win_keys), :], jnp.ones((win_keys, LANES), BF16)], axis=1)
    dist = (t0 + lax.broadcasted_iota(jnp.int32, (tq, win_keys), 0)
            - (start + lax.broadcasted_iota(jnp.int32, (tq, win_keys), 1)))
    in_window = (dist >= 0) & (dist < WINDOW)
    o_parts = []
    win_score = lambda part: _dot_nt(q_plain[part * part_rows:(part + 1) * part_rows], kx)
    s_next = win_score(0)
    for part in range(n_part):
        s = per_head(in_window, s_next, NEG_INF)
        if part + 1 < n_part:
            s_next = win_score(part + 1)
        prob = jnp.exp(s - jnp.max(s, axis=1, keepdims=True))
        acc = _dot(prob.astype(BF16), vx)
        o_parts.append(acc[:, :LANES] / acc[:, LANES:])
    o_win = jnp.concatenate(o_parts, axis=0)

    gates = jax.nn.sigmoid(sm_ref[:, COL_GL - COL_FA:COL_GL - COL_FA + NSA_HEADS * N_BRANCH] + gb_ref[...])
    for r in range(NSA_REP):
        halves = []
        for g in range(NSA_KV_HEADS):
            k = g * NSA_REP + r
            rsl = slice(k * tq, (k + 1) * tq)
            gate = lambda br: gates[:, k * N_BRANCH + br:k * N_BRANCH + br + 1]
            halves.append(gate(0) * o_cmp[rsl] + gate(1) * o_slc[rsl] + gate(2) * o_win[rsl])
        o_ref[:, r * LANES:(r + 1) * LANES] = jnp.where(first, halves[0], halves[1]).astype(BF16)


def _nsa(big, kpos, kc, vc, cpos, ovt, slope_tab, small, gate_b, bsz, t_len, tq=256, tks=512, n_part=3):
    nq = t_len // tq
    kv_col = COL_KV // KV_WIDTH
    kv_spec = lambda c: pl.BlockSpec((t_len, KV_WIDTH), lambda b, i: (b, kv_col + c))
    full = lambda a: pl.BlockSpec(a.shape, lambda b, i: tuple(0 for _ in a.shape))
    n_cmp = kc.shape[1]
    rows = NSA_HEADS * tq
    return pl.pallas_call(
        functools.partial(_nsa_kernel, tq=tq, tks=tks),
        grid=(bsz, nq),
        in_specs=[pl.BlockSpec((tq, NSA_WIDTH), lambda b, i: (b * nq + i, COL_QC // NSA_WIDTH)),
                  kv_spec(0), kv_spec(1), kv_spec(2), kv_spec(3), full(kpos),
                  pl.BlockSpec((1, n_cmp, KV_WIDTH), lambda b, i: (b, 0, 0)),
                  pl.BlockSpec((1, n_cmp, KV_WIDTH), lambda b, i: (b, 0, 0)),
                  full(cpos), full(ovt), full(slope_tab),
                  pl.BlockSpec((tq, LANES), lambda b, i: (b * nq + i, COL_FA // LANES)),
                  full(gate_b)],
        out_specs=pl.BlockSpec((tq, NSA_WIDTH), lambda b, i: (b * nq + i, 0)),
        out_shape=jax.ShapeDtypeStruct((bsz * t_len, NSA_WIDTH), BF16),
        scratch_shapes=[pltpu.VMEM((rows, 2 * LANES), BF16),
                        pltpu.VMEM((ovt.shape[0], tq), F32),
                        pltpu.VMEM((2, n_part, rows // n_part, tks), F32),
                        pltpu.VMEM((2, n_part, rows // n_part, tks), BF16),
                        pltpu.VMEM((n_part, rows // n_part, 1), F32),
                        pltpu.VMEM((n_part, rows // n_part, 2 * LANES), F32)],
        compiler_params=_params(("parallel", "arbitrary")),
        name="nsa",
    )(big, big, big, big, big, kpos, kc, vc, cpos, ovt, slope_tab, small, gate_b)


def _position_columns(pos):
    lane = jnp.arange(LANES)[None, :]
    hi = (SEL_BLOCK * (pos // SEL_BLOCK))[:, None]
    lo = (pos % SEL_BLOCK)[:, None]
    cols = jnp.where((lane >= COL_POS) & (lane < COL_POS + 3), hi, 0) + jnp.where((lane >= COL_POS + 3) & (lane < COL_POS + 6), lo, 0)
    return cols.astype(BF16)


def _nsa_tables(t_len):
    pos = jnp.arange(t_len)
    lane = jnp.arange(LANES)[None, :]
    block_one_hot = ((lane < t_len // SEL_BLOCK) & (lane == (pos // SEL_BLOCK)[:, None])).astype(BF16)
    kpos = _position_columns(pos) + block_one_hot
    n_cmp = t_len // CMP_STRIDE
    c_start = jnp.arange(n_cmp) * CMP_STRIDE
    cpos = _position_columns(c_start + CMP_LEN - 1)
    s_start = (jnp.arange(t_len // SEL_BLOCK) * SEL_BLOCK)[:, None]
    ovt = ((c_start[None, :] < s_start + SEL_BLOCK) & (c_start[None, :] + CMP_LEN - 1 >= s_start)).astype(BF16)
    slopes = jnp.exp2(-8.0 * jnp.arange(1, NSA_HEADS + 1, dtype=F32) / NSA_HEADS)
    pieces = jnp.stack([piece.astype(F32) for piece in _split3(slopes)], axis=1)
    slope_tab = jnp.zeros((NSA_HEADS, LANES), F32)
    slope_tab = slope_tab.at[:, COL_POS:COL_POS + 3].set(pieces).at[:, COL_POS + 3:COL_POS + 6].set(pieces)
    return kpos, cpos, ovt, slope_tab


def _mem_kv_kernel(mem_ref, g_ref, wk_ref, wv_ref, k_ref, v_ref):
    mb = _rms(mem_ref[0], g_ref[...]).astype(BF16)
    k_ref[0] = _dot(mb, wk_ref[...]).astype(BF16)
    v_ref[0] = _dot(mb, wv_ref[...]).astype(BF16)


def _mem_kv(mem, g, wk, wv):
    bsz, m_len, _ = mem.shape
    w_spec = pl.BlockSpec((D_MODEL, D_MODEL), lambda b: (0, 0))
    m_spec = pl.BlockSpec((1, m_len, D_MODEL), lambda b: (b, 0, 0))
    return pl.pallas_call(
        _mem_kv_kernel,
        grid=(bsz,),
        in_specs=[m_spec, pl.BlockSpec((1, D_MODEL), lambda b: (0, 0)), w_spec, w_spec],
        out_specs=[m_spec, m_spec],
        out_shape=[jax.ShapeDtypeStruct((bsz, m_len, D_MODEL), BF16)] * 2,
        compiler_params=_params(("parallel",)),
        name="mem_kv",
    )(mem, g, wk, wv)


def _route(logits):
    lane = lax.broadcasted_iota(jnp.int32, logits.shape, 1)
    big_i = jnp.int32(1 << 20)
    is_coarse = lane < N_GROUPS
    lc = jnp.where(is_coarse, logits, NEG_INF)
    mc = jnp.max(lc, axis=1, keepdims=True)
    grp = jnp.min(jnp.where(is_coarse & (lc == mc), lane, big_i), axis=1, keepdims=True)
    w_grp = 1.0 / jnp.sum(jnp.where(is_coarse, jnp.exp(lc - mc), 0.0), axis=1, keepdims=True)
    e_id = lane - N_GROUPS
    in_grp = (e_id >= grp * EXPERTS_PER_GROUP) & (e_id < (grp + 1) * EXPERTS_PER_GROUP)
    lf = jnp.where(in_grp, logits, NEG_INF)
    m1 = jnp.max(lf, axis=1, keepdims=True)
    i1 = jnp.min(jnp.where(in_grp & (lf == m1), lane, big_i), axis=1, keepdims=True)
    lf2 = jnp.where(lane == i1, NEG_INF, lf)
    m2 = jnp.max(lf2, axis=1, keepdims=True)
    i2 = jnp.min(jnp.where(in_grp & (lane != i1) & (lf2 == m2), lane, big_i), axis=1, keepdims=True)
    denom = jnp.sum(jnp.where(in_grp, jnp.exp(lf - m1), 0.0), axis=1, keepdims=True)
    p1 = 1.0 / denom
    p2 = jnp.exp(m2 - m1) / denom
    tot = p1 + p2
    w1 = w_grp * (p1 / tot)
    w2 = w_grp * (p2 / tot)
    weights = jnp.where(lane + N_GROUPS == i1, w1, jnp.where(lane + N_GROUPS == i2, w2, 0.0))
    return jnp.where(lane == grp + N_EXPERTS, 1.0, weights)


def _mix_out_kernel(x_ref, oa_ref, ob_ref, oc_ref, woa_ref, wob_ref, woc_ref, gx_ref, wq_ref, mk_ref, mv_ref,
                    wo_ref, gf_ref, wr_ref, br_ref, x2_ref, hb_ref, comb_ref, cnt_ref):
    n_part = 2
    rows = x_ref.shape[0] // n_part
    parts = [slice(k * rows, (k + 1) * rows) for k in range(n_part)]
    x1 = [x_ref[r, :] + _dot(oa_ref[r, :], woa_ref[...]) + _dot(ob_ref[r, :], wob_ref[...])
          + _dot(oc_ref[r, :], woc_ref[...]) for r in parts]
    hq = [_rms(v, gx_ref[...]).astype(BF16) for v in x1]
    q = [(_dot(v, wq_ref[...]) * (MEM_HEAD_DIM ** -0.5)).astype(BF16) for v in hq]
    heads = [[] for _ in parts]
    for h in range(MEM_HEADS):
        sl = slice(h * MEM_HEAD_DIM, (h + 1) * MEM_HEAD_DIM)
        for k in range(n_part):
            s = _dot_nt(q[k][:, sl], mk_ref[0, :, sl])
            p = jnp.exp(s - jnp.max(s, axis=1, keepdims=True))
            p = p / jnp.sum(p, axis=1, keepdims=True)
            heads[k].append(_dot(p.astype(BF16), mv_ref[0, :, sl]).astype(BF16))
    x2 = [x1[k] + _dot(jnp.concatenate(heads[k], axis=1), wo_ref[...]) for k in range(n_part)]
    hn = [_rms(v, gf_ref[...]) for v in x2]
    for k, r in enumerate(parts):
        x2_ref[r, :] = x2[k]
        hb_ref[r, :] = hn[k].astype(BF16)
        h0, h1, _ = _split3(hn[k])
        partial = _dot(h0, wr_ref[...]) + _dot(h1, wr_ref[...])
        logits = partial + pltpu.roll(partial, ROUTER_WIDTH - ROUTER_PIECE, axis=1) + br_ref[...]
        routed = _route(logits)
        comb_ref[r, :] = routed
        per_part = rows // MOE_TILE
        for sub in range(per_part):
            cnt_ref[k * per_part + sub] = jnp.sum(routed[sub * MOE_TILE:(sub + 1) * MOE_TILE], axis=0, keepdims=True)


def _mix_out(x, o_a, o_b, o_c, wo_a, wo_b, wo_c, g_x, wq, mem_k, mem_v, wo, g_f, w_r, b_r, t_len, tm=1024):
    n = x.shape[0]
    per_batch = t_len // tm
    m_len = mem_k.shape[1]
    tile = lambda width: pl.BlockSpec((tm, width), lambda i: (i, 0))
    full = lambda shape: pl.BlockSpec(shape, lambda i: tuple(0 for _ in shape))
    mem_spec = pl.BlockSpec((1, m_len, D_MODEL), lambda i: (i // per_batch, 0, 0))
    return pl.pallas_call(
        _mix_out_kernel,
        grid=(n // tm,),
        in_specs=[tile(D_MODEL), tile(FOX_WIDTH), tile(POOL_WIDTH), tile(NSA_WIDTH),
                  full(wo_a.shape), full(wo_b.shape), full(wo_c.shape), full(g_x.shape), full(wq.shape),
                  mem_spec, mem_spec, full(wo.shape), full(g_f.shape), full(w_r.shape), full(b_r.shape)],
        out_specs=[tile(D_MODEL), tile(D_MODEL), tile(ROUTER_WIDTH),
                   pl.BlockSpec((tm // MOE_TILE, 1, ROUTER_WIDTH), lambda i: (i, 0, 0))],
        out_shape=[jax.ShapeDtypeStruct((n, D_MODEL), F32),
                   jax.ShapeDtypeStruct((n, D_MODEL), BF16),
                   jax.ShapeDtypeStruct((n, ROUTER_WIDTH), F32),
                   jax.ShapeDtypeStruct((n // MOE_TILE, 1, ROUTER_WIDTH), F32)],
        compiler_params=_params(("parallel",)),
        name="mix_out",
    )(x, o_a, o_b, o_c, wo_a, wo_b, wo_c, g_x, wq, mem_k, mem_v, wo, g_f, w_r, b_r)


def _group_ranks(comb, tri):
    lane = lax.broadcasted_iota(jnp.int32, comb.shape, 1)
    member = jnp.where((lane >= N_EXPERTS) & (lane < N_EXPERTS + N_GROUPS), comb, 0.0)
    return member, _dot(tri, member.astype(BF16))


def _moe_sort_kernel(dst_ref, hb_ref, comb_ref, tri_ref, xs_in, cs_in, xs_ref, cs_ref, stage_x, stage_c, sem):
    del xs_in, cs_in
    t = pl.program_id(0)
    comb = comb_ref[...]
    member, rank = _group_ranks(comb, tri_ref[...])
    rank_t = jnp.transpose(rank)
    member_t = jnp.transpose(member)
    slot = lax.broadcasted_iota(jnp.int32, (MOE_TILE, MOE_TILE), 0).astype(F32)
    pieces = _split3(comb)
    buf = t % 2

    def copies(into, g, dst):
        return (pltpu.make_async_copy(stage_x.at[into, g], xs_ref.at[pl.ds(dst, MOE_TILE)], sem.at[0, into, g]),
                pltpu.make_async_copy(stage_c.at[into, g], cs_ref.at[pl.ds(dst, MOE_TILE)], sem.at[1, into, g]))

    for g in range(N_GROUPS):
        row = N_EXPERTS + g
        take = jnp.where((rank_t[row:row + 1, :] == slot) & (member_t[row:row + 1, :] > 0.5), 1.0, 0.0).astype(BF16)
        stage_x[buf, g] = _dot(take, hb_ref[...]).astype(BF16)
        stage_c[buf, g] = _dot(take, pieces[0]) + _dot(take, pieces[1]) + _dot(take, pieces[2])

    @pl.when(t > 0)
    def _():
        for g in range(N_GROUPS):
            for copy in copies(1 - buf, g, 0):
                copy.wait()

    for g in range(N_GROUPS):
        dst = pl.multiple_of(dst_ref[t * N_GROUPS + g], MOE_SEG_ALIGN)
        for copy in copies(buf, g, dst):
            copy.start()

    @pl.when(t == pl.num_programs(0) - 1)
    def _():
        for g in range(N_GROUPS):
            for copy in copies(buf, g, 0):
                copy.wait()


def _moe_sort(hb, comb, tri, seg_dst, rows):
    n = hb.shape[0]
    any_spec = pl.BlockSpec(memory_space=pl.ANY)
    xs0 = jnp.zeros((rows, D_MODEL), BF16)
    cs0 = jnp.zeros((rows, ROUTER_WIDTH), F32)
    return pl.pallas_call(
        _moe_sort_kernel,
        grid_spec=pltpu.PrefetchScalarGridSpec(
            num_scalar_prefetch=1,
            grid=(n // MOE_TILE,),
            in_specs=[pl.BlockSpec((MOE_TILE, D_MODEL), lambda t, dst: (t, 0)),
                      pl.BlockSpec((MOE_TILE, ROUTER_WIDTH), lambda t, dst: (t, 0)),
                      pl.BlockSpec((MOE_TILE, MOE_TILE), lambda t, dst: (0, 0)),
                      any_spec, any_spec],
            out_specs=[any_spec, any_spec],
            scratch_shapes=[pltpu.VMEM((2, N_GROUPS, MOE_TILE, D_MODEL), BF16),
                            pltpu.VMEM((2, N_GROUPS, MOE_TILE, ROUTER_WIDTH), F32),
                            pltpu.SemaphoreType.DMA((2, 2, N_GROUPS))]),
        out_shape=[jax.ShapeDtypeStruct((rows, D_MODEL), BF16), jax.ShapeDtypeStruct((rows, ROUTER_WIDTH), F32)],
        input_output_aliases={4: 0, 5: 1},
        compiler_params=_params(("arbitrary",)),
        name="moe_sort",
    )(seg_dst, hb, comb, tri, xs0, cs0)


def _moe_ffn_kernel(group_ref, valid_ref, xs_ref, cs_ref, wg_ref, wu_ref, wd_ref, ys_ref):
    i = pl.program_id(0)

    @pl.when(valid_ref[i] == 0)
    def _():
        ys_ref[...] = jnp.zeros(ys_ref.shape, F32)

    @pl.when(valid_ref[i] != 0)
    def _():
        x = xs_ref[...]
        cs = cs_ref[...]
        lane = lax.broadcasted_iota(jnp.int32, cs.shape, 1)
        first_expert = group_ref[i] * EXPERTS_PER_GROUP
        total = None
        for e in range(EXPERTS_PER_GROUP):
            he = (jax.nn.silu(_dot(x, wg_ref[0, e])) * _dot(x, wu_ref[0, e])).astype(BF16)
            c_e = jnp.sum(jnp.where(lane == first_expert + e, cs, 0.0), axis=1, keepdims=True)
            y = c_e * _dot(he, wd_ref[0, e])
            total = y if total is None else total + y
        ys_ref[...] = total


def _moe_ffn(xs, cs, wg, wu, wd, layer, tile_group, tile_valid):
    rows = xs.shape[0]
    tile = lambda width: pl.BlockSpec((MOE_FFN_TILE, width), lambda i, grp, ok: (i, 0))
    experts = lambda shape: pl.BlockSpec((1, EXPERTS_PER_GROUP) + shape, lambda i, grp, ok: (layer, grp[i], 0, 0))
    return pl.pallas_call(
        _moe_ffn_kernel,
        grid_spec=pltpu.PrefetchScalarGridSpec(
            num_scalar_prefetch=2,
            grid=(rows // MOE_FFN_TILE,),
            in_specs=[tile(D_MODEL), tile(ROUTER_WIDTH),
                      experts((D_MODEL, EXPERT_FF)), experts((D_MODEL, EXPERT_FF)), experts((EXPERT_FF, D_MODEL))],
            out_specs=tile(D_MODEL)),
        out_shape=jax.ShapeDtypeStruct((rows, D_MODEL), F32),
        compiler_params=_params(("arbitrary",)),
        name="moe_ffn",
    )(tile_group, tile_valid, xs, cs, wg, wu, wd)


def _moe_unsort_kernel(dst_ref, ys_ref, comb_ref, tri_ref, x_ref, gfin_ref, o_ref, buf, sem, *, final_norm):
    t = pl.program_id(0)
    slot = t % 2

    def copy(tile, g, into):
        dst = pl.multiple_of(dst_ref[tile * N_GROUPS + g], MOE_SEG_ALIGN)
        return pltpu.make_async_copy(ys_ref.at[pl.ds(dst, MOE_TILE)], buf.at[into, g], sem.at[into, g])

    @pl.when(t == 0)
    def _():
        for g in range(N_GROUPS):
            copy(0, g, 0).start()

    @pl.when(t + 1 < pl.num_programs(0))
    def _():
        for g in range(N_GROUPS):
            copy(t + 1, g, 1 - slot).start()

    for g in range(N_GROUPS):
        copy(t, g, slot).wait()
    member, rank = _group_ranks(comb_ref[...], tri_ref[...])
    lane_slot = lax.broadcasted_iota(jnp.int32, (MOE_TILE, MOE_TILE), 1).astype(F32)
    y = x_ref[...]
    for g in range(N_GROUPS):
        col = N_EXPERTS + g
        put = jnp.where((rank[:, col:col + 1] == lane_slot) & (member[:, col:col + 1] > 0.5), 1.0, 0.0).astype(BF16)
        seg = buf[slot, g]
        hi = seg.astype(BF16)
        lo = (seg - hi.astype(F32)).astype(BF16)
        y = y + (_dot(put, hi) + _dot(put, lo))
    o_ref[...] = _rms(y, gfin_ref[...]) if final_norm else y


def _moe_unsort(ys, comb, tri, x2, g_fin, seg_dst, final_norm):
    n = x2.shape[0]
    return pl.pallas_call(
        functools.partial(_moe_unsort_kernel, final_norm=final_norm),
        grid_spec=pltpu.PrefetchScalarGridSpec(
            num_scalar_prefetch=1,
            grid=(n // MOE_TILE,),
            in_specs=[pl.BlockSpec(memory_space=pl.ANY),
                      pl.BlockSpec((MOE_TILE, ROUTER_WIDTH), lambda t, dst: (t, 0)),
                      pl.BlockSpec((MOE_TILE, MOE_TILE), lambda t, dst: (0, 0)),
                      pl.BlockSpec((MOE_TILE, D_MODEL), lambda t, dst: (t, 0)),
                      pl.BlockSpec((1, D_MODEL), lambda t, dst: (0, 0))],
            out_specs=pl.BlockSpec((MOE_TILE, D_MODEL), lambda t, dst: (t, 0)),
            scratch_shapes=[pltpu.VMEM((2, N_GROUPS, MOE_TILE, D_MODEL), F32),
                            pltpu.SemaphoreType.DMA((2, N_GROUPS))]),
        out_shape=jax.ShapeDtypeStruct((n, D_MODEL), F32),
        compiler_params=_params(("arbitrary",)),
        name="moe_unsort",
    )(seg_dst, ys, comb, tri, x2, g_fin)


def _moe_plan(cnt, n_tok):
    n_tiles = cnt.shape[0]
    rows = n_tok + n_tiles * N_GROUPS * MOE_SEG_ALIGN + N_GROUPS * (MOE_FFN_TILE + MOE_GAP)
    rows = -(-rows // MOE_FFN_TILE) * MOE_FFN_TILE
    padded = (cnt + MOE_SEG_ALIGN - 1) // MOE_SEG_ALIGN * MOE_SEG_ALIGN
    within = jnp.cumsum(padded, axis=0) - padded
    region = (jnp.sum(padded, axis=0) + MOE_FFN_TILE - 1) // MOE_FFN_TILE * MOE_FFN_TILE
    start = jnp.cumsum(region + MOE_GAP) - (region + MOE_GAP)
    seg_dst = (start[None, :] + within).reshape(-1).astype(jnp.int32)
    tile_row = jnp.arange(rows // MOE_FFN_TILE) * MOE_FFN_TILE
    begun = tile_row[:, None] >= start[None, :]
    tile_valid = jnp.any(begun & (tile_row[:, None] < (start + region)[None, :]), axis=1).astype(jnp.int32)
    tile_group = jnp.clip(jnp.sum(begun, axis=1) - 1, 0, N_GROUPS - 1).astype(jnp.int32)
    return seg_dst, tile_group, tile_valid, rows


def _moe(x2, hb, comb, cnt, wg, wu, wd, layer, g_fin, final_norm):
    n = x2.shape[0]
    counts = jnp.round(cnt[:, 0, N_EXPERTS:N_EXPERTS + N_GROUPS]).astype(jnp.int32)
    seg_dst, tile_group, tile_valid, rows = _moe_plan(counts, n)
    tri = (jnp.arange(MOE_TILE)[:, None] > jnp.arange(MOE_TILE)[None, :]).astype(BF16)
    xs, cs = _moe_sort(hb, comb, tri, seg_dst, rows)
    ys = _moe_ffn(xs, cs, wg, wu, wd, layer, tile_group, tile_valid)
    return _moe_unsort(ys, comb, tri, x2, g_fin, seg_dst, final_norm)


def _pair_order(a, axis):
    shape = a.shape
    a = a.reshape(shape[:axis] + (NSA_HEADS, HEAD_DIM) + shape[axis + 1:])
    a = jnp.take(a, jnp.array(NSA_PAIR_ORDER), axis=axis)
    return a.reshape(shape)


def _split_in_proj_weight(w):
    edges = [0]
    for width in (FOX_WIDTH, FOX_WIDTH, FOX_WIDTH, FOX_HEADS, POOL_WIDTH, NSA_WIDTH) + (KV_WIDTH,) * 6 + (NSA_HEADS * N_BRANCH,):
        edges.append(edges[-1] + width)
    part = lambda k: w[:, edges[k]:edges[k + 1]]
    qa, ka, va, fa, ub, qc = (part(k) for k in range(6))
    kvs = [part(k) for k in range(6, 12)]
    gl = part(12)
    w_big = jnp.concatenate([qa, ka, va, _pair_order(qc, 1)] + kvs[2:], axis=1).astype(BF16)
    pad = jnp.zeros((w.shape[0], LANES - FOX_HEADS - NSA_HEADS * N_BRANCH), w.dtype)
    w_small = jnp.concatenate([ub, fa, gl, pad] + kvs[:2], axis=1).astype(BF16)
    return w_big, w_small


def _block_diag(w):
    groups, cin, cout = w.shape
    out = jnp.zeros((groups * cin, groups * cout), w.dtype)
    for gi in range(groups):
        out = out.at[gi * cin:(gi + 1) * cin, gi * cout:(gi + 1) * cout].set(w[gi])
    return out


def _layer(x, mem, p, experts, layer, bsz, t_len, final_g, final_norm):
    row = lambda v: v.reshape(1, -1)
    w_big, w_small = _split_in_proj_weight(p["w_in"])
    big, small = _in_proj(x, row(p["mix_norm_g"]), w_big, w_small)

    b_row = jnp.pad(p["fox_forget_b"], (0, LANES - FOX_HEADS)).reshape(1, LANES)
    ck = _forget_cumsum(small, b_row, _forget_placement(), bsz, t_len)
    o_a = _fox(big, ck, bsz, t_len)

    o_b = _pool(small, _block_diag(p["pool_w"]).astype(BF16), row(p["pool_scale"]), bsz, t_len)

    pe_k, w1_k, w2_k = _compress_weights(p["cmp_pe_k"], p["cmp_w1_k"], p["cmp_w2_k"])
    pe_v, w1_v, w2_v = _compress_weights(p["cmp_pe_v"], p["cmp_w1_v"], p["cmp_w2_v"])
    kc, vc = _compress(small, pe_k, pe_v, w1_k, w2_k, w1_v, w2_v, bsz, t_len)
    kpos, cpos, ovt, slope_tab = _nsa_tables(t_len)
    o_c = _nsa(big, kpos, kc, vc, cpos, ovt, slope_tab, small, row(p["nsa_gate_b"]), bsz, t_len)

    mem_k, mem_v = _mem_kv(mem, row(p["mem_norm_g"]), p["xattn_wk"].astype(BF16), p["xattn_wv"].astype(BF16))
    w_out = p["w_out"]
    wo_a = w_out[:FOX_WIDTH].astype(BF16)
    wo_b = w_out[FOX_WIDTH:FOX_WIDTH + POOL_WIDTH].astype(BF16)
    wo_c = _pair_order(w_out[FOX_WIDTH + POOL_WIDTH:], 0).astype(BF16)
    w_r = jnp.concatenate([p["router_coarse_w"], p["router_fine_w"],
                           jnp.zeros((D_MODEL, ROUTER_PIECE - N_GROUPS - N_EXPERTS), F32)], axis=1)
    w_r0, w_r1, _ = _split3(w_r)
    w_r = jnp.concatenate([w_r0, w_r1, jnp.zeros((D_MODEL, ROUTER_WIDTH - 2 * ROUTER_PIECE), BF16)], axis=1)
    b_r = jnp.concatenate([p["router_coarse_b"], p["router_fine_b"],
                           jnp.zeros((ROUTER_WIDTH - N_GROUPS - N_EXPERTS,), F32)]).reshape(1, -1)
    x2, hb, comb, cnt = _mix_out(x, o_a, o_b, o_c, wo_a, wo_b, wo_c, row(p["xattn_norm_g"]),
                                 p["xattn_wq"].astype(BF16), mem_k, mem_v, p["xattn_wo"].astype(BF16),
                                 row(p["ffn_norm_g"]), w_r, b_r, t_len)

    return _moe(x2, hb, comb, cnt, *experts, layer, row(final_g), final_norm)


def kernel(x, mem, mix_norm_g, w_in, fox_forget_b, pool_w, pool_scale, cmp_pe_k, cmp_pe_v, cmp_w1_k, cmp_w2_k, cmp_w1_v, cmp_w2_v, nsa_gate_b, w_out, xattn_norm_g, mem_norm_g, xattn_wq, xattn_wk, xattn_wv, xattn_wo, ffn_norm_g, router_coarse_w, router_coarse_b, router_fine_w, router_fine_b, exp_w_gate, exp_w_up, exp_w_down, final_norm_g):
    stacked = dict(mix_norm_g=mix_norm_g, w_in=w_in, fox_forget_b=fox_forget_b, pool_w=pool_w, pool_scale=pool_scale,
                   cmp_pe_k=cmp_pe_k, cmp_pe_v=cmp_pe_v, cmp_w1_k=cmp_w1_k, cmp_w2_k=cmp_w2_k, cmp_w1_v=cmp_w1_v,
                   cmp_w2_v=cmp_w2_v, nsa_gate_b=nsa_gate_b, w_out=w_out, xattn_norm_g=xattn_norm_g,
                   mem_norm_g=mem_norm_g, xattn_wq=xattn_wq, xattn_wk=xattn_wk, xattn_wv=xattn_wv, xattn_wo=xattn_wo,
                   ffn_norm_g=ffn_norm_g, router_coarse_w=router_coarse_w, router_coarse_b=router_coarse_b,
                   router_fine_w=router_fine_w, router_fine_b=router_fine_b)
    experts = (exp_w_gate.astype(BF16), exp_w_up.astype(BF16), exp_w_down.astype(BF16))
    bsz, t_len, d = x.shape
    depth = w_in.shape[0]
    h = x.reshape(bsz * t_len, d)
    for layer in range(depth):
        p = {name: value[layer] for name, value in stacked.items()}
        h = _layer(h, mem, p, experts, layer, bsz, t_len, final_norm_g, final_norm=(layer == depth - 1))
    return h.reshape(bsz, t_len, d)
```

```python
import functools

import jax
import jax.numpy as jnp
from jax import lax
from jax.experimental import pallas as pl
from jax.experimental.pallas import tpu as pltpu

F32 = jnp.float32
BF16 = jnp.bfloat16

D_MODEL = 1024
HEAD_DIM = 64
FOX_HEADS = 6
FOX_WIDTH = FOX_HEADS * HEAD_DIM
POOL_WIDTH = 256
POOL_GROUP_DIM = 64
POOL_WINDOWS = (2, 4, 8, 16)
POOL_HALO = 16
NSA_HEADS = 6
NSA_KV_HEADS = 2
NSA_REP = NSA_HEADS // NSA_KV_HEADS
NSA_WIDTH = NSA_HEADS * HEAD_DIM
KV_WIDTH = NSA_KV_HEADS * HEAD_DIM
N_BRANCH = 3
CMP_LEN = 32
CMP_STRIDE = 16
CMP_HIDDEN = 256
SEL_BLOCK = 64
SEL_TOPN = 16
WINDOW = 512
FORCE_SCORE = 1.0e4
MEM_HEADS = 4
MEM_HEAD_DIM = D_MODEL // MEM_HEADS
N_GROUPS = 4
EXPERTS_PER_GROUP = 4
N_EXPERTS = N_GROUPS * EXPERTS_PER_GROUP
EXPERT_FF = 512
EPS = 1e-6
NEG_INF = -1e30

LANES = 128
SUBLANES = 8
BIG_WIDTH = 3 * FOX_WIDTH + NSA_WIDTH + 4 * KV_WIDTH
COL_QC = 3 * FOX_WIDTH
COL_KV = COL_QC + NSA_WIDTH
SMALL_WIDTH = POOL_WIDTH + LANES + 2 * KV_WIDTH
COL_FA = POOL_WIDTH
COL_GL = POOL_WIDTH + FOX_HEADS
COL_CMP = POOL_WIDTH + LANES
ROUTER_WIDTH = LANES
ROUTER_PIECE = 32
MOE_TILE = 256
MOE_SEG_ALIGN = 16
MOE_FFN_TILE = 512
MOE_GAP = 512
NSA_PAIR_ORDER = tuple(g * NSA_REP + r for r in range(NSA_REP) for g in range(NSA_KV_HEADS))
COL_POS = HEAD_DIM

VMEM_LIMIT = 56 * 1024 * 1024

_NT = (((1,), (1,)), ((), ()))


def _params(semantics):
    return pltpu.CompilerParams(dimension_semantics=semantics, vmem_limit_bytes=VMEM_LIMIT)


def _rms(x, g):
    return x * lax.rsqrt(jnp.mean(x * x, axis=-1, keepdims=True) + EPS) * g


def _dot(a, b):
    return jnp.dot(a, b, preferred_element_type=F32)


def _dot_nt(a, b):
    return lax.dot_general(a, b, _NT, preferred_element_type=F32)


def _split3(a):
    a0 = a.astype(BF16)
    r = a - a0.astype(F32)
    a1 = r.astype(BF16)
    a2 = (r - a1.astype(F32)).astype(BF16)
    return a0, a1, a2


def _in_proj_kernel(x_ref, g_ref, wb_ref, ws_ref, big_ref, small_ref):
    yb = _rms(x_ref[...], g_ref[...]).astype(BF16)
    big_ref[...] = _dot(yb, wb_ref[...]).astype(BF16)
    small_ref[...] = _dot(yb, ws_ref[...])


def _in_proj(x, g, w_big, w_small, tm=512):
    n = x.shape[0]
    return pl.pallas_call(
        _in_proj_kernel,
        grid=(n // tm,),
        in_specs=[pl.BlockSpec((tm, D_MODEL), lambda i: (i, 0)),
                  pl.BlockSpec((1, D_MODEL), lambda i: (0, 0)),
                  pl.BlockSpec((D_MODEL, BIG_WIDTH), lambda i: (0, 0)),
                  pl.BlockSpec((D_MODEL, SMALL_WIDTH), lambda i: (0, 0))],
        out_specs=[pl.BlockSpec((tm, BIG_WIDTH), lambda i: (i, 0)),
                   pl.BlockSpec((tm, SMALL_WIDTH), lambda i: (i, 0))],
        out_shape=[jax.ShapeDtypeStruct((n, BIG_WIDTH), BF16),
                   jax.ShapeDtypeStruct((n, SMALL_WIDTH), F32)],
        compiler_params=_params(("parallel",)),
        name="in_proj",
    )(x, g, w_big, w_small)


def _forget_kernel(f_ref, b_ref, place_ref, ck_ref):
    c = jax.nn.log_sigmoid(f_ref[...] + b_ref[...])
    t_len = c.shape[0]
    row = lax.broadcasted_iota(jnp.int32, c.shape, 0)
    sh = 1
    while sh < t_len:
        c = c + jnp.where(row >= sh, pltpu.roll(c, sh, axis=0), 0.0)
        sh *= 2
    n0, n1, n2 = _split3(-c)
    ck_ref[...] = (_dot(n0, place_ref[0]) + _dot(n1, place_ref[1]) + _dot(n2, place_ref[2])).astype(BF16)


def _forget_cumsum(small, b_row, place, bsz, t_len):
    return pl.pallas_call(
        _forget_kernel,
        grid=(bsz,),
        in_specs=[pl.BlockSpec((t_len, LANES), lambda b: (b, COL_FA // LANES)),
                  pl.BlockSpec((1, LANES), lambda b: (0, 0)),
                  pl.BlockSpec(place.shape, lambda b: (0, 0, 0))],
        out_specs=pl.BlockSpec((t_len, FOX_WIDTH), lambda b: (b, 0)),
        out_shape=jax.ShapeDtypeStruct((bsz * t_len, FOX_WIDTH), BF16),
        compiler_params=_params(("parallel",)),
        name="forget",
    )(small, b_row, place)


def _forget_placement():
    src = jnp.arange(LANES)[None, :, None]
    dst = jnp.arange(FOX_WIDTH)[None, None, :]
    piece = jnp.arange(3)[:, None, None]
    target = LANES * (src // 2) + 3 * (src % 2) + piece
    return ((src < FOX_HEADS) & (dst == target)).astype(BF16)


def _flash_sweep(n_tiles, n_stream, score, values, last_mask, s_scr, p_scr, m_scr, acc_scr):
    def run_stage(j, a, first=False, last=False):
        b = 1 - a
        for st in range(n_stream):
            s = score(0, st) if first else s_scr[a, st]
            if last:
                s = last_mask(s, j)
            m_old = jnp.full(m_scr.shape[1:], NEG_INF, F32) if first else m_scr[st]
            m_new = jnp.maximum(m_old, jnp.max(s, axis=1, keepdims=True))
            prob = jnp.exp(s - m_new).astype(BF16)
            if not last:
                p_scr[a, st] = prob
                s_scr[b, st] = score(j + 1, st)
            acc = jnp.zeros(acc_scr.shape[1:], F32) if first else acc_scr[st]
            if not first:
                acc = (acc + _dot(p_scr[b, st], values(j - 1, st))) * jnp.exp(m_old - m_new)
            if last:
                acc = acc + _dot(prob, values(j, st))
            acc_scr[st] = acc
            m_scr[st] = m_new

    n_main = n_tiles - 1

    @pl.when(n_main == 0)
    def _():
        run_stage(0, 0, first=True, last=True)

    @pl.when(n_main > 0)
    def _():
        run_stage(0, 0, first=True)

    def two_stages(jj, carry):
        run_stage(2 * jj + 1, 1)
        run_stage(2 * jj + 2, 0)
        return carry

    lax.fori_loop(0, jnp.maximum(n_main - 1, 0) // 2, two_stages, 0)

    @pl.when(jnp.logical_and(n_main >= 2, n_main % 2 == 0))
    def _():
        run_stage(n_main - 1, 1)

    for a in range(2):
        @pl.when(jnp.logical_and(n_main >= 1, n_main % 2 == a))
        def _():
            run_stage(n_main, a, last=True)


def _fox_kernel(q_ref, k_ref, v_ref, ck_ref, o_ref, q_scr, s_scr, p_scr, m_scr, acc_scr, *, tq, tk):
    i = pl.program_id(1)
    t0 = i * tq
    n_pair = FOX_HEADS // 2
    lane = lax.broadcasted_iota(jnp.int32, (tq, LANES), 1)
    first = lane < HEAD_DIM
    scale = jnp.asarray(HEAD_DIM ** -0.5, BF16)
    pick_a = jnp.where(lane < 3, 1.0, 0.0).astype(BF16)
    pick_b = jnp.where((lane >= 3) & (lane < 6), 1.0, 0.0).astype(BF16)
    for p in range(n_pair):
        qp = q_ref[:, p * LANES:(p + 1) * LANES] * scale
        zero = jnp.zeros_like(qp)
        qa = jnp.concatenate([jnp.where(first, qp, zero), pick_a], axis=1)
        qb = jnp.concatenate([jnp.where(first, zero, qp), pick_b], axis=1)
        q_scr[p] = jnp.concatenate([qa, qb], axis=0)
    ones_v = jnp.ones((tk, LANES), BF16)

    def score(j, p):
        ks = pl.multiple_of(j * tk, tk)
        psl = slice(p * LANES, (p + 1) * LANES)
        kx = jnp.concatenate([k_ref[pl.ds(ks, tk), psl], ck_ref[pl.ds(ks, tk), psl]], axis=1)
        return _dot_nt(q_scr[p], kx)

    def values(j, p):
        ks = pl.multiple_of(j * tk, tk)
        return jnp.concatenate([v_ref[pl.ds(ks, tk), p * LANES:(p + 1) * LANES], ones_v], axis=1)

    def causal_mask(s, j):
        t_row = t0 + lax.broadcasted_iota(jnp.int32, (tq, tk), 0)
        pos = j * tk + lax.broadcasted_iota(jnp.int32, (tq, tk), 1)
        return jnp.where((pos <= t_row)[None], s.reshape(2, tq, tk), NEG_INF).reshape(2 * tq, tk)

    _flash_sweep(t0 // tk + 1, n_pair, score, values, causal_mask, s_scr, p_scr, m_scr, acc_scr)
    for p in range(n_pair):
        acc = acc_scr[p]
        oa = acc[:tq, :LANES] / acc[:tq, LANES:]
        ob = acc[tq:, :LANES] / acc[tq:, LANES:]
        o_ref[:, p * LANES:(p + 1) * LANES] = jnp.where(first, oa, ob).astype(BF16)


def _fox(big, ck, bsz, t_len, tq=256, tk=512):
    nq = t_len // tq
    n_pair = FOX_HEADS // 2
    seq = lambda c: pl.BlockSpec((t_len, FOX_WIDTH), lambda b, i: (b, c))
    return pl.pallas_call(
        functools.partial(_fox_kernel, tq=tq, tk=tk),
        grid=(bsz, nq),
        in_specs=[pl.BlockSpec((tq, FOX_WIDTH), lambda b, i: (b * nq + i, 0)), seq(1), seq(2), seq(0)],
        out_specs=pl.BlockSpec((tq, FOX_WIDTH), lambda b, i: (b * nq + i, 0)),
        out_shape=jax.ShapeDtypeStruct((bsz * t_len, FOX_WIDTH), BF16),
        scratch_shapes=[pltpu.VMEM((n_pair, 2 * tq, 2 * LANES), BF16),
                        pltpu.VMEM((2, n_pair, 2 * tq, tk), F32),
                        pltpu.VMEM((2, n_pair, 2 * tq, tk), BF16),
                        pltpu.VMEM((n_pair, 2 * tq, 1), F32),
                        pltpu.VMEM((n_pair, 2 * tq, 2 * LANES), F32)],
        compiler_params=_params(("parallel", "arbitrary")),
        name="fox",
    )(big, big, big, ck)


def _pool_kernel(u_ref, halo_ref, w_ref, scale_ref, o_ref, x_scr, *, tt):
    i = pl.program_id(1)
    u = u_ref[...]
    x_scr[POOL_HALO:, :] = u
    x_scr[:POOL_HALO, :] = jnp.where(i > 0, halo_ref[...], 0.0)
    t_pos = i * tt + lax.broadcasted_iota(jnp.int32, (tt, 1), 0)
    lane = lax.broadcasted_iota(jnp.int32, (tt, POOL_WIDTH), 1)
    run = u
    pooled = jnp.zeros_like(u)
    for back in range(1, max(POOL_WINDOWS)):
        run = run + x_scr[POOL_HALO - back:POOL_HALO - back + tt, :]
        w = back + 1
        if w in POOL_WINDOWS:
            gi = POOL_WINDOWS.index(w)
            count = jnp.minimum(t_pos + 1, w).astype(F32)
            in_group = (lane >= gi * POOL_GROUP_DIM) & (lane < (gi + 1) * POOL_GROUP_DIM)
            pooled = jnp.where(in_group, run / count - u, pooled)
    mixed = _dot(pooled.astype(BF16), w_ref[...])
    o_ref[...] = (mixed * scale_ref[...]).astype(BF16)


def _pool(small, w_bd, scale, bsz, t_len, tt=512):
    nt = t_len // tt
    per = tt // POOL_HALO
    return pl.pallas_call(
        functools.partial(_pool_kernel, tt=tt),
        grid=(bsz, nt),
        in_specs=[pl.BlockSpec((tt, POOL_WIDTH), lambda b, i: (b * nt + i, 0)),
                  pl.BlockSpec((POOL_HALO, POOL_WIDTH), lambda b, i: (jnp.maximum((b * nt + i) * per - 1, 0), 0)),
                  pl.BlockSpec((POOL_WIDTH, POOL_WIDTH), lambda b, i: (0, 0)),
                  pl.BlockSpec((1, POOL_WIDTH), lambda b, i: (0, 0))],
        out_specs=pl.BlockSpec((tt, POOL_WIDTH), lambda b, i: (b * nt + i, 0)),
        out_shape=jax.ShapeDtypeStruct((bsz * t_len, POOL_WIDTH), BF16),
        scratch_shapes=[pltpu.VMEM((tt + POOL_HALO, POOL_WIDTH), F32)],
        compiler_params=_params(("parallel", "arbitrary")),
        name="pool",
    )(small, small, w_bd, scale)


def _compress_kernel(xk_ref, xv_ref, pek_ref, pev_ref, w1k_ref, w2k_ref, w1v_ref, w2v_ref, kc_ref, vc_ref):
    def one(x_ref, pe_ref, w1_ref, w2_ref, out_ref):
        n_chunk = x_ref.shape[0] // CMP_STRIDE
        first = jnp.zeros((n_chunk, w1_ref.shape[2]), F32)
        second = jnp.zeros((n_chunk, w1_ref.shape[2]), F32)
        for r in range(CMP_STRIDE):
            x_r = x_ref[pl.ds(r, n_chunk, stride=CMP_STRIDE), :]
            first = first + _dot((x_r + pe_ref[r:r + 1, :]).astype(BF16), w1_ref[r])
            second = second + _dot((x_r + pe_ref[CMP_STRIDE + r:CMP_STRIDE + r + 1, :]).astype(BF16),
                                   w1_ref[CMP_STRIDE + r])
        hidden = first + pltpu.roll(second, n_chunk - 1, axis=0)
        out_ref[0] = _dot(jax.nn.gelu(hidden).astype(BF16), w2_ref[...])

    one(xk_ref, pek_ref, w1k_ref, w2k_ref, kc_ref)
    one(xv_ref, pev_ref, w1v_ref, w2v_ref, vc_ref)


def _compress(small, pe_k, pe_v, w1_k, w2_k, w1_v, w2_v, bsz, t_len):
    n_chunk = t_len // CMP_STRIDE
    full = lambda a: pl.BlockSpec(a.shape, lambda b: tuple(0 for _ in a.shape))
    x_spec = lambda col: pl.BlockSpec((t_len, KV_WIDTH), lambda b: (b, col // KV_WIDTH))
    o_spec = pl.BlockSpec((1, n_chunk, KV_WIDTH), lambda b: (b, 0, 0))
    return pl.pallas_call(
        _compress_kernel,
        grid=(bsz,),
        in_specs=[x_spec(COL_CMP), x_spec(COL_CMP + KV_WIDTH), full(pe_k), full(pe_v),
                  full(w1_k), full(w2_k), full(w1_v), full(w2_v)],
        out_specs=[o_spec, o_spec],
        out_shape=[jax.ShapeDtypeStruct((bsz, n_chunk, KV_WIDTH), F32)] * 2,
        compiler_params=_params(("parallel",)),
        name="compress",
    )(small, small, pe_k, pe_v, w1_k, w2_k, w1_v, w2_v)


def _compress_weights(pe, w1, w2):
    eye = jnp.eye(NSA_KV_HEADS, dtype=w1.dtype)
    w1_r = w1.reshape(CMP_LEN, HEAD_DIM, CMP_HIDDEN)
    w1_bd = jnp.einsum("gh,rdc->rgdhc", eye, w1_r).reshape(CMP_LEN, KV_WIDTH, NSA_KV_HEADS * CMP_HIDDEN)
    w2_bd = jnp.einsum("gh,cd->gchd", eye, w2).reshape(NSA_KV_HEADS * CMP_HIDDEN, KV_WIDTH)
    return jnp.tile(pe, (1, NSA_KV_HEADS)), w1_bd.astype(BF16), w2_bd.astype(BF16)


def _nsa_kernel(q_ref, ksl_ref, vsl_ref, kwn_ref, vwn_ref, kpos_ref, kc_ref, vc_ref, cpos_ref, ovt_ref, slope_ref,
                sm_ref, gb_ref, o_ref, qs_scr, s_scr, p_scr, m_scr, acc_scr, *, tq, tks):
    i = pl.program_id(1)
    t0 = i * tq
    n_head = NSA_HEADS
    rows = n_head * tq
    n_cmp = kc_ref.shape[1]
    n_blk = ovt_ref.shape[0]
    win_keys = WINDOW + tq
    lane = lax.broadcasted_iota(jnp.int32, (tq, LANES), 1)
    first = lane < HEAD_DIM
    scale = jnp.asarray(HEAD_DIM ** -0.5, BF16)

    pairs = [q_ref[:, r * LANES:(r + 1) * LANES] * scale for r in range(NSA_REP)]
    zero = jnp.zeros_like(pairs[0])
    q_heads, q_slopes = [], []
    for g in range(NSA_KV_HEADS):
        for r in range(NSA_REP):
            q_heads.append(jnp.where(first, pairs[r], zero) if g == 0 else jnp.where(first, zero, pairs[r]))
            q_slopes.append(jnp.broadcast_to(slope_ref[g * NSA_REP + r:g * NSA_REP + r + 1, :], (tq, LANES)))
    q_main = jnp.concatenate(q_heads, axis=0)
    q_plain = jnp.concatenate([q_main, jnp.concatenate(q_slopes, axis=0).astype(BF16)], axis=1)

    n_part = s_scr.shape[1]
    part_rows = rows // n_part

    def per_head(mask, s, fill):
        return jnp.where(mask[None], s.reshape(s.shape[0] // tq, tq, s.shape[-1]), fill).reshape(s.shape)

    t_col = t0 + lax.broadcasted_iota(jnp.int32, (tq, n_cmp), 0)
    c_end = lax.broadcasted_iota(jnp.int32, (tq, n_cmp), 1) * CMP_STRIDE + (CMP_LEN - 1)
    valid_c = c_end <= t_col
    kc_x = jnp.concatenate([kc_ref[0].astype(BF16), cpos_ref[...]], axis=1)
    vc_b = vc_ref[0].astype(BF16)
    p_parts, o_parts = [], []
    for part in range(n_part):
        rsl = slice(part * part_rows, (part + 1) * part_rows)
        s = per_head(valid_c, _dot_nt(q_plain[rsl], kc_x), NEG_INF)
        p = per_head(valid_c, jnp.exp(s - jnp.max(s, axis=1, keepdims=True)), 0.0)
        l_c = jnp.sum(p, axis=1, keepdims=True)
        p = p * jnp.where(l_c > 0.0, 1.0 / l_c, 0.0)
        p_parts.append(p)
        o_parts.append(_dot(p.astype(BF16), vc_b))
    p_c = jnp.concatenate(p_parts, axis=0)
    o_cmp = jnp.concatenate(o_parts, axis=0)

    blk_t = lax.broadcasted_iota(jnp.int32, (n_blk, tq), 0)
    cur_t = (t0 + lax.broadcasted_iota(jnp.int32, (n_blk, tq), 1)) // SEL_BLOCK
    forced_t = (blk_t == 0) | (blk_t == cur_t) | (blk_t == cur_t - 1)
    sub = lax.broadcasted_iota(jnp.int32, (SUBLANES, tq), 0)
    n_top = float(min(SEL_TOPN, n_blk))
    for g in range(NSA_KV_HEADS):
        p_sum = p_c[g * NSA_REP * tq:(g * NSA_REP + 1) * tq]
        for r in range(1, NSA_REP):
            p_sum = p_sum + p_c[(g * NSA_REP + r) * tq:(g * NSA_REP + r + 1) * tq]
        p0, p1, p2 = _split3(p_sum)
        ovt = ovt_ref[...]
        imp_t = _dot_nt(ovt, p0) + _dot_nt(ovt, p1) + _dot_nt(ovt, p2)
        score = jnp.where(forced_t, FORCE_SCORE, jnp.where(blk_t <= cur_t, imp_t, -1.0))
        chunks = [score[c * SUBLANES:(c + 1) * SUBLANES] for c in range(n_blk // SUBLANES)]
        ranks = [jnp.zeros((SUBLANES, tq), F32) for _ in chunks]
        for jp in range(n_blk):
            other = jnp.broadcast_to(score[jp:jp + 1], (SUBLANES, tq))
            for c, chunk in enumerate(chunks):
                ge = jnp.where(other >= chunk, 1.0, 0.0)
                gt = jnp.where(other > chunk, 1.0, 0.0)
                if jp < c * SUBLANES:
                    ranks[c] = ranks[c] + ge
                elif jp >= (c + 1) * SUBLANES:
                    ranks[c] = ranks[c] + gt
                else:
                    ranks[c] = ranks[c] + jnp.where(sub > jp - c * SUBLANES, ge, gt)
        rank = jnp.concatenate(ranks, axis=0)
        neg_t = jnp.where(rank < n_top, 0.0, NEG_INF)
        neg = jnp.transpose(jnp.concatenate([neg_t, jnp.zeros((LANES - n_blk, tq), F32)], axis=0))
        for r in range(NSA_REP):
            k = g * NSA_REP + r
            q_const = (neg + q_slopes[k]).astype(BF16)
            qs_scr[k * tq:(k + 1) * tq, :] = jnp.concatenate([q_heads[k], q_const], axis=1)

    ones_s = jnp.ones((tks, LANES), BF16)

    def slc_score(j, part):
        ks = pl.multiple_of(j * tks, tks)
        kx = jnp.concatenate([ksl_ref[pl.ds(ks, tks), :], kpos_ref[pl.ds(ks, tks), :]], axis=1)
        return _dot_nt(qs_scr[part * part_rows:(part + 1) * part_rows, :], kx)

    def slc_values(j, part):
        ks = pl.multiple_of(j * tks, tks)
        return jnp.concatenate([vsl_ref[pl.ds(ks, tks), :], ones_s], axis=1)

    def slc_mask(s, j):
        t_row = t0 + lax.broadcasted_iota(jnp.int32, (tq, tks), 0)
        pos = j * tks + lax.broadcasted_iota(jnp.int32, (tq, tks), 1)
        return jnp.where((pos <= t_row)[None], s.reshape(part_rows // tq, tq, tks), NEG_INF).reshape(part_rows, tks)

    _flash_sweep(t0 // tks + 1, n_part, slc_score, slc_values, slc_mask, s_scr, p_scr, m_scr, acc_scr)
    acc = acc_scr[...].reshape(rows, 2 * LANES)
    o_slc = acc[:, :LANES] / acc[:, LANES:]

    start = pl.multiple_of(jnp.maximum(t0 - WINDOW, 0), tq)
    kx = jnp.concatenate([kwn_ref[pl.ds(start, win_keys), :], kpos_ref[pl.ds(start, win_keys), :]], axis=1)
    vx = jnp.concatenate([vwn_ref[pl.ds(start, win_keys), :], jnp.ones((win_keys, LANES), BF16)], axis=1)
    dist = (t0 + lax.broadcasted_iota(jnp.int32, (tq, win_keys), 0)
            - (start + lax.broadcasted_iota(jnp.int32, (tq, win_keys), 1)))
    in_window = (dist >= 0) & (dist < WINDOW)
    o_parts = []
    win_score = lambda part: _dot_nt(q_plain[part * part_rows:(part + 1) * part_rows], kx)
    s_next = win_score(0)
    for part in range(n_part):
        s = per_head(in_window, s_next, NEG_INF)
        if part + 1 < n_part:
            s_next = win_score(part + 1)
        prob = jnp.exp(s - jnp.max(s, axis=1, keepdims=True))
        acc = _dot(prob.astype(BF16), vx)
        o_parts.append(acc[:, :LANES] / acc[:, LANES:])
    o_win = jnp.concatenate(o_parts, axis=0)

    gates = jax.nn.sigmoid(sm_ref[:, COL_GL - COL_FA:COL_GL - COL_FA + NSA_HEADS * N_BRANCH] + gb_ref[...])
    for r in range(NSA_REP):
        halves = []
        for g in range(NSA_KV_HEADS):
            k = g * NSA_REP + r
            rsl = slice(k * tq, (k + 1) * tq)
            gate = lambda br: gates[:, k * N_BRANCH + br:k * N_BRANCH + br + 1]
            halves.append(gate(0) * o_cmp[rsl] + gate(1) * o_slc[rsl] + gate(2) * o_win[rsl])
        o_ref[:, r * LANES:(r + 1) * LANES] = jnp.where(first, halves[0], halves[1]).astype(BF16)


def _nsa(big, kpos, kc, vc, cpos, ovt, slope_tab, small, gate_b, bsz, t_len, tq=256, tks=512, n_part=3):
    nq = t_len // tq
    kv_col = COL_KV // KV_WIDTH
    kv_spec = lambda c: pl.BlockSpec((t_len, KV_WIDTH), lambda b, i: (b, kv_col + c))
    full = lambda a: pl.BlockSpec(a.shape, lambda b, i: tuple(0 for _ in a.shape))
    n_cmp = kc.shape[1]
    rows = NSA_HEADS * tq
    return pl.pallas_call(
        functools.partial(_nsa_kernel, tq=tq, tks=tks),
        grid=(bsz, nq),
        in_specs=[pl.BlockSpec((tq, NSA_WIDTH), lambda b, i: (b * nq + i, COL_QC // NSA_WIDTH)),
                  kv_spec(0), kv_spec(1), kv_spec(2), kv_spec(3), full(kpos),
                  pl.BlockSpec((1, n_cmp, KV_WIDTH), lambda b, i: (b, 0, 0)),
                  pl.BlockSpec((1, n_cmp, KV_WIDTH), lambda b, i: (b, 0, 0)),
                  full(cpos), full(ovt), full(slope_tab),
                  pl.BlockSpec((tq, LANES), lambda b, i: (b * nq + i, COL_FA // LANES)),
                  full(gate_b)],
        out_specs=pl.BlockSpec((tq, NSA_WIDTH), lambda b, i: (b * nq + i, 0)),
        out_shape=jax.ShapeDtypeStruct((bsz * t_len, NSA_WIDTH), BF16),
        scratch_shapes=[pltpu.VMEM((rows, 2 * LANES), BF16),
                        pltpu.VMEM((2, n_part, rows // n_part, tks), F32),
                        pltpu.VMEM((2, n_part, rows // n_part, tks), BF16),
                        pltpu.VMEM((n_part, rows // n_part, 1), F32),
                        pltpu.VMEM((n_part, rows // n_part, 2 * LANES), F32)],
        compiler_params=_params(("parallel", "arbitrary")),
        name="nsa",
    )(big, big, big, big, big, kpos, kc, vc, cpos, ovt, slope_tab, small, gate_b)


def _position_columns(pos):
    lane = jnp.arange(LANES)[None, :]
    hi = (SEL_BLOCK * (pos // SEL_BLOCK))[:, None]
    lo = (pos % SEL_BLOCK)[:, None]
    cols = jnp.where((lane >= COL_POS) & (lane < COL_POS + 3), hi, 0) + jnp.where((lane >= COL_POS + 3) & (lane < COL_POS + 6), lo, 0)
    return cols.astype(BF16)


def _nsa_tables(t_len):
    pos = jnp.arange(t_len)
    lane = jnp.arange(LANES)[None, :]
    block_one_hot = ((lane < t_len // SEL_BLOCK) & (lane == (pos // SEL_BLOCK)[:, None])).astype(BF16)
    kpos = _position_columns(pos) + block_one_hot
    n_cmp = t_len // CMP_STRIDE
    c_start = jnp.arange(n_cmp) * CMP_STRIDE
    cpos = _position_columns(c_start + CMP_LEN - 1)
    s_start = (jnp.arange(t_len // SEL_BLOCK) * SEL_BLOCK)[:, None]
    ovt = ((c_start[None, :] < s_start + SEL_BLOCK) & (c_start[None, :] + CMP_LEN - 1 >= s_start)).astype(BF16)
    slopes = jnp.exp2(-8.0 * jnp.arange(1, NSA_HEADS + 1, dtype=F32) / NSA_HEADS)
    pieces = jnp.stack([piece.astype(F32) for piece in _split3(slopes)], axis=1)
    slope_tab = jnp.zeros((NSA_HEADS, LANES), F32)
    slope_tab = slope_tab.at[:, COL_POS:COL_POS + 3].set(pieces).at[:, COL_POS + 3:COL_POS + 6].set(pieces)
    return kpos, cpos, ovt, slope_tab


def _mem_kv_kernel(mem_ref, g_ref, wk_ref, wv_ref, k_ref, v_ref):
    mb = _rms(mem_ref[0], g_ref[...]).astype(BF16)
    k_ref[0] = _dot(mb, wk_ref[...]).astype(BF16)
    v_ref[0] = _dot(mb, wv_ref[...]).astype(BF16)


def _mem_kv(mem, g, wk, wv):
    bsz, m_len, _ = mem.shape
    w_spec = pl.BlockSpec((D_MODEL, D_MODEL), lambda b: (0, 0))
    m_spec = pl.BlockSpec((1, m_len, D_MODEL), lambda b: (b, 0, 0))
    return pl.pallas_call(
        _mem_kv_kernel,
        grid=(bsz,),
        in_specs=[m_spec, pl.BlockSpec((1, D_MODEL), lambda b: (0, 0)), w_spec, w_spec],
        out_specs=[m_spec, m_spec],
        out_shape=[jax.ShapeDtypeStruct((bsz, m_len, D_MODEL), BF16)] * 2,
        compiler_params=_params(("parallel",)),
        name="mem_kv",
    )(mem, g, wk, wv)


def _route(logits):
    lane = lax.broadcasted_iota(jnp.int32, logits.shape, 1)
    big_i = jnp.int32(1 << 20)
    is_coarse = lane < N_GROUPS
    lc = jnp.where(is_coarse, logits, NEG_INF)
    mc = jnp.max(lc, axis=1, keepdims=True)
    grp = jnp.min(jnp.where(is_coarse & (lc == mc), lane, big_i), axis=1, keepdims=True)
    w_grp = 1.0 / jnp.sum(jnp.where(is_coarse, jnp.exp(lc - mc), 0.0), axis=1, keepdims=True)
    e_id = lane - N_GROUPS
    in_grp = (e_id >= grp * EXPERTS_PER_GROUP) & (e_id < (grp + 1) * EXPERTS_PER_GROUP)
    lf = jnp.where(in_grp, logits, NEG_INF)
    m1 = jnp.max(lf, axis=1, keepdims=True)
    i1 = jnp.min(jnp.where(in_grp & (lf == m1), lane, big_i), axis=1, keepdims=True)
    lf2 = jnp.where(lane == i1, NEG_INF, lf)
    m2 = jnp.max(lf2, axis=1, keepdims=True)
    i2 = jnp.min(jnp.where(in_grp & (lane != i1) & (lf2 == m2), lane, big_i), axis=1, keepdims=True)
    denom = jnp.sum(jnp.where(in_grp, jnp.exp(lf - m1), 0.0), axis=1, keepdims=True)
    p1 = 1.0 / denom
    p2 = jnp.exp(m2 - m1) / denom
    tot = p1 + p2
    w1 = w_grp * (p1 / tot)
    w2 = w_grp * (p2 / tot)
    weights = jnp.where(lane + N_GROUPS == i1, w1, jnp.where(lane + N_GROUPS == i2, w2, 0.0))
    return jnp.where(lane == grp + N_EXPERTS, 1.0, weights)


def _mix_out_kernel(x_ref, oa_ref, ob_ref, oc_ref, woa_ref, wob_ref, woc_ref, gx_ref, wq_ref, mk_ref, mv_ref,
                    wo_ref, gf_ref, wr_ref, br_ref, x2_ref, hb_ref, comb_ref, cnt_ref):
    n_part = 2
    rows = x_ref.shape[0] // n_part
    parts = [slice(k * rows, (k + 1) * rows) for k in range(n_part)]
    x1 = [x_ref[r, :] + _dot(oa_ref[r, :], woa_ref[...]) + _dot(ob_ref[r, :], wob_ref[...])
          + _dot(oc_ref[r, :], woc_ref[...]) for r in parts]
    hq = [_rms(v, gx_ref[...]).astype(BF16) for v in x1]
    q = [(_dot(v, wq_ref[...]) * (MEM_HEAD_DIM ** -0.5)).astype(BF16) for v in hq]
    heads = [[] for _ in parts]
    for h in range(MEM_HEADS):
        sl = slice(h * MEM_HEAD_DIM, (h + 1) * MEM_HEAD_DIM)
        for k in range(n_part):
            s = _dot_nt(q[k][:, sl], mk_ref[0, :, sl])
            p = jnp.exp(s - jnp.max(s, axis=1, keepdims=True))
            p = p / jnp.sum(p, axis=1, keepdims=True)
            heads[k].append(_dot(p.astype(BF16), mv_ref[0, :, sl]).astype(BF16))
    x2 = [x1[k] + _dot(jnp.concatenate(heads[k], axis=1), wo_ref[...]) for k in range(n_part)]
    hn = [_rms(v, gf_ref[...]) for v in x2]
    for k, r in enumerate(parts):
        x2_ref[r, :] = x2[k]
        hb_ref[r, :] = hn[k].astype(BF16)
        h0, h1, _ = _split3(hn[k])
        partial = _dot(h0, wr_ref[...]) + _dot(h1, wr_ref[...])
        logits = partial + pltpu.roll(partial, ROUTER_WIDTH - ROUTER_PIECE, axis=1) + br_ref[...]
        routed = _route(logits)
        comb_ref[r, :] = routed
        per_part = rows // MOE_TILE
        for sub in range(per_part):
            cnt_ref[k * per_part + sub] = jnp.sum(routed[sub * MOE_TILE:(sub + 1) * MOE_TILE], axis=0, keepdims=True)


def _mix_out(x, o_a, o_b, o_c, wo_a, wo_b, wo_c, g_x, wq, mem_k, mem_v, wo, g_f, w_r, b_r, t_len, tm=1024):
    n = x.shape[0]
    per_batch = t_len // tm
    m_len = mem_k.shape[1]
    tile = lambda width: pl.BlockSpec((tm, width), lambda i: (i, 0))
    full = lambda shape: pl.BlockSpec(shape, lambda i: tuple(0 for _ in shape))
    mem_spec = pl.BlockSpec((1, m_len, D_MODEL), lambda i: (i // per_batch, 0, 0))
    return pl.pallas_call(
        _mix_out_kernel,
        grid=(n // tm,),
        in_specs=[tile(D_MODEL), tile(FOX_WIDTH), tile(POOL_WIDTH), tile(NSA_WIDTH),
                  full(wo_a.shape), full(wo_b.shape), full(wo_c.shape), full(g_x.shape), full(wq.shape),
                  mem_spec, mem_spec, full(wo.shape), full(g_f.shape), full(w_r.shape), full(b_r.shape)],
        out_specs=[tile(D_MODEL), tile(D_MODEL), tile(ROUTER_WIDTH),
                   pl.BlockSpec((tm // MOE_TILE, 1, ROUTER_WIDTH), lambda i: (i, 0, 0))],
        out_shape=[jax.ShapeDtypeStruct((n, D_MODEL), F32),
                   jax.ShapeDtypeStruct((n, D_MODEL), BF16),
                   jax.ShapeDtypeStruct((n, ROUTER_WIDTH), F32),
                   jax.ShapeDtypeStruct((n // MOE_TILE, 1, ROUTER_WIDTH), F32)],
        compiler_params=_params(("parallel",)),
        name="mix_out",
    )(x, o_a, o_b, o_c, wo_a, wo_b, wo_c, g_x, wq, mem_k, mem_v, wo, g_f, w_r, b_r)


def _group_ranks(comb, tri):
    lane = lax.broadcasted_iota(jnp.int32, comb.shape, 1)
    member = jnp.where((lane >= N_EXPERTS) & (lane < N_EXPERTS + N_GROUPS), comb, 0.0)
    return member, _dot(tri, member.astype(BF16))


def _moe_sort_kernel(dst_ref, hb_ref, comb_ref, tri_ref, xs_in, cs_in, xs_ref, cs_ref, stage_x, stage_c, sem):
    del xs_in, cs_in
    t = pl.program_id(0)
    comb = comb_ref[...]
    member, rank = _group_ranks(comb, tri_ref[...])
    rank_t = jnp.transpose(rank)
    member_t = jnp.transpose(member)
    slot = lax.broadcasted_iota(jnp.int32, (MOE_TILE, MOE_TILE), 0).astype(F32)
    pieces = _split3(comb)
    buf = t % 2

    def copies(into, g, dst):
        return (pltpu.make_async_copy(stage_x.at[into, g], xs_ref.at[pl.ds(dst, MOE_TILE)], sem.at[0, into, g]),
                pltpu.make_async_copy(stage_c.at[into, g], cs_ref.at[pl.ds(dst, MOE_TILE)], sem.at[1, into, g]))

    for g in range(N_GROUPS):
        row = N_EXPERTS + g
        take = jnp.where((rank_t[row:row + 1, :] == slot) & (member_t[row:row + 1, :] > 0.5), 1.0, 0.0).astype(BF16)
        stage_x[buf, g] = _dot(take, hb_ref[...]).astype(BF16)
        stage_c[buf, g] = _dot(take, pieces[0]) + _dot(take, pieces[1]) + _dot(take, pieces[2])

    @pl.when(t > 0)
    def _():
        for g in range(N_GROUPS):
            for copy in copies(1 - buf, g, 0):
                copy.wait()

    for g in range(N_GROUPS):
        dst = pl.multiple_of(dst_ref[t * N_GROUPS + g], MOE_SEG_ALIGN)
        for copy in copies(buf, g, dst):
            copy.start()

    @pl.when(t == pl.num_programs(0) - 1)
    def _():
        for g in range(N_GROUPS):
            for copy in copies(buf, g, 0):
                copy.wait()


def _moe_sort(hb, comb, tri, seg_dst, rows):
    n = hb.shape[0]
    any_spec = pl.BlockSpec(memory_space=pl.ANY)
    xs0 = jnp.zeros((rows, D_MODEL), BF16)
    cs0 = jnp.zeros((rows, ROUTER_WIDTH), F32)
    return pl.pallas_call(
        _moe_sort_kernel,
        grid_spec=pltpu.PrefetchScalarGridSpec(
            num_scalar_prefetch=1,
            grid=(n // MOE_TILE,),
            in_specs=[pl.BlockSpec((MOE_TILE, D_MODEL), lambda t, dst: (t, 0)),
                      pl.BlockSpec((MOE_TILE, ROUTER_WIDTH), lambda t, dst: (t, 0)),
                      pl.BlockSpec((MOE_TILE, MOE_TILE), lambda t, dst: (0, 0)),
                      any_spec, any_spec],
            out_specs=[any_spec, any_spec],
            scratch_shapes=[pltpu.VMEM((2, N_GROUPS, MOE_TILE, D_MODEL), BF16),
                            pltpu.VMEM((2, N_GROUPS, MOE_TILE, ROUTER_WIDTH), F32),
                            pltpu.SemaphoreType.DMA((2, 2, N_GROUPS))]),
        out_shape=[jax.ShapeDtypeStruct((rows, D_MODEL), BF16), jax.ShapeDtypeStruct((rows, ROUTER_WIDTH), F32)],
        input_output_aliases={4: 0, 5: 1},
        compiler_params=_params(("arbitrary",)),
        name="moe_sort",
    )(seg_dst, hb, comb, tri, xs0, cs0)


def _moe_ffn_kernel(group_ref, valid_ref, xs_ref, cs_ref, wg_ref, wu_ref, wd_ref, ys_ref):
    i = pl.program_id(0)

    @pl.when(valid_ref[i] == 0)
    def _():
        ys_ref[...] = jnp.zeros(ys_ref.shape, F32)

    @pl.when(valid_ref[i] != 0)
    def _():
        x = xs_ref[...]
        cs = cs_ref[...]
        lane = lax.broadcasted_iota(jnp.int32, cs.shape, 1)
        first_expert = group_ref[i] * EXPERTS_PER_GROUP
        total = None
        for e in range(EXPERTS_PER_GROUP):
            he = (jax.nn.silu(_dot(x, wg_ref[0, e])) * _dot(x, wu_ref[0, e])).astype(BF16)
            c_e = jnp.sum(jnp.where(lane == first_expert + e, cs, 0.0), axis=1, keepdims=True)
            y = c_e * _dot(he, wd_ref[0, e])
            total = y if total is None else total + y
        ys_ref[...] = total


def _moe_ffn(xs, cs, wg, wu, wd, layer, tile_group, tile_valid):
    rows = xs.shape[0]
    tile = lambda width: pl.BlockSpec((MOE_FFN_TILE, width), lambda i, grp, ok: (i, 0))
    experts = lambda shape: pl.BlockSpec((1, EXPERTS_PER_GROUP) + shape, lambda i, grp, ok: (layer, grp[i], 0, 0))
    return pl.pallas_call(
        _moe_ffn_kernel,
        grid_spec=pltpu.PrefetchScalarGridSpec(
            num_scalar_prefetch=2,
            grid=(rows // MOE_FFN_TILE,),
            in_specs=[tile(D_MODEL), tile(ROUTER_WIDTH),
                      experts((D_MODEL, EXPERT_FF)), experts((D_MODEL, EXPERT_FF)), experts((EXPERT_FF, D_MODEL))],
            out_specs=tile(D_MODEL)),
        out_shape=jax.ShapeDtypeStruct((rows, D_MODEL), F32),
        compiler_params=_params(("arbitrary",)),
        name="moe_ffn",
    )(tile_group, tile_valid, xs, cs, wg, wu, wd)


def _moe_unsort_kernel(dst_ref, len_ref, ys_ref, comb_ref, tri_ref, x_ref, gfin_ref, o_ref, buf, sem, *, final_norm):
    t = pl.program_id(0)
    slot = t % 2
    half = MOE_TILE // 2

    def on_segment(tile, g, into, action):
        dst = pl.multiple_of(dst_ref[tile * N_GROUPS + g], MOE_SEG_ALIGN)
        short = len_ref[tile * N_GROUPS + g] <= half
        for rows, cond in ((half, short), (MOE_TILE, jnp.logical_not(short))):
            copy = pltpu.make_async_copy(ys_ref.at[pl.ds(dst, rows)], buf.at[into, g, pl.ds(0, rows)],
                                         sem.at[into, g])
            pl.when(cond)(functools.partial(action, copy))

    start = lambda copy: copy.start()
    wait = lambda copy: copy.wait()

    @pl.when(t == 0)
    def _():
        buf[...] = jnp.zeros(buf.shape, F32)
        for g in range(N_GROUPS):
            on_segment(0, g, 0, start)

    @pl.when(t + 1 < pl.num_programs(0))
    def _():
        for g in range(N_GROUPS):
            on_segment(t + 1, g, 1 - slot, start)

    member, rank = _group_ranks(comb_ref[...], tri_ref[...])
    lane_slot = lax.broadcasted_iota(jnp.int32, (MOE_TILE, MOE_TILE), 1).astype(F32)
    puts = []
    for g in range(N_GROUPS):
        col = N_EXPERTS + g
        puts.append(jnp.where((rank[:, col:col + 1] == lane_slot) & (member[:, col:col + 1] > 0.5), 1.0, 0.0)
                    .astype(BF16))
    y = x_ref[...]
    for g in range(N_GROUPS):
        on_segment(t, g, slot, wait)
        seg = buf[slot, g]
        hi = seg.astype(BF16)
        lo = (seg - hi.astype(F32)).astype(BF16)
        y = y + (_dot(puts[g], hi) + _dot(puts[g], lo))
    o_ref[...] = _rms(y, gfin_ref[...]) if final_norm else y


def _moe_unsort(ys, comb, tri, x2, g_fin, seg_dst, seg_len, final_norm):
    n = x2.shape[0]
    return pl.pallas_call(
        functools.partial(_moe_unsort_kernel, final_norm=final_norm),
        grid_spec=pltpu.PrefetchScalarGridSpec(
            num_scalar_prefetch=2,
            grid=(n // MOE_TILE,),
            in_specs=[pl.BlockSpec(memory_space=pl.ANY),
                      pl.BlockSpec((MOE_TILE, ROUTER_WIDTH), lambda t, dst, length: (t, 0)),
                      pl.BlockSpec((MOE_TILE, MOE_TILE), lambda t, dst, length: (0, 0)),
                      pl.BlockSpec((MOE_TILE, D_MODEL), lambda t, dst, length: (t, 0)),
                      pl.BlockSpec((1, D_MODEL), lambda t, dst, length: (0, 0))],
            out_specs=pl.BlockSpec((MOE_TILE, D_MODEL), lambda t, dst, length: (t, 0)),
            scratch_shapes=[pltpu.VMEM((2, N_GROUPS, MOE_TILE, D_MODEL), F32),
                            pltpu.SemaphoreType.DMA((2, N_GROUPS))]),
        out_shape=jax.ShapeDtypeStruct((n, D_MODEL), F32),
        compiler_params=_params(("arbitrary",)),
        name="moe_unsort",
    )(seg_dst, seg_len, ys, comb, tri, x2, g_fin)


def _moe_plan(cnt, n_tok):
    n_tiles = cnt.shape[0]
    rows = n_tok + n_tiles * N_GROUPS * MOE_SEG_ALIGN + N_GROUPS * (MOE_FFN_TILE + MOE_GAP)
    rows = -(-rows // MOE_FFN_TILE) * MOE_FFN_TILE
    padded = (cnt + MOE_SEG_ALIGN - 1) // MOE_SEG_ALIGN * MOE_SEG_ALIGN
    within = jnp.cumsum(padded, axis=0) - padded
    region = (jnp.sum(padded, axis=0) + MOE_FFN_TILE - 1) // MOE_FFN_TILE * MOE_FFN_TILE
    start = jnp.cumsum(region + MOE_GAP) - (region + MOE_GAP)
    seg_dst = (start[None, :] + within).reshape(-1).astype(jnp.int32)
    tile_row = jnp.arange(rows // MOE_FFN_TILE) * MOE_FFN_TILE
    begun = tile_row[:, None] >= start[None, :]
    tile_valid = jnp.any(begun & (tile_row[:, None] < (start + region)[None, :]), axis=1).astype(jnp.int32)
    tile_group = jnp.clip(jnp.sum(begun, axis=1) - 1, 0, N_GROUPS - 1).astype(jnp.int32)
    return seg_dst, padded.reshape(-1).astype(jnp.int32), tile_group, tile_valid, rows


def _moe(x2, hb, comb, cnt, wg, wu, wd, layer, g_fin, final_norm):
    n = x2.shape[0]
    counts = jnp.round(cnt[:, 0, N_EXPERTS:N_EXPERTS + N_GROUPS]).astype(jnp.int32)
    seg_dst, seg_len, tile_group, tile_valid, rows = _moe_plan(counts, n)
    tri = (jnp.arange(MOE_TILE)[:, None] > jnp.arange(MOE_TILE)[None, :]).astype(BF16)
    xs, cs = _moe_sort(hb, comb, tri, seg_dst, rows)
    ys = _moe_ffn(xs, cs, wg, wu, wd, layer, tile_group, tile_valid)
    return _moe_unsort(ys, comb, tri, x2, g_fin, seg_dst, seg_len, final_norm)


def _pair_order(a, axis):
    shape = a.shape
    a = a.reshape(shape[:axis] + (NSA_HEADS, HEAD_DIM) + shape[axis + 1:])
    a = jnp.take(a, jnp.array(NSA_PAIR_ORDER), axis=axis)
    return a.reshape(shape)


def _split_in_proj_weight(w):
    edges = [0]
    for width in (FOX_WIDTH, FOX_WIDTH, FOX_WIDTH, FOX_HEADS, POOL_WIDTH, NSA_WIDTH) + (KV_WIDTH,) * 6 + (NSA_HEADS * N_BRANCH,):
        edges.append(edges[-1] + width)
    part = lambda k: w[:, edges[k]:edges[k + 1]]
    qa, ka, va, fa, ub, qc = (part(k) for k in range(6))
    kvs = [part(k) for k in range(6, 12)]
    gl = part(12)
    w_big = jnp.concatenate([qa, ka, va, _pair_order(qc, 1)] + kvs[2:], axis=1).astype(BF16)
    pad = jnp.zeros((w.shape[0], LANES - FOX_HEADS - NSA_HEADS * N_BRANCH), w.dtype)
    w_small = jnp.concatenate([ub, fa, gl, pad] + kvs[:2], axis=1).astype(BF16)
    return w_big, w_small


def _block_diag(w):
    groups, cin, cout = w.shape
    out = jnp.zeros((groups * cin, groups * cout), w.dtype)
    for gi in range(groups):
        out = out.at[gi * cin:(gi + 1) * cin, gi * cout:(gi + 1) * cout].set(w[gi])
    return out


def _layer(x, mem, p, experts, layer, bsz, t_len, final_g, final_norm):
    row = lambda v: v.reshape(1, -1)
    w_big, w_small = _split_in_proj_weight(p["w_in"])
    big, small = _in_proj(x, row(p["mix_norm_g"]), w_big, w_small)

    b_row = jnp.pad(p["fox_forget_b"], (0, LANES - FOX_HEADS)).reshape(1, LANES)
    ck = _forget_cumsum(small, b_row, _forget_placement(), bsz, t_len)
    o_a = _fox(big, ck, bsz, t_len)

    o_b = _pool(small, _block_diag(p["pool_w"]).astype(BF16), row(p["pool_scale"]), bsz, t_len)

    pe_k, w1_k, w2_k = _compress_weights(p["cmp_pe_k"], p["cmp_w1_k"], p["cmp_w2_k"])
    pe_v, w1_v, w2_v = _compress_weights(p["cmp_pe_v"], p["cmp_w1_v"], p["cmp_w2_v"])
    kc, vc = _compress(small, pe_k, pe_v, w1_k, w2_k, w1_v, w2_v, bsz, t_len)
    kpos, cpos, ovt, slope_tab = _nsa_tables(t_len)
    o_c = _nsa(big, kpos, kc, vc, cpos, ovt, slope_tab, small, row(p["nsa_gate_b"]), bsz, t_len)

    mem_k, mem_v = _mem_kv(mem, row(p["mem_norm_g"]), p["xattn_wk"].astype(BF16), p["xattn_wv"].astype(BF16))
    w_out = p["w_out"]
    wo_a = w_out[:FOX_WIDTH].astype(BF16)
    wo_b = w_out[FOX_WIDTH:FOX_WIDTH + POOL_WIDTH].astype(BF16)
    wo_c = _pair_order(w_out[FOX_WIDTH + POOL_WIDTH:], 0).astype(BF16)
    w_r = jnp.concatenate([p["router_coarse_w"], p["router_fine_w"],
                           jnp.zeros((D_MODEL, ROUTER_PIECE - N_GROUPS - N_EXPERTS), F32)], axis=1)
    w_r0, w_r1, _ = _split3(w_r)
    w_r = jnp.concatenate([w_r0, w_r1, jnp.zeros((D_MODEL, ROUTER_WIDTH - 2 * ROUTER_PIECE), BF16)], axis=1)
    b_r = jnp.concatenate([p["router_coarse_b"], p["router_fine_b"],
                           jnp.zeros((ROUTER_WIDTH - N_GROUPS - N_EXPERTS,), F32)]).reshape(1, -1)
    x2, hb, comb, cnt = _mix_out(x, o_a, o_b, o_c, wo_a, wo_b, wo_c, row(p["xattn_norm_g"]),
                                 p["xattn_wq"].astype(BF16), mem_k, mem_v, p["xattn_wo"].astype(BF16),
                                 row(p["ffn_norm_g"]), w_r, b_r, t_len)

    return _moe(x2, hb, comb, cnt, *experts, layer, row(final_g), final_norm)


def kernel(x, mem, mix_norm_g, w_in, fox_forget_b, pool_w, pool_scale, cmp_pe_k, cmp_pe_v, cmp_w1_k, cmp_w2_k, cmp_w1_v, cmp_w2_v, nsa_gate_b, w_out, xattn_norm_g, mem_norm_g, xattn_wq, xattn_wk, xattn_wv, xattn_wo, ffn_norm_g, router_coarse_w, router_coarse_b, router_fine_w, router_fine_b, exp_w_gate, exp_w_up, exp_w_down, final_norm_g):
    stacked = dict(mix_norm_g=mix_norm_g, w_in=w_in, fox_forget_b=fox_forget_b, pool_w=pool_w, pool_scale=pool_scale,
                   cmp_pe_k=cmp_pe_k, cmp_pe_v=cmp_pe_v, cmp_w1_k=cmp_w1_k, cmp_w2_k=cmp_w2_k, cmp_w1_v=cmp_w1_v,
                   cmp_w2_v=cmp_w2_v, nsa_gate_b=nsa_gate_b, w_out=w_out, xattn_norm_g=xattn_norm_g,
                   mem_norm_g=mem_norm_g, xattn_wq=xattn_wq, xattn_wk=xattn_wk, xattn_wv=xattn_wv, xattn_wo=xattn_wo,
                   ffn_norm_g=ffn_norm_g, router_coarse_w=router_coarse_w, router_coarse_b=router_coarse_b,
                   router_fine_w=router_fine_w, router_fine_b=router_fine_b)
    experts = (exp_w_gate.astype(BF16), exp_w_up.astype(BF16), exp_w_down.astype(BF16))
    bsz, t_len, d = x.shape
    depth = w_in.shape[0]
    h = x.reshape(bsz * t_len, d)
    for layer in range(depth):
        p = {name: value[layer] for name, value in stacked.items()}
        h = _layer(h, mem, p, experts, layer, bsz, t_len, final_norm_g, final_norm=(layer == depth - 1))
    return h.reshape(bsz, t_len, d)
```

```python
import functools

import jax
import jax.numpy as jnp
from jax import lax
from jax.experimental import pallas as pl
from jax.experimental.pallas import tpu as pltpu

F32 = jnp.float32
BF16 = jnp.bfloat16

D_MODEL = 1024
HEAD_DIM = 64
FOX_HEADS = 6
FOX_WIDTH = FOX_HEADS * HEAD_DIM
POOL_WIDTH = 256
POOL_GROUP_DIM = 64
POOL_WINDOWS = (2, 4, 8, 16)
POOL_HALO = 16
NSA_HEADS = 6
NSA_KV_HEADS = 2
NSA_REP = NSA_HEADS // NSA_KV_HEADS
NSA_WIDTH = NSA_HEADS * HEAD_DIM
KV_WIDTH = NSA_KV_HEADS * HEAD_DIM
N_BRANCH = 3
CMP_LEN = 32
CMP_STRIDE = 16
CMP_HIDDEN = 256
SEL_BLOCK = 64
SEL_TOPN = 16
WINDOW = 512
FORCE_SCORE = 1.0e4
MEM_HEADS = 4
MEM_HEAD_DIM = D_MODEL // MEM_HEADS
N_GROUPS = 4
EXPERTS_PER_GROUP = 4
N_EXPERTS = N_GROUPS * EXPERTS_PER_GROUP
EXPERT_FF = 512
EPS = 1e-6
NEG_INF = -1e30

LANES = 128
SUBLANES = 8
BIG_WIDTH = 3 * FOX_WIDTH + NSA_WIDTH + 4 * KV_WIDTH
COL_QC = 3 * FOX_WIDTH
COL_KV = COL_QC + NSA_WIDTH
SMALL_WIDTH = POOL_WIDTH + LANES + 2 * KV_WIDTH
COL_FA = POOL_WIDTH
COL_GL = POOL_WIDTH + FOX_HEADS
COL_CMP = POOL_WIDTH + LANES
ROUTER_WIDTH = LANES
ROUTER_PIECE = 32
MOE_TILE = 256
MOE_SEG_ALIGN = 16
MOE_FFN_TILE = 512
MOE_GAP = 512
NSA_PAIR_ORDER = tuple(g * NSA_REP + r for r in range(NSA_REP) for g in range(NSA_KV_HEADS))
N_PIECES = 3
COL_POS = HEAD_DIM

VMEM_LIMIT = 56 * 1024 * 1024

_NT = (((1,), (1,)), ((), ()))


def _params(semantics):
    return pltpu.CompilerParams(dimension_semantics=semantics, vmem_limit_bytes=VMEM_LIMIT)


def _rms(x, g):
    return x * lax.rsqrt(jnp.mean(x * x, axis=-1, keepdims=True) + EPS) * g


def _dot(a, b):
    return jnp.dot(a, b, preferred_element_type=F32)


def _dot_nt(a, b):
    return lax.dot_general(a, b, _NT, preferred_element_type=F32)


def _split3(a):
    a0 = a.astype(BF16)
    r = a - a0.astype(F32)
    a1 = r.astype(BF16)
    a2 = (r - a1.astype(F32)).astype(BF16)
    return a0, a1, a2


def _in_proj_kernel(x_ref, g_ref, wb_ref, ws_ref, big_ref, small_ref):
    yb = _rms(x_ref[...], g_ref[...]).astype(BF16)
    big_ref[...] = _dot(yb, wb_ref[...]).astype(BF16)
    small_ref[...] = _dot(yb, ws_ref[...])


def _in_proj(x, g, w_big, w_small, tm=512):
    n = x.shape[0]
    return pl.pallas_call(
        _in_proj_kernel,
        grid=(n // tm,),
        in_specs=[pl.BlockSpec((tm, D_MODEL), lambda i: (i, 0)),
                  pl.BlockSpec((1, D_MODEL), lambda i: (0, 0)),
                  pl.BlockSpec((D_MODEL, BIG_WIDTH), lambda i: (0, 0)),
                  pl.BlockSpec((D_MODEL, SMALL_WIDTH), lambda i: (0, 0))],
        out_specs=[pl.BlockSpec((tm, BIG_WIDTH), lambda i: (i, 0)),
                   pl.BlockSpec((tm, SMALL_WIDTH), lambda i: (i, 0))],
        out_shape=[jax.ShapeDtypeStruct((n, BIG_WIDTH), BF16),
                   jax.ShapeDtypeStruct((n, SMALL_WIDTH), F32)],
        compiler_params=_params(("parallel",)),
        name="in_proj",
    )(x, g, w_big, w_small)


def _forget_kernel(f_ref, b_ref, place_ref, ck_ref):
    c = jax.nn.log_sigmoid(f_ref[...] + b_ref[...])
    t_len = c.shape[0]
    row = lax.broadcasted_iota(jnp.int32, c.shape, 0)
    sh = 1
    while sh < t_len:
        c = c + jnp.where(row >= sh, pltpu.roll(c, sh, axis=0), 0.0)
        sh *= 2
    n0, n1, n2 = _split3(-c)
    ck_ref[...] = (_dot(n0, place_ref[0]) + _dot(n1, place_ref[1]) + _dot(n2, place_ref[2])).astype(BF16)


def _forget_cumsum(small, b_row, place, bsz, t_len):
    return pl.pallas_call(
        _forget_kernel,
        grid=(bsz,),
        in_specs=[pl.BlockSpec((t_len, LANES), lambda b: (b, COL_FA // LANES)),
                  pl.BlockSpec((1, LANES), lambda b: (0, 0)),
                  pl.BlockSpec(place.shape, lambda b: (0, 0, 0))],
        out_specs=pl.BlockSpec((t_len, FOX_WIDTH), lambda b: (b, 0)),
        out_shape=jax.ShapeDtypeStruct((bsz * t_len, FOX_WIDTH), BF16),
        compiler_params=_params(("parallel",)),
        name="forget",
    )(small, b_row, place)


def _forget_placement():
    src = jnp.arange(LANES)[None, :, None]
    dst = jnp.arange(FOX_WIDTH)[None, None, :]
    piece = jnp.arange(N_PIECES)[:, None, None]
    target = LANES * (src // 2) + N_PIECES * (src % 2) + piece
    return ((src < FOX_HEADS) & (dst == target)).astype(BF16)


def _flash_sweep(n_tiles, n_stream, score, values, last_mask, s_scr, p_scr, m_scr, acc_scr):
    def run_stage(j, a, first=False, last=False):
        b = 1 - a
        for st in range(n_stream):
            s = score(0, st) if first else s_scr[a, st]
            if last:
                s = last_mask(s, j)
            m_old = jnp.full(m_scr.shape[1:], NEG_INF, F32) if first else m_scr[st]
            m_new = jnp.maximum(m_old, jnp.max(s, axis=1, keepdims=True))
            prob = jnp.exp(s - m_new).astype(BF16)
            if not last:
                p_scr[a, st] = prob
                s_scr[b, st] = score(j + 1, st)
            acc = jnp.zeros(acc_scr.shape[1:], F32) if first else acc_scr[st]
            if not first:
                acc = (acc + _dot(p_scr[b, st], values(j - 1, st))) * jnp.exp(m_old - m_new)
            if last:
                acc = acc + _dot(prob, values(j, st))
            acc_scr[st] = acc
            m_scr[st] = m_new

    n_main = n_tiles - 1

    @pl.when(n_main == 0)
    def _():
        run_stage(0, 0, first=True, last=True)

    @pl.when(n_main > 0)
    def _():
        run_stage(0, 0, first=True)

    def two_stages(jj, carry):
        run_stage(2 * jj + 1, 1)
        run_stage(2 * jj + 2, 0)
        return carry

    lax.fori_loop(0, jnp.maximum(n_main - 1, 0) // 2, two_stages, 0)

    @pl.when(jnp.logical_and(n_main >= 2, n_main % 2 == 0))
    def _():
        run_stage(n_main - 1, 1)

    for a in range(2):
        @pl.when(jnp.logical_and(n_main >= 1, n_main % 2 == a))
        def _():
            run_stage(n_main, a, last=True)


def _fox_kernel(q_ref, k_ref, v_ref, ck_ref, o_ref, q_scr, s_scr, p_scr, m_scr, acc_scr, *, tq, tk):
    i = pl.program_id(1)
    t0 = i * tq
    n_pair = FOX_HEADS // 2
    lane = lax.broadcasted_iota(jnp.int32, (tq, LANES), 1)
    first = lane < HEAD_DIM
    scale = jnp.asarray(HEAD_DIM ** -0.5, BF16)
    pick_a = jnp.where(lane < N_PIECES, 1.0, 0.0).astype(BF16)
    pick_b = jnp.where((lane >= N_PIECES) & (lane < 2 * N_PIECES), 1.0, 0.0).astype(BF16)
    for p in range(n_pair):
        qp = q_ref[:, p * LANES:(p + 1) * LANES] * scale
        zero = jnp.zeros_like(qp)
        qa = jnp.concatenate([jnp.where(first, qp, zero), pick_a], axis=1)
        qb = jnp.concatenate([jnp.where(first, zero, qp), pick_b], axis=1)
        q_scr[p] = jnp.concatenate([qa, qb], axis=0)
    ones_v = jnp.ones((tk, LANES), BF16)

    def score(j, p):
        ks = pl.multiple_of(j * tk, tk)
        psl = slice(p * LANES, (p + 1) * LANES)
        kx = jnp.concatenate([k_ref[pl.ds(ks, tk), psl], ck_ref[pl.ds(ks, tk), psl]], axis=1)
        return _dot_nt(q_scr[p], kx)

    def values(j, p):
        ks = pl.multiple_of(j * tk, tk)
        return jnp.concatenate([v_ref[pl.ds(ks, tk), p * LANES:(p + 1) * LANES], ones_v], axis=1)

    def causal_mask(s, j):
        t_row = t0 + lax.broadcasted_iota(jnp.int32, (tq, tk), 0)
        pos = j * tk + lax.broadcasted_iota(jnp.int32, (tq, tk), 1)
        return jnp.where((pos <= t_row)[None], s.reshape(2, tq, tk), NEG_INF).reshape(2 * tq, tk)

    _flash_sweep(t0 // tk + 1, n_pair, score, values, causal_mask, s_scr, p_scr, m_scr, acc_scr)
    for p in range(n_pair):
        acc = acc_scr[p]
        oa = acc[:tq, :LANES] / acc[:tq, LANES:]
        ob = acc[tq:, :LANES] / acc[tq:, LANES:]
        o_ref[:, p * LANES:(p + 1) * LANES] = jnp.where(first, oa, ob).astype(BF16)


def _fox(big, ck, bsz, t_len, tq=256, tk=512):
    nq = t_len // tq
    n_pair = FOX_HEADS // 2
    seq = lambda c: pl.BlockSpec((t_len, FOX_WIDTH), lambda b, i: (b, c))
    return pl.pallas_call(
        functools.partial(_fox_kernel, tq=tq, tk=tk),
        grid=(bsz, nq),
        in_specs=[pl.BlockSpec((tq, FOX_WIDTH), lambda b, i: (b * nq + i, 0)), seq(1), seq(2), seq(0)],
        out_specs=pl.BlockSpec((tq, FOX_WIDTH), lambda b, i: (b * nq + i, 0)),
        out_shape=jax.ShapeDtypeStruct((bsz * t_len, FOX_WIDTH), BF16),
        scratch_shapes=[pltpu.VMEM((n_pair, 2 * tq, 2 * LANES), BF16),
                        pltpu.VMEM((2, n_pair, 2 * tq, tk), F32),
                        pltpu.VMEM((2, n_pair, 2 * tq, tk), BF16),
                        pltpu.VMEM((n_pair, 2 * tq, 1), F32),
                        pltpu.VMEM((n_pair, 2 * tq, 2 * LANES), F32)],
        compiler_params=_params(("parallel", "arbitrary")),
        name="fox",
    )(big, big, big, ck)


def _pool_kernel(u_ref, halo_ref, w_ref, scale_ref, o_ref, x_scr, *, tt):
    i = pl.program_id(1)
    u = u_ref[...]
    x_scr[POOL_HALO:, :] = u
    x_scr[:POOL_HALO, :] = jnp.where(i > 0, halo_ref[...], 0.0)
    t_pos = i * tt + lax.broadcasted_iota(jnp.int32, (tt, 1), 0)
    lane = lax.broadcasted_iota(jnp.int32, (tt, POOL_WIDTH), 1)
    run = u
    pooled = jnp.zeros_like(u)
    for back in range(1, max(POOL_WINDOWS)):
        run = run + x_scr[POOL_HALO - back:POOL_HALO - back + tt, :]
        w = back + 1
        if w in POOL_WINDOWS:
            gi = POOL_WINDOWS.index(w)
            count = jnp.minimum(t_pos + 1, w).astype(F32)
            in_group = (lane >= gi * POOL_GROUP_DIM) & (lane < (gi + 1) * POOL_GROUP_DIM)
            pooled = jnp.where(in_group, run / count - u, pooled)
    mixed = _dot(pooled.astype(BF16), w_ref[...])
    o_ref[...] = (mixed * scale_ref[...]).astype(BF16)


def _pool(small, w_bd, scale, bsz, t_len, tt=512):
    nt = t_len // tt
    per = tt // POOL_HALO
    return pl.pallas_call(
        functools.partial(_pool_kernel, tt=tt),
        grid=(bsz, nt),
        in_specs=[pl.BlockSpec((tt, POOL_WIDTH), lambda b, i: (b * nt + i, 0)),
                  pl.BlockSpec((POOL_HALO, POOL_WIDTH), lambda b, i: (jnp.maximum((b * nt + i) * per - 1, 0), 0)),
                  pl.BlockSpec((POOL_WIDTH, POOL_WIDTH), lambda b, i: (0, 0)),
                  pl.BlockSpec((1, POOL_WIDTH), lambda b, i: (0, 0))],
        out_specs=pl.BlockSpec((tt, POOL_WIDTH), lambda b, i: (b * nt + i, 0)),
        out_shape=jax.ShapeDtypeStruct((bsz * t_len, POOL_WIDTH), BF16),
        scratch_shapes=[pltpu.VMEM((tt + POOL_HALO, POOL_WIDTH), F32)],
        compiler_params=_params(("parallel", "arbitrary")),
        name="pool",
    )(small, small, w_bd, scale)


def _compress_kernel(xk_ref, xv_ref, pek_ref, pev_ref, w1k_ref, w2k_ref, w1v_ref, w2v_ref, kc_ref, vc_ref):
    def one(x_ref, pe_ref, w1_ref, w2_ref, out_ref):
        n_chunk = x_ref.shape[0] // CMP_STRIDE
        first = jnp.zeros((n_chunk, w1_ref.shape[2]), F32)
        second = jnp.zeros((n_chunk, w1_ref.shape[2]), F32)
        for r in range(CMP_STRIDE):
            x_r = x_ref[pl.ds(r, n_chunk, stride=CMP_STRIDE), :]
            first = first + _dot((x_r + pe_ref[r:r + 1, :]).astype(BF16), w1_ref[r])
            second = second + _dot((x_r + pe_ref[CMP_STRIDE + r:CMP_STRIDE + r + 1, :]).astype(BF16),
                                   w1_ref[CMP_STRIDE + r])
        hidden = first + pltpu.roll(second, n_chunk - 1, axis=0)
        out_ref[0] = _dot(jax.nn.gelu(hidden).astype(BF16), w2_ref[...])

    one(xk_ref, pek_ref, w1k_ref, w2k_ref, kc_ref)
    one(xv_ref, pev_ref, w1v_ref, w2v_ref, vc_ref)


def _compress(small, pe_k, pe_v, w1_k, w2_k, w1_v, w2_v, bsz, t_len):
    n_chunk = t_len // CMP_STRIDE
    full = lambda a: pl.BlockSpec(a.shape, lambda b: tuple(0 for _ in a.shape))
    x_spec = lambda col: pl.BlockSpec((t_len, KV_WIDTH), lambda b: (b, col // KV_WIDTH))
    o_spec = pl.BlockSpec((1, n_chunk, KV_WIDTH), lambda b: (b, 0, 0))
    return pl.pallas_call(
        _compress_kernel,
        grid=(bsz,),
        in_specs=[x_spec(COL_CMP), x_spec(COL_CMP + KV_WIDTH), full(pe_k), full(pe_v),
                  full(w1_k), full(w2_k), full(w1_v), full(w2_v)],
        out_specs=[o_spec, o_spec],
        out_shape=[jax.ShapeDtypeStruct((bsz, n_chunk, KV_WIDTH), F32)] * 2,
        compiler_params=_params(("parallel",)),
        name="compress",
    )(small, small, pe_k, pe_v, w1_k, w2_k, w1_v, w2_v)


def _compress_weights(pe, w1, w2):
    def block_diag(w):
        zero = jnp.zeros_like(w)
        rows = [jnp.concatenate([w if h == g else zero for h in range(NSA_KV_HEADS)], axis=-1)
                for g in range(NSA_KV_HEADS)]
        return jnp.concatenate(rows, axis=-2)

    w1_bd = block_diag(w1.reshape(CMP_LEN, HEAD_DIM, CMP_HIDDEN).astype(BF16))
    return jnp.tile(pe, (1, NSA_KV_HEADS)), w1_bd, block_diag(w2.astype(BF16))


def _nsa_kernel(q_ref, ksl_ref, vsl_ref, kwn_ref, vwn_ref, kpos_ref, kc_ref, vc_ref, cpos_ref, ovt_ref, slope_ref,
                sm_ref, gb_ref, o_ref, qs_scr, s_scr, p_scr, m_scr, acc_scr, *, tq, tks):
    i = pl.program_id(1)
    t0 = i * tq
    n_head = NSA_HEADS
    rows = n_head * tq
    n_cmp = kc_ref.shape[1]
    n_blk = ovt_ref.shape[0]
    win_keys = WINDOW + tq
    lane = lax.broadcasted_iota(jnp.int32, (tq, LANES), 1)
    first = lane < HEAD_DIM
    scale = jnp.asarray(HEAD_DIM ** -0.5, BF16)

    pairs = [q_ref[:, r * LANES:(r + 1) * LANES] * scale for r in range(NSA_REP)]
    zero = jnp.zeros_like(pairs[0])
    q_heads, q_slopes = [], []
    for g in range(NSA_KV_HEADS):
        for r in range(NSA_REP):
            q_heads.append(jnp.where(first, pairs[r], zero) if g == 0 else jnp.where(first, zero, pairs[r]))
            q_slopes.append(jnp.broadcast_to(slope_ref[g * NSA_REP + r:g * NSA_REP + r + 1, :], (tq, LANES)))
    q_main = jnp.concatenate(q_heads, axis=0)
    q_plain = jnp.concatenate([q_main, jnp.concatenate(q_slopes, axis=0).astype(BF16)], axis=1)

    n_part = s_scr.shape[1]
    part_rows = rows // n_part

    def per_head(mask, s, fill):
        return jnp.where(mask[None], s.reshape(s.shape[0] // tq, tq, s.shape[-1]), fill).reshape(s.shape)

    t_col = t0 + lax.broadcasted_iota(jnp.int32, (tq, n_cmp), 0)
    c_end = lax.broadcasted_iota(jnp.int32, (tq, n_cmp), 1) * CMP_STRIDE + (CMP_LEN - 1)
    valid_c = c_end <= t_col
    kc_x = jnp.concatenate([kc_ref[0].astype(BF16), cpos_ref[...]], axis=1)
    vc_b = vc_ref[0].astype(BF16)
    p_parts, o_parts = [], []
    for part in range(n_part):
        rsl = slice(part * part_rows, (part + 1) * part_rows)
        s = per_head(valid_c, _dot_nt(q_plain[rsl], kc_x), NEG_INF)
        p = per_head(valid_c, jnp.exp(s - jnp.max(s, axis=1, keepdims=True)), 0.0)
        l_c = jnp.sum(p, axis=1, keepdims=True)
        p = p * jnp.where(l_c > 0.0, 1.0 / l_c, 0.0)
        p_parts.append(p)
        o_parts.append(_dot(p.astype(BF16), vc_b))
    p_c = jnp.concatenate(p_parts, axis=0)
    o_cmp = jnp.concatenate(o_parts, axis=0)

    blk_t = lax.broadcasted_iota(jnp.int32, (n_blk, tq), 0)
    cur_t = (t0 + lax.broadcasted_iota(jnp.int32, (n_blk, tq), 1)) // SEL_BLOCK
    forced_t = (blk_t == 0) | (blk_t == cur_t) | (blk_t == cur_t - 1)
    sub = lax.broadcasted_iota(jnp.int32, (SUBLANES, tq), 0)
    n_top = float(min(SEL_TOPN, n_blk))
    for g in range(NSA_KV_HEADS):
        p_sum = p_c[g * NSA_REP * tq:(g * NSA_REP + 1) * tq]
        for r in range(1, NSA_REP):
            p_sum = p_sum + p_c[(g * NSA_REP + r) * tq:(g * NSA_REP + r + 1) * tq]
        p0, p1, p2 = _split3(p_sum)
        ovt = ovt_ref[...]
        imp_t = _dot_nt(ovt, p0) + _dot_nt(ovt, p1) + _dot_nt(ovt, p2)
        score = jnp.where(forced_t, FORCE_SCORE, jnp.where(blk_t <= cur_t, imp_t, -1.0))
        chunks = [score[c * SUBLANES:(c + 1) * SUBLANES] for c in range(n_blk // SUBLANES)]
        ranks = [jnp.zeros((SUBLANES, tq), F32) for _ in chunks]
        for jp in range(n_blk):
            other = jnp.broadcast_to(score[jp:jp + 1], (SUBLANES, tq))
            for c, chunk in enumerate(chunks):
                ge = jnp.where(other >= chunk, 1.0, 0.0)
                gt = jnp.where(other > chunk, 1.0, 0.0)
                if jp < c * SUBLANES:
                    ranks[c] = ranks[c] + ge
                elif jp >= (c + 1) * SUBLANES:
                    ranks[c] = ranks[c] + gt
                else:
                    ranks[c] = ranks[c] + jnp.where(sub > jp - c * SUBLANES, ge, gt)
        rank = jnp.concatenate(ranks, axis=0)
        neg_t = jnp.where(rank < n_top, 0.0, NEG_INF)
        neg = jnp.transpose(jnp.concatenate([neg_t, jnp.zeros((LANES - n_blk, tq), F32)], axis=0))
        for r in range(NSA_REP):
            k = g * NSA_REP + r
            q_const = (neg + q_slopes[k]).astype(BF16)
            qs_scr[k * tq:(k + 1) * tq, :] = jnp.concatenate([q_heads[k], q_const], axis=1)

    ones_s = jnp.ones((tks, LANES), BF16)

    def slc_score(j, part):
        ks = pl.multiple_of(j * tks, tks)
        kx = jnp.concatenate([ksl_ref[pl.ds(ks, tks), :], kpos_ref[pl.ds(ks, tks), :]], axis=1)
        return _dot_nt(qs_scr[part * part_rows:(part + 1) * part_rows, :], kx)

    def slc_values(j, part):
        ks = pl.multiple_of(j * tks, tks)
        return jnp.concatenate([vsl_ref[pl.ds(ks, tks), :], ones_s], axis=1)

    def slc_mask(s, j):
        t_row = t0 + lax.broadcasted_iota(jnp.int32, (tq, tks), 0)
        pos = j * tks + lax.broadcasted_iota(jnp.int32, (tq, tks), 1)
        return jnp.where((pos <= t_row)[None], s.reshape(part_rows // tq, tq, tks), NEG_INF).reshape(part_rows, tks)

    _flash_sweep(t0 // tks + 1, n_part, slc_score, slc_values, slc_mask, s_scr, p_scr, m_scr, acc_scr)
    acc = acc_scr[...].reshape(rows, 2 * LANES)
    o_slc = acc[:, :LANES] / acc[:, LANES:]

    start = pl.multiple_of(jnp.maximum(t0 - WINDOW, 0), tq)
    kx = jnp.concatenate([kwn_ref[pl.ds(start, win_keys), :], kpos_ref[pl.ds(start, win_keys), :]], axis=1)
    vx = jnp.concatenate([vwn_ref[pl.ds(start, win_keys), :], jnp.ones((win_keys, LANES), BF16)], axis=1)
    dist = (t0 + lax.broadcasted_iota(jnp.int32, (tq, win_keys), 0)
            - (start + lax.broadcasted_iota(jnp.int32, (tq, win_keys), 1)))
    in_window = (dist >= 0) & (dist < WINDOW)
    o_parts = []
    win_score = lambda part: _dot_nt(q_plain[part * part_rows:(part + 1) * part_rows], kx)
    s_next = win_score(0)
    for part in range(n_part):
        s = per_head(in_window, s_next, NEG_INF)
        if part + 1 < n_part:
            s_next = win_score(part + 1)
        prob = jnp.exp(s - jnp.max(s, axis=1, keepdims=True))
        acc = _dot(prob.astype(BF16), vx)
        o_parts.append(acc[:, :LANES] / acc[:, LANES:])
    o_win = jnp.concatenate(o_parts, axis=0)

    gates = jax.nn.sigmoid(sm_ref[:, COL_GL - COL_FA:COL_GL - COL_FA + NSA_HEADS * N_BRANCH] + gb_ref[...])
    for r in range(NSA_REP):
        halves = []
        for g in range(NSA_KV_HEADS):
            k = g * NSA_REP + r
            rsl = slice(k * tq, (k + 1) * tq)
            gate = lambda br: gates[:, k * N_BRANCH + br:k * N_BRANCH + br + 1]
            halves.append(gate(0) * o_cmp[rsl] + gate(1) * o_slc[rsl] + gate(2) * o_win[rsl])
        o_ref[:, r * LANES:(r + 1) * LANES] = jnp.where(first, halves[0], halves[1]).astype(BF16)


def _nsa(big, kpos, kc, vc, cpos, ovt, slope_tab, small, gate_b, bsz, t_len, tq=256, tks=512, n_part=3):
    nq = t_len // tq
    kv_col = COL_KV // KV_WIDTH
    kv_spec = lambda c: pl.BlockSpec((t_len, KV_WIDTH), lambda b, i: (b, kv_col + c))
    full = lambda a: pl.BlockSpec(a.shape, lambda b, i: tuple(0 for _ in a.shape))
    n_cmp = kc.shape[1]
    rows = NSA_HEADS * tq
    return pl.pallas_call(
        functools.partial(_nsa_kernel, tq=tq, tks=tks),
        grid=(bsz, nq),
        in_specs=[pl.BlockSpec((tq, NSA_WIDTH), lambda b, i: (b * nq + i, COL_QC // NSA_WIDTH)),
                  kv_spec(0), kv_spec(1), kv_spec(2), kv_spec(3), full(kpos),
                  pl.BlockSpec((1, n_cmp, KV_WIDTH), lambda b, i: (b, 0, 0)),
                  pl.BlockSpec((1, n_cmp, KV_WIDTH), lambda b, i: (b, 0, 0)),
                  full(cpos), full(ovt), full(slope_tab),
                  pl.BlockSpec((tq, LANES), lambda b, i: (b * nq + i, COL_FA // LANES)),
                  full(gate_b)],
        out_specs=pl.BlockSpec((tq, NSA_WIDTH), lambda b, i: (b * nq + i, 0)),
        out_shape=jax.ShapeDtypeStruct((bsz * t_len, NSA_WIDTH), BF16),
        scratch_shapes=[pltpu.VMEM((rows, 2 * LANES), BF16),
                        pltpu.VMEM((2, n_part, rows // n_part, tks), F32),
                        pltpu.VMEM((2, n_part, rows // n_part, tks), BF16),
                        pltpu.VMEM((n_part, rows // n_part, 1), F32),
                        pltpu.VMEM((n_part, rows // n_part, 2 * LANES), F32)],
        compiler_params=_params(("parallel", "arbitrary")),
        name="nsa",
    )(big, big, big, big, big, kpos, kc, vc, cpos, ovt, slope_tab, small, gate_b)


def _position_columns(pos):
    lane = jnp.arange(LANES)[None, :]
    hi = (SEL_BLOCK * (pos // SEL_BLOCK))[:, None]
    lo = (pos % SEL_BLOCK)[:, None]
    cols = (jnp.where((lane >= COL_POS) & (lane < COL_POS + N_PIECES), hi, 0)
            + jnp.where((lane >= COL_POS + N_PIECES) & (lane < COL_POS + 2 * N_PIECES), lo, 0))
    return cols.astype(BF16)


def _nsa_tables(t_len):
    pos = jnp.arange(t_len)
    lane = jnp.arange(LANES)[None, :]
    block_one_hot = ((lane < t_len // SEL_BLOCK) & (lane == (pos // SEL_BLOCK)[:, None])).astype(BF16)
    kpos = _position_columns(pos) + block_one_hot
    n_cmp = t_len // CMP_STRIDE
    c_start = jnp.arange(n_cmp) * CMP_STRIDE
    cpos = _position_columns(c_start + CMP_LEN - 1)
    s_start = (jnp.arange(t_len // SEL_BLOCK) * SEL_BLOCK)[:, None]
    ovt = ((c_start[None, :] < s_start + SEL_BLOCK) & (c_start[None, :] + CMP_LEN - 1 >= s_start)).astype(BF16)
    slopes = jnp.exp2(-8.0 * jnp.arange(1, NSA_HEADS + 1, dtype=F32) / NSA_HEADS)
    pieces = jnp.stack([piece.astype(F32) for piece in _split3(slopes)], axis=1)
    slope_tab = jnp.zeros((NSA_HEADS, LANES), F32)
    slope_tab = (slope_tab.at[:, COL_POS:COL_POS + N_PIECES].set(pieces)
                 .at[:, COL_POS + N_PIECES:COL_POS + 2 * N_PIECES].set(pieces))
    return kpos, cpos, ovt, slope_tab


def _mem_kv_kernel(mem_ref, g_ref, wk_ref, wv_ref, k_ref, v_ref):
    mb = _rms(mem_ref[0], g_ref[...]).astype(BF16)
    k_ref[0] = _dot(mb, wk_ref[...]).astype(BF16)
    v_ref[0] = _dot(mb, wv_ref[...]).astype(BF16)


def _mem_kv(mem, g, wk, wv):
    bsz, m_len, _ = mem.shape
    w_spec = pl.BlockSpec((D_MODEL, D_MODEL), lambda b: (0, 0))
    m_spec = pl.BlockSpec((1, m_len, D_MODEL), lambda b: (b, 0, 0))
    return pl.pallas_call(
        _mem_kv_kernel,
        grid=(bsz,),
        in_specs=[m_spec, pl.BlockSpec((1, D_MODEL), lambda b: (0, 0)), w_spec, w_spec],
        out_specs=[m_spec, m_spec],
        out_shape=[jax.ShapeDtypeStruct((bsz, m_len, D_MODEL), BF16)] * 2,
        compiler_params=_params(("parallel",)),
        name="mem_kv",
    )(mem, g, wk, wv)


def _route(logits):
    lane = lax.broadcasted_iota(jnp.int32, logits.shape, 1)
    big_i = jnp.int32(1 << 20)
    is_coarse = lane < N_GROUPS
    lc = jnp.where(is_coarse, logits, NEG_INF)
    mc = jnp.max(lc, axis=1, keepdims=True)
    grp = jnp.min(jnp.where(is_coarse & (lc == mc), lane, big_i), axis=1, keepdims=True)
    w_grp = 1.0 / jnp.sum(jnp.where(is_coarse, jnp.exp(lc - mc), 0.0), axis=1, keepdims=True)
    e_id = lane - N_GROUPS
    in_grp = (e_id >= grp * EXPERTS_PER_GROUP) & (e_id < (grp + 1) * EXPERTS_PER_GROUP)
    lf = jnp.where(in_grp, logits, NEG_INF)
    m1 = jnp.max(lf, axis=1, keepdims=True)
    i1 = jnp.min(jnp.where(in_grp & (lf == m1), lane, big_i), axis=1, keepdims=True)
    lf2 = jnp.where(lane == i1, NEG_INF, lf)
    m2 = jnp.max(lf2, axis=1, keepdims=True)
    i2 = jnp.min(jnp.where(in_grp & (lane != i1) & (lf2 == m2), lane, big_i), axis=1, keepdims=True)
    denom = jnp.sum(jnp.where(in_grp, jnp.exp(lf - m1), 0.0), axis=1, keepdims=True)
    p1 = 1.0 / denom
    p2 = jnp.exp(m2 - m1) / denom
    tot = p1 + p2
    w1 = w_grp * (p1 / tot)
    w2 = w_grp * (p2 / tot)
    weights = jnp.where(lane + N_GROUPS == i1, w1, jnp.where(lane + N_GROUPS == i2, w2, 0.0))
    return jnp.where(lane == grp + N_EXPERTS, 1.0, weights)


def _mix_out_kernel(x_ref, oa_ref, ob_ref, oc_ref, woa_ref, wob_ref, woc_ref, gx_ref, wq_ref, mk_ref, mv_ref,
                    wo_ref, gf_ref, wr_ref, br_ref, x2_ref, hb_ref, comb_ref, cnt_ref):
    n_part = 2
    rows = x_ref.shape[0] // n_part
    parts = [slice(k * rows, (k + 1) * rows) for k in range(n_part)]
    x1 = [x_ref[r, :] + _dot(oa_ref[r, :], woa_ref[...]) + _dot(ob_ref[r, :], wob_ref[...])
          + _dot(oc_ref[r, :], woc_ref[...]) for r in parts]
    hq = [_rms(v, gx_ref[...]).astype(BF16) for v in x1]
    q = [(_dot(v, wq_ref[...]) * (MEM_HEAD_DIM ** -0.5)).astype(BF16) for v in hq]
    heads = [[] for _ in parts]
    for h in range(MEM_HEADS):
        sl = slice(h * MEM_HEAD_DIM, (h + 1) * MEM_HEAD_DIM)
        for k in range(n_part):
            s = _dot_nt(q[k][:, sl], mk_ref[0, :, sl])
            p = jnp.exp(s - jnp.max(s, axis=1, keepdims=True))
            p = p / jnp.sum(p, axis=1, keepdims=True)
            heads[k].append(_dot(p.astype(BF16), mv_ref[0, :, sl]).astype(BF16))
    x2 = [x1[k] + _dot(jnp.concatenate(heads[k], axis=1), wo_ref[...]) for k in range(n_part)]
    hn = [_rms(v, gf_ref[...]) for v in x2]
    for k, r in enumerate(parts):
        x2_ref[r, :] = x2[k]
        hb_ref[r, :] = hn[k].astype(BF16)
        h0, h1, _ = _split3(hn[k])
        partial = _dot(h0, wr_ref[...]) + _dot(h1, wr_ref[...])
        logits = partial + pltpu.roll(partial, ROUTER_WIDTH - ROUTER_PIECE, axis=1) + br_ref[...]
        routed = _route(logits)
        comb_ref[r, :] = routed
        per_part = rows // MOE_TILE
        for sub in range(per_part):
            cnt_ref[k * per_part + sub] = jnp.sum(routed[sub * MOE_TILE:(sub + 1) * MOE_TILE], axis=0, keepdims=True)


def _mix_out(x, o_a, o_b, o_c, wo_a, wo_b, wo_c, g_x, wq, mem_k, mem_v, wo, g_f, w_r, b_r, t_len, tm=1024):
    n = x.shape[0]
    per_batch = t_len // tm
    m_len = mem_k.shape[1]
    tile = lambda width: pl.BlockSpec((tm, width), lambda i: (i, 0))
    full = lambda shape: pl.BlockSpec(shape, lambda i: tuple(0 for _ in shape))
    mem_spec = pl.BlockSpec((1, m_len, D_MODEL), lambda i: (i // per_batch, 0, 0))
    return pl.pallas_call(
        _mix_out_kernel,
        grid=(n // tm,),
        in_specs=[tile(D_MODEL), tile(FOX_WIDTH), tile(POOL_WIDTH), tile(NSA_WIDTH),
                  full(wo_a.shape), full(wo_b.shape), full(wo_c.shape), full(g_x.shape), full(wq.shape),
                  mem_spec, mem_spec, full(wo.shape), full(g_f.shape), full(w_r.shape), full(b_r.shape)],
        out_specs=[tile(D_MODEL), tile(D_MODEL), tile(ROUTER_WIDTH),
                   pl.BlockSpec((tm // MOE_TILE, 1, ROUTER_WIDTH), lambda i: (i, 0, 0))],
        out_shape=[jax.ShapeDtypeStruct((n, D_MODEL), F32),
                   jax.ShapeDtypeStruct((n, D_MODEL), BF16),
                   jax.ShapeDtypeStruct((n, ROUTER_WIDTH), F32),
                   jax.ShapeDtypeStruct((n // MOE_TILE, 1, ROUTER_WIDTH), F32)],
        compiler_params=_params(("parallel",)),
        name="mix_out",
    )(x, o_a, o_b, o_c, wo_a, wo_b, wo_c, g_x, wq, mem_k, mem_v, wo, g_f, w_r, b_r)


def _group_ranks(comb, tri):
    lane = lax.broadcasted_iota(jnp.int32, comb.shape, 1)
    member = jnp.where((lane >= N_EXPERTS) & (lane < N_EXPERTS + N_GROUPS), comb, 0.0)
    return member, _dot(tri, member.astype(BF16))


def _moe_sort_kernel(dst_ref, hb_ref, comb_ref, tri_ref, xs_in, cs_in, xs_ref, cs_ref, stage_x, stage_c, sem):
    del xs_in, cs_in
    t = pl.program_id(0)
    comb = comb_ref[...]
    member, rank = _group_ranks(comb, tri_ref[...])
    rank_t = jnp.transpose(rank)
    member_t = jnp.transpose(member)
    slot = lax.broadcasted_iota(jnp.int32, (MOE_TILE, MOE_TILE), 0).astype(F32)
    pieces = _split3(comb)
    buf = t % 2

    def copies(into, g, dst):
        return (pltpu.make_async_copy(stage_x.at[into, g], xs_ref.at[pl.ds(dst, MOE_TILE)], sem.at[0, into, g]),
                pltpu.make_async_copy(stage_c.at[into, g], cs_ref.at[pl.ds(dst, MOE_TILE)], sem.at[1, into, g]))

    for g in range(N_GROUPS):
        row = N_EXPERTS + g
        take = jnp.where((rank_t[row:row + 1, :] == slot) & (member_t[row:row + 1, :] > 0.5), 1.0, 0.0).astype(BF16)
        stage_x[buf, g] = _dot(take, hb_ref[...]).astype(BF16)
        stage_c[buf, g] = _dot(take, pieces[0]) + _dot(take, pieces[1]) + _dot(take, pieces[2])

    @pl.when(t > 0)
    def _():
        for g in range(N_GROUPS):
            for copy in copies(1 - buf, g, 0):
                copy.wait()

    for g in range(N_GROUPS):
        dst = pl.multiple_of(dst_ref[t * N_GROUPS + g], MOE_SEG_ALIGN)
        for copy in copies(buf, g, dst):
            copy.start()

    @pl.when(t == pl.num_programs(0) - 1)
    def _():
        for g in range(N_GROUPS):
            for copy in copies(buf, g, 0):
                copy.wait()


def _moe_sort(hb, comb, tri, seg_dst, rows):
    n = hb.shape[0]
    any_spec = pl.BlockSpec(memory_space=pl.ANY)
    xs0 = jnp.zeros((rows, D_MODEL), BF16)
    cs0 = jnp.zeros((rows, ROUTER_WIDTH), F32)
    return pl.pallas_call(
        _moe_sort_kernel,
        grid_spec=pltpu.PrefetchScalarGridSpec(
            num_scalar_prefetch=1,
            grid=(n // MOE_TILE,),
            in_specs=[pl.BlockSpec((MOE_TILE, D_MODEL), lambda t, dst: (t, 0)),
                      pl.BlockSpec((MOE_TILE, ROUTER_WIDTH), lambda t, dst: (t, 0)),
                      pl.BlockSpec((MOE_TILE, MOE_TILE), lambda t, dst: (0, 0)),
                      any_spec, any_spec],
            out_specs=[any_spec, any_spec],
            scratch_shapes=[pltpu.VMEM((2, N_GROUPS, MOE_TILE, D_MODEL), BF16),
                            pltpu.VMEM((2, N_GROUPS, MOE_TILE, ROUTER_WIDTH), F32),
                            pltpu.SemaphoreType.DMA((2, 2, N_GROUPS))]),
        out_shape=[jax.ShapeDtypeStruct((rows, D_MODEL), BF16), jax.ShapeDtypeStruct((rows, ROUTER_WIDTH), F32)],
        input_output_aliases={4: 0, 5: 1},
        compiler_params=_params(("arbitrary",)),
        name="moe_sort",
    )(seg_dst, hb, comb, tri, xs0, cs0)


def _moe_ffn_kernel(group_ref, valid_ref, xs_ref, cs_ref, wg_ref, wu_ref, wd_ref, ys_ref):
    i = pl.program_id(0)

    @pl.when(valid_ref[i] == 0)
    def _():
        ys_ref[...] = jnp.zeros(ys_ref.shape, F32)

    @pl.when(valid_ref[i] != 0)
    def _():
        x = xs_ref[...]
        cs = cs_ref[...]
        lane = lax.broadcasted_iota(jnp.int32, cs.shape, 1)
        first_expert = group_ref[i] * EXPERTS_PER_GROUP
        total = None
        for e in range(EXPERTS_PER_GROUP):
            he = (jax.nn.silu(_dot(x, wg_ref[0, e])) * _dot(x, wu_ref[0, e])).astype(BF16)
            c_e = jnp.sum(jnp.where(lane == first_expert + e, cs, 0.0), axis=1, keepdims=True)
            y = c_e * _dot(he, wd_ref[0, e])
            total = y if total is None else total + y
        ys_ref[...] = total


def _moe_ffn(xs, cs, wg, wu, wd, layer, tile_group, tile_valid):
    rows = xs.shape[0]
    tile = lambda width: pl.BlockSpec((MOE_FFN_TILE, width), lambda i, grp, ok: (i, 0))
    experts = lambda shape: pl.BlockSpec((1, EXPERTS_PER_GROUP) + shape, lambda i, grp, ok: (layer, grp[i], 0, 0))
    return pl.pallas_call(
        _moe_ffn_kernel,
        grid_spec=pltpu.PrefetchScalarGridSpec(
            num_scalar_prefetch=2,
            grid=(rows // MOE_FFN_TILE,),
            in_specs=[tile(D_MODEL), tile(ROUTER_WIDTH),
                      experts((D_MODEL, EXPERT_FF)), experts((D_MODEL, EXPERT_FF)), experts((EXPERT_FF, D_MODEL))],
            out_specs=tile(D_MODEL)),
        out_shape=jax.ShapeDtypeStruct((rows, D_MODEL), F32),
        compiler_params=_params(("arbitrary",)),
        name="moe_ffn",
    )(tile_group, tile_valid, xs, cs, wg, wu, wd)


def _moe_unsort_kernel(dst_ref, ys_ref, comb_ref, tri_ref, x_ref, gfin_ref, o_ref, buf, sem, *, final_norm):
    t = pl.program_id(0)
    slot = t % 2

    def copy(tile, g, into):
        dst = pl.multiple_of(dst_ref[tile * N_GROUPS + g], MOE_SEG_ALIGN)
        return pltpu.make_async_copy(ys_ref.at[pl.ds(dst, MOE_TILE)], buf.at[into, g], sem.at[into, g])

    @pl.when(t == 0)
    def _():
        for g in range(N_GROUPS):
            copy(0, g, 0).start()

    @pl.when(t + 1 < pl.num_programs(0))
    def _():
        for g in range(N_GROUPS):
            copy(t + 1, g, 1 - slot).start()

    member, rank = _group_ranks(comb_ref[...], tri_ref[...])
    lane_slot = lax.broadcasted_iota(jnp.int32, (MOE_TILE, MOE_TILE), 1).astype(F32)
    y = x_ref[...]
    for g in range(N_GROUPS):
        copy(t, g, slot).wait()
        col = N_EXPERTS + g
        put = jnp.where((rank[:, col:col + 1] == lane_slot) & (member[:, col:col + 1] > 0.5), 1.0, 0.0).astype(BF16)
        seg = buf[slot, g]
        hi = seg.astype(BF16)
        lo = (seg - hi.astype(F32)).astype(BF16)
        y = y + (_dot(put, hi) + _dot(put, lo))
    o_ref[...] = _rms(y, gfin_ref[...]) if final_norm else y


def _moe_unsort(ys, comb, tri, x2, g_fin, seg_dst, final_norm):
    n = x2.shape[0]
    return pl.pallas_call(
        functools.partial(_moe_unsort_kernel, final_norm=final_norm),
        grid_spec=pltpu.PrefetchScalarGridSpec(
            num_scalar_prefetch=1,
            grid=(n // MOE_TILE,),
            in_specs=[pl.BlockSpec(memory_space=pl.ANY),
                      pl.BlockSpec((MOE_TILE, ROUTER_WIDTH), lambda t, dst: (t, 0)),
                      pl.BlockSpec((MOE_TILE, MOE_TILE), lambda t, dst: (0, 0)),
                      pl.BlockSpec((MOE_TILE, D_MODEL), lambda t, dst: (t, 0)),
                      pl.BlockSpec((1, D_MODEL), lambda t, dst: (0, 0))],
            out_specs=pl.BlockSpec((MOE_TILE, D_MODEL), lambda t, dst: (t, 0)),
            scratch_shapes=[pltpu.VMEM((2, N_GROUPS, MOE_TILE, D_MODEL), F32),
                            pltpu.SemaphoreType.DMA((2, N_GROUPS))]),
        out_shape=jax.ShapeDtypeStruct((n, D_MODEL), F32),
        compiler_params=_params(("arbitrary",)),
        name="moe_unsort",
    )(seg_dst, ys, comb, tri, x2, g_fin)


def _moe_plan(cnt, n_tok):
    n_tiles = cnt.shape[0]
    rows = n_tok + n_tiles * N_GROUPS * MOE_SEG_ALIGN + N_GROUPS * (MOE_FFN_TILE + MOE_GAP)
    rows = -(-rows // MOE_FFN_TILE) * MOE_FFN_TILE
    padded = (cnt + MOE_SEG_ALIGN - 1) // MOE_SEG_ALIGN * MOE_SEG_ALIGN
    within = jnp.cumsum(padded, axis=0) - padded
    region = (jnp.sum(padded, axis=0) + MOE_FFN_TILE - 1) // MOE_FFN_TILE * MOE_FFN_TILE
    start = jnp.cumsum(region + MOE_GAP) - (region + MOE_GAP)
    seg_dst = (start[None, :] + within).reshape(-1).astype(jnp.int32)
    tile_row = jnp.arange(rows // MOE_FFN_TILE) * MOE_FFN_TILE
    begun = tile_row[:, None] >= start[None, :]
    tile_valid = jnp.any(begun & (tile_row[:, None] < (start + region)[None, :]), axis=1).astype(jnp.int32)
    tile_group = jnp.clip(jnp.sum(begun, axis=1) - 1, 0, N_GROUPS - 1).astype(jnp.int32)
    return seg_dst, tile_group, tile_valid, rows


def _moe(x2, hb, comb, cnt, wg, wu, wd, layer, g_fin, final_norm):
    n = x2.shape[0]
    counts = jnp.round(cnt[:, 0, N_EXPERTS:N_EXPERTS + N_GROUPS]).astype(jnp.int32)
    seg_dst, tile_group, tile_valid, rows = _moe_plan(counts, n)
    tri = (jnp.arange(MOE_TILE)[:, None] > jnp.arange(MOE_TILE)[None, :]).astype(BF16)
    xs, cs = _moe_sort(hb, comb, tri, seg_dst, rows)
    ys = _moe_ffn(xs, cs, wg, wu, wd, layer, tile_group, tile_valid)
    return _moe_unsort(ys, comb, tri, x2, g_fin, seg_dst, final_norm)


def _pair_order(a, axis):
    shape = a.shape
    a = a.reshape(shape[:axis] + (NSA_HEADS, HEAD_DIM) + shape[axis + 1:])
    a = jnp.take(a, jnp.array(NSA_PAIR_ORDER), axis=axis)
    return a.reshape(shape)


def _split_in_proj_weight(w):
    edges = [0]
    for width in (FOX_WIDTH, FOX_WIDTH, FOX_WIDTH, FOX_HEADS, POOL_WIDTH, NSA_WIDTH) + (KV_WIDTH,) * 6 + (NSA_HEADS * N_BRANCH,):
        edges.append(edges[-1] + width)
    part = lambda k: w[:, edges[k]:edges[k + 1]]
    qa, ka, va, fa, ub, qc = (part(k) for k in range(6))
    kvs = [part(k) for k in range(6, 12)]
    gl = part(12)
    w_big = jnp.concatenate([qa, ka, va, _pair_order(qc, 1)] + kvs[2:], axis=1).astype(BF16)
    pad = jnp.zeros((w.shape[0], LANES - FOX_HEADS - NSA_HEADS * N_BRANCH), w.dtype)
    w_small = jnp.concatenate([ub, fa, gl, pad] + kvs[:2], axis=1).astype(BF16)
    return w_big, w_small


def _block_diag(w):
    groups, cin, cout = w.shape
    out = jnp.zeros((groups * cin, groups * cout), w.dtype)
    for gi in range(groups):
        out = out.at[gi * cin:(gi + 1) * cin, gi * cout:(gi + 1) * cout].set(w[gi])
    return out


def _layer(x, mem, p, experts, layer, bsz, t_len, final_g, final_norm):
    row = lambda v: v.reshape(1, -1)
    w_big, w_small = _split_in_proj_weight(p["w_in"])
    big, small = _in_proj(x, row(p["mix_norm_g"]), w_big, w_small)

    b_row = jnp.pad(p["fox_forget_b"], (0, LANES - FOX_HEADS)).reshape(1, LANES)
    ck = _forget_cumsum(small, b_row, _forget_placement(), bsz, t_len)
    o_a = _fox(big, ck, bsz, t_len)

    o_b = _pool(small, _block_diag(p["pool_w"]).astype(BF16), row(p["pool_scale"]), bsz, t_len)

    pe_k, w1_k, w2_k = _compress_weights(p["cmp_pe_k"], p["cmp_w1_k"], p["cmp_w2_k"])
    pe_v, w1_v, w2_v = _compress_weights(p["cmp_pe_v"], p["cmp_w1_v"], p["cmp_w2_v"])
    kc, vc = _compress(small, pe_k, pe_v, w1_k, w2_k, w1_v, w2_v, bsz, t_len)
    kpos, cpos, ovt, slope_tab = _nsa_tables(t_len)
    o_c = _nsa(big, kpos, kc, vc, cpos, ovt, slope_tab, small, row(p["nsa_gate_b"]), bsz, t_len)

    mem_k, mem_v = _mem_kv(mem, row(p["mem_norm_g"]), p["xattn_wk"].astype(BF16), p["xattn_wv"].astype(BF16))
    w_out = p["w_out"]
    wo_a = w_out[:FOX_WIDTH].astype(BF16)
    wo_b = w_out[FOX_WIDTH:FOX_WIDTH + POOL_WIDTH].astype(BF16)
    wo_c = _pair_order(w_out[FOX_WIDTH + POOL_WIDTH:], 0).astype(BF16)
    w_r = jnp.concatenate([p["router_coarse_w"], p["router_fine_w"],
                           jnp.zeros((D_MODEL, ROUTER_PIECE - N_GROUPS - N_EXPERTS), F32)], axis=1)
    w_r0, w_r1, _ = _split3(w_r)
    w_r = jnp.concatenate([w_r0, w_r1, jnp.zeros((D_MODEL, ROUTER_WIDTH - 2 * ROUTER_PIECE), BF16)], axis=1)
    b_r = jnp.concatenate([p["router_coarse_b"], p["router_fine_b"],
                           jnp.zeros((ROUTER_WIDTH - N_GROUPS - N_EXPERTS,), F32)]).reshape(1, -1)
    x2, hb, comb, cnt = _mix_out(x, o_a, o_b, o_c, wo_a, wo_b, wo_c, row(p["xattn_norm_g"]),
                                 p["xattn_wq"].astype(BF16), mem_k, mem_v, p["xattn_wo"].astype(BF16),
                                 row(p["ffn_norm_g"]), w_r, b_r, t_len)

    return _moe(x2, hb, comb, cnt, *experts, layer, row(final_g), final_norm)


def kernel(x, mem, mix_norm_g, w_in, fox_forget_b, pool_w, pool_scale, cmp_pe_k, cmp_pe_v, cmp_w1_k, cmp_w2_k, cmp_w1_v, cmp_w2_v, nsa_gate_b, w_out, xattn_norm_g, mem_norm_g, xattn_wq, xattn_wk, xattn_wv, xattn_wo, ffn_norm_g, router_coarse_w, router_coarse_b, router_fine_w, router_fine_b, exp_w_gate, exp_w_up, exp_w_down, final_norm_g):
    stacked = dict(mix_norm_g=mix_norm_g, w_in=w_in, fox_forget_b=fox_forget_b, pool_w=pool_w, pool_scale=pool_scale,
                   cmp_pe_k=cmp_pe_k, cmp_pe_v=cmp_pe_v, cmp_w1_k=cmp_w1_k, cmp_w2_k=cmp_w2_k, cmp_w1_v=cmp_w1_v,
                   cmp_w2_v=cmp_w2_v, nsa_gate_b=nsa_gate_b, w_out=w_out, xattn_norm_g=xattn_norm_g,
                   mem_norm_g=mem_norm_g, xattn_wq=xattn_wq, xattn_wk=xattn_wk, xattn_wv=xattn_wv, xattn_wo=xattn_wo,
                   ffn_norm_g=ffn_norm_g, router_coarse_w=router_coarse_w, router_coarse_b=router_coarse_b,
                   router_fine_w=router_fine_w, router_fine_b=router_fine_b)
    experts = (exp_w_gate.astype(BF16), exp_w_up.astype(BF16), exp_w_down.astype(BF16))
    bsz, t_len, d = x.shape
    depth = w_in.shape[0]
    h = x.reshape(bsz * t_len, d)
    for layer in range(depth):
        p = {name: value[layer] for name, value in stacked.items()}
        h = _layer(h, mem, p, experts, layer, bsz, t_len, final_norm_g, final_norm=(layer == depth - 1))
    return h.reshape(bsz, t_len, d)
```

```python
import functools

import jax
import jax.numpy as jnp
from jax import lax
from jax.experimental import pallas as pl
from jax.experimental.pallas import tpu as pltpu

F32 = jnp.float32
BF16 = jnp.bfloat16

D_MODEL = 1024
HEAD_DIM = 64
FOX_HEADS = 6
FOX_WIDTH = FOX_HEADS * HEAD_DIM
POOL_WIDTH = 256
POOL_GROUP_DIM = 64
POOL_WINDOWS = (2, 4, 8, 16)
POOL_HALO = 16
NSA_HEADS = 6
NSA_KV_HEADS = 2
NSA_REP = NSA_HEADS // NSA_KV_HEADS
NSA_WIDTH = NSA_HEADS * HEAD_DIM
KV_WIDTH = NSA_KV_HEADS * HEAD_DIM
N_BRANCH = 3
CMP_LEN = 32
CMP_STRIDE = 16
CMP_HIDDEN = 256
SEL_BLOCK = 64
SEL_TOPN = 16
WINDOW = 512
FORCE_SCORE = 1.0e4
MEM_HEADS = 4
MEM_HEAD_DIM = D_MODEL // MEM_HEADS
N_GROUPS = 4
EXPERTS_PER_GROUP = 4
N_EXPERTS = N_GROUPS * EXPERTS_PER_GROUP
EXPERT_FF = 512
EPS = 1e-6
NEG_INF = -1e30

LANES = 128
SUBLANES = 8
BIG_WIDTH = 3 * FOX_WIDTH + NSA_WIDTH + 4 * KV_WIDTH
COL_QC = 3 * FOX_WIDTH
COL_KV = COL_QC + NSA_WIDTH
SMALL_WIDTH = POOL_WIDTH + LANES + 2 * KV_WIDTH
COL_FA = POOL_WIDTH
COL_GL = POOL_WIDTH + FOX_HEADS
COL_CMP = POOL_WIDTH + LANES
ROUTER_WIDTH = LANES
ROUTER_PIECE = 32
MOE_TILE = 256
MOE_SEG_ALIGN = 16
MOE_FFN_TILE = 512
MOE_GAP = 512
NSA_PAIR_ORDER = tuple(g * NSA_REP + r for r in range(NSA_REP) for g in range(NSA_KV_HEADS))
N_PIECES = 3
COL_POS = HEAD_DIM

VMEM_LIMIT = 56 * 1024 * 1024

_NT = (((1,), (1,)), ((), ()))


def _params(semantics):
    return pltpu.CompilerParams(dimension_semantics=semantics, vmem_limit_bytes=VMEM_LIMIT)


def _rms(x, g):
    return x * lax.rsqrt(jnp.mean(x * x, axis=-1, keepdims=True) + EPS) * g


def _dot(a, b):
    return jnp.dot(a, b, preferred_element_type=F32)


def _dot_nt(a, b):
    return lax.dot_general(a, b, _NT, preferred_element_type=F32)


def _split3(a):
    a0 = a.astype(BF16)
    r = a - a0.astype(F32)
    a1 = r.astype(BF16)
    a2 = (r - a1.astype(F32)).astype(BF16)
    return a0, a1, a2


def _in_proj_kernel(x_ref, g_ref, wb_ref, ws_ref, big_ref, small_ref):
    yb = _rms(x_ref[...], g_ref[...]).astype(BF16)
    big_ref[...] = _dot(yb, wb_ref[...]).astype(BF16)
    small_ref[...] = _dot(yb, ws_ref[...])


def _in_proj(x, g, w_big, w_small, tm=1024):
    n = x.shape[0]
    return pl.pallas_call(
        _in_proj_kernel,
        grid=(n // tm,),
        in_specs=[pl.BlockSpec((tm, D_MODEL), lambda i: (i, 0)),
                  pl.BlockSpec((1, D_MODEL), lambda i: (0, 0)),
                  pl.BlockSpec((D_MODEL, BIG_WIDTH), lambda i: (0, 0)),
                  pl.BlockSpec((D_MODEL, SMALL_WIDTH), lambda i: (0, 0))],
        out_specs=[pl.BlockSpec((tm, BIG_WIDTH), lambda i: (i, 0)),
                   pl.BlockSpec((tm, SMALL_WIDTH), lambda i: (i, 0))],
        out_shape=[jax.ShapeDtypeStruct((n, BIG_WIDTH), BF16),
                   jax.ShapeDtypeStruct((n, SMALL_WIDTH), F32)],
        compiler_params=_params(("parallel",)),
        name="in_proj",
    )(x, g, w_big, w_small)


def _forget_kernel(f_ref, b_ref, place_ref, ck_ref):
    c = jax.nn.log_sigmoid(f_ref[...] + b_ref[...])
    t_len = c.shape[0]
    row = lax.broadcasted_iota(jnp.int32, c.shape, 0)
    sh = 1
    while sh < t_len:
        c = c + jnp.where(row >= sh, pltpu.roll(c, sh, axis=0), 0.0)
        sh *= 2
    n0, n1, n2 = _split3(-c)
    ck_ref[...] = (_dot(n0, place_ref[0]) + _dot(n1, place_ref[1]) + _dot(n2, place_ref[2])).astype(BF16)


def _forget_cumsum(small, b_row, place, bsz, t_len):
    return pl.pallas_call(
        _forget_kernel,
        grid=(bsz,),
        in_specs=[pl.BlockSpec((t_len, LANES), lambda b: (b, COL_FA // LANES)),
                  pl.BlockSpec((1, LANES), lambda b: (0, 0)),
                  pl.BlockSpec(place.shape, lambda b: (0, 0, 0))],
        out_specs=pl.BlockSpec((t_len, FOX_WIDTH), lambda b: (b, 0)),
        out_shape=jax.ShapeDtypeStruct((bsz * t_len, FOX_WIDTH), BF16),
        compiler_params=_params(("parallel",)),
        name="forget",
    )(small, b_row, place)


def _forget_placement():
    src = jnp.arange(LANES)[None, :, None]
    dst = jnp.arange(FOX_WIDTH)[None, None, :]
    piece = jnp.arange(N_PIECES)[:, None, None]
    target = LANES * (src // 2) + N_PIECES * (src % 2) + piece
    return ((src < FOX_HEADS) & (dst == target)).astype(BF16)


def _flash_sweep(n_tiles, n_stream, score, values, last_mask, s_scr, p_scr, m_scr, acc_scr):
    def run_stage(j, a, first=False, last=False):
        b = 1 - a
        for st in range(n_stream):
            s = score(0, st) if first else s_scr[a, st]
            if last:
                s = last_mask(s, j)
            m_old = jnp.full(m_scr.shape[1:], NEG_INF, F32) if first else m_scr[st]
            m_new = jnp.maximum(m_old, jnp.max(s, axis=1, keepdims=True))
            prob = jnp.exp(s - m_new).astype(BF16)
            if not last:
                p_scr[a, st] = prob
                s_scr[b, st] = score(j + 1, st)
            acc = jnp.zeros(acc_scr.shape[1:], F32) if first else acc_scr[st]
            if not first:
                acc = (acc + _dot(p_scr[b, st], values(j - 1, st))) * jnp.exp(m_old - m_new)
            if last:
                acc = acc + _dot(prob, values(j, st))
            acc_scr[st] = acc
            m_scr[st] = m_new

    n_main = n_tiles - 1

    @pl.when(n_main == 0)
    def _():
        run_stage(0, 0, first=True, last=True)

    @pl.when(n_main > 0)
    def _():
        run_stage(0, 0, first=True)

    def two_stages(jj, carry):
        run_stage(2 * jj + 1, 1)
        run_stage(2 * jj + 2, 0)
        return carry

    lax.fori_loop(0, jnp.maximum(n_main - 1, 0) // 2, two_stages, 0)

    @pl.when(jnp.logical_and(n_main >= 2, n_main % 2 == 0))
    def _():
        run_stage(n_main - 1, 1)

    for a in range(2):
        @pl.when(jnp.logical_and(n_main >= 1, n_main % 2 == a))
        def _():
            run_stage(n_main, a, last=True)


def _fox_kernel(q_ref, k_ref, v_ref, ck_ref, o_ref, q_scr, s_scr, p_scr, m_scr, acc_scr, *, tq, tk):
    i = pl.program_id(1)
    t0 = i * tq
    n_pair = FOX_HEADS // 2
    lane = lax.broadcasted_iota(jnp.int32, (tq, LANES), 1)
    first = lane < HEAD_DIM
    scale = jnp.asarray(HEAD_DIM ** -0.5, BF16)
    pick_a = jnp.where(lane < N_PIECES, 1.0, 0.0).astype(BF16)
    pick_b = jnp.where((lane >= N_PIECES) & (lane < 2 * N_PIECES), 1.0, 0.0).astype(BF16)
    for p in range(n_pair):
        qp = q_ref[:, p * LANES:(p + 1) * LANES] * scale
        zero = jnp.zeros_like(qp)
        qa = jnp.concatenate([jnp.where(first, qp, zero), pick_a], axis=1)
        qb = jnp.concatenate([jnp.where(first, zero, qp), pick_b], axis=1)
        q_scr[p] = jnp.concatenate([qa, qb], axis=0)
    ones_v = jnp.ones((tk, LANES), BF16)

    def score(j, p):
        ks = pl.multiple_of(j * tk, tk)
        psl = slice(p * LANES, (p + 1) * LANES)
        kx = jnp.concatenate([k_ref[pl.ds(ks, tk), psl], ck_ref[pl.ds(ks, tk), psl]], axis=1)
        return _dot_nt(q_scr[p], kx)

    def values(j, p):
        ks = pl.multiple_of(j * tk, tk)
        return jnp.concatenate([v_ref[pl.ds(ks, tk), p * LANES:(p + 1) * LANES], ones_v], axis=1)

    def causal_mask(s, j):
        t_row = t0 + lax.broadcasted_iota(jnp.int32, (tq, tk), 0)
        pos = j * tk + lax.broadcasted_iota(jnp.int32, (tq, tk), 1)
        return jnp.where((pos <= t_row)[None], s.reshape(2, tq, tk), NEG_INF).reshape(2 * tq, tk)

    _flash_sweep(t0 // tk + 1, n_pair, score, values, causal_mask, s_scr, p_scr, m_scr, acc_scr)
    for p in range(n_pair):
        acc = acc_scr[p]
        oa = acc[:tq, :LANES] / acc[:tq, LANES:]
        ob = acc[tq:, :LANES] / acc[tq:, LANES:]
        o_ref[:, p * LANES:(p + 1) * LANES] = jnp.where(first, oa, ob).astype(BF16)


def _fox(big, ck, bsz, t_len, tq=256, tk=512):
    nq = t_len // tq
    n_pair = FOX_HEADS // 2
    seq = lambda c: pl.BlockSpec((t_len, FOX_WIDTH), lambda b, i: (b, c))
    return pl.pallas_call(
        functools.partial(_fox_kernel, tq=tq, tk=tk),
        grid=(bsz, nq),
        in_specs=[pl.BlockSpec((tq, FOX_WIDTH), lambda b, i: (b * nq + i, 0)), seq(1), seq(2), seq(0)],
        out_specs=pl.BlockSpec((tq, FOX_WIDTH), lambda b, i: (b * nq + i, 0)),
        out_shape=jax.ShapeDtypeStruct((bsz * t_len, FOX_WIDTH), BF16),
        scratch_shapes=[pltpu.VMEM((n_pair, 2 * tq, 2 * LANES), BF16),
                        pltpu.VMEM((2, n_pair, 2 * tq, tk), F32),
                        pltpu.VMEM((2, n_pair, 2 * tq, tk), BF16),
                        pltpu.VMEM((n_pair, 2 * tq, 1), F32),
                        pltpu.VMEM((n_pair, 2 * tq, 2 * LANES), F32)],
        compiler_params=_params(("parallel", "arbitrary")),
        name="fox",
    )(big, big, big, ck)


def _pool_kernel(u_ref, halo_ref, w_ref, scale_ref, o_ref, x_scr, *, tt):
    i = pl.program_id(1)
    u = u_ref[...]
    x_scr[POOL_HALO:, :] = u
    x_scr[:POOL_HALO, :] = jnp.where(i > 0, halo_ref[...], 0.0)
    t_pos = i * tt + lax.broadcasted_iota(jnp.int32, (tt, 1), 0)
    lane = lax.broadcasted_iota(jnp.int32, (tt, POOL_WIDTH), 1)
    run = u
    pooled = jnp.zeros_like(u)
    for back in range(1, max(POOL_WINDOWS)):
        run = run + x_scr[POOL_HALO - back:POOL_HALO - back + tt, :]
        w = back + 1
        if w in POOL_WINDOWS:
            gi = POOL_WINDOWS.index(w)
            count = jnp.minimum(t_pos + 1, w).astype(F32)
            in_group = (lane >= gi * POOL_GROUP_DIM) & (lane < (gi + 1) * POOL_GROUP_DIM)
            pooled = jnp.where(in_group, run / count - u, pooled)
    mixed = _dot(pooled.astype(BF16), w_ref[...])
    o_ref[...] = (mixed * scale_ref[...]).astype(BF16)


def _pool(small, w_bd, scale, bsz, t_len, tt=2048):
    nt = t_len // tt
    per = tt // POOL_HALO
    return pl.pallas_call(
        functools.partial(_pool_kernel, tt=tt),
        grid=(bsz, nt),
        in_specs=[pl.BlockSpec((tt, POOL_WIDTH), lambda b, i: (b * nt + i, 0)),
                  pl.BlockSpec((POOL_HALO, POOL_WIDTH), lambda b, i: (jnp.maximum((b * nt + i) * per - 1, 0), 0)),
                  pl.BlockSpec((POOL_WIDTH, POOL_WIDTH), lambda b, i: (0, 0)),
                  pl.BlockSpec((1, POOL_WIDTH), lambda b, i: (0, 0))],
        out_specs=pl.BlockSpec((tt, POOL_WIDTH), lambda b, i: (b * nt + i, 0)),
        out_shape=jax.ShapeDtypeStruct((bsz * t_len, POOL_WIDTH), BF16),
        scratch_shapes=[pltpu.VMEM((tt + POOL_HALO, POOL_WIDTH), F32)],
        compiler_params=_params(("parallel", "arbitrary")),
        name="pool",
    )(small, small, w_bd, scale)


def _compress_kernel(xk_ref, xv_ref, pek_ref, pev_ref, w1k_ref, w2k_ref, w1v_ref, w2v_ref, kc_ref, vc_ref):
    def one(x_ref, pe_ref, w1_ref, w2_ref, out_ref):
        n_chunk = x_ref.shape[0] // CMP_STRIDE
        first = jnp.zeros((n_chunk, w1_ref.shape[2]), F32)
        second = jnp.zeros((n_chunk, w1_ref.shape[2]), F32)
        for r in range(CMP_STRIDE):
            x_r = x_ref[pl.ds(r, n_chunk, stride=CMP_STRIDE), :]
            first = first + _dot((x_r + pe_ref[r:r + 1, :]).astype(BF16), w1_ref[r])
            second = second + _dot((x_r + pe_ref[CMP_STRIDE + r:CMP_STRIDE + r + 1, :]).astype(BF16),
                                   w1_ref[CMP_STRIDE + r])
        hidden = first + pltpu.roll(second, n_chunk - 1, axis=0)
        out_ref[0] = _dot(jax.nn.gelu(hidden).astype(BF16), w2_ref[...])

    one(xk_ref, pek_ref, w1k_ref, w2k_ref, kc_ref)
    one(xv_ref, pev_ref, w1v_ref, w2v_ref, vc_ref)


def _compress(small, pe_k, pe_v, w1_k, w2_k, w1_v, w2_v, bsz, t_len):
    n_chunk = t_len // CMP_STRIDE
    full = lambda a: pl.BlockSpec(a.shape, lambda b: tuple(0 for _ in a.shape))
    x_spec = lambda col: pl.BlockSpec((t_len, KV_WIDTH), lambda b: (b, col // KV_WIDTH))
    o_spec = pl.BlockSpec((1, n_chunk, KV_WIDTH), lambda b: (b, 0, 0))
    return pl.pallas_call(
        _compress_kernel,
        grid=(bsz,),
        in_specs=[x_spec(COL_CMP), x_spec(COL_CMP + KV_WIDTH), full(pe_k), full(pe_v),
                  full(w1_k), full(w2_k), full(w1_v), full(w2_v)],
        out_specs=[o_spec, o_spec],
        out_shape=[jax.ShapeDtypeStruct((bsz, n_chunk, KV_WIDTH), F32)] * 2,
        compiler_params=_params(("parallel",)),
        name="compress",
    )(small, small, pe_k, pe_v, w1_k, w2_k, w1_v, w2_v)


def _compress_weights(pe, w1, w2):
    def block_diag(w):
        zero = jnp.zeros_like(w)
        rows = [jnp.concatenate([w if h == g else zero for h in range(NSA_KV_HEADS)], axis=-1)
                for g in range(NSA_KV_HEADS)]
        return jnp.concatenate(rows, axis=-2)

    w1_bd = block_diag(w1.reshape(CMP_LEN, HEAD_DIM, CMP_HIDDEN).astype(BF16))
    return jnp.tile(pe, (1, NSA_KV_HEADS)), w1_bd, block_diag(w2.astype(BF16))


def _nsa_kernel(q_ref, ksl_ref, vsl_ref, kwn_ref, vwn_ref, kpos_ref, kc_ref, vc_ref, cpos_ref, ovt_ref, slope_ref,
                sm_ref, gb_ref, o_ref, qs_scr, s_scr, p_scr, m_scr, acc_scr, *, tq, tks):
    i = pl.program_id(1)
    t0 = i * tq
    n_head = NSA_HEADS
    rows = n_head * tq
    n_cmp = kc_ref.shape[1]
    n_blk = ovt_ref.shape[0]
    win_keys = WINDOW + tq
    lane = lax.broadcasted_iota(jnp.int32, (tq, LANES), 1)
    first = lane < HEAD_DIM
    scale = jnp.asarray(HEAD_DIM ** -0.5, BF16)

    pairs = [q_ref[:, r * LANES:(r + 1) * LANES] * scale for r in range(NSA_REP)]
    zero = jnp.zeros_like(pairs[0])
    q_heads, q_slopes = [], []
    for g in range(NSA_KV_HEADS):
        for r in range(NSA_REP):
            q_heads.append(jnp.where(first, pairs[r], zero) if g == 0 else jnp.where(first, zero, pairs[r]))
            q_slopes.append(jnp.broadcast_to(slope_ref[g * NSA_REP + r:g * NSA_REP + r + 1, :], (tq, LANES)))
    q_main = jnp.concatenate(q_heads, axis=0)
    q_plain = jnp.concatenate([q_main, jnp.concatenate(q_slopes, axis=0).astype(BF16)], axis=1)

    n_part = s_scr.shape[1]
    part_rows = rows // n_part

    def per_head(mask, s, fill):
        return jnp.where(mask[None], s.reshape(s.shape[0] // tq, tq, s.shape[-1]), fill).reshape(s.shape)

    t_col = t0 + lax.broadcasted_iota(jnp.int32, (tq, n_cmp), 0)
    c_end = lax.broadcasted_iota(jnp.int32, (tq, n_cmp), 1) * CMP_STRIDE + (CMP_LEN - 1)
    valid_c = c_end <= t_col
    kc_x = jnp.concatenate([kc_ref[0].astype(BF16), cpos_ref[...]], axis=1)
    vc_b = vc_ref[0].astype(BF16)
    p_parts, o_parts = [], []
    for part in range(n_part):
        rsl = slice(part * part_rows, (part + 1) * part_rows)
        s = per_head(valid_c, _dot_nt(q_plain[rsl], kc_x), NEG_INF)
        p = per_head(valid_c, jnp.exp(s - jnp.max(s, axis=1, keepdims=True)), 0.0)
        l_c = jnp.sum(p, axis=1, keepdims=True)
        p = p * jnp.where(l_c > 0.0, 1.0 / l_c, 0.0)
        p_parts.append(p)
        o_parts.append(_dot(p.astype(BF16), vc_b))
    p_c = jnp.concatenate(p_parts, axis=0)
    o_cmp = jnp.concatenate(o_parts, axis=0)

    blk_t = lax.broadcasted_iota(jnp.int32, (n_blk, tq), 0)
    cur_t = (t0 + lax.broadcasted_iota(jnp.int32, (n_blk, tq), 1)) // SEL_BLOCK
    forced_t = (blk_t == 0) | (blk_t == cur_t) | (blk_t == cur_t - 1)
    sub = lax.broadcasted_iota(jnp.int32, (SUBLANES, tq), 0)
    n_top = float(min(SEL_TOPN, n_blk))
    for g in range(NSA_KV_HEADS):
        p_sum = p_c[g * NSA_REP * tq:(g * NSA_REP + 1) * tq]
        for r in range(1, NSA_REP):
            p_sum = p_sum + p_c[(g * NSA_REP + r) * tq:(g * NSA_REP + r + 1) * tq]
        p0, p1, p2 = _split3(p_sum)
        ovt = ovt_ref[...]
        imp_t = _dot_nt(ovt, p0) + _dot_nt(ovt, p1) + _dot_nt(ovt, p2)
        score = jnp.where(forced_t, FORCE_SCORE, jnp.where(blk_t <= cur_t, imp_t, -1.0))
        chunks = [score[c * SUBLANES:(c + 1) * SUBLANES] for c in range(n_blk // SUBLANES)]
        ranks = [jnp.zeros((SUBLANES, tq), F32) for _ in chunks]
        for jp in range(n_blk):
            other = jnp.broadcast_to(score[jp:jp + 1], (SUBLANES, tq))
            for c, chunk in enumerate(chunks):
                ge = jnp.where(other >= chunk, 1.0, 0.0)
                gt = jnp.where(other > chunk, 1.0, 0.0)
                if jp < c * SUBLANES:
                    ranks[c] = ranks[c] + ge
                elif jp >= (c + 1) * SUBLANES:
                    ranks[c] = ranks[c] + gt
                else:
                    ranks[c] = ranks[c] + jnp.where(sub > jp - c * SUBLANES, ge, gt)
        rank = jnp.concatenate(ranks, axis=0)
        neg_t = jnp.where(rank < n_top, 0.0, NEG_INF)
        neg = jnp.transpose(jnp.concatenate([neg_t, jnp.zeros((LANES - n_blk, tq), F32)], axis=0))
        for r in range(NSA_REP):
            k = g * NSA_REP + r
            q_const = (neg + q_slopes[k]).astype(BF16)
            qs_scr[k * tq:(k + 1) * tq, :] = jnp.concatenate([q_heads[k], q_const], axis=1)

    ones_s = jnp.ones((tks, LANES), BF16)

    def slc_score(j, part):
        ks = pl.multiple_of(j * tks, tks)
        kx = jnp.concatenate([ksl_ref[pl.ds(ks, tks), :], kpos_ref[pl.ds(ks, tks), :]], axis=1)
        return _dot_nt(qs_scr[part * part_rows:(part + 1) * part_rows, :], kx)

    def slc_values(j, part):
        ks = pl.multiple_of(j * tks, tks)
        return jnp.concatenate([vsl_ref[pl.ds(ks, tks), :], ones_s], axis=1)

    def slc_mask(s, j):
        t_row = t0 + lax.broadcasted_iota(jnp.int32, (tq, tks), 0)
        pos = j * tks + lax.broadcasted_iota(jnp.int32, (tq, tks), 1)
        return jnp.where((pos <= t_row)[None], s.reshape(part_rows // tq, tq, tks), NEG_INF).reshape(part_rows, tks)

    _flash_sweep(t0 // tks + 1, n_part, slc_score, slc_values, slc_mask, s_scr, p_scr, m_scr, acc_scr)
    acc = acc_scr[...].reshape(rows, 2 * LANES)
    o_slc = acc[:, :LANES] / acc[:, LANES:]

    start = pl.multiple_of(jnp.maximum(t0 - WINDOW, 0), tq)
    kx = jnp.concatenate([kwn_ref[pl.ds(start, win_keys), :], kpos_ref[pl.ds(start, win_keys), :]], axis=1)
    vx = jnp.concatenate([vwn_ref[pl.ds(start, win_keys), :], jnp.ones((win_keys, LANES), BF16)], axis=1)
    dist = (t0 + lax.broadcasted_iota(jnp.int32, (tq, win_keys), 0)
            - (start + lax.broadcasted_iota(jnp.int32, (tq, win_keys), 1)))
    in_window = (dist >= 0) & (dist < WINDOW)
    o_parts = []
    win_score = lambda part: _dot_nt(q_plain[part * part_rows:(part + 1) * part_rows], kx)
    s_next = win_score(0)
    for part in range(n_part):
        s = per_head(in_window, s_next, NEG_INF)
        if part + 1 < n_part:
            s_next = win_score(part + 1)
        prob = jnp.exp(s - jnp.max(s, axis=1, keepdims=True))
        acc = _dot(prob.astype(BF16), vx)
        o_parts.append(acc[:, :LANES] / acc[:, LANES:])
    o_win = jnp.concatenate(o_parts, axis=0)

    gates = jax.nn.sigmoid(sm_ref[:, COL_GL - COL_FA:COL_GL - COL_FA + NSA_HEADS * N_BRANCH] + gb_ref[...])
    for r in range(NSA_REP):
        halves = []
        for g in range(NSA_KV_HEADS):
            k = g * NSA_REP + r
            rsl = slice(k * tq, (k + 1) * tq)
            gate = lambda br: gates[:, k * N_BRANCH + br:k * N_BRANCH + br + 1]
            halves.append(gate(0) * o_cmp[rsl] + gate(1) * o_slc[rsl] + gate(2) * o_win[rsl])
        o_ref[:, r * LANES:(r + 1) * LANES] = jnp.where(first, halves[0], halves[1]).astype(BF16)


def _nsa(big, kpos, kc, vc, cpos, ovt, slope_tab, small, gate_b, bsz, t_len, tq=256, tks=512, n_part=3):
    nq = t_len // tq
    kv_col = COL_KV // KV_WIDTH
    kv_spec = lambda c: pl.BlockSpec((t_len, KV_WIDTH), lambda b, i: (b, kv_col + c))
    full = lambda a: pl.BlockSpec(a.shape, lambda b, i: tuple(0 for _ in a.shape))
    n_cmp = kc.shape[1]
    rows = NSA_HEADS * tq
    return pl.pallas_call(
        functools.partial(_nsa_kernel, tq=tq, tks=tks),
        grid=(bsz, nq),
        in_specs=[pl.BlockSpec((tq, NSA_WIDTH), lambda b, i: (b * nq + i, COL_QC // NSA_WIDTH)),
                  kv_spec(0), kv_spec(1), kv_spec(2), kv_spec(3), full(kpos),
                  pl.BlockSpec((1, n_cmp, KV_WIDTH), lambda b, i: (b, 0, 0)),
                  pl.BlockSpec((1, n_cmp, KV_WIDTH), lambda b, i: (b, 0, 0)),
                  full(cpos), full(ovt), full(slope_tab),
                  pl.BlockSpec((tq, LANES), lambda b, i: (b * nq + i, COL_FA // LANES)),
                  full(gate_b)],
        out_specs=pl.BlockSpec((tq, NSA_WIDTH), lambda b, i: (b * nq + i, 0)),
        out_shape=jax.ShapeDtypeStruct((bsz * t_len, NSA_WIDTH), BF16),
        scratch_shapes=[pltpu.VMEM((rows, 2 * LANES), BF16),
                        pltpu.VMEM((2, n_part, rows // n_part, tks), F32),
                        pltpu.VMEM((2, n_part, rows // n_part, tks), BF16),
                        pltpu.VMEM((n_part, rows // n_part, 1), F32),
                        pltpu.VMEM((n_part, rows // n_part, 2 * LANES), F32)],
        compiler_params=_params(("parallel", "arbitrary")),
        name="nsa",
    )(big, big, big, big, big, kpos, kc, vc, cpos, ovt, slope_tab, small, gate_b)


def _position_columns(pos):
    lane = jnp.arange(LANES)[None, :]
    hi = (SEL_BLOCK * (pos // SEL_BLOCK))[:, None]
    lo = (pos % SEL_BLOCK)[:, None]
    cols = (jnp.where((lane >= COL_POS) & (lane < COL_POS + N_PIECES), hi, 0)
            + jnp.where((lane >= COL_POS + N_PIECES) & (lane < COL_POS + 2 * N_PIECES), lo, 0))
    return cols.astype(BF16)


def _nsa_tables(t_len):
    pos = jnp.arange(t_len)
    lane = jnp.arange(LANES)[None, :]
    block_one_hot = ((lane < t_len // SEL_BLOCK) & (lane == (pos // SEL_BLOCK)[:, None])).astype(BF16)
    kpos = _position_columns(pos) + block_one_hot
    n_cmp = t_len // CMP_STRIDE
    c_start = jnp.arange(n_cmp) * CMP_STRIDE
    cpos = _position_columns(c_start + CMP_LEN - 1)
    s_start = (jnp.arange(t_len // SEL_BLOCK) * SEL_BLOCK)[:, None]
    ovt = ((c_start[None, :] < s_start + SEL_BLOCK) & (c_start[None, :] + CMP_LEN - 1 >= s_start)).astype(BF16)
    slopes = jnp.exp2(-8.0 * jnp.arange(1, NSA_HEADS + 1, dtype=F32) / NSA_HEADS)
    pieces = jnp.stack([piece.astype(F32) for piece in _split3(slopes)], axis=1)
    slope_tab = jnp.zeros((NSA_HEADS, LANES), F32)
    slope_tab = (slope_tab.at[:, COL_POS:COL_POS + N_PIECES].set(pieces)
                 .at[:, COL_POS + N_PIECES:COL_POS + 2 * N_PIECES].set(pieces))
    return kpos, cpos, ovt, slope_tab


def _mem_kv_kernel(mem_ref, g_ref, wk_ref, wv_ref, k_ref, v_ref):
    mb = _rms(mem_ref[0], g_ref[...]).astype(BF16)
    k_ref[0] = _dot(mb, wk_ref[...]).astype(BF16)
    v_ref[0] = _dot(mb, wv_ref[...]).astype(BF16)


def _mem_kv(mem, g, wk, wv):
    bsz, m_len, _ = mem.shape
    w_spec = pl.BlockSpec((D_MODEL, D_MODEL), lambda b: (0, 0))
    m_spec = pl.BlockSpec((1, m_len, D_MODEL), lambda b: (b, 0, 0))
    return pl.pallas_call(
        _mem_kv_kernel,
        grid=(bsz,),
        in_specs=[m_spec, pl.BlockSpec((1, D_MODEL), lambda b: (0, 0)), w_spec, w_spec],
        out_specs=[m_spec, m_spec],
        out_shape=[jax.ShapeDtypeStruct((bsz, m_len, D_MODEL), BF16)] * 2,
        compiler_params=_params(("parallel",)),
        name="mem_kv",
    )(mem, g, wk, wv)


def _route(logits):
    lane = lax.broadcasted_iota(jnp.int32, logits.shape, 1)
    big_i = jnp.int32(1 << 20)
    is_coarse = lane < N_GROUPS
    lc = jnp.where(is_coarse, logits, NEG_INF)
    mc = jnp.max(lc, axis=1, keepdims=True)
    grp = jnp.min(jnp.where(is_coarse & (lc == mc), lane, big_i), axis=1, keepdims=True)
    w_grp = 1.0 / jnp.sum(jnp.where(is_coarse, jnp.exp(lc - mc), 0.0), axis=1, keepdims=True)
    e_id = lane - N_GROUPS
    in_grp = (e_id >= grp * EXPERTS_PER_GROUP) & (e_id < (grp + 1) * EXPERTS_PER_GROUP)
    lf = jnp.where(in_grp, logits, NEG_INF)
    m1 = jnp.max(lf, axis=1, keepdims=True)
    i1 = jnp.min(jnp.where(in_grp & (lf == m1), lane, big_i), axis=1, keepdims=True)
    lf2 = jnp.where(lane == i1, NEG_INF, lf)
    m2 = jnp.max(lf2, axis=1, keepdims=True)
    i2 = jnp.min(jnp.where(in_grp & (lane != i1) & (lf2 == m2), lane, big_i), axis=1, keepdims=True)
    denom = jnp.sum(jnp.where(in_grp, jnp.exp(lf - m1), 0.0), axis=1, keepdims=True)
    p1 = 1.0 / denom
    p2 = jnp.exp(m2 - m1) / denom
    tot = p1 + p2
    w1 = w_grp * (p1 / tot)
    w2 = w_grp * (p2 / tot)
    weights = jnp.where(lane + N_GROUPS == i1, w1, jnp.where(lane + N_GROUPS == i2, w2, 0.0))
    return jnp.where(lane == grp + N_EXPERTS, 1.0, weights)


def _mix_out_kernel(x_ref, oa_ref, ob_ref, oc_ref, woa_ref, wob_ref, woc_ref, gx_ref, wq_ref, mk_ref, mv_ref,
                    wo_ref, gf_ref, wr_ref, br_ref, x2_ref, hb_ref, comb_ref, cnt_ref):
    n_part = 2
    rows = x_ref.shape[0] // n_part
    parts = [slice(k * rows, (k + 1) * rows) for k in range(n_part)]
    x1 = [x_ref[r, :] + _dot(oa_ref[r, :], woa_ref[...]) + _dot(ob_ref[r, :], wob_ref[...])
          + _dot(oc_ref[r, :], woc_ref[...]) for r in parts]
    hq = [_rms(v, gx_ref[...]).astype(BF16) for v in x1]
    q = [(_dot(v, wq_ref[...]) * (MEM_HEAD_DIM ** -0.5)).astype(BF16) for v in hq]
    heads = [[] for _ in parts]
    for h in range(MEM_HEADS):
        sl = slice(h * MEM_HEAD_DIM, (h + 1) * MEM_HEAD_DIM)
        for k in range(n_part):
            s = _dot_nt(q[k][:, sl], mk_ref[0, :, sl])
            p = jnp.exp(s - jnp.max(s, axis=1, keepdims=True))
            p = p / jnp.sum(p, axis=1, keepdims=True)
            heads[k].append(_dot(p.astype(BF16), mv_ref[0, :, sl]).astype(BF16))
    x2 = [x1[k] + _dot(jnp.concatenate(heads[k], axis=1), wo_ref[...]) for k in range(n_part)]
    hn = [_rms(v, gf_ref[...]) for v in x2]
    for k, r in enumerate(parts):
        x2_ref[r, :] = x2[k]
        hb_ref[r, :] = hn[k].astype(BF16)
        h0, h1, _ = _split3(hn[k])
        partial = _dot(h0, wr_ref[...]) + _dot(h1, wr_ref[...])
        logits = partial + pltpu.roll(partial, ROUTER_WIDTH - ROUTER_PIECE, axis=1) + br_ref[...]
        routed = _route(logits)
        comb_ref[r, :] = routed
        per_part = rows // MOE_TILE
        for sub in range(per_part):
            cnt_ref[k * per_part + sub] = jnp.sum(routed[sub * MOE_TILE:(sub + 1) * MOE_TILE], axis=0, keepdims=True)


def _mix_out(x, o_a, o_b, o_c, wo_a, wo_b, wo_c, g_x, wq, mem_k, mem_v, wo, g_f, w_r, b_r, t_len, tm=1024):
    n = x.shape[0]
    per_batch = t_len // tm
    m_len = mem_k.shape[1]
    tile = lambda width: pl.BlockSpec((tm, width), lambda i: (i, 0))
    full = lambda shape: pl.BlockSpec(shape, lambda i: tuple(0 for _ in shape))
    mem_spec = pl.BlockSpec((1, m_len, D_MODEL), lambda i: (i // per_batch, 0, 0))
    return pl.pallas_call(
        _mix_out_kernel,
        grid=(n // tm,),
        in_specs=[tile(D_MODEL), tile(FOX_WIDTH), tile(POOL_WIDTH), tile(NSA_WIDTH),
                  full(wo_a.shape), full(wo_b.shape), full(wo_c.shape), full(g_x.shape), full(wq.shape),
                  mem_spec, mem_spec, full(wo.shape), full(g_f.shape), full(w_r.shape), full(b_r.shape)],
        out_specs=[tile(D_MODEL), tile(D_MODEL), tile(ROUTER_WIDTH),
                   pl.BlockSpec((tm // MOE_TILE, 1, ROUTER_WIDTH), lambda i: (i, 0, 0))],
        out_shape=[jax.ShapeDtypeStruct((n, D_MODEL), F32),
                   jax.ShapeDtypeStruct((n, D_MODEL), BF16),
                   jax.ShapeDtypeStruct((n, ROUTER_WIDTH), F32),
                   jax.ShapeDtypeStruct((n // MOE_TILE, 1, ROUTER_WIDTH), F32)],
        compiler_params=_params(("parallel",)),
        name="mix_out",
    )(x, o_a, o_b, o_c, wo_a, wo_b, wo_c, g_x, wq, mem_k, mem_v, wo, g_f, w_r, b_r)


def _group_ranks(comb, tri):
    lane = lax.broadcasted_iota(jnp.int32, comb.shape, 1)
    member = jnp.where((lane >= N_EXPERTS) & (lane < N_EXPERTS + N_GROUPS), comb, 0.0)
    return member, _dot(tri, member.astype(BF16))


def _moe_sort_kernel(dst_ref, hb_ref, comb_ref, tri_ref, xs_in, cs_in, xs_ref, cs_ref, stage_x, stage_c, sem):
    del xs_in, cs_in
    t = pl.program_id(0)
    comb = comb_ref[...]
    member, rank = _group_ranks(comb, tri_ref[...])
    rank_t = jnp.transpose(rank)
    member_t = jnp.transpose(member)
    slot = lax.broadcasted_iota(jnp.int32, (MOE_TILE, MOE_TILE), 0).astype(F32)
    pieces = _split3(comb)
    buf = t % 2

    def copies(into, g, dst):
        return (pltpu.make_async_copy(stage_x.at[into, g], xs_ref.at[pl.ds(dst, MOE_TILE)], sem.at[0, into, g]),
                pltpu.make_async_copy(stage_c.at[into, g], cs_ref.at[pl.ds(dst, MOE_TILE)], sem.at[1, into, g]))

    for g in range(N_GROUPS):
        row = N_EXPERTS + g
        take = jnp.where((rank_t[row:row + 1, :] == slot) & (member_t[row:row + 1, :] > 0.5), 1.0, 0.0).astype(BF16)
        stage_x[buf, g] = _dot(take, hb_ref[...]).astype(BF16)
        stage_c[buf, g] = _dot(take, pieces[0]) + _dot(take, pieces[1]) + _dot(take, pieces[2])

    @pl.when(t > 0)
    def _():
        for g in range(N_GROUPS):
            for copy in copies(1 - buf, g, 0):
                copy.wait()

    for g in range(N_GROUPS):
        dst = pl.multiple_of(dst_ref[t * N_GROUPS + g], MOE_SEG_ALIGN)
        for copy in copies(buf, g, dst):
            copy.start()

    @pl.when(t == pl.num_programs(0) - 1)
    def _():
        for g in range(N_GROUPS):
            for copy in copies(buf, g, 0):
                copy.wait()


def _moe_sort(hb, comb, tri, seg_dst, rows):
    n = hb.shape[0]
    any_spec = pl.BlockSpec(memory_space=pl.ANY)
    xs0 = jnp.zeros((rows, D_MODEL), BF16)
    cs0 = jnp.zeros((rows, ROUTER_WIDTH), F32)
    return pl.pallas_call(
        _moe_sort_kernel,
        grid_spec=pltpu.PrefetchScalarGridSpec(
            num_scalar_prefetch=1,
            grid=(n // MOE_TILE,),
            in_specs=[pl.BlockSpec((MOE_TILE, D_MODEL), lambda t, dst: (t, 0)),
                      pl.BlockSpec((MOE_TILE, ROUTER_WIDTH), lambda t, dst: (t, 0)),
                      pl.BlockSpec((MOE_TILE, MOE_TILE), lambda t, dst: (0, 0)),
                      any_spec, any_spec],
            out_specs=[any_spec, any_spec],
            scratch_shapes=[pltpu.VMEM((2, N_GROUPS, MOE_TILE, D_MODEL), BF16),
                            pltpu.VMEM((2, N_GROUPS, MOE_TILE, ROUTER_WIDTH), F32),
                            pltpu.SemaphoreType.DMA((2, 2, N_GROUPS))]),
        out_shape=[jax.ShapeDtypeStruct((rows, D_MODEL), BF16), jax.ShapeDtypeStruct((rows, ROUTER_WIDTH), F32)],
        input_output_aliases={4: 0, 5: 1},
        compiler_params=_params(("arbitrary",)),
        name="moe_sort",
    )(seg_dst, hb, comb, tri, xs0, cs0)


def _moe_ffn_kernel(group_ref, valid_ref, xs_ref, cs_ref, wg_ref, wu_ref, wd_ref, ys_ref):
    i = pl.program_id(0)

    @pl.when(valid_ref[i] == 0)
    def _():
        ys_ref[...] = jnp.zeros(ys_ref.shape, F32)

    @pl.when(valid_ref[i] != 0)
    def _():
        x = xs_ref[...]
        cs = cs_ref[...]
        lane = lax.broadcasted_iota(jnp.int32, cs.shape, 1)
        first_expert = group_ref[i] * EXPERTS_PER_GROUP
        total = None
        for e in range(EXPERTS_PER_GROUP):
            he = (jax.nn.silu(_dot(x, wg_ref[0, e])) * _dot(x, wu_ref[0, e])).astype(BF16)
            c_e = jnp.sum(jnp.where(lane == first_expert + e, cs, 0.0), axis=1, keepdims=True)
            y = c_e * _dot(he, wd_ref[0, e])
            total = y if total is None else total + y
        ys_ref[...] = total


def _moe_ffn(xs, cs, wg, wu, wd, layer, tile_group, tile_valid):
    rows = xs.shape[0]
    tile = lambda width: pl.BlockSpec((MOE_FFN_TILE, width), lambda i, grp, ok: (i, 0))
    experts = lambda shape: pl.BlockSpec((1, EXPERTS_PER_GROUP) + shape, lambda i, grp, ok: (layer, grp[i], 0, 0))
    return pl.pallas_call(
        _moe_ffn_kernel,
        grid_spec=pltpu.PrefetchScalarGridSpec(
            num_scalar_prefetch=2,
            grid=(rows // MOE_FFN_TILE,),
            in_specs=[tile(D_MODEL), tile(ROUTER_WIDTH),
                      experts((D_MODEL, EXPERT_FF)), experts((D_MODEL, EXPERT_FF)), experts((EXPERT_FF, D_MODEL))],
            out_specs=tile(D_MODEL)),
        out_shape=jax.ShapeDtypeStruct((rows, D_MODEL), F32),
        compiler_params=_params(("arbitrary",)),
        name="moe_ffn",
    )(tile_group, tile_valid, xs, cs, wg, wu, wd)


def _moe_unsort_kernel(dst_ref, ys_ref, comb_ref, tri_ref, x_ref, gfin_ref, o_ref, buf, sem, *, final_norm):
    t = pl.program_id(0)
    slot = t % 2

    def copy(tile, g, into):
        dst = pl.multiple_of(dst_ref[tile * N_GROUPS + g], MOE_SEG_ALIGN)
        return pltpu.make_async_copy(ys_ref.at[pl.ds(dst, MOE_TILE)], buf.at[into, g], sem.at[into, g])

    @pl.when(t == 0)
    def _():
        for g in range(N_GROUPS):
            copy(0, g, 0).start()

    @pl.when(t + 1 < pl.num_programs(0))
    def _():
        for g in range(N_GROUPS):
            copy(t + 1, g, 1 - slot).start()

    member, rank = _group_ranks(comb_ref[...], tri_ref[...])
    lane_slot = lax.broadcasted_iota(jnp.int32, (MOE_TILE, MOE_TILE), 1).astype(F32)
    y = x_ref[...]
    for g in range(N_GROUPS):
        copy(t, g, slot).wait()
        col = N_EXPERTS + g
        put = jnp.where((rank[:, col:col + 1] == lane_slot) & (member[:, col:col + 1] > 0.5), 1.0, 0.0).astype(BF16)
        seg = buf[slot, g]
        hi = seg.astype(BF16)
        lo = (seg - hi.astype(F32)).astype(BF16)
        y = y + (_dot(put, hi) + _dot(put, lo))
    o_ref[...] = _rms(y, gfin_ref[...]) if final_norm else y


def _moe_unsort(ys, comb, tri, x2, g_fin, seg_dst, final_norm):
    n = x2.shape[0]
    return pl.pallas_call(
        functools.partial(_moe_unsort_kernel, final_norm=final_norm),
        grid_spec=pltpu.PrefetchScalarGridSpec(
            num_scalar_prefetch=1,
            grid=(n // MOE_TILE,),
            in_specs=[pl.BlockSpec(memory_space=pl.ANY),
                      pl.BlockSpec((MOE_TILE, ROUTER_WIDTH), lambda t, dst: (t, 0)),
                      pl.BlockSpec((MOE_TILE, MOE_TILE), lambda t, dst: (0, 0)),
                      pl.BlockSpec((MOE_TILE, D_MODEL), lambda t, dst: (t, 0)),
                      pl.BlockSpec((1, D_MODEL), lambda t, dst: (0, 0))],
            out_specs=pl.BlockSpec((MOE_TILE, D_MODEL), lambda t, dst: (t, 0)),
            scratch_shapes=[pltpu.VMEM((2, N_GROUPS, MOE_TILE, D_MODEL), F32),
                            pltpu.SemaphoreType.DMA((2, N_GROUPS))]),
        out_shape=jax.ShapeDtypeStruct((n, D_MODEL), F32),
        compiler_params=_params(("arbitrary",)),
        name="moe_unsort",
    )(seg_dst, ys, comb, tri, x2, g_fin)


def _moe_plan(cnt, n_tok):
    n_tiles = cnt.shape[0]
    rows = n_tok + n_tiles * N_GROUPS * MOE_SEG_ALIGN + N_GROUPS * (MOE_FFN_TILE + MOE_GAP)
    rows = -(-rows // MOE_FFN_TILE) * MOE_FFN_TILE
    padded = (cnt + MOE_SEG_ALIGN - 1) // MOE_SEG_ALIGN * MOE_SEG_ALIGN
    within = jnp.cumsum(padded, axis=0) - padded
    region = (jnp.sum(padded, axis=0) + MOE_FFN_TILE - 1) // MOE_FFN_TILE * MOE_FFN_TILE
    start = jnp.cumsum(region + MOE_GAP) - (region + MOE_GAP)
    seg_dst = (start[None, :] + within).reshape(-1).astype(jnp.int32)
    tile_row = jnp.arange(rows // MOE_FFN_TILE) * MOE_FFN_TILE
    begun = tile_row[:, None] >= start[None, :]
    tile_valid = jnp.any(begun & (tile_row[:, None] < (start + region)[None, :]), axis=1).astype(jnp.int32)
    tile_group = jnp.clip(jnp.sum(begun, axis=1) - 1, 0, N_GROUPS - 1).astype(jnp.int32)
    return seg_dst, tile_group, tile_valid, rows


def _moe(x2, hb, comb, cnt, wg, wu, wd, layer, g_fin, final_norm):
    n = x2.shape[0]
    counts = jnp.round(cnt[:, 0, N_EXPERTS:N_EXPERTS + N_GROUPS]).astype(jnp.int32)
    seg_dst, tile_group, tile_valid, rows = _moe_plan(counts, n)
    tri = (jnp.arange(MOE_TILE)[:, None] > jnp.arange(MOE_TILE)[None, :]).astype(BF16)
    xs, cs = _moe_sort(hb, comb, tri, seg_dst, rows)
    ys = _moe_ffn(xs, cs, wg, wu, wd, layer, tile_group, tile_valid)
    return _moe_unsort(ys, comb, tri, x2, g_fin, seg_dst, final_norm)


def _pair_order(a, axis):
    shape = a.shape
    a = a.reshape(shape[:axis] + (NSA_HEADS, HEAD_DIM) + shape[axis + 1:])
    a = jnp.take(a, jnp.array(NSA_PAIR_ORDER), axis=axis)
    return a.reshape(shape)


def _split_in_proj_weight(w):
    edges = [0]
    for width in (FOX_WIDTH, FOX_WIDTH, FOX_WIDTH, FOX_HEADS, POOL_WIDTH, NSA_WIDTH) + (KV_WIDTH,) * 6 + (NSA_HEADS * N_BRANCH,):
        edges.append(edges[-1] + width)
    part = lambda k: w[:, edges[k]:edges[k + 1]]
    qa, ka, va, fa, ub, qc = (part(k) for k in range(6))
    kvs = [part(k) for k in range(6, 12)]
    gl = part(12)
    w_big = jnp.concatenate([qa, ka, va, _pair_order(qc, 1)] + kvs[2:], axis=1).astype(BF16)
    pad = jnp.zeros((w.shape[0], LANES - FOX_HEADS - NSA_HEADS * N_BRANCH), w.dtype)
    w_small = jnp.concatenate([ub, fa, gl, pad] + kvs[:2], axis=1).astype(BF16)
    return w_big, w_small


def _block_diag(w):
    groups, cin, cout = w.shape
    out = jnp.zeros((groups * cin, groups * cout), w.dtype)
    for gi in range(groups):
        out = out.at[gi * cin:(gi + 1) * cin, gi * cout:(gi + 1) * cout].set(w[gi])
    return out


def _layer(x, mem, p, experts, layer, bsz, t_len, final_g, final_norm):
    row = lambda v: v.reshape(1, -1)
    w_big, w_small = _split_in_proj_weight(p["w_in"])
    big, small = _in_proj(x, row(p["mix_norm_g"]), w_big, w_small)

    b_row = jnp.pad(p["fox_forget_b"], (0, LANES - FOX_HEADS)).reshape(1, LANES)
    ck = _forget_cumsum(small, b_row, _forget_placement(), bsz, t_len)
    o_a = _fox(big, ck, bsz, t_len)

    o_b = _pool(small, _block_diag(p["pool_w"]).astype(BF16), row(p["pool_scale"]), bsz, t_len)

    pe_k, w1_k, w2_k = _compress_weights(p["cmp_pe_k"], p["cmp_w1_k"], p["cmp_w2_k"])
    pe_v, w1_v, w2_v = _compress_weights(p["cmp_pe_v"], p["cmp_w1_v"], p["cmp_w2_v"])
    kc, vc = _compress(small, pe_k, pe_v, w1_k, w2_k, w1_v, w2_v, bsz, t_len)
    kpos, cpos, ovt, slope_tab = _nsa_tables(t_len)
    o_c = _nsa(big, kpos, kc, vc, cpos, ovt, slope_tab, small, row(p["nsa_gate_b"]), bsz, t_len)

    mem_k, mem_v = _mem_kv(mem, row(p["mem_norm_g"]), p["xattn_wk"].astype(BF16), p["xattn_wv"].astype(BF16))
    w_out = p["w_out"]
    wo_a = w_out[:FOX_WIDTH].astype(BF16)
    wo_b = w_out[FOX_WIDTH:FOX_WIDTH + POOL_WIDTH].astype(BF16)
    wo_c = _pair_order(w_out[FOX_WIDTH + POOL_WIDTH:], 0).astype(BF16)
    w_r = jnp.concatenate([p["router_coarse_w"], p["router_fine_w"],
                           jnp.zeros((D_MODEL, ROUTER_PIECE - N_GROUPS - N_EXPERTS), F32)], axis=1)
    w_r0, w_r1, _ = _split3(w_r)
    w_r = jnp.concatenate([w_r0, w_r1, jnp.zeros((D_MODEL, ROUTER_WIDTH - 2 * ROUTER_PIECE), BF16)], axis=1)
    b_r = jnp.concatenate([p["router_coarse_b"], p["router_fine_b"],
                           jnp.zeros((ROUTER_WIDTH - N_GROUPS - N_EXPERTS,), F32)]).reshape(1, -1)
    x2, hb, comb, cnt = _mix_out(x, o_a, o_b, o_c, wo_a, wo_b, wo_c, row(p["xattn_norm_g"]),
                                 p["xattn_wq"].astype(BF16), mem_k, mem_v, p["xattn_wo"].astype(BF16),
                                 row(p["ffn_norm_g"]), w_r, b_r, t_len)

    return _moe(x2, hb, comb, cnt, *experts, layer, row(final_g), final_norm)


def kernel(x, mem, mix_norm_g, w_in, fox_forget_b, pool_w, pool_scale, cmp_pe_k, cmp_pe_v, cmp_w1_k, cmp_w2_k, cmp_w1_v, cmp_w2_v, nsa_gate_b, w_out, xattn_norm_g, mem_norm_g, xattn_wq, xattn_wk, xattn_wv, xattn_wo, ffn_norm_g, router_coarse_w, router_coarse_b, router_fine_w, router_fine_b, exp_w_gate, exp_w_up, exp_w_down, final_norm_g):
    stacked = dict(mix_norm_g=mix_norm_g, w_in=w_in, fox_forget_b=fox_forget_b, pool_w=pool_w, pool_scale=pool_scale,
                   cmp_pe_k=cmp_pe_k, cmp_pe_v=cmp_pe_v, cmp_w1_k=cmp_w1_k, cmp_w2_k=cmp_w2_k, cmp_w1_v=cmp_w1_v,
                   cmp_w2_v=cmp_w2_v, nsa_gate_b=nsa_gate_b, w_out=w_out, xattn_norm_g=xattn_norm_g,
                   mem_norm_g=mem_norm_g, xattn_wq=xattn_wq, xattn_wk=xattn_wk, xattn_wv=xattn_wv, xattn_wo=xattn_wo,
                   ffn_norm_g=ffn_norm_g, router_coarse_w=router_coarse_w, router_coarse_b=router_coarse_b,
                   router_fine_w=router_fine_w, router_fine_b=router_fine_b)
    experts = (exp_w_gate.astype(BF16), exp_w_up.astype(BF16), exp_w_down.astype(BF16))
    bsz, t_len, d = x.shape
    depth = w_in.shape[0]
    h = x.reshape(bsz * t_len, d)
    for layer in range(depth):
        p = {name: value[layer] for name, value in stacked.items()}
        h = _layer(h, mem, p, experts, layer, bsz, t_len, final_norm_g, final_norm=(layer == depth - 1))
    return h.reshape(bsz, t_len, d)
```

```python
import functools

import jax
import jax.numpy as jnp
from jax import lax
from jax.experimental import pallas as pl
from jax.experimental.pallas import tpu as pltpu

F32 = jnp.float32
BF16 = jnp.bfloat16

D_MODEL = 1024
HEAD_DIM = 64
FOX_HEADS = 6
FOX_WIDTH = FOX_HEADS * HEAD_DIM
POOL_WIDTH = 256
POOL_GROUP_DIM = 64
POOL_WINDOWS = (2, 4, 8, 16)
POOL_HALO = 16
NSA_HEADS = 6
NSA_KV_HEADS = 2
NSA_REP = NSA_HEADS // NSA_KV_HEADS
NSA_WIDTH = NSA_HEADS * HEAD_DIM
KV_WIDTH = NSA_KV_HEADS * HEAD_DIM
N_BRANCH = 3
CMP_LEN = 32
CMP_STRIDE = 16
CMP_HIDDEN = 256
SEL_BLOCK = 64
SEL_TOPN = 16
WINDOW = 512
FORCE_SCORE = 1.0e4
MEM_HEADS = 4
MEM_HEAD_DIM = D_MODEL // MEM_HEADS
N_GROUPS = 4
EXPERTS_PER_GROUP = 4
N_EXPERTS = N_GROUPS * EXPERTS_PER_GROUP
EXPERT_FF = 512
EPS = 1e-6
NEG_INF = -1e30

LANES = 128
SUBLANES = 8
BIG_WIDTH = 3 * FOX_WIDTH + NSA_WIDTH + 4 * KV_WIDTH
COL_QC = 3 * FOX_WIDTH
COL_KV = COL_QC + NSA_WIDTH
SMALL_WIDTH = POOL_WIDTH + LANES + 2 * KV_WIDTH
COL_FA = POOL_WIDTH
COL_GL = POOL_WIDTH + FOX_HEADS
COL_CMP = POOL_WIDTH + LANES
ROUTER_WIDTH = LANES
ROUTER_PIECE = 32
MOE_TILE = 256
MOE_SEG_ALIGN = 16
MOE_FFN_TILE = 512
MOE_GAP = 512
NSA_PAIR_ORDER = tuple(g * NSA_REP + r for r in range(NSA_REP) for g in range(NSA_KV_HEADS))
N_PIECES = 3
COL_POS = HEAD_DIM

VMEM_LIMIT = 56 * 1024 * 1024

_NT = (((1,), (1,)), ((), ()))


def _params(semantics):
    return pltpu.CompilerParams(dimension_semantics=semantics, vmem_limit_bytes=VMEM_LIMIT)


def _rms(x, g):
    return x * lax.rsqrt(jnp.mean(x * x, axis=-1, keepdims=True) + EPS) * g


def _dot(a, b):
    return jnp.dot(a, b, preferred_element_type=F32)


def _dot_nt(a, b):
    return lax.dot_general(a, b, _NT, preferred_element_type=F32)


def _split3(a):
    a0 = a.astype(BF16)
    r = a - a0.astype(F32)
    a1 = r.astype(BF16)
    a2 = (r - a1.astype(F32)).astype(BF16)
    return a0, a1, a2


def _in_proj_kernel(x_ref, g_ref, wb_ref, ws_ref, big_ref, small_ref):
    yb = _rms(x_ref[...], g_ref[...]).astype(BF16)
    big_ref[...] = _dot(yb, wb_ref[...]).astype(BF16)
    small_ref[...] = _dot(yb, ws_ref[...])


def _in_proj(x, g, w_big, w_small, tm=1024):
    n = x.shape[0]
    return pl.pallas_call(
        _in_proj_kernel,
        grid=(n // tm,),
        in_specs=[pl.BlockSpec((tm, D_MODEL), lambda i: (i, 0)),
                  pl.BlockSpec((1, D_MODEL), lambda i: (0, 0)),
                  pl.BlockSpec((D_MODEL, BIG_WIDTH), lambda i: (0, 0)),
                  pl.BlockSpec((D_MODEL, SMALL_WIDTH), lambda i: (0, 0))],
        out_specs=[pl.BlockSpec((tm, BIG_WIDTH), lambda i: (i, 0)),
                   pl.BlockSpec((tm, SMALL_WIDTH), lambda i: (i, 0))],
        out_shape=[jax.ShapeDtypeStruct((n, BIG_WIDTH), BF16),
                   jax.ShapeDtypeStruct((n, SMALL_WIDTH), F32)],
        compiler_params=_params(("parallel",)),
        name="in_proj",
    )(x, g, w_big, w_small)


def _forget_kernel(f_ref, b_ref, tri_ref, place_ref, ck_ref):
    log_f = jax.nn.log_sigmoid(f_ref[...] + b_ref[...])
    blk = tri_ref.shape[0]
    tri = tri_ref[...]
    carry = jnp.zeros((1, LANES), F32)
    sums = []
    for k in range(log_f.shape[0] // blk):
        x0, x1, x2 = _split3(log_f[k * blk:(k + 1) * blk])
        sums.append(_dot(tri, x0) + _dot(tri, x1) + _dot(tri, x2) + carry)
        carry = sums[-1][blk - 1:blk, :]
    c = jnp.concatenate(sums, axis=0)
    n0, n1, n2 = _split3(-c)
    ck_ref[...] = (_dot(n0, place_ref[0]) + _dot(n1, place_ref[1]) + _dot(n2, place_ref[2])).astype(BF16)


def _forget_cumsum(small, b_row, place, bsz, t_len, blk=256):
    tri = (jnp.arange(blk)[:, None] >= jnp.arange(blk)[None, :]).astype(BF16)
    return pl.pallas_call(
        _forget_kernel,
        grid=(bsz,),
        in_specs=[pl.BlockSpec((t_len, LANES), lambda b: (b, COL_FA // LANES)),
                  pl.BlockSpec((1, LANES), lambda b: (0, 0)),
                  pl.BlockSpec(tri.shape, lambda b: (0, 0)),
                  pl.BlockSpec(place.shape, lambda b: (0, 0, 0))],
        out_specs=pl.BlockSpec((t_len, FOX_WIDTH), lambda b: (b, 0)),
        out_shape=jax.ShapeDtypeStruct((bsz * t_len, FOX_WIDTH), BF16),
        compiler_params=_params(("parallel",)),
        name="forget",
    )(small, b_row, tri, place)


def _forget_placement():
    src = jnp.arange(LANES)[None, :, None]
    dst = jnp.arange(FOX_WIDTH)[None, None, :]
    piece = jnp.arange(N_PIECES)[:, None, None]
    target = LANES * (src // 2) + N_PIECES * (src % 2) + piece
    return ((src < FOX_HEADS) & (dst == target)).astype(BF16)


def _flash_sweep(n_tiles, n_stream, score, values, last_mask, s_scr, p_scr, m_scr, acc_scr):
    def run_stage(j, a, first=False, last=False):
        b = 1 - a
        for st in range(n_stream):
            s = score(0, st) if first else s_scr[a, st]
            if last:
                s = last_mask(s, j)
            m_old = jnp.full(m_scr.shape[1:], NEG_INF, F32) if first else m_scr[st]
            m_new = jnp.maximum(m_old, jnp.max(s, axis=1, keepdims=True))
            prob = jnp.exp(s - m_new).astype(BF16)
            if not last:
                p_scr[a, st] = prob
                s_scr[b, st] = score(j + 1, st)
            acc = jnp.zeros(acc_scr.shape[1:], F32) if first else acc_scr[st]
            if not first:
                acc = (acc + _dot(p_scr[b, st], values(j - 1, st))) * jnp.exp(m_old - m_new)
            if last:
                acc = acc + _dot(prob, values(j, st))
            acc_scr[st] = acc
            m_scr[st] = m_new

    n_main = n_tiles - 1

    @pl.when(n_main == 0)
    def _():
        run_stage(0, 0, first=True, last=True)

    @pl.when(n_main > 0)
    def _():
        run_stage(0, 0, first=True)

    def two_stages(jj, carry):
        run_stage(2 * jj + 1, 1)
        run_stage(2 * jj + 2, 0)
        return carry

    lax.fori_loop(0, jnp.maximum(n_main - 1, 0) // 2, two_stages, 0)

    @pl.when(jnp.logical_and(n_main >= 2, n_main % 2 == 0))
    def _():
        run_stage(n_main - 1, 1)

    for a in range(2):
        @pl.when(jnp.logical_and(n_main >= 1, n_main % 2 == a))
        def _():
            run_stage(n_main, a, last=True)


def _fox_kernel(q_ref, k_ref, v_ref, ck_ref, o_ref, q_scr, s_scr, p_scr, m_scr, acc_scr, *, tq, tk):
    i = pl.program_id(1)
    t0 = i * tq
    n_pair = FOX_HEADS // 2
    lane = lax.broadcasted_iota(jnp.int32, (tq, LANES), 1)
    first = lane < HEAD_DIM
    scale = jnp.asarray(HEAD_DIM ** -0.5, BF16)
    pick_a = jnp.where(lane < N_PIECES, 1.0, 0.0).astype(BF16)
    pick_b = jnp.where((lane >= N_PIECES) & (lane < 2 * N_PIECES), 1.0, 0.0).astype(BF16)
    for p in range(n_pair):
        qp = q_ref[:, p * LANES:(p + 1) * LANES] * scale
        zero = jnp.zeros_like(qp)
        qa = jnp.concatenate([jnp.where(first, qp, zero), pick_a], axis=1)
        qb = jnp.concatenate([jnp.where(first, zero, qp), pick_b], axis=1)
        q_scr[p] = jnp.concatenate([qa, qb], axis=0)
    ones_v = jnp.ones((tk, LANES), BF16)

    def score(j, p):
        ks = pl.multiple_of(j * tk, tk)
        psl = slice(p * LANES, (p + 1) * LANES)
        kx = jnp.concatenate([k_ref[pl.ds(ks, tk), psl], ck_ref[pl.ds(ks, tk), psl]], axis=1)
        return _dot_nt(q_scr[p], kx)

    def values(j, p):
        ks = pl.multiple_of(j * tk, tk)
        return jnp.concatenate([v_ref[pl.ds(ks, tk), p * LANES:(p + 1) * LANES], ones_v], axis=1)

    def causal_mask(s, j):
        t_row = t0 + lax.broadcasted_iota(jnp.int32, (tq, tk), 0)
        pos = j * tk + lax.broadcasted_iota(jnp.int32, (tq, tk), 1)
        return jnp.where((pos <= t_row)[None], s.reshape(2, tq, tk), NEG_INF).reshape(2 * tq, tk)

    _flash_sweep(t0 // tk + 1, n_pair, score, values, causal_mask, s_scr, p_scr, m_scr, acc_scr)
    for p in range(n_pair):
        acc = acc_scr[p]
        oa = acc[:tq, :LANES] / acc[:tq, LANES:]
        ob = acc[tq:, :LANES] / acc[tq:, LANES:]
        o_ref[:, p * LANES:(p + 1) * LANES] = jnp.where(first, oa, ob).astype(BF16)


def _fox(big, ck, bsz, t_len, tq=256, tk=512):
    nq = t_len // tq
    n_pair = FOX_HEADS // 2
    seq = lambda c: pl.BlockSpec((t_len, FOX_WIDTH), lambda b, i: (b, c))
    return pl.pallas_call(
        functools.partial(_fox_kernel, tq=tq, tk=tk),
        grid=(bsz, nq),
        in_specs=[pl.BlockSpec((tq, FOX_WIDTH), lambda b, i: (b * nq + i, 0)), seq(1), seq(2), seq(0)],
        out_specs=pl.BlockSpec((tq, FOX_WIDTH), lambda b, i: (b * nq + i, 0)),
        out_shape=jax.ShapeDtypeStruct((bsz * t_len, FOX_WIDTH), BF16),
        scratch_shapes=[pltpu.VMEM((n_pair, 2 * tq, 2 * LANES), BF16),
                        pltpu.VMEM((2, n_pair, 2 * tq, tk), F32),
                        pltpu.VMEM((2, n_pair, 2 * tq, tk), BF16),
                        pltpu.VMEM((n_pair, 2 * tq, 1), F32),
                        pltpu.VMEM((n_pair, 2 * tq, 2 * LANES), F32)],
        compiler_params=_params(("parallel", "arbitrary")),
        name="fox",
    )(big, big, big, ck)


def _pool_kernel(u_ref, halo_ref, w_ref, scale_ref, o_ref, x_scr, *, tt):
    i = pl.program_id(1)
    u = u_ref[...]
    x_scr[POOL_HALO:, :] = u
    x_scr[:POOL_HALO, :] = jnp.where(i > 0, halo_ref[...], 0.0)
    t_pos = i * tt + lax.broadcasted_iota(jnp.int32, (tt, 1), 0)
    lane = lax.broadcasted_iota(jnp.int32, (tt, POOL_WIDTH), 1)
    run = u
    pooled = jnp.zeros_like(u)
    for back in range(1, max(POOL_WINDOWS)):
        run = run + x_scr[POOL_HALO - back:POOL_HALO - back + tt, :]
        w = back + 1
        if w in POOL_WINDOWS:
            gi = POOL_WINDOWS.index(w)
            count = jnp.minimum(t_pos + 1, w).astype(F32)
            in_group = (lane >= gi * POOL_GROUP_DIM) & (lane < (gi + 1) * POOL_GROUP_DIM)
            pooled = jnp.where(in_group, run / count - u, pooled)
    mixed = _dot(pooled.astype(BF16), w_ref[...])
    o_ref[...] = (mixed * scale_ref[...]).astype(BF16)


def _pool(small, w_bd, scale, bsz, t_len, tt=2048):
    nt = t_len // tt
    per = tt // POOL_HALO
    return pl.pallas_call(
        functools.partial(_pool_kernel, tt=tt),
        grid=(bsz, nt),
        in_specs=[pl.BlockSpec((tt, POOL_WIDTH), lambda b, i: (b * nt + i, 0)),
                  pl.BlockSpec((POOL_HALO, POOL_WIDTH), lambda b, i: (jnp.maximum((b * nt + i) * per - 1, 0), 0)),
                  pl.BlockSpec((POOL_WIDTH, POOL_WIDTH), lambda b, i: (0, 0)),
                  pl.BlockSpec((1, POOL_WIDTH), lambda b, i: (0, 0))],
        out_specs=pl.BlockSpec((tt, POOL_WIDTH), lambda b, i: (b * nt + i, 0)),
        out_shape=jax.ShapeDtypeStruct((bsz * t_len, POOL_WIDTH), BF16),
        scratch_shapes=[pltpu.VMEM((tt + POOL_HALO, POOL_WIDTH), F32)],
        compiler_params=_params(("parallel", "arbitrary")),
        name="pool",
    )(small, small, w_bd, scale)


def _compress_kernel(xk_ref, xv_ref, pek_ref, pev_ref, w1k_ref, w2k_ref, w1v_ref, w2v_ref, kc_ref, vc_ref):
    def one(x_ref, pe_ref, w1_ref, w2_ref, out_ref):
        n_chunk = x_ref.shape[0] // CMP_STRIDE
        first = jnp.zeros((n_chunk, w1_ref.shape[2]), F32)
        second = jnp.zeros((n_chunk, w1_ref.shape[2]), F32)
        for r in range(CMP_STRIDE):
            x_r = x_ref[pl.ds(r, n_chunk, stride=CMP_STRIDE), :]
            first = first + _dot((x_r + pe_ref[r:r + 1, :]).astype(BF16), w1_ref[r])
            second = second + _dot((x_r + pe_ref[CMP_STRIDE + r:CMP_STRIDE + r + 1, :]).astype(BF16),
                                   w1_ref[CMP_STRIDE + r])
        hidden = first + pltpu.roll(second, n_chunk - 1, axis=0)
        out_ref[0] = _dot(jax.nn.gelu(hidden).astype(BF16), w2_ref[...])

    one(xk_ref, pek_ref, w1k_ref, w2k_ref, kc_ref)
    one(xv_ref, pev_ref, w1v_ref, w2v_ref, vc_ref)


def _compress(small, pe_k, pe_v, w1_k, w2_k, w1_v, w2_v, bsz, t_len):
    n_chunk = t_len // CMP_STRIDE
    full = lambda a: pl.BlockSpec(a.shape, lambda b: tuple(0 for _ in a.shape))
    x_spec = lambda col: pl.BlockSpec((t_len, KV_WIDTH), lambda b: (b, col // KV_WIDTH))
    o_spec = pl.BlockSpec((1, n_chunk, KV_WIDTH), lambda b: (b, 0, 0))
    return pl.pallas_call(
        _compress_kernel,
        grid=(bsz,),
        in_specs=[x_spec(COL_CMP), x_spec(COL_CMP + KV_WIDTH), full(pe_k), full(pe_v),
                  full(w1_k), full(w2_k), full(w1_v), full(w2_v)],
        out_specs=[o_spec, o_spec],
        out_shape=[jax.ShapeDtypeStruct((bsz, n_chunk, KV_WIDTH), F32)] * 2,
        compiler_params=_params(("parallel",)),
        name="compress",
    )(small, small, pe_k, pe_v, w1_k, w2_k, w1_v, w2_v)


def _compress_weights(pe, w1, w2):
    def block_diag(w):
        zero = jnp.zeros_like(w)
        rows = [jnp.concatenate([w if h == g else zero for h in range(NSA_KV_HEADS)], axis=-1)
                for g in range(NSA_KV_HEADS)]
        return jnp.concatenate(rows, axis=-2)

    w1_bd = block_diag(w1.reshape(CMP_LEN, HEAD_DIM, CMP_HIDDEN).astype(BF16))
    return jnp.tile(pe, (1, NSA_KV_HEADS)), w1_bd, block_diag(w2.astype(BF16))


def _nsa_kernel(q_ref, ksl_ref, vsl_ref, kwn_ref, vwn_ref, kpos_ref, kc_ref, vc_ref, cpos_ref, ovt_ref, slope_ref,
                sm_ref, gb_ref, o_ref, qs_scr, s_scr, p_scr, m_scr, acc_scr, *, tq, tks):
    i = pl.program_id(1)
    t0 = i * tq
    n_head = NSA_HEADS
    rows = n_head * tq
    n_cmp = kc_ref.shape[1]
    n_blk = ovt_ref.shape[0]
    win_keys = WINDOW + tq
    lane = lax.broadcasted_iota(jnp.int32, (tq, LANES), 1)
    first = lane < HEAD_DIM
    scale = jnp.asarray(HEAD_DIM ** -0.5, BF16)

    pairs = [q_ref[:, r * LANES:(r + 1) * LANES] * scale for r in range(NSA_REP)]
    zero = jnp.zeros_like(pairs[0])
    q_heads, q_slopes = [], []
    for g in range(NSA_KV_HEADS):
        for r in range(NSA_REP):
            q_heads.append(jnp.where(first, pairs[r], zero) if g == 0 else jnp.where(first, zero, pairs[r]))
            q_slopes.append(jnp.broadcast_to(slope_ref[g * NSA_REP + r:g * NSA_REP + r + 1, :], (tq, LANES)))
    q_main = jnp.concatenate(q_heads, axis=0)
    q_plain = jnp.concatenate([q_main, jnp.concatenate(q_slopes, axis=0).astype(BF16)], axis=1)

    n_part = s_scr.shape[1]
    part_rows = rows // n_part

    def per_head(mask, s, fill):
        return jnp.where(mask[None], s.reshape(s.shape[0] // tq, tq, s.shape[-1]), fill).reshape(s.shape)

    t_col = t0 + lax.broadcasted_iota(jnp.int32, (tq, n_cmp), 0)
    c_end = lax.broadcasted_iota(jnp.int32, (tq, n_cmp), 1) * CMP_STRIDE + (CMP_LEN - 1)
    valid_c = c_end <= t_col
    kc_x = jnp.concatenate([kc_ref[0].astype(BF16), cpos_ref[...]], axis=1)
    vc_b = vc_ref[0].astype(BF16)
    p_parts, o_parts = [], []
    for part in range(n_part):
        rsl = slice(part * part_rows, (part + 1) * part_rows)
        s = per_head(valid_c, _dot_nt(q_plain[rsl], kc_x), NEG_INF)
        p = per_head(valid_c, jnp.exp(s - jnp.max(s, axis=1, keepdims=True)), 0.0)
        l_c = jnp.sum(p, axis=1, keepdims=True)
        p = p * jnp.where(l_c > 0.0, 1.0 / l_c, 0.0)
        p_parts.append(p)
        o_parts.append(_dot(p.astype(BF16), vc_b))
    p_c = jnp.concatenate(p_parts, axis=0)
    o_cmp = jnp.concatenate(o_parts, axis=0)

    blk_t = lax.broadcasted_iota(jnp.int32, (n_blk, tq), 0)
    cur_t = (t0 + lax.broadcasted_iota(jnp.int32, (n_blk, tq), 1)) // SEL_BLOCK
    forced_t = (blk_t == 0) | (blk_t == cur_t) | (blk_t == cur_t - 1)
    sub = lax.broadcasted_iota(jnp.int32, (SUBLANES, tq), 0)
    n_top = float(min(SEL_TOPN, n_blk))
    for g in range(NSA_KV_HEADS):
        p_sum = p_c[g * NSA_REP * tq:(g * NSA_REP + 1) * tq]
        for r in range(1, NSA_REP):
            p_sum = p_sum + p_c[(g * NSA_REP + r) * tq:(g * NSA_REP + r + 1) * tq]
        p0, p1, p2 = _split3(p_sum)
        ovt = ovt_ref[...]
        imp_t = _dot_nt(ovt, p0) + _dot_nt(ovt, p1) + _dot_nt(ovt, p2)
        score = jnp.where(forced_t, FORCE_SCORE, jnp.where(blk_t <= cur_t, imp_t, -1.0))
        chunks = [score[c * SUBLANES:(c + 1) * SUBLANES] for c in range(n_blk // SUBLANES)]
        ranks = [jnp.zeros((SUBLANES, tq), F32) for _ in chunks]
        for jp in range(n_blk):
            other = jnp.broadcast_to(score[jp:jp + 1], (SUBLANES, tq))
            for c, chunk in enumerate(chunks):
                ge = jnp.where(other >= chunk, 1.0, 0.0)
                gt = jnp.where(other > chunk, 1.0, 0.0)
                if jp < c * SUBLANES:
                    ranks[c] = ranks[c] + ge
                elif jp >= (c + 1) * SUBLANES:
                    ranks[c] = ranks[c] + gt
                else:
                    ranks[c] = ranks[c] + jnp.where(sub > jp - c * SUBLANES, ge, gt)
        rank = jnp.concatenate(ranks, axis=0)
        neg_t = jnp.where(rank < n_top, 0.0, NEG_INF)
        neg = jnp.transpose(jnp.concatenate([neg_t, jnp.zeros((LANES - n_blk, tq), F32)], axis=0))
        for r in range(NSA_REP):
            k = g * NSA_REP + r
            q_const = (neg + q_slopes[k]).astype(BF16)
            qs_scr[k * tq:(k + 1) * tq, :] = jnp.concatenate([q_heads[k], q_const], axis=1)

    ones_s = jnp.ones((tks, LANES), BF16)

    def slc_score(j, part):
        ks = pl.multiple_of(j * tks, tks)
        kx = jnp.concatenate([ksl_ref[pl.ds(ks, tks), :], kpos_ref[pl.ds(ks, tks), :]], axis=1)
        return _dot_nt(qs_scr[part * part_rows:(part + 1) * part_rows, :], kx)

    def slc_values(j, part):
        ks = pl.multiple_of(j * tks, tks)
        return jnp.concatenate([vsl_ref[pl.ds(ks, tks), :], ones_s], axis=1)

    def slc_mask(s, j):
        t_row = t0 + lax.broadcasted_iota(jnp.int32, (tq, tks), 0)
        pos = j * tks + lax.broadcasted_iota(jnp.int32, (tq, tks), 1)
        return jnp.where((pos <= t_row)[None], s.reshape(part_rows // tq, tq, tks), NEG_INF).reshape(part_rows, tks)

    _flash_sweep(t0 // tks + 1, n_part, slc_score, slc_values, slc_mask, s_scr, p_scr, m_scr, acc_scr)
    acc = acc_scr[...].reshape(rows, 2 * LANES)
    o_slc = acc[:, :LANES] / acc[:, LANES:]

    start = pl.multiple_of(jnp.maximum(t0 - WINDOW, 0), tq)
    kx = jnp.concatenate([kwn_ref[pl.ds(start, win_keys), :], kpos_ref[pl.ds(start, win_keys), :]], axis=1)
    vx = jnp.concatenate([vwn_ref[pl.ds(start, win_keys), :], jnp.ones((win_keys, LANES), BF16)], axis=1)
    dist = (t0 + lax.broadcasted_iota(jnp.int32, (tq, win_keys), 0)
            - (start + lax.broadcasted_iota(jnp.int32, (tq, win_keys), 1)))
    in_window = (dist >= 0) & (dist < WINDOW)
    o_parts = []
    win_score = lambda part: _dot_nt(q_plain[part * part_rows:(part + 1) * part_rows], kx)
    s_next = win_score(0)
    for part in range(n_part):
        s = per_head(in_window, s_next, NEG_INF)
        if part + 1 < n_part:
            s_next = win_score(part + 1)
        prob = jnp.exp(s - jnp.max(s, axis=1, keepdims=True))
        acc = _dot(prob.astype(BF16), vx)
        o_parts.append(acc[:, :LANES] / acc[:, LANES:])
    o_win = jnp.concatenate(o_parts, axis=0)

    gates = jax.nn.sigmoid(sm_ref[:, COL_GL - COL_FA:COL_GL - COL_FA + NSA_HEADS * N_BRANCH] + gb_ref[...])
    for r in range(NSA_REP):
        halves = []
        for g in range(NSA_KV_HEADS):
            k = g * NSA_REP + r
            rsl = slice(k * tq, (k + 1) * tq)
            gate = lambda br: gates[:, k * N_BRANCH + br:k * N_BRANCH + br + 1]
            halves.append(gate(0) * o_cmp[rsl] + gate(1) * o_slc[rsl] + gate(2) * o_win[rsl])
        o_ref[:, r * LANES:(r + 1) * LANES] = jnp.where(first, halves[0], halves[1]).astype(BF16)


def _nsa(big, kpos, kc, vc, cpos, ovt, slope_tab, small, gate_b, bsz, t_len, tq=256, tks=512, n_part=3):
    nq = t_len // tq
    kv_col = COL_KV // KV_WIDTH
    kv_spec = lambda c: pl.BlockSpec((t_len, KV_WIDTH), lambda b, i: (b, kv_col + c))
    full = lambda a: pl.BlockSpec(a.shape, lambda b, i: tuple(0 for _ in a.shape))
    n_cmp = kc.shape[1]
    rows = NSA_HEADS * tq
    return pl.pallas_call(
        functools.partial(_nsa_kernel, tq=tq, tks=tks),
        grid=(bsz, nq),
        in_specs=[pl.BlockSpec((tq, NSA_WIDTH), lambda b, i: (b * nq + i, COL_QC // NSA_WIDTH)),
                  kv_spec(0), kv_spec(1), kv_spec(2), kv_spec(3), full(kpos),
                  pl.BlockSpec((1, n_cmp, KV_WIDTH), lambda b, i: (b, 0, 0)),
                  pl.BlockSpec((1, n_cmp, KV_WIDTH), lambda b, i: (b, 0, 0)),
                  full(cpos), full(ovt), full(slope_tab),
                  pl.BlockSpec((tq, LANES), lambda b, i: (b * nq + i, COL_FA // LANES)),
                  full(gate_b)],
        out_specs=pl.BlockSpec((tq, NSA_WIDTH), lambda b, i: (b * nq + i, 0)),
        out_shape=jax.ShapeDtypeStruct((bsz * t_len, NSA_WIDTH), BF16),
        scratch_shapes=[pltpu.VMEM((rows, 2 * LANES), BF16),
                        pltpu.VMEM((2, n_part, rows // n_part, tks), F32),
                        pltpu.VMEM((2, n_part, rows // n_part, tks), BF16),
                        pltpu.VMEM((n_part, rows // n_part, 1), F32),
                        pltpu.VMEM((n_part, rows // n_part, 2 * LANES), F32)],
        compiler_params=_params(("parallel", "arbitrary")),
        name="nsa",
    )(big, big, big, big, big, kpos, kc, vc, cpos, ovt, slope_tab, small, gate_b)


def _position_columns(pos):
    lane = jnp.arange(LANES)[None, :]
    hi = (SEL_BLOCK * (pos // SEL_BLOCK))[:, None]
    lo = (pos % SEL_BLOCK)[:, None]
    cols = (jnp.where((lane >= COL_POS) & (lane < COL_POS + N_PIECES), hi, 0)
            + jnp.where((lane >= COL_POS + N_PIECES) & (lane < COL_POS + 2 * N_PIECES), lo, 0))
    return cols.astype(BF16)


def _nsa_tables(t_len):
    pos = jnp.arange(t_len)
    lane = jnp.arange(LANES)[None, :]
    block_one_hot = ((lane < t_len // SEL_BLOCK) & (lane == (pos // SEL_BLOCK)[:, None])).astype(BF16)
    kpos = _position_columns(pos) + block_one_hot
    n_cmp = t_len // CMP_STRIDE
    c_start = jnp.arange(n_cmp) * CMP_STRIDE
    cpos = _position_columns(c_start + CMP_LEN - 1)
    s_start = (jnp.arange(t_len // SEL_BLOCK) * SEL_BLOCK)[:, None]
    ovt = ((c_start[None, :] < s_start + SEL_BLOCK) & (c_start[None, :] + CMP_LEN - 1 >= s_start)).astype(BF16)
    slopes = jnp.exp2(-8.0 * jnp.arange(1, NSA_HEADS + 1, dtype=F32) / NSA_HEADS)
    pieces = jnp.stack([piece.astype(F32) for piece in _split3(slopes)], axis=1)
    slope_tab = jnp.zeros((NSA_HEADS, LANES), F32)
    slope_tab = (slope_tab.at[:, COL_POS:COL_POS + N_PIECES].set(pieces)
                 .at[:, COL_POS + N_PIECES:COL_POS + 2 * N_PIECES].set(pieces))
    return kpos, cpos, ovt, slope_tab


def _mem_kv_kernel(mem_ref, g_ref, wk_ref, wv_ref, k_ref, v_ref):
    mb = _rms(mem_ref[0], g_ref[...]).astype(BF16)
    k_ref[0] = _dot(mb, wk_ref[...]).astype(BF16)
    v_ref[0] = _dot(mb, wv_ref[...]).astype(BF16)


def _mem_kv(mem, g, wk, wv):
    bsz, m_len, _ = mem.shape
    w_spec = pl.BlockSpec((D_MODEL, D_MODEL), lambda b: (0, 0))
    m_spec = pl.BlockSpec((1, m_len, D_MODEL), lambda b: (b, 0, 0))
    return pl.pallas_call(
        _mem_kv_kernel,
        grid=(bsz,),
        in_specs=[m_spec, pl.BlockSpec((1, D_MODEL), lambda b: (0, 0)), w_spec, w_spec],
        out_specs=[m_spec, m_spec],
        out_shape=[jax.ShapeDtypeStruct((bsz, m_len, D_MODEL), BF16)] * 2,
        compiler_params=_params(("parallel",)),
        name="mem_kv",
    )(mem, g, wk, wv)


def _route(logits):
    lane = lax.broadcasted_iota(jnp.int32, logits.shape, 1)
    big_i = jnp.int32(1 << 20)
    is_coarse = lane < N_GROUPS
    lc = jnp.where(is_coarse, logits, NEG_INF)
    mc = jnp.max(lc, axis=1, keepdims=True)
    grp = jnp.min(jnp.where(is_coarse & (lc == mc), lane, big_i), axis=1, keepdims=True)
    w_grp = 1.0 / jnp.sum(jnp.where(is_coarse, jnp.exp(lc - mc), 0.0), axis=1, keepdims=True)
    e_id = lane - N_GROUPS
    in_grp = (e_id >= grp * EXPERTS_PER_GROUP) & (e_id < (grp + 1) * EXPERTS_PER_GROUP)
    lf = jnp.where(in_grp, logits, NEG_INF)
    m1 = jnp.max(lf, axis=1, keepdims=True)
    i1 = jnp.min(jnp.where(in_grp & (lf == m1), lane, big_i), axis=1, keepdims=True)
    lf2 = jnp.where(lane == i1, NEG_INF, lf)
    m2 = jnp.max(lf2, axis=1, keepdims=True)
    i2 = jnp.min(jnp.where(in_grp & (lane != i1) & (lf2 == m2), lane, big_i), axis=1, keepdims=True)
    denom = jnp.sum(jnp.where(in_grp, jnp.exp(lf - m1), 0.0), axis=1, keepdims=True)
    p1 = 1.0 / denom
    p2 = jnp.exp(m2 - m1) / denom
    tot = p1 + p2
    w1 = w_grp * (p1 / tot)
    w2 = w_grp * (p2 / tot)
    weights = jnp.where(lane + N_GROUPS == i1, w1, jnp.where(lane + N_GROUPS == i2, w2, 0.0))
    return jnp.where(lane == grp + N_EXPERTS, 1.0, weights)


def _mix_out_kernel(x_ref, oa_ref, ob_ref, oc_ref, woa_ref, wob_ref, woc_ref, gx_ref, wq_ref, mk_ref, mv_ref,
                    wo_ref, gf_ref, wr_ref, br_ref, x2_ref, hb_ref, comb_ref, cnt_ref):
    n_part = 2
    rows = x_ref.shape[0] // n_part
    parts = [slice(k * rows, (k + 1) * rows) for k in range(n_part)]
    x1 = [x_ref[r, :] + _dot(oa_ref[r, :], woa_ref[...]) + _dot(ob_ref[r, :], wob_ref[...])
          + _dot(oc_ref[r, :], woc_ref[...]) for r in parts]
    hq = [_rms(v, gx_ref[...]).astype(BF16) for v in x1]
    q = [(_dot(v, wq_ref[...]) * (MEM_HEAD_DIM ** -0.5)).astype(BF16) for v in hq]
    heads = [[] for _ in parts]
    for h in range(MEM_HEADS):
        sl = slice(h * MEM_HEAD_DIM, (h + 1) * MEM_HEAD_DIM)
        for k in range(n_part):
            s = _dot_nt(q[k][:, sl], mk_ref[0, :, sl])
            p = jnp.exp(s - jnp.max(s, axis=1, keepdims=True))
            p = p / jnp.sum(p, axis=1, keepdims=True)
            heads[k].append(_dot(p.astype(BF16), mv_ref[0, :, sl]).astype(BF16))
    x2 = [x1[k] + _dot(jnp.concatenate(heads[k], axis=1), wo_ref[...]) for k in range(n_part)]
    hn = [_rms(v, gf_ref[...]) for v in x2]
    for k, r in enumerate(parts):
        x2_ref[r, :] = x2[k]
        hb_ref[r, :] = hn[k].astype(BF16)
        h0, h1, _ = _split3(hn[k])
        partial = _dot(h0, wr_ref[...]) + _dot(h1, wr_ref[...])
        logits = partial + pltpu.roll(partial, ROUTER_WIDTH - ROUTER_PIECE, axis=1) + br_ref[...]
        routed = _route(logits)
        comb_ref[r, :] = routed
        per_part = rows // MOE_TILE
        for sub in range(per_part):
            cnt_ref[k * per_part + sub] = jnp.sum(routed[sub * MOE_TILE:(sub + 1) * MOE_TILE], axis=0, keepdims=True)


def _mix_out(x, o_a, o_b, o_c, wo_a, wo_b, wo_c, g_x, wq, mem_k, mem_v, wo, g_f, w_r, b_r, t_len, tm=1024):
    n = x.shape[0]
    per_batch = t_len // tm
    m_len = mem_k.shape[1]
    tile = lambda width: pl.BlockSpec((tm, width), lambda i: (i, 0))
    full = lambda shape: pl.BlockSpec(shape, lambda i: tuple(0 for _ in shape))
    mem_spec = pl.BlockSpec((1, m_len, D_MODEL), lambda i: (i // per_batch, 0, 0))
    return pl.pallas_call(
        _mix_out_kernel,
        grid=(n // tm,),
        in_specs=[tile(D_MODEL), tile(FOX_WIDTH), tile(POOL_WIDTH), tile(NSA_WIDTH),
                  full(wo_a.shape), full(wo_b.shape), full(wo_c.shape), full(g_x.shape), full(wq.shape),
                  mem_spec, mem_spec, full(wo.shape), full(g_f.shape), full(w_r.shape), full(b_r.shape)],
        out_specs=[tile(D_MODEL), tile(D_MODEL), tile(ROUTER_WIDTH),
                   pl.BlockSpec((tm // MOE_TILE, 1, ROUTER_WIDTH), lambda i: (i, 0, 0))],
        out_shape=[jax.ShapeDtypeStruct((n, D_MODEL), F32),
                   jax.ShapeDtypeStruct((n, D_MODEL), BF16),
                   jax.ShapeDtypeStruct((n, ROUTER_WIDTH), F32),
                   jax.ShapeDtypeStruct((n // MOE_TILE, 1, ROUTER_WIDTH), F32)],
        compiler_params=_params(("parallel",)),
        name="mix_out",
    )(x, o_a, o_b, o_c, wo_a, wo_b, wo_c, g_x, wq, mem_k, mem_v, wo, g_f, w_r, b_r)


def _group_ranks(comb, tri):
    lane = lax.broadcasted_iota(jnp.int32, comb.shape, 1)
    member = jnp.where((lane >= N_EXPERTS) & (lane < N_EXPERTS + N_GROUPS), comb, 0.0)
    return member, _dot(tri, member.astype(BF16))


def _moe_sort_kernel(dst_ref, hb_ref, comb_ref, tri_ref, xs_in, cs_in, xs_ref, cs_ref, stage_x, stage_c, sem):
    del xs_in, cs_in
    t = pl.program_id(0)
    comb = comb_ref[...]
    member, rank = _group_ranks(comb, tri_ref[...])
    rank_t = jnp.transpose(rank)
    member_t = jnp.transpose(member)
    slot = lax.broadcasted_iota(jnp.int32, (MOE_TILE, MOE_TILE), 0).astype(F32)
    pieces = _split3(comb)
    buf = t % 2

    def copies(into, g, dst):
        return (pltpu.make_async_copy(stage_x.at[into, g], xs_ref.at[pl.ds(dst, MOE_TILE)], sem.at[0, into, g]),
                pltpu.make_async_copy(stage_c.at[into, g], cs_ref.at[pl.ds(dst, MOE_TILE)], sem.at[1, into, g]))

    for g in range(N_GROUPS):
        row = N_EXPERTS + g
        take = jnp.where((rank_t[row:row + 1, :] == slot) & (member_t[row:row + 1, :] > 0.5), 1.0, 0.0).astype(BF16)
        stage_x[buf, g] = _dot(take, hb_ref[...]).astype(BF16)
        stage_c[buf, g] = _dot(take, pieces[0]) + _dot(take, pieces[1]) + _dot(take, pieces[2])

    @pl.when(t > 0)
    def _():
        for g in range(N_GROUPS):
            for copy in copies(1 - buf, g, 0):
                copy.wait()

    for g in range(N_GROUPS):
        dst = pl.multiple_of(dst_ref[t * N_GROUPS + g], MOE_SEG_ALIGN)
        for copy in copies(buf, g, dst):
            copy.start()

    @pl.when(t == pl.num_programs(0) - 1)
    def _():
        for g in range(N_GROUPS):
            for copy in copies(buf, g, 0):
                copy.wait()


def _moe_sort(hb, comb, tri, seg_dst, rows):
    n = hb.shape[0]
    any_spec = pl.BlockSpec(memory_space=pl.ANY)
    xs0 = jnp.zeros((rows, D_MODEL), BF16)
    cs0 = jnp.zeros((rows, ROUTER_WIDTH), F32)
    return pl.pallas_call(
        _moe_sort_kernel,
        grid_spec=pltpu.PrefetchScalarGridSpec(
            num_scalar_prefetch=1,
            grid=(n // MOE_TILE,),
            in_specs=[pl.BlockSpec((MOE_TILE, D_MODEL), lambda t, dst: (t, 0)),
                      pl.BlockSpec((MOE_TILE, ROUTER_WIDTH), lambda t, dst: (t, 0)),
                      pl.BlockSpec((MOE_TILE, MOE_TILE), lambda t, dst: (0, 0)),
                      any_spec, any_spec],
            out_specs=[any_spec, any_spec],
            scratch_shapes=[pltpu.VMEM((2, N_GROUPS, MOE_TILE, D_MODEL), BF16),
                            pltpu.VMEM((2, N_GROUPS, MOE_TILE, ROUTER_WIDTH), F32),
                            pltpu.SemaphoreType.DMA((2, 2, N_GROUPS))]),
        out_shape=[jax.ShapeDtypeStruct((rows, D_MODEL), BF16), jax.ShapeDtypeStruct((rows, ROUTER_WIDTH), F32)],
        input_output_aliases={4: 0, 5: 1},
        compiler_params=_params(("arbitrary",)),
        name="moe_sort",
    )(seg_dst, hb, comb, tri, xs0, cs0)


def _moe_ffn_kernel(group_ref, valid_ref, xs_ref, cs_ref, wg_ref, wu_ref, wd_ref, ys_ref):
    i = pl.program_id(0)

    @pl.when(valid_ref[i] == 0)
    def _():
        ys_ref[...] = jnp.zeros(ys_ref.shape, F32)

    @pl.when(valid_ref[i] != 0)
    def _():
        x = xs_ref[...]
        cs = cs_ref[...]
        lane = lax.broadcasted_iota(jnp.int32, cs.shape, 1)
        first_expert = group_ref[i] * EXPERTS_PER_GROUP
        total = None
        for e in range(EXPERTS_PER_GROUP):
            he = (jax.nn.silu(_dot(x, wg_ref[0, e])) * _dot(x, wu_ref[0, e])).astype(BF16)
            c_e = jnp.sum(jnp.where(lane == first_expert + e, cs, 0.0), axis=1, keepdims=True)
            y = c_e * _dot(he, wd_ref[0, e])
            total = y if total is None else total + y
        ys_ref[...] = total


def _moe_ffn(xs, cs, wg, wu, wd, layer, tile_group, tile_valid):
    rows = xs.shape[0]
    tile = lambda width: pl.BlockSpec((MOE_FFN_TILE, width), lambda i, grp, ok: (i, 0))
    experts = lambda shape: pl.BlockSpec((1, EXPERTS_PER_GROUP) + shape, lambda i, grp, ok: (layer, grp[i], 0, 0))
    return pl.pallas_call(
        _moe_ffn_kernel,
        grid_spec=pltpu.PrefetchScalarGridSpec(
            num_scalar_prefetch=2,
            grid=(rows // MOE_FFN_TILE,),
            in_specs=[tile(D_MODEL), tile(ROUTER_WIDTH),
                      experts((D_MODEL, EXPERT_FF)), experts((D_MODEL, EXPERT_FF)), experts((EXPERT_FF, D_MODEL))],
            out_specs=tile(D_MODEL)),
        out_shape=jax.ShapeDtypeStruct((rows, D_MODEL), F32),
        compiler_params=_params(("arbitrary",)),
        name="moe_ffn",
    )(tile_group, tile_valid, xs, cs, wg, wu, wd)


def _moe_unsort_kernel(dst_ref, ys_ref, comb_ref, tri_ref, x_ref, gfin_ref, o_ref, buf, sem, *, final_norm):
    t = pl.program_id(0)
    slot = t % 2

    def copy(tile, g, into):
        dst = pl.multiple_of(dst_ref[tile * N_GROUPS + g], MOE_SEG_ALIGN)
        return pltpu.make_async_copy(ys_ref.at[pl.ds(dst, MOE_TILE)], buf.at[into, g], sem.at[into, g])

    @pl.when(t == 0)
    def _():
        for g in range(N_GROUPS):
            copy(0, g, 0).start()

    @pl.when(t + 1 < pl.num_programs(0))
    def _():
        for g in range(N_GROUPS):
            copy(t + 1, g, 1 - slot).start()

    member, rank = _group_ranks(comb_ref[...], tri_ref[...])
    lane_slot = lax.broadcasted_iota(jnp.int32, (MOE_TILE, MOE_TILE), 1).astype(F32)
    y = x_ref[...]
    for g in range(N_GROUPS):
        copy(t, g, slot).wait()
        col = N_EXPERTS + g
        put = jnp.where((rank[:, col:col + 1] == lane_slot) & (member[:, col:col + 1] > 0.5), 1.0, 0.0).astype(BF16)
        seg = buf[slot, g]
        hi = seg.astype(BF16)
        lo = (seg - hi.astype(F32)).astype(BF16)
        y = y + (_dot(put, hi) + _dot(put, lo))
    o_ref[...] = _rms(y, gfin_ref[...]) if final_norm else y


def _moe_unsort(ys, comb, tri, x2, g_fin, seg_dst, final_norm):
    n = x2.shape[0]
    return pl.pallas_call(
        functools.partial(_moe_unsort_kernel, final_norm=final_norm),
        grid_spec=pltpu.PrefetchScalarGridSpec(
            num_scalar_prefetch=1,
            grid=(n // MOE_TILE,),
            in_specs=[pl.BlockSpec(memory_space=pl.ANY),
                      pl.BlockSpec((MOE_TILE, ROUTER_WIDTH), lambda t, dst: (t, 0)),
                      pl.BlockSpec((MOE_TILE, MOE_TILE), lambda t, dst: (0, 0)),
                      pl.BlockSpec((MOE_TILE, D_MODEL), lambda t, dst: (t, 0)),
                      pl.BlockSpec((1, D_MODEL), lambda t, dst: (0, 0))],
            out_specs=pl.BlockSpec((MOE_TILE, D_MODEL), lambda t, dst: (t, 0)),
            scratch_shapes=[pltpu.VMEM((2, N_GROUPS, MOE_TILE, D_MODEL), F32),
                            pltpu.SemaphoreType.DMA((2, N_GROUPS))]),
        out_shape=jax.ShapeDtypeStruct((n, D_MODEL), F32),
        compiler_params=_params(("arbitrary",)),
        name="moe_unsort",
    )(seg_dst, ys, comb, tri, x2, g_fin)


def _moe_plan(cnt, n_tok):
    n_tiles = cnt.shape[0]
    rows = n_tok + n_tiles * N_GROUPS * MOE_SEG_ALIGN + N_GROUPS * (MOE_FFN_TILE + MOE_GAP)
    rows = -(-rows // MOE_FFN_TILE) * MOE_FFN_TILE
    padded = (cnt + MOE_SEG_ALIGN - 1) // MOE_SEG_ALIGN * MOE_SEG_ALIGN
    within = jnp.cumsum(padded, axis=0) - padded
    region = (jnp.sum(padded, axis=0) + MOE_FFN_TILE - 1) // MOE_FFN_TILE * MOE_FFN_TILE
    start = jnp.cumsum(region + MOE_GAP) - (region + MOE_GAP)
    seg_dst = (start[None, :] + within).reshape(-1).astype(jnp.int32)
    tile_row = jnp.arange(rows // MOE_FFN_TILE) * MOE_FFN_TILE
    begun = tile_row[:, None] >= start[None, :]
    tile_valid = jnp.any(begun & (tile_row[:, None] < (start + region)[None, :]), axis=1).astype(jnp.int32)
    tile_group = jnp.clip(jnp.sum(begun, axis=1) - 1, 0, N_GROUPS - 1).astype(jnp.int32)
    return seg_dst, tile_group, tile_valid, rows


def _moe(x2, hb, comb, cnt, wg, wu, wd, layer, g_fin, final_norm):
    n = x2.shape[0]
    counts = jnp.round(cnt[:, 0, N_EXPERTS:N_EXPERTS + N_GROUPS]).astype(jnp.int32)
    seg_dst, tile_group, tile_valid, rows = _moe_plan(counts, n)
    tri = (jnp.arange(MOE_TILE)[:, None] > jnp.arange(MOE_TILE)[None, :]).astype(BF16)
    xs, cs = _moe_sort(hb, comb, tri, seg_dst, rows)
    ys = _moe_ffn(xs, cs, wg, wu, wd, layer, tile_group, tile_valid)
    return _moe_unsort(ys, comb, tri, x2, g_fin, seg_dst, final_norm)


def _pair_order(a, axis):
    shape = a.shape
    a = a.reshape(shape[:axis] + (NSA_HEADS, HEAD_DIM) + shape[axis + 1:])
    a = jnp.take(a, jnp.array(NSA_PAIR_ORDER), axis=axis)
    return a.reshape(shape)


def _split_in_proj_weight(w):
    edges = [0]
    for width in (FOX_WIDTH, FOX_WIDTH, FOX_WIDTH, FOX_HEADS, POOL_WIDTH, NSA_WIDTH) + (KV_WIDTH,) * 6 + (NSA_HEADS * N_BRANCH,):
        edges.append(edges[-1] + width)
    part = lambda k: w[:, edges[k]:edges[k + 1]]
    qa, ka, va, fa, ub, qc = (part(k) for k in range(6))
    kvs = [part(k) for k in range(6, 12)]
    gl = part(12)
    w_big = jnp.concatenate([qa, ka, va, _pair_order(qc, 1)] + kvs[2:], axis=1).astype(BF16)
    pad = jnp.zeros((w.shape[0], LANES - FOX_HEADS - NSA_HEADS * N_BRANCH), w.dtype)
    w_small = jnp.concatenate([ub, fa, gl, pad] + kvs[:2], axis=1).astype(BF16)
    return w_big, w_small


def _block_diag(w):
    groups, cin, cout = w.shape
    out = jnp.zeros((groups * cin, groups * cout), w.dtype)
    for gi in range(groups):
        out = out.at[gi * cin:(gi + 1) * cin, gi * cout:(gi + 1) * cout].set(w[gi])
    return out


def _layer(x, mem, p, experts, layer, bsz, t_len, final_g, final_norm):
    row = lambda v: v.reshape(1, -1)
    w_big, w_small = _split_in_proj_weight(p["w_in"])
    big, small = _in_proj(x, row(p["mix_norm_g"]), w_big, w_small)

    b_row = jnp.pad(p["fox_forget_b"], (0, LANES - FOX_HEADS)).reshape(1, LANES)
    ck = _forget_cumsum(small, b_row, _forget_placement(), bsz, t_len)
    o_a = _fox(big, ck, bsz, t_len)

    o_b = _pool(small, _block_diag(p["pool_w"]).astype(BF16), row(p["pool_scale"]), bsz, t_len)

    pe_k, w1_k, w2_k = _compress_weights(p["cmp_pe_k"], p["cmp_w1_k"], p["cmp_w2_k"])
    pe_v, w1_v, w2_v = _compress_weights(p["cmp_pe_v"], p["cmp_w1_v"], p["cmp_w2_v"])
    kc, vc = _compress(small, pe_k, pe_v, w1_k, w2_k, w1_v, w2_v, bsz, t_len)
    kpos, cpos, ovt, slope_tab = _nsa_tables(t_len)
    o_c = _nsa(big, kpos, kc, vc, cpos, ovt, slope_tab, small, row(p["nsa_gate_b"]), bsz, t_len)

    mem_k, mem_v = _mem_kv(mem, row(p["mem_norm_g"]), p["xattn_wk"].astype(BF16), p["xattn_wv"].astype(BF16))
    w_out = p["w_out"]
    wo_a = w_out[:FOX_WIDTH].astype(BF16)
    wo_b = w_out[FOX_WIDTH:FOX_WIDTH + POOL_WIDTH].astype(BF16)
    wo_c = _pair_order(w_out[FOX_WIDTH + POOL_WIDTH:], 0).astype(BF16)
    w_r = jnp.concatenate([p["router_coarse_w"], p["router_fine_w"],
                           jnp.zeros((D_MODEL, ROUTER_PIECE - N_GROUPS - N_EXPERTS), F32)], axis=1)
    w_r0, w_r1, _ = _split3(w_r)
    w_r = jnp.concatenate([w_r0, w_r1, jnp.zeros((D_MODEL, ROUTER_WIDTH - 2 * ROUTER_PIECE), BF16)], axis=1)
    b_r = jnp.concatenate([p["router_coarse_b"], p["router_fine_b"],
                           jnp.zeros((ROUTER_WIDTH - N_GROUPS - N_EXPERTS,), F32)]).reshape(1, -1)
    x2, hb, comb, cnt = _mix_out(x, o_a, o_b, o_c, wo_a, wo_b, wo_c, row(p["xattn_norm_g"]),
                                 p["xattn_wq"].astype(BF16), mem_k, mem_v, p["xattn_wo"].astype(BF16),
                                 row(p["ffn_norm_g"]), w_r, b_r, t_len)

    return _moe(x2, hb, comb, cnt, *experts, layer, row(final_g), final_norm)


def kernel(x, mem, mix_norm_g, w_in, fox_forget_b, pool_w, pool_scale, cmp_pe_k, cmp_pe_v, cmp_w1_k, cmp_w2_k, cmp_w1_v, cmp_w2_v, nsa_gate_b, w_out, xattn_norm_g, mem_norm_g, xattn_wq, xattn_wk, xattn_wv, xattn_wo, ffn_norm_g, router_coarse_w, router_coarse_b, router_fine_w, router_fine_b, exp_w_gate, exp_w_up, exp_w_down, final_norm_g):
    stacked = dict(mix_norm_g=mix_norm_g, w_in=w_in, fox_forget_b=fox_forget_b, pool_w=pool_w, pool_scale=pool_scale,
                   cmp_pe_k=cmp_pe_k, cmp_pe_v=cmp_pe_v, cmp_w1_k=cmp_w1_k, cmp_w2_k=cmp_w2_k, cmp_w1_v=cmp_w1_v,
                   cmp_w2_v=cmp_w2_v, nsa_gate_b=nsa_gate_b, w_out=w_out, xattn_norm_g=xattn_norm_g,
                   mem_norm_g=mem_norm_g, xattn_wq=xattn_wq, xattn_wk=xattn_wk, xattn_wv=xattn_wv, xattn_wo=xattn_wo,
                   ffn_norm_g=ffn_norm_g, router_coarse_w=router_coarse_w, router_coarse_b=router_coarse_b,
                   router_fine_w=router_fine_w, router_fine_b=router_fine_b)
    experts = (exp_w_gate.astype(BF16), exp_w_up.astype(BF16), exp_w_down.astype(BF16))
    bsz, t_len, d = x.shape
    depth = w_in.shape[0]
    h = x.reshape(bsz * t_len, d)
    for layer in range(depth):
        p = {name: value[layer] for name, value in stacked.items()}
        h = _layer(h, mem, p, experts, layer, bsz, t_len, final_norm_g, final_norm=(layer == depth - 1))
    return h.reshape(bsz, t_len, d)
```

```python
import functools

import jax
import jax.numpy as jnp
from jax import lax
from jax.experimental import pallas as pl
from jax.experimental.pallas import tpu as pltpu

F32 = jnp.float32
BF16 = jnp.bfloat16

D_MODEL = 1024
HEAD_DIM = 64
FOX_HEADS = 6
FOX_WIDTH = FOX_HEADS * HEAD_DIM
POOL_WIDTH = 256
POOL_GROUP_DIM = 64
POOL_WINDOWS = (2, 4, 8, 16)
POOL_HALO = 16
NSA_HEADS = 6
NSA_KV_HEADS = 2
NSA_REP = NSA_HEADS // NSA_KV_HEADS
NSA_WIDTH = NSA_HEADS * HEAD_DIM
KV_WIDTH = NSA_KV_HEADS * HEAD_DIM
N_BRANCH = 3
CMP_LEN = 32
CMP_STRIDE = 16
CMP_HIDDEN = 256
SEL_BLOCK = 64
SEL_TOPN = 16
WINDOW = 512
FORCE_SCORE = 1.0e4
MEM_HEADS = 4
MEM_HEAD_DIM = D_MODEL // MEM_HEADS
N_GROUPS = 4
EXPERTS_PER_GROUP = 4
N_EXPERTS = N_GROUPS * EXPERTS_PER_GROUP
EXPERT_FF = 512
EPS = 1e-6
NEG_INF = -1e30

LANES = 128
SUBLANES = 8
BIG_WIDTH = 3 * FOX_WIDTH + NSA_WIDTH + 4 * KV_WIDTH
COL_QC = 3 * FOX_WIDTH
COL_KV = COL_QC + NSA_WIDTH
SMALL_WIDTH = POOL_WIDTH + LANES + 2 * KV_WIDTH
COL_FA = POOL_WIDTH
COL_GL = POOL_WIDTH + FOX_HEADS
COL_CMP = POOL_WIDTH + LANES
ROUTER_WIDTH = LANES
ROUTER_PIECE = 32
MOE_TILE = 256
MOE_SEG_ALIGN = 16
MOE_FFN_TILE = 512
MOE_GAP = 512
NSA_PAIR_ORDER = tuple(g * NSA_REP + r for r in range(NSA_REP) for g in range(NSA_KV_HEADS))
N_PIECES = 3
COL_POS = HEAD_DIM

VMEM_LIMIT = 56 * 1024 * 1024

_NT = (((1,), (1,)), ((), ()))


def _params(semantics):
    return pltpu.CompilerParams(dimension_semantics=semantics, vmem_limit_bytes=VMEM_LIMIT)


def _rms(x, g):
    return x * lax.rsqrt(jnp.mean(x * x, axis=-1, keepdims=True) + EPS) * g


def _dot(a, b):
    return jnp.dot(a, b, preferred_element_type=F32)


def _dot_nt(a, b):
    return lax.dot_general(a, b, _NT, preferred_element_type=F32)


def _split3(a):
    a0 = a.astype(BF16)
    r = a - a0.astype(F32)
    a1 = r.astype(BF16)
    a2 = (r - a1.astype(F32)).astype(BF16)
    return a0, a1, a2


def _in_proj_kernel(x_ref, g_ref, wb_ref, ws_ref, big_ref, small_ref):
    yb = _rms(x_ref[...], g_ref[...]).astype(BF16)
    big_ref[...] = _dot(yb, wb_ref[...]).astype(BF16)
    small_ref[...] = _dot(yb, ws_ref[...])


def _in_proj(x, g, w_big, w_small, tm=1024):
    n = x.shape[0]
    return pl.pallas_call(
        _in_proj_kernel,
        grid=(n // tm,),
        in_specs=[pl.BlockSpec((tm, D_MODEL), lambda i: (i, 0)),
                  pl.BlockSpec((1, D_MODEL), lambda i: (0, 0)),
                  pl.BlockSpec((D_MODEL, BIG_WIDTH), lambda i: (0, 0)),
                  pl.BlockSpec((D_MODEL, SMALL_WIDTH), lambda i: (0, 0))],
        out_specs=[pl.BlockSpec((tm, BIG_WIDTH), lambda i: (i, 0)),
                   pl.BlockSpec((tm, SMALL_WIDTH), lambda i: (i, 0))],
        out_shape=[jax.ShapeDtypeStruct((n, BIG_WIDTH), BF16),
                   jax.ShapeDtypeStruct((n, SMALL_WIDTH), F32)],
        compiler_params=_params(("parallel",)),
        name="in_proj",
    )(x, g, w_big, w_small)


def _forget_kernel(f_ref, b_ref, tri_ref, place_ref, ck_ref):
    log_f = jax.nn.log_sigmoid(f_ref[...] + b_ref[...])
    blk = tri_ref.shape[0]
    tri = tri_ref[...]
    carry = jnp.zeros((1, LANES), F32)
    sums = []
    for k in range(log_f.shape[0] // blk):
        x0, x1, x2 = _split3(log_f[k * blk:(k + 1) * blk])
        sums.append(_dot(tri, x0) + _dot(tri, x1) + _dot(tri, x2) + carry)
        carry = sums[-1][blk - 1:blk, :]
    c = jnp.concatenate(sums, axis=0)
    n0, n1, n2 = _split3(-c)
    ck_ref[...] = (_dot(n0, place_ref[0]) + _dot(n1, place_ref[1]) + _dot(n2, place_ref[2])).astype(BF16)


def _forget_cumsum(small, b_row, place, bsz, t_len, blk=256):
    tri = (jnp.arange(blk)[:, None] >= jnp.arange(blk)[None, :]).astype(BF16)
    return pl.pallas_call(
        _forget_kernel,
        grid=(bsz,),
        in_specs=[pl.BlockSpec((t_len, LANES), lambda b: (b, COL_FA // LANES)),
                  pl.BlockSpec((1, LANES), lambda b: (0, 0)),
                  pl.BlockSpec(tri.shape, lambda b: (0, 0)),
                  pl.BlockSpec(place.shape, lambda b: (0, 0, 0))],
        out_specs=pl.BlockSpec((t_len, FOX_WIDTH), lambda b: (b, 0)),
        out_shape=jax.ShapeDtypeStruct((bsz * t_len, FOX_WIDTH), BF16),
        compiler_params=_params(("parallel",)),
        name="forget",
    )(small, b_row, tri, place)


def _forget_placement():
    src = jnp.arange(LANES)[None, :, None]
    dst = jnp.arange(FOX_WIDTH)[None, None, :]
    piece = jnp.arange(N_PIECES)[:, None, None]
    target = LANES * (src // 2) + N_PIECES * (src % 2) + piece
    return ((src < FOX_HEADS) & (dst == target)).astype(BF16)


def _flash_sweep(n_tiles, n_stream, score, values, last_mask, s_scr, p_scr, m_scr, acc_scr):
    def run_stage(j, a, first=False, last=False):
        b = 1 - a
        for st in range(n_stream):
            s = score(0, st) if first else s_scr[a, st]
            if last:
                s = last_mask(s, j)
            m_old = jnp.full(m_scr.shape[1:], NEG_INF, F32) if first else m_scr[st]
            m_new = jnp.maximum(m_old, jnp.max(s, axis=1, keepdims=True))
            prob = jnp.exp(s - m_new).astype(BF16)
            if not last:
                p_scr[a, st] = prob
                s_scr[b, st] = score(j + 1, st)
            acc = jnp.zeros(acc_scr.shape[1:], F32) if first else acc_scr[st]
            if not first:
                acc = (acc + _dot(p_scr[b, st], values(j - 1, st))) * jnp.exp(m_old - m_new)
            if last:
                acc = acc + _dot(prob, values(j, st))
            acc_scr[st] = acc
            m_scr[st] = m_new

    n_main = n_tiles - 1

    @pl.when(n_main == 0)
    def _():
        run_stage(0, 0, first=True, last=True)

    @pl.when(n_main > 0)
    def _():
        run_stage(0, 0, first=True)

    def two_stages(jj, carry):
        run_stage(2 * jj + 1, 1)
        run_stage(2 * jj + 2, 0)
        return carry

    lax.fori_loop(0, jnp.maximum(n_main - 1, 0) // 2, two_stages, 0)

    @pl.when(jnp.logical_and(n_main >= 2, n_main % 2 == 0))
    def _():
        run_stage(n_main - 1, 1)

    for a in range(2):
        @pl.when(jnp.logical_and(n_main >= 1, n_main % 2 == a))
        def _():
            run_stage(n_main, a, last=True)


def _fox_kernel(q_ref, k_ref, v_ref, ck_ref, o_ref, q_scr, s_scr, p_scr, m_scr, acc_scr, *, tq, tk):
    i = pl.program_id(1)
    t0 = i * tq
    n_pair = FOX_HEADS // 2
    lane = lax.broadcasted_iota(jnp.int32, (tq, LANES), 1)
    first = lane < HEAD_DIM
    scale = jnp.asarray(HEAD_DIM ** -0.5, BF16)
    pick_a = jnp.where(lane < N_PIECES, 1.0, 0.0).astype(BF16)
    pick_b = jnp.where((lane >= N_PIECES) & (lane < 2 * N_PIECES), 1.0, 0.0).astype(BF16)
    for p in range(n_pair):
        qp = q_ref[:, p * LANES:(p + 1) * LANES] * scale
        zero = jnp.zeros_like(qp)
        qa = jnp.concatenate([jnp.where(first, qp, zero), pick_a], axis=1)
        qb = jnp.concatenate([jnp.where(first, zero, qp), pick_b], axis=1)
        q_scr[p] = jnp.concatenate([qa, qb], axis=0)
    ones_v = jnp.ones((tk, LANES), BF16)

    def score(j, p):
        ks = pl.multiple_of(j * tk, tk)
        psl = slice(p * LANES, (p + 1) * LANES)
        kx = jnp.concatenate([k_ref[pl.ds(ks, tk), psl], ck_ref[pl.ds(ks, tk), psl]], axis=1)
        return _dot_nt(q_scr[p], kx)

    def values(j, p):
        ks = pl.multiple_of(j * tk, tk)
        return jnp.concatenate([v_ref[pl.ds(ks, tk), p * LANES:(p + 1) * LANES], ones_v], axis=1)

    def causal_mask(s, j):
        t_row = t0 + lax.broadcasted_iota(jnp.int32, (tq, tk), 0)
        pos = j * tk + lax.broadcasted_iota(jnp.int32, (tq, tk), 1)
        return jnp.where((pos <= t_row)[None], s.reshape(2, tq, tk), NEG_INF).reshape(2 * tq, tk)

    _flash_sweep(t0 // tk + 1, n_pair, score, values, causal_mask, s_scr, p_scr, m_scr, acc_scr)
    for p in range(n_pair):
        acc = acc_scr[p]
        oa = acc[:tq, :LANES] / acc[:tq, LANES:]
        ob = acc[tq:, :LANES] / acc[tq:, LANES:]
        o_ref[:, p * LANES:(p + 1) * LANES] = jnp.where(first, oa, ob).astype(BF16)


def _fox(big, ck, bsz, t_len, tq=256, tk=512):
    nq = t_len // tq
    n_pair = FOX_HEADS // 2
    seq = lambda c: pl.BlockSpec((t_len, FOX_WIDTH), lambda b, i: (b, c))
    return pl.pallas_call(
        functools.partial(_fox_kernel, tq=tq, tk=tk),
        grid=(bsz, nq),
        in_specs=[pl.BlockSpec((tq, FOX_WIDTH), lambda b, i: (b * nq + i, 0)), seq(1), seq(2), seq(0)],
        out_specs=pl.BlockSpec((tq, FOX_WIDTH), lambda b, i: (b * nq + i, 0)),
        out_shape=jax.ShapeDtypeStruct((bsz * t_len, FOX_WIDTH), BF16),
        scratch_shapes=[pltpu.VMEM((n_pair, 2 * tq, 2 * LANES), BF16),
                        pltpu.VMEM((2, n_pair, 2 * tq, tk), F32),
                        pltpu.VMEM((2, n_pair, 2 * tq, tk), BF16),
                        pltpu.VMEM((n_pair, 2 * tq, 1), F32),
                        pltpu.VMEM((n_pair, 2 * tq, 2 * LANES), F32)],
        compiler_params=_params(("parallel", "arbitrary")),
        name="fox",
    )(big, big, big, ck)


def _pool_kernel(u_ref, halo_ref, w_ref, scale_ref, o_ref, x_scr, *, tt):
    i = pl.program_id(1)
    u = u_ref[...]
    x_scr[POOL_HALO:, :] = u
    x_scr[:POOL_HALO, :] = jnp.where(i > 0, halo_ref[...], 0.0)
    t_pos = i * tt + lax.broadcasted_iota(jnp.int32, (tt, 1), 0)
    lane = lax.broadcasted_iota(jnp.int32, (tt, POOL_WIDTH), 1)
    run = u
    pooled = jnp.zeros_like(u)
    for back in range(1, max(POOL_WINDOWS)):
        run = run + x_scr[POOL_HALO - back:POOL_HALO - back + tt, :]
        w = back + 1
        if w in POOL_WINDOWS:
            gi = POOL_WINDOWS.index(w)
            count = jnp.minimum(t_pos + 1, w).astype(F32)
            in_group = (lane >= gi * POOL_GROUP_DIM) & (lane < (gi + 1) * POOL_GROUP_DIM)
            pooled = jnp.where(in_group, run / count - u, pooled)
    mixed = _dot(pooled.astype(BF16), w_ref[...])
    o_ref[...] = (mixed * scale_ref[...]).astype(BF16)


def _pool(small, w_bd, scale, bsz, t_len, tt=2048):
    nt = t_len // tt
    per = tt // POOL_HALO
    return pl.pallas_call(
        functools.partial(_pool_kernel, tt=tt),
        grid=(bsz, nt),
        in_specs=[pl.BlockSpec((tt, POOL_WIDTH), lambda b, i: (b * nt + i, 0)),
                  pl.BlockSpec((POOL_HALO, POOL_WIDTH), lambda b, i: (jnp.maximum((b * nt + i) * per - 1, 0), 0)),
                  pl.BlockSpec((POOL_WIDTH, POOL_WIDTH), lambda b, i: (0, 0)),
                  pl.BlockSpec((1, POOL_WIDTH), lambda b, i: (0, 0))],
        out_specs=pl.BlockSpec((tt, POOL_WIDTH), lambda b, i: (b * nt + i, 0)),
        out_shape=jax.ShapeDtypeStruct((bsz * t_len, POOL_WIDTH), BF16),
        scratch_shapes=[pltpu.VMEM((tt + POOL_HALO, POOL_WIDTH), F32)],
        compiler_params=_params(("parallel", "arbitrary")),
        name="pool",
    )(small, small, w_bd, scale)


def _compress_kernel(xk_ref, xv_ref, pek_ref, pev_ref, w1k_ref, w2k_ref, w1v_ref, w2v_ref, kc_ref, vc_ref):
    def one(x_ref, pe_ref, w1_ref, w2_ref, out_ref):
        n_chunk = x_ref.shape[0] // CMP_STRIDE
        first = jnp.zeros((n_chunk, w1_ref.shape[2]), F32)
        second = jnp.zeros((n_chunk, w1_ref.shape[2]), F32)
        for r in range(CMP_STRIDE):
            x_r = x_ref[pl.ds(r, n_chunk, stride=CMP_STRIDE), :]
            first = first + _dot((x_r + pe_ref[r:r + 1, :]).astype(BF16), w1_ref[r])
            second = second + _dot((x_r + pe_ref[CMP_STRIDE + r:CMP_STRIDE + r + 1, :]).astype(BF16),
                                   w1_ref[CMP_STRIDE + r])
        hidden = first + pltpu.roll(second, n_chunk - 1, axis=0)
        out_ref[0] = _dot(jax.nn.gelu(hidden).astype(BF16), w2_ref[...])

    one(xk_ref, pek_ref, w1k_ref, w2k_ref, kc_ref)
    one(xv_ref, pev_ref, w1v_ref, w2v_ref, vc_ref)


def _compress(small, pe_k, pe_v, w1_k, w2_k, w1_v, w2_v, bsz, t_len):
    n_chunk = t_len // CMP_STRIDE
    full = lambda a: pl.BlockSpec(a.shape, lambda b: tuple(0 for _ in a.shape))
    x_spec = lambda col: pl.BlockSpec((t_len, KV_WIDTH), lambda b: (b, col // KV_WIDTH))
    o_spec = pl.BlockSpec((1, n_chunk, KV_WIDTH), lambda b: (b, 0, 0))
    return pl.pallas_call(
        _compress_kernel,
        grid=(bsz,),
        in_specs=[x_spec(COL_CMP), x_spec(COL_CMP + KV_WIDTH), full(pe_k), full(pe_v),
                  full(w1_k), full(w2_k), full(w1_v), full(w2_v)],
        out_specs=[o_spec, o_spec],
        out_shape=[jax.ShapeDtypeStruct((bsz, n_chunk, KV_WIDTH), F32)] * 2,
        compiler_params=_params(("parallel",)),
        name="compress",
    )(small, small, pe_k, pe_v, w1_k, w2_k, w1_v, w2_v)


def _compress_weights(pe, w1, w2):
    def block_diag(w):
        zero = jnp.zeros_like(w)
        rows = [jnp.concatenate([w if h == g else zero for h in range(NSA_KV_HEADS)], axis=-1)
                for g in range(NSA_KV_HEADS)]
        return jnp.concatenate(rows, axis=-2)

    w1_bd = block_diag(w1.reshape(CMP_LEN, HEAD_DIM, CMP_HIDDEN).astype(BF16))
    return jnp.tile(pe, (1, NSA_KV_HEADS)), w1_bd, block_diag(w2.astype(BF16))


def _nsa_kernel(q_ref, ksl_ref, vsl_ref, kwn_ref, vwn_ref, kpos_ref, kc_ref, vc_ref, cpos_ref, ovt_ref, slope_ref,
                sm_ref, gb_ref, o_ref, qs_scr, s_scr, p_scr, m_scr, acc_scr, *, tq, tks):
    i = pl.program_id(1)
    t0 = i * tq
    n_head = NSA_HEADS
    rows = n_head * tq
    n_cmp = kc_ref.shape[1]
    n_blk = ovt_ref.shape[0]
    win_keys = WINDOW + tq
    lane = lax.broadcasted_iota(jnp.int32, (tq, LANES), 1)
    first = lane < HEAD_DIM
    scale = jnp.asarray(HEAD_DIM ** -0.5, BF16)

    pairs = [q_ref[:, r * LANES:(r + 1) * LANES] * scale for r in range(NSA_REP)]
    zero = jnp.zeros_like(pairs[0])
    q_heads, q_slopes = [], []
    for g in range(NSA_KV_HEADS):
        for r in range(NSA_REP):
            q_heads.append(jnp.where(first, pairs[r], zero) if g == 0 else jnp.where(first, zero, pairs[r]))
            q_slopes.append(jnp.broadcast_to(slope_ref[g * NSA_REP + r:g * NSA_REP + r + 1, :], (tq, LANES)))
    q_main = jnp.concatenate(q_heads, axis=0)
    q_plain = jnp.concatenate([q_main, jnp.concatenate(q_slopes, axis=0).astype(BF16)], axis=1)

    n_part = s_scr.shape[1]
    part_rows = rows // n_part

    def per_head(mask, s, fill):
        return jnp.where(mask[None], s.reshape(s.shape[0] // tq, tq, s.shape[-1]), fill).reshape(s.shape)

    t_col = t0 + lax.broadcasted_iota(jnp.int32, (tq, n_cmp), 0)
    c_end = lax.broadcasted_iota(jnp.int32, (tq, n_cmp), 1) * CMP_STRIDE + (CMP_LEN - 1)
    valid_c = c_end <= t_col
    kc_x = jnp.concatenate([kc_ref[0].astype(BF16), cpos_ref[...]], axis=1)
    vc_b = vc_ref[0].astype(BF16)
    p_parts, o_parts = [], []
    for part in range(n_part):
        rsl = slice(part * part_rows, (part + 1) * part_rows)
        s = per_head(valid_c, _dot_nt(q_plain[rsl], kc_x), NEG_INF)
        p = per_head(valid_c, jnp.exp(s - jnp.max(s, axis=1, keepdims=True)), 0.0)
        l_c = jnp.sum(p, axis=1, keepdims=True)
        p = p * jnp.where(l_c > 0.0, 1.0 / l_c, 0.0)
        p_parts.append(p)
        o_parts.append(_dot(p.astype(BF16), vc_b))
    p_c = jnp.concatenate(p_parts, axis=0)
    o_cmp = jnp.concatenate(o_parts, axis=0)

    blk_t = lax.broadcasted_iota(jnp.int32, (n_blk, tq), 0)
    cur_t = (t0 + lax.broadcasted_iota(jnp.int32, (n_blk, tq), 1)) // SEL_BLOCK
    forced_t = (blk_t == 0) | (blk_t == cur_t) | (blk_t == cur_t - 1)
    sub = lax.broadcasted_iota(jnp.int32, (SUBLANES, tq), 0)
    n_top = float(min(SEL_TOPN, n_blk))
    for g in range(NSA_KV_HEADS):
        p_sum = p_c[g * NSA_REP * tq:(g * NSA_REP + 1) * tq]
        for r in range(1, NSA_REP):
            p_sum = p_sum + p_c[(g * NSA_REP + r) * tq:(g * NSA_REP + r + 1) * tq]
        p0, p1, p2 = _split3(p_sum)
        ovt = ovt_ref[...]
        imp_t = _dot_nt(ovt, p0) + _dot_nt(ovt, p1) + _dot_nt(ovt, p2)
        score = jnp.where(forced_t, FORCE_SCORE, jnp.where(blk_t <= cur_t, imp_t, -1.0))
        chunks = [score[c * SUBLANES:(c + 1) * SUBLANES] for c in range(n_blk // SUBLANES)]
        ranks = [jnp.zeros((SUBLANES, tq), F32) for _ in chunks]
        for jp in range(n_blk):
            other = jnp.broadcast_to(score[jp:jp + 1], (SUBLANES, tq))
            for c, chunk in enumerate(chunks):
                ge = jnp.where(other >= chunk, 1.0, 0.0)
                gt = jnp.where(other > chunk, 1.0, 0.0)
                if jp < c * SUBLANES:
                    ranks[c] = ranks[c] + ge
                elif jp >= (c + 1) * SUBLANES:
                    ranks[c] = ranks[c] + gt
                else:
                    ranks[c] = ranks[c] + jnp.where(sub > jp - c * SUBLANES, ge, gt)
        rank = jnp.concatenate(ranks, axis=0)
        neg_t = jnp.where(rank < n_top, 0.0, NEG_INF)
        neg = jnp.transpose(jnp.concatenate([neg_t, jnp.zeros((LANES - n_blk, tq), F32)], axis=0))
        for r in range(NSA_REP):
            k = g * NSA_REP + r
            q_const = (neg + q_slopes[k]).astype(BF16)
            qs_scr[k * tq:(k + 1) * tq, :] = jnp.concatenate([q_heads[k], q_const], axis=1)

    ones_s = jnp.ones((tks, LANES), BF16)

    def slc_score(j, part):
        ks = pl.multiple_of(j * tks, tks)
        kx = jnp.concatenate([ksl_ref[pl.ds(ks, tks), :], kpos_ref[pl.ds(ks, tks), :]], axis=1)
        return _dot_nt(qs_scr[part * part_rows:(part + 1) * part_rows, :], kx)

    def slc_values(j, part):
        ks = pl.multiple_of(j * tks, tks)
        return jnp.concatenate([vsl_ref[pl.ds(ks, tks), :], ones_s], axis=1)

    def slc_mask(s, j):
        t_row = t0 + lax.broadcasted_iota(jnp.int32, (tq, tks), 0)
        pos = j * tks + lax.broadcasted_iota(jnp.int32, (tq, tks), 1)
        return jnp.where((pos <= t_row)[None], s.reshape(part_rows // tq, tq, tks), NEG_INF).reshape(part_rows, tks)

    _flash_sweep(t0 // tks + 1, n_part, slc_score, slc_values, slc_mask, s_scr, p_scr, m_scr, acc_scr)
    acc = acc_scr[...].reshape(rows, 2 * LANES)
    o_slc = acc[:, :LANES] / acc[:, LANES:]

    start = pl.multiple_of(jnp.maximum(t0 - WINDOW, 0), tq)
    kx = jnp.concatenate([kwn_ref[pl.ds(start, win_keys), :], kpos_ref[pl.ds(start, win_keys), :]], axis=1)
    vx = jnp.concatenate([vwn_ref[pl.ds(start, win_keys), :], jnp.ones((win_keys, LANES), BF16)], axis=1)
    dist = (t0 + lax.broadcasted_iota(jnp.int32, (tq, win_keys), 0)
            - (start + lax.broadcasted_iota(jnp.int32, (tq, win_keys), 1)))
    in_window = (dist >= 0) & (dist < WINDOW)
    o_parts = []
    win_score = lambda part: _dot_nt(q_plain[part * part_rows:(part + 1) * part_rows], kx)
    s_next = win_score(0)
    for part in range(n_part):
        s = per_head(in_window, s_next, NEG_INF)
        if part + 1 < n_part:
            s_next = win_score(part + 1)
        prob = jnp.exp(s - jnp.max(s, axis=1, keepdims=True))
        acc = _dot(prob.astype(BF16), vx)
        o_parts.append(acc[:, :LANES] / acc[:, LANES:])
    o_win = jnp.concatenate(o_parts, axis=0)

    gates = jax.nn.sigmoid(sm_ref[:, COL_GL - COL_FA:COL_GL - COL_FA + NSA_HEADS * N_BRANCH] + gb_ref[...])
    for r in range(NSA_REP):
        halves = []
        for g in range(NSA_KV_HEADS):
            k = g * NSA_REP + r
            rsl = slice(k * tq, (k + 1) * tq)
            gate = lambda br: gates[:, k * N_BRANCH + br:k * N_BRANCH + br + 1]
            halves.append(gate(0) * o_cmp[rsl] + gate(1) * o_slc[rsl] + gate(2) * o_win[rsl])
        o_ref[:, r * LANES:(r + 1) * LANES] = jnp.where(first, halves[0], halves[1]).astype(BF16)


def _nsa(big, kpos, kc, vc, cpos, ovt, slope_tab, small, gate_b, bsz, t_len, tq=256, tks=512, n_part=3):
    nq = t_len // tq
    kv_col = COL_KV // KV_WIDTH
    kv_spec = lambda c: pl.BlockSpec((t_len, KV_WIDTH), lambda b, i: (b, kv_col + c))
    full = lambda a: pl.BlockSpec(a.shape, lambda b, i: tuple(0 for _ in a.shape))
    n_cmp = kc.shape[1]
    rows = NSA_HEADS * tq
    return pl.pallas_call(
        functools.partial(_nsa_kernel, tq=tq, tks=tks),
        grid=(bsz, nq),
        in_specs=[pl.BlockSpec((tq, NSA_WIDTH), lambda b, i: (b * nq + i, COL_QC // NSA_WIDTH)),
                  kv_spec(0), kv_spec(1), kv_spec(2), kv_spec(3), full(kpos),
                  pl.BlockSpec((1, n_cmp, KV_WIDTH), lambda b, i: (b, 0, 0)),
                  pl.BlockSpec((1, n_cmp, KV_WIDTH), lambda b, i: (b, 0, 0)),
                  full(cpos), full(ovt), full(slope_tab),
                  pl.BlockSpec((tq, LANES), lambda b, i: (b * nq + i, COL_FA // LANES)),
                  full(gate_b)],
        out_specs=pl.BlockSpec((tq, NSA_WIDTH), lambda b, i: (b * nq + i, 0)),
        out_shape=jax.ShapeDtypeStruct((bsz * t_len, NSA_WIDTH), BF16),
        scratch_shapes=[pltpu.VMEM((rows, 2 * LANES), BF16),
                        pltpu.VMEM((2, n_part, rows // n_part, tks), F32),
                        pltpu.VMEM((2, n_part, rows // n_part, tks), BF16),
                        pltpu.VMEM((n_part, rows // n_part, 1), F32),
                        pltpu.VMEM((n_part, rows // n_part, 2 * LANES), F32)],
        compiler_params=_params(("parallel", "arbitrary")),
        name="nsa",
    )(big, big, big, big, big, kpos, kc, vc, cpos, ovt, slope_tab, small, gate_b)


def _position_columns(pos):
    lane = jnp.arange(LANES)[None, :]
    hi = (SEL_BLOCK * (pos // SEL_BLOCK))[:, None]
    lo = (pos % SEL_BLOCK)[:, None]
    cols = (jnp.where((lane >= COL_POS) & (lane < COL_POS + N_PIECES), hi, 0)
            + jnp.where((lane >= COL_POS + N_PIECES) & (lane < COL_POS + 2 * N_PIECES), lo, 0))
    return cols.astype(BF16)


def _nsa_tables(t_len):
    pos = jnp.arange(t_len)
    lane = jnp.arange(LANES)[None, :]
    block_one_hot = ((lane < t_len // SEL_BLOCK) & (lane == (pos // SEL_BLOCK)[:, None])).astype(BF16)
    kpos = _position_columns(pos) + block_one_hot
    n_cmp = t_len // CMP_STRIDE
    c_start = jnp.arange(n_cmp) * CMP_STRIDE
    cpos = _position_columns(c_start + CMP_LEN - 1)
    s_start = (jnp.arange(t_len // SEL_BLOCK) * SEL_BLOCK)[:, None]
    ovt = ((c_start[None, :] < s_start + SEL_BLOCK) & (c_start[None, :] + CMP_LEN - 1 >= s_start)).astype(BF16)
    slopes = jnp.exp2(-8.0 * jnp.arange(1, NSA_HEADS + 1, dtype=F32) / NSA_HEADS)
    pieces = jnp.stack([piece.astype(F32) for piece in _split3(slopes)], axis=1)
    slope_tab = jnp.zeros((NSA_HEADS, LANES), F32)
    slope_tab = (slope_tab.at[:, COL_POS:COL_POS + N_PIECES].set(pieces)
                 .at[:, COL_POS + N_PIECES:COL_POS + 2 * N_PIECES].set(pieces))
    return kpos, cpos, ovt, slope_tab


def _mem_kv_kernel(mem_ref, g_ref, wk_ref, wv_ref, k_ref, v_ref):
    mb = _rms(mem_ref[0], g_ref[...]).astype(BF16)
    k_ref[0] = _dot(mb, wk_ref[...]).astype(BF16)
    v_ref[0] = _dot(mb, wv_ref[...]).astype(BF16)


def _mem_kv(mem, g, wk, wv):
    bsz, m_len, _ = mem.shape
    w_spec = pl.BlockSpec((D_MODEL, D_MODEL), lambda b: (0, 0))
    m_spec = pl.BlockSpec((1, m_len, D_MODEL), lambda b: (b, 0, 0))
    return pl.pallas_call(
        _mem_kv_kernel,
        grid=(bsz,),
        in_specs=[m_spec, pl.BlockSpec((1, D_MODEL), lambda b: (0, 0)), w_spec, w_spec],
        out_specs=[m_spec, m_spec],
        out_shape=[jax.ShapeDtypeStruct((bsz, m_len, D_MODEL), BF16)] * 2,
        compiler_params=_params(("parallel",)),
        name="mem_kv",
    )(mem, g, wk, wv)


def _route(logits):
    lane = lax.broadcasted_iota(jnp.int32, logits.shape, 1)
    big_i = jnp.int32(1 << 20)
    is_coarse = lane < N_GROUPS
    lc = jnp.where(is_coarse, logits, NEG_INF)
    mc = jnp.max(lc, axis=1, keepdims=True)
    grp = jnp.min(jnp.where(is_coarse & (lc == mc), lane, big_i), axis=1, keepdims=True)
    w_grp = 1.0 / jnp.sum(jnp.where(is_coarse, jnp.exp(lc - mc), 0.0), axis=1, keepdims=True)
    e_id = lane - N_GROUPS
    in_grp = (e_id >= grp * EXPERTS_PER_GROUP) & (e_id < (grp + 1) * EXPERTS_PER_GROUP)
    lf = jnp.where(in_grp, logits, NEG_INF)
    m1 = jnp.max(lf, axis=1, keepdims=True)
    i1 = jnp.min(jnp.where(in_grp & (lf == m1), lane, big_i), axis=1, keepdims=True)
    lf2 = jnp.where(lane == i1, NEG_INF, lf)
    m2 = jnp.max(lf2, axis=1, keepdims=True)
    i2 = jnp.min(jnp.where(in_grp & (lane != i1) & (lf2 == m2), lane, big_i), axis=1, keepdims=True)
    denom = jnp.sum(jnp.where(in_grp, jnp.exp(lf - m1), 0.0), axis=1, keepdims=True)
    p1 = 1.0 / denom
    p2 = jnp.exp(m2 - m1) / denom
    tot = p1 + p2
    w1 = w_grp * (p1 / tot)
    w2 = w_grp * (p2 / tot)
    weights = jnp.where(lane + N_GROUPS == i1, w1, jnp.where(lane + N_GROUPS == i2, w2, 0.0))
    return jnp.where(lane == grp + N_EXPERTS, 1.0, weights)


def _mix_out_kernel(x_ref, oa_ref, ob_ref, oc_ref, woa_ref, wob_ref, woc_ref, gx_ref, wq_ref, mk_ref, mv_ref,
                    wo_ref, gf_ref, wr_ref, br_ref, x2_ref, hb_ref, comb_ref, cnt_ref):
    n_part = 2
    rows = x_ref.shape[0] // n_part
    parts = [slice(k * rows, (k + 1) * rows) for k in range(n_part)]
    x1 = [x_ref[r, :] + _dot(oa_ref[r, :], woa_ref[...]) + _dot(ob_ref[r, :], wob_ref[...])
          + _dot(oc_ref[r, :], woc_ref[...]) for r in parts]
    hq = [_rms(v, gx_ref[...]).astype(BF16) for v in x1]
    q = [(_dot(v, wq_ref[...]) * (MEM_HEAD_DIM ** -0.5)).astype(BF16) for v in hq]
    heads = [[] for _ in parts]
    for h in range(MEM_HEADS):
        sl = slice(h * MEM_HEAD_DIM, (h + 1) * MEM_HEAD_DIM)
        for k in range(n_part):
            s = _dot_nt(q[k][:, sl], mk_ref[0, :, sl])
            p = jnp.exp(s - jnp.max(s, axis=1, keepdims=True))
            p = p / jnp.sum(p, axis=1, keepdims=True)
            heads[k].append(_dot(p.astype(BF16), mv_ref[0, :, sl]).astype(BF16))
    x2 = [x1[k] + _dot(jnp.concatenate(heads[k], axis=1), wo_ref[...]) for k in range(n_part)]
    hn = [_rms(v, gf_ref[...]) for v in x2]
    for k, r in enumerate(parts):
        x2_ref[r, :] = x2[k]
        hb_ref[r, :] = hn[k].astype(BF16)
        h0, h1, _ = _split3(hn[k])
        partial = _dot(h0, wr_ref[...]) + _dot(h1, wr_ref[...])
        logits = partial + pltpu.roll(partial, ROUTER_WIDTH - ROUTER_PIECE, axis=1) + br_ref[...]
        routed = _route(logits)
        comb_ref[r, :] = routed
        per_part = rows // MOE_TILE
        for sub in range(per_part):
            cnt_ref[k * per_part + sub] = jnp.sum(routed[sub * MOE_TILE:(sub + 1) * MOE_TILE], axis=0, keepdims=True)


def _mix_out(x, o_a, o_b, o_c, wo_a, wo_b, wo_c, g_x, wq, mem_k, mem_v, wo, g_f, w_r, b_r, t_len, tm=1024):
    n = x.shape[0]
    per_batch = t_len // tm
    m_len = mem_k.shape[1]
    tile = lambda width: pl.BlockSpec((tm, width), lambda i: (i, 0))
    full = lambda shape: pl.BlockSpec(shape, lambda i: tuple(0 for _ in shape))
    mem_spec = pl.BlockSpec((1, m_len, D_MODEL), lambda i: (i // per_batch, 0, 0))
    return pl.pallas_call(
        _mix_out_kernel,
        grid=(n // tm,),
        in_specs=[tile(D_MODEL), tile(FOX_WIDTH), tile(POOL_WIDTH), tile(NSA_WIDTH),
                  full(wo_a.shape), full(wo_b.shape), full(wo_c.shape), full(g_x.shape), full(wq.shape),
                  mem_spec, mem_spec, full(wo.shape), full(g_f.shape), full(w_r.shape), full(b_r.shape)],
        out_specs=[tile(D_MODEL), tile(D_MODEL), tile(ROUTER_WIDTH),
                   pl.BlockSpec((tm // MOE_TILE, 1, ROUTER_WIDTH), lambda i: (i, 0, 0))],
        out_shape=[jax.ShapeDtypeStruct((n, D_MODEL), F32),
                   jax.ShapeDtypeStruct((n, D_MODEL), BF16),
                   jax.ShapeDtypeStruct((n, ROUTER_WIDTH), F32),
                   jax.ShapeDtypeStruct((n // MOE_TILE, 1, ROUTER_WIDTH), F32)],
        compiler_params=_params(("parallel",)),
        name="mix_out",
    )(x, o_a, o_b, o_c, wo_a, wo_b, wo_c, g_x, wq, mem_k, mem_v, wo, g_f, w_r, b_r)


def _group_ranks(comb, tri):
    lane = lax.broadcasted_iota(jnp.int32, comb.shape, 1)
    member = jnp.where((lane >= N_EXPERTS) & (lane < N_EXPERTS + N_GROUPS), comb, 0.0)
    return member, _dot(tri, member.astype(BF16))


def _moe_sort_kernel(dst_ref, hb_ref, comb_ref, tri_ref, xs_in, cs_in, xs_ref, cs_ref, stage_x, stage_c, sem):
    del xs_in, cs_in
    t = pl.program_id(0)
    comb = comb_ref[...]
    member, rank = _group_ranks(comb, tri_ref[...])
    rank_t = jnp.transpose(rank)
    member_t = jnp.transpose(member)
    slot = lax.broadcasted_iota(jnp.int32, (MOE_TILE, MOE_TILE), 0).astype(F32)
    pieces = _split3(comb)
    buf = t % 2

    def copies(into, g, dst):
        return (pltpu.make_async_copy(stage_x.at[into, g], xs_ref.at[pl.ds(dst, MOE_TILE)], sem.at[0, into, g]),
                pltpu.make_async_copy(stage_c.at[into, g], cs_ref.at[pl.ds(dst, MOE_TILE)], sem.at[1, into, g]))

    for g in range(N_GROUPS):
        row = N_EXPERTS + g
        take = jnp.where((rank_t[row:row + 1, :] == slot) & (member_t[row:row + 1, :] > 0.5), 1.0, 0.0).astype(BF16)
        stage_x[buf, g] = _dot(take, hb_ref[...]).astype(BF16)
        stage_c[buf, g] = _dot(take, pieces[0]) + _dot(take, pieces[1]) + _dot(take, pieces[2])

    @pl.when(t > 0)
    def _():
        for g in range(N_GROUPS):
            for copy in copies(1 - buf, g, 0):
                copy.wait()

    for g in range(N_GROUPS):
        dst = pl.multiple_of(dst_ref[t * N_GROUPS + g], MOE_SEG_ALIGN)
        for copy in copies(buf, g, dst):
            copy.start(priority=g % 2)

    @pl.when(t == pl.num_programs(0) - 1)
    def _():
        for g in range(N_GROUPS):
            for copy in copies(buf, g, 0):
                copy.wait()


def _moe_sort(hb, comb, tri, seg_dst, rows):
    n = hb.shape[0]
    any_spec = pl.BlockSpec(memory_space=pl.ANY)
    xs0 = jnp.zeros((rows, D_MODEL), BF16)
    cs0 = jnp.zeros((rows, ROUTER_WIDTH), F32)
    return pl.pallas_call(
        _moe_sort_kernel,
        grid_spec=pltpu.PrefetchScalarGridSpec(
            num_scalar_prefetch=1,
            grid=(n // MOE_TILE,),
            in_specs=[pl.BlockSpec((MOE_TILE, D_MODEL), lambda t, dst: (t, 0)),
                      pl.BlockSpec((MOE_TILE, ROUTER_WIDTH), lambda t, dst: (t, 0)),
                      pl.BlockSpec((MOE_TILE, MOE_TILE), lambda t, dst: (0, 0)),
                      any_spec, any_spec],
            out_specs=[any_spec, any_spec],
            scratch_shapes=[pltpu.VMEM((2, N_GROUPS, MOE_TILE, D_MODEL), BF16),
                            pltpu.VMEM((2, N_GROUPS, MOE_TILE, ROUTER_WIDTH), F32),
                            pltpu.SemaphoreType.DMA((2, 2, N_GROUPS))]),
        out_shape=[jax.ShapeDtypeStruct((rows, D_MODEL), BF16), jax.ShapeDtypeStruct((rows, ROUTER_WIDTH), F32)],
        input_output_aliases={4: 0, 5: 1},
        compiler_params=_params(("arbitrary",)),
        name="moe_sort",
    )(seg_dst, hb, comb, tri, xs0, cs0)


def _moe_ffn_kernel(group_ref, valid_ref, xs_ref, cs_ref, wg_ref, wu_ref, wd_ref, ys_ref):
    i = pl.program_id(0)

    @pl.when(valid_ref[i] == 0)
    def _():
        ys_ref[...] = jnp.zeros(ys_ref.shape, F32)

    @pl.when(valid_ref[i] != 0)
    def _():
        x = xs_ref[...]
        cs = cs_ref[...]
        lane = lax.broadcasted_iota(jnp.int32, cs.shape, 1)
        first_expert = group_ref[i] * EXPERTS_PER_GROUP
        total = None
        for e in range(EXPERTS_PER_GROUP):
            he = (jax.nn.silu(_dot(x, wg_ref[0, e])) * _dot(x, wu_ref[0, e])).astype(BF16)
            c_e = jnp.sum(jnp.where(lane == first_expert + e, cs, 0.0), axis=1, keepdims=True)
            y = c_e * _dot(he, wd_ref[0, e])
            total = y if total is None else total + y
        ys_ref[...] = total


def _moe_ffn(xs, cs, wg, wu, wd, layer, tile_group, tile_valid):
    rows = xs.shape[0]
    tile = lambda width: pl.BlockSpec((MOE_FFN_TILE, width), lambda i, grp, ok: (i, 0))
    experts = lambda shape: pl.BlockSpec((1, EXPERTS_PER_GROUP) + shape, lambda i, grp, ok: (layer, grp[i], 0, 0))
    return pl.pallas_call(
        _moe_ffn_kernel,
        grid_spec=pltpu.PrefetchScalarGridSpec(
            num_scalar_prefetch=2,
            grid=(rows // MOE_FFN_TILE,),
            in_specs=[tile(D_MODEL), tile(ROUTER_WIDTH),
                      experts((D_MODEL, EXPERT_FF)), experts((D_MODEL, EXPERT_FF)), experts((EXPERT_FF, D_MODEL))],
            out_specs=tile(D_MODEL)),
        out_shape=jax.ShapeDtypeStruct((rows, D_MODEL), F32),
        compiler_params=_params(("arbitrary",)),
        name="moe_ffn",
    )(tile_group, tile_valid, xs, cs, wg, wu, wd)


def _moe_unsort_kernel(dst_ref, ys_ref, comb_ref, tri_ref, x_ref, gfin_ref, o_ref, buf, sem, *, final_norm):
    t = pl.program_id(0)
    slot = t % 2

    def copy(tile, g, into):
        dst = pl.multiple_of(dst_ref[tile * N_GROUPS + g], MOE_SEG_ALIGN)
        return pltpu.make_async_copy(ys_ref.at[pl.ds(dst, MOE_TILE)], buf.at[into, g], sem.at[into, g])

    @pl.when(t == 0)
    def _():
        for g in range(N_GROUPS):
            copy(0, g, 0).start()

    @pl.when(t + 1 < pl.num_programs(0))
    def _():
        for g in range(N_GROUPS):
            copy(t + 1, g, 1 - slot).start(priority=g % 2)

    member, rank = _group_ranks(comb_ref[...], tri_ref[...])
    lane_slot = lax.broadcasted_iota(jnp.int32, (MOE_TILE, MOE_TILE), 1).astype(F32)
    y = x_ref[...]
    for g in range(N_GROUPS):
        copy(t, g, slot).wait()
        col = N_EXPERTS + g
        put = jnp.where((rank[:, col:col + 1] == lane_slot) & (member[:, col:col + 1] > 0.5), 1.0, 0.0).astype(BF16)
        seg = buf[slot, g]
        hi = seg.astype(BF16)
        lo = (seg - hi.astype(F32)).astype(BF16)
        y = y + (_dot(put, hi) + _dot(put, lo))
    o_ref[...] = _rms(y, gfin_ref[...]) if final_norm else y


def _moe_unsort(ys, comb, tri, x2, g_fin, seg_dst, final_norm):
    n = x2.shape[0]
    return pl.pallas_call(
        functools.partial(_moe_unsort_kernel, final_norm=final_norm),
        grid_spec=pltpu.PrefetchScalarGridSpec(
            num_scalar_prefetch=1,
            grid=(n // MOE_TILE,),
            in_specs=[pl.BlockSpec(memory_space=pl.ANY),
                      pl.BlockSpec((MOE_TILE, ROUTER_WIDTH), lambda t, dst: (t, 0)),
                      pl.BlockSpec((MOE_TILE, MOE_TILE), lambda t, dst: (0, 0)),
                      pl.BlockSpec((MOE_TILE, D_MODEL), lambda t, dst: (t, 0)),
                      pl.BlockSpec((1, D_MODEL), lambda t, dst: (0, 0))],
            out_specs=pl.BlockSpec((MOE_TILE, D_MODEL), lambda t, dst: (t, 0)),
            scratch_shapes=[pltpu.VMEM((2, N_GROUPS, MOE_TILE, D_MODEL), F32),
                            pltpu.SemaphoreType.DMA((2, N_GROUPS))]),
        out_shape=jax.ShapeDtypeStruct((n, D_MODEL), F32),
        compiler_params=_params(("arbitrary",)),
        name="moe_unsort",
    )(seg_dst, ys, comb, tri, x2, g_fin)


def _moe_plan(cnt, n_tok):
    n_tiles = cnt.shape[0]
    rows = n_tok + n_tiles * N_GROUPS * MOE_SEG_ALIGN + N_GROUPS * (MOE_FFN_TILE + MOE_GAP)
    rows = -(-rows // MOE_FFN_TILE) * MOE_FFN_TILE
    padded = (cnt + MOE_SEG_ALIGN - 1) // MOE_SEG_ALIGN * MOE_SEG_ALIGN
    within = jnp.cumsum(padded, axis=0) - padded
    region = (jnp.sum(padded, axis=0) + MOE_FFN_TILE - 1) // MOE_FFN_TILE * MOE_FFN_TILE
    start = jnp.cumsum(region + MOE_GAP) - (region + MOE_GAP)
    seg_dst = (start[None, :] + within).reshape(-1).astype(jnp.int32)
    tile_row = jnp.arange(rows // MOE_FFN_TILE) * MOE_FFN_TILE
    begun = tile_row[:, None] >= start[None, :]
    tile_valid = jnp.any(begun & (tile_row[:, None] < (start + region)[None, :]), axis=1).astype(jnp.int32)
    tile_group = jnp.clip(jnp.sum(begun, axis=1) - 1, 0, N_GROUPS - 1).astype(jnp.int32)
    return seg_dst, tile_group, tile_valid, rows


def _moe(x2, hb, comb, cnt, wg, wu, wd, layer, g_fin, final_norm):
    n = x2.shape[0]
    counts = jnp.round(cnt[:, 0, N_EXPERTS:N_EXPERTS + N_GROUPS]).astype(jnp.int32)
    seg_dst, tile_group, tile_valid, rows = _moe_plan(counts, n)
    tri = (jnp.arange(MOE_TILE)[:, None] > jnp.arange(MOE_TILE)[None, :]).astype(BF16)
    xs, cs = _moe_sort(hb, comb, tri, seg_dst, rows)
    ys = _moe_ffn(xs, cs, wg, wu, wd, layer, tile_group, tile_valid)
    return _moe_unsort(ys, comb, tri, x2, g_fin, seg_dst, final_norm)


def _pair_order(a, axis):
    shape = a.shape
    a = a.reshape(shape[:axis] + (NSA_HEADS, HEAD_DIM) + shape[axis + 1:])
    a = jnp.take(a, jnp.array(NSA_PAIR_ORDER), axis=axis)
    return a.reshape(shape)


def _split_in_proj_weight(w):
    edges = [0]
    for width in (FOX_WIDTH, FOX_WIDTH, FOX_WIDTH, FOX_HEADS, POOL_WIDTH, NSA_WIDTH) + (KV_WIDTH,) * 6 + (NSA_HEADS * N_BRANCH,):
        edges.append(edges[-1] + width)
    part = lambda k: w[:, edges[k]:edges[k + 1]]
    qa, ka, va, fa, ub, qc = (part(k) for k in range(6))
    kvs = [part(k) for k in range(6, 12)]
    gl = part(12)
    w_big = jnp.concatenate([qa, ka, va, _pair_order(qc, 1)] + kvs[2:], axis=1).astype(BF16)
    pad = jnp.zeros((w.shape[0], LANES - FOX_HEADS - NSA_HEADS * N_BRANCH), w.dtype)
    w_small = jnp.concatenate([ub, fa, gl, pad] + kvs[:2], axis=1).astype(BF16)
    return w_big, w_small


def _block_diag(w):
    groups, cin, cout = w.shape
    out = jnp.zeros((groups * cin, groups * cout), w.dtype)
    for gi in range(groups):
        out = out.at[gi * cin:(gi + 1) * cin, gi * cout:(gi + 1) * cout].set(w[gi])
    return out


def _layer(x, mem, p, experts, layer, bsz, t_len, final_g, final_norm):
    row = lambda v: v.reshape(1, -1)
    w_big, w_small = _split_in_proj_weight(p["w_in"])
    big, small = _in_proj(x, row(p["mix_norm_g"]), w_big, w_small)

    b_row = jnp.pad(p["fox_forget_b"], (0, LANES - FOX_HEADS)).reshape(1, LANES)
    ck = _forget_cumsum(small, b_row, _forget_placement(), bsz, t_len)
    o_a = _fox(big, ck, bsz, t_len)

    o_b = _pool(small, _block_diag(p["pool_w"]).astype(BF16), row(p["pool_scale"]), bsz, t_len)

    pe_k, w1_k, w2_k = _compress_weights(p["cmp_pe_k"], p["cmp_w1_k"], p["cmp_w2_k"])
    pe_v, w1_v, w2_v = _compress_weights(p["cmp_pe_v"], p["cmp_w1_v"], p["cmp_w2_v"])
    kc, vc = _compress(small, pe_k, pe_v, w1_k, w2_k, w1_v, w2_v, bsz, t_len)
    kpos, cpos, ovt, slope_tab = _nsa_tables(t_len)
    o_c = _nsa(big, kpos, kc, vc, cpos, ovt, slope_tab, small, row(p["nsa_gate_b"]), bsz, t_len)

    mem_k, mem_v = _mem_kv(mem, row(p["mem_norm_g"]), p["xattn_wk"].astype(BF16), p["xattn_wv"].astype(BF16))
    w_out = p["w_out"]
    wo_a = w_out[:FOX_WIDTH].astype(BF16)
    wo_b = w_out[FOX_WIDTH:FOX_WIDTH + POOL_WIDTH].astype(BF16)
    wo_c = _pair_order(w_out[FOX_WIDTH + POOL_WIDTH:], 0).astype(BF16)
    w_r = jnp.concatenate([p["router_coarse_w"], p["router_fine_w"],
                           jnp.zeros((D_MODEL, ROUTER_PIECE - N_GROUPS - N_EXPERTS), F32)], axis=1)
    w_r0, w_r1, _ = _split3(w_r)
    w_r = jnp.concatenate([w_r0, w_r1, jnp.zeros((D_MODEL, ROUTER_WIDTH - 2 * ROUTER_PIECE), BF16)], axis=1)
    b_r = jnp.concatenate([p["router_coarse_b"], p["router_fine_b"],
                           jnp.zeros((ROUTER_WIDTH - N_GROUPS - N_EXPERTS,), F32)]).reshape(1, -1)
    x2, hb, comb, cnt = _mix_out(x, o_a, o_b, o_c, wo_a, wo_b, wo_c, row(p["xattn_norm_g"]),
                                 p["xattn_wq"].astype(BF16), mem_k, mem_v, p["xattn_wo"].astype(BF16),
                                 row(p["ffn_norm_g"]), w_r, b_r, t_len)

    return _moe(x2, hb, comb, cnt, *experts, layer, row(final_g), final_norm)


def kernel(x, mem, mix_norm_g, w_in, fox_forget_b, pool_w, pool_scale, cmp_pe_k, cmp_pe_v, cmp_w1_k, cmp_w2_k, cmp_w1_v, cmp_w2_v, nsa_gate_b, w_out, xattn_norm_g, mem_norm_g, xattn_wq, xattn_wk, xattn_wv, xattn_wo, ffn_norm_g, router_coarse_w, router_coarse_b, router_fine_w, router_fine_b, exp_w_gate, exp_w_up, exp_w_down, final_norm_g):
    stacked = dict(mix_norm_g=mix_norm_g, w_in=w_in, fox_forget_b=fox_forget_b, pool_w=pool_w, pool_scale=pool_scale,
                   cmp_pe_k=cmp_pe_k, cmp_pe_v=cmp_pe_v, cmp_w1_k=cmp_w1_k, cmp_w2_k=cmp_w2_k, cmp_w1_v=cmp_w1_v,
                   cmp_w2_v=cmp_w2_v, nsa_gate_b=nsa_gate_b, w_out=w_out, xattn_norm_g=xattn_norm_g,
                   mem_norm_g=mem_norm_g, xattn_wq=xattn_wq, xattn_wk=xattn_wk, xattn_wv=xattn_wv, xattn_wo=xattn_wo,
                   ffn_norm_g=ffn_norm_g, router_coarse_w=router_coarse_w, router_coarse_b=router_coarse_b,
                   router_fine_w=router_fine_w, router_fine_b=router_fine_b)
    experts = (exp_w_gate.astype(BF16), exp_w_up.astype(BF16), exp_w_down.astype(BF16))
    bsz, t_len, d = x.shape
    depth = w_in.shape[0]
    h = x.reshape(bsz * t_len, d)
    for layer in range(depth):
        p = {name: value[layer] for name, value in stacked.items()}
        h = _layer(h, mem, p, experts, layer, bsz, t_len, final_norm_g, final_norm=(layer == depth - 1))
    return h.reshape(bsz, t_len, d)
```
